```python
import math
import jax, jax.numpy as jnp
from jax import lax
import numpy as np

D_MODEL = 2048
BATCH = 2
SEQ = 4096
DEPTH = 1

HEAD_DIM = 64
N_Q_HEADS = D_MODEL // 128
N_KV_HEADS = 4
Q_PER_KV = N_Q_HEADS // N_KV_HEADS
ATTN_W = N_Q_HEADS * HEAD_DIM
KV_W = N_KV_HEADS * HEAD_DIM
WINDOW = 128
BLOCK = 128
SSM_W = D_MODEL // 2
GROUP = 16
N_GROUPS = SSM_W // GROUP
STATE = 64
IN_SIZES = (ATTN_W, KV_W, KV_W, ATTN_W, SSM_W, SSM_W, D_MODEL, D_MODEL)
IN_W = sum(IN_SIZES)
NORM_EPS = 1e-6

kernel_name = "hybrid_swa_sink_s5_gated_merge"


def rms_norm(x, w):
    xf = x.astype(jnp.float32)
    y = xf * lax.rsqrt(jnp.mean(xf * xf, axis=-1, keepdims=True) + NORM_EPS)
    return (y * w.astype(jnp.float32)).astype(x.dtype)


def sliding_window_attention(q, k, v, sinks):
    b, l = q.shape[0], q.shape[1]
    nb = l // BLOCK
    qb = q.reshape(b, nb, BLOCK, N_KV_HEADS, Q_PER_KV, HEAD_DIM)
    kb = k.reshape(b, nb, BLOCK, N_KV_HEADS, HEAD_DIM)
    vb = v.reshape(b, nb, BLOCK, N_KV_HEADS, HEAD_DIM)
    k_prev = jnp.concatenate([jnp.zeros_like(kb[:, :1]), kb[:, :-1]], axis=1)
    v_prev = jnp.concatenate([jnp.zeros_like(vb[:, :1]), vb[:, :-1]], axis=1)
    kk = jnp.concatenate([k_prev, kb], axis=2)
    vv = jnp.concatenate([v_prev, vb], axis=2)
    scale = 1.0 / math.sqrt(HEAD_DIM)
    scores = jnp.einsum('bnqgrd,bnsgd->bngrqs', qb, kk).astype(jnp.float32) * scale
    q_loc = jnp.arange(BLOCK)[:, None] + BLOCK
    k_loc = jnp.arange(2 * BLOCK)[None, :]
    diff = q_loc - k_loc
    k_abs = (jnp.arange(nb)[:, None, None] - 1) * BLOCK + k_loc[None]
    valid = (diff >= 0)[None] & (diff < WINDOW)[None] & (k_abs >= 0)
    scores = jnp.where(valid[None, :, None, None], scores, -1e30)
    sink = jnp.broadcast_to(
        sinks.astype(jnp.float32).reshape(1, 1, N_KV_HEADS, Q_PER_KV, 1, 1),
        scores.shape[:-1] + (1,))
    probs = jax.nn.softmax(jnp.concatenate([scores, sink], axis=-1), axis=-1)[..., :-1]
    out = jnp.einsum('bngrqs,bnsgd->bnqgrd', probs.astype(v.dtype), vv)
    return out.reshape(b, l, ATTN_W)


def s5_ssm(u, A_re, A_im, log_dt, B_re, B_im, C_re, C_im, D_skip):
    dt = jnp.exp(log_dt)[:, None]
    mag = jnp.exp(dt * A_re)
    ab_re = mag * jnp.cos(dt * A_im)
    ab_im = mag * jnp.sin(dt * A_im)
    num_re = ab_re - 1.0
    num_im = ab_im
    den = A_re * A_re + A_im * A_im
    cf_re = (num_re * A_re + num_im * A_im) / den
    cf_im = (num_im * A_re - num_re * A_im) / den
    bu_re = jnp.einsum('blgh,gph->blgp', u, B_re)
    bu_im = jnp.einsum('blgh,gph->blgp', u, B_im)
    b_re = cf_re * bu_re - cf_im * bu_im
    b_im = cf_re * bu_im + cf_im * bu_re
    a_re = jnp.broadcast_to(ab_re, b_re.shape)
    a_im = jnp.broadcast_to(ab_im, b_im.shape)

    def combine(e1, e2):
        a1r, a1i, b1r, b1i = e1
        a2r, a2i, b2r, b2i = e2
        return (a2r * a1r - a2i * a1i,
                a2r * a1i + a2i * a1r,
                a2r * b1r - a2i * b1i + b2r,
                a2r * b1i + a2i * b1r + b2i)

    _, _, s_re, s_im = lax.associative_scan(combine, (a_re, a_im, b_re, b_im), axis=1)
    y = (jnp.einsum('blgp,ghp->blgh', s_re, C_re)
         - jnp.einsum('blgp,ghp->blgh', s_im, C_im)
         + D_skip * u)
    return y


def setup_inputs(seed: int = 0) -> dict:
    key = jax.random.key(seed)
    ks = jax.random.split(key, 20)
    f32 = jnp.float32
    n = jnp.arange(STATE, dtype=f32)
    x = jax.random.normal(ks[0], (BATCH, SEQ, D_MODEL), f32)
    norm_w = 1.0 + 0.02 * jax.random.normal(ks[1], (D_MODEL,), f32)
    w_in = jax.random.normal(ks[2], (D_MODEL, IN_W), f32) * D_MODEL ** -0.5
    q_norm_w = 1.0 + 0.02 * jax.random.normal(ks[3], (HEAD_DIM,), f32)
    k_norm_w = 1.0 + 0.02 * jax.random.normal(ks[4], (HEAD_DIM,), f32)
    sinks = jax.random.normal(ks[5], (N_Q_HEADS,), f32)
    w_attn_proj = jax.random.normal(ks[6], (ATTN_W, D_MODEL), f32) * ATTN_W ** -0.5
    A_re = -0.5 + 0.01 * jax.random.normal(ks[7], (N_GROUPS, STATE), f32)
    A_im = math.pi * n[None, :] + 0.01 * jax.random.normal(ks[8], (N_GROUPS, STATE), f32)
    log_dt = jax.random.uniform(ks[9], (N_GROUPS,), f32, math.log(1e-3), math.log(1e-1))
    b_scale = (2.0 * GROUP) ** -0.5
    B_re = jax.random.normal(ks[10], (N_GROUPS, STATE, GROUP), f32) * b_scale
    B_im = jax.random.normal(ks[11], (N_GROUPS, STATE, GROUP), f32) * b_scale
    c_scale = (2.0 * STATE) ** -0.5
    C_re = jax.random.normal(ks[12], (N_GROUPS, GROUP, STATE), f32) * c_scale
    C_im = jax.random.normal(ks[13], (N_GROUPS, GROUP, STATE), f32) * c_scale
    D_skip = jax.random.normal(ks[14], (N_GROUPS, GROUP), f32)
    w_glu = jax.random.normal(ks[15], (SSM_W, 2 * SSM_W), f32) * SSM_W ** -0.5
    b_glu = 0.01 * jax.random.normal(ks[16], (2 * SSM_W,), f32)
    w_ssm_proj = jax.random.normal(ks[17], (SSM_W, D_MODEL), f32) * SSM_W ** -0.5
    w_out = jax.random.normal(ks[18], (D_MODEL, D_MODEL), f32) * D_MODEL ** -0.5
    return {"x": x, "norm_w": norm_w, "w_in": w_in, "q_norm_w": q_norm_w,
            "k_norm_w": k_norm_w, "sinks": sinks, "w_attn_proj": w_attn_proj,
            "A_re": A_re, "A_im": A_im, "log_dt": log_dt, "B_re": B_re, "B_im": B_im,
            "C_re": C_re, "C_im": C_im, "D_skip": D_skip, "w_glu": w_glu,
            "b_glu": b_glu, "w_ssm_proj": w_ssm_proj, "w_out": w_out}


def reference(x, norm_w, w_in, q_norm_w, k_norm_w, sinks, w_attn_proj, A_re, A_im,
              log_dt, B_re, B_im, C_re, C_im, D_skip, w_glu, b_glu, w_ssm_proj, w_out):
    b, l, _ = x.shape
    split_pts = list(np.cumsum(IN_SIZES)[:-1])
    f32 = jnp.float32
    for _layer in range(DEPTH):
        h = rms_norm(x, norm_w)
        proj = h @ w_in
        q, k, v, a_gate, u, z, g_a, g_s = jnp.split(proj, split_pts, axis=-1)
        q = rms_norm(q.reshape(b, l, N_Q_HEADS, HEAD_DIM), q_norm_w)
        k = rms_norm(k.reshape(b, l, N_KV_HEADS, HEAD_DIM), k_norm_w)
        v = v.reshape(b, l, N_KV_HEADS, HEAD_DIM)
        attn = sliding_window_attention(q, k, v, sinks)
        y_a = (attn * jax.nn.silu(a_gate)) @ w_attn_proj
        u_g = u.reshape(b, l, N_GROUPS, GROUP).astype(f32)
        y_ssm = s5_ssm(u_g, A_re.astype(f32), A_im.astype(f32), log_dt.astype(f32),
                       B_re.astype(f32), B_im.astype(f32), C_re.astype(f32),
                       C_im.astype(f32), D_skip.astype(f32))
        y_ssm = jax.nn.gelu(y_ssm.reshape(b, l, SSM_W)).astype(x.dtype)
        glu_a, glu_b = jnp.split(y_ssm @ w_glu + b_glu, 2, axis=-1)
        y_s = (glu_a * jax.nn.sigmoid(glu_b) * jax.nn.silu(z)) @ w_ssm_proj
        merged = jax.nn.sigmoid(g_a) * y_a + jax.nn.sigmoid(g_s) * y_s
        x = x + merged @ w_out
    return x
```

```python
import functools
import math

import jax
import jax.numpy as jnp
from jax import lax
from jax.experimental import pallas as pl
from jax.experimental.pallas import tpu as pltpu

F32 = jnp.float32
BF16 = jnp.bfloat16

D_MODEL = 2048
HEAD_DIM = 64
N_Q_HEADS = 16
N_KV_HEADS = 4
Q_PER_KV = 4
ATTN_W = N_Q_HEADS * HEAD_DIM
KV_W = N_KV_HEADS * HEAD_DIM
WINDOW = 128
SSM_W = D_MODEL // 2
GROUP = 16
N_GROUPS = SSM_W // GROUP
STATE = 64
NORM_EPS = 1e-6

COL_Q, COL_GATE, COL_U, COL_Z = 0, 1024, 2048, 3072
COL_GA, COL_GS, COL_K, COL_V = 4096, 6144, 8192, 8448
IN_W = 8704

CHUNK = 16
CHUNK_W = CHUNK * GROUP
LANES = 128
VMEM_LIMIT = 48 * 1024 * 1024


def _dot(a, b):
    return jnp.dot(a, b, preferred_element_type=F32)


def _in_proj_kernel(x_ref, nw_ref, w_ref, o_ref, h_ref):
    @pl.when(pl.program_id(1) == 0)
    def _():
        x = x_ref[...]
        ms = jnp.mean(x * x, axis=-1, keepdims=True)
        h_ref[...] = (x * lax.rsqrt(ms + NORM_EPS) * nw_ref[...]).astype(BF16)

    o_ref[...] = _dot(h_ref[...], w_ref[...]).astype(BF16)


def _in_proj(x2, norm_w, w_in_bf, tm=1024, tn=512):
    m = x2.shape[0]
    return pl.pallas_call(
        _in_proj_kernel,
        grid=(m // tm, IN_W // tn),
        in_specs=[
            pl.BlockSpec((tm, D_MODEL), lambda i, j: (i, 0)),
            pl.BlockSpec((1, D_MODEL), lambda i, j: (0, 0)),
            pl.BlockSpec((D_MODEL, tn), lambda i, j: (0, j)),
        ],
        out_specs=pl.BlockSpec((tm, tn), lambda i, j: (i, j)),
        out_shape=jax.ShapeDtypeStruct((m, IN_W), BF16),
        scratch_shapes=[pltpu.VMEM((tm, D_MODEL), BF16)],
        compiler_params=pltpu.CompilerParams(
            dimension_semantics=("arbitrary", "arbitrary"), vmem_limit_bytes=VMEM_LIMIT),
        name="in_proj",
    )(x2, norm_w.reshape(1, D_MODEL), w_in_bf)


def _head_norm(t, w):
    ms = jnp.mean(t * t, axis=-1, keepdims=True)
    return t * lax.rsqrt(ms + NORM_EPS) * w


def _swa_kernel(sink_ref, q_ref, g_ref, kc_ref, vc_ref, kp_ref, vp_ref, qw_ref, kw_ref, o_ref,
                *, n_sub):
    qw = qw_ref[...] * (1.0 / math.sqrt(HEAD_DIM))
    kw = kw_ref[...]

    row = lax.broadcasted_iota(jnp.int32, (WINDOW, 2 * WINDOW), 0)
    col = lax.broadcasted_iota(jnp.int32, (WINDOW, 2 * WINDOW), 1)
    rel = col - row
    band = (rel >= 1) & (rel <= WINDOW)
    first_key = jnp.where(pl.program_id(1) == 0, WINDOW, 0)
    band_first = band & (col >= first_key)

    for g in range(N_KV_HEADS):
        kcol = slice(g * HEAD_DIM, (g + 1) * HEAD_DIM)
        k_prev = _head_norm(kp_ref[:, kcol].astype(F32), kw).astype(BF16)
        v_prev = vp_ref[:, kcol]
        for n in range(n_sub):
            rows = slice(n * WINDOW, (n + 1) * WINDOW)
            k_cur = _head_norm(kc_ref[rows, kcol].astype(F32), kw).astype(BF16)
            v_cur = vc_ref[rows, kcol]
            k_ctx = jnp.concatenate([k_prev, k_cur], axis=0)
            v_ctx = jnp.concatenate([v_prev, v_cur], axis=0)
            mask = band_first if n == 0 else band

            qg = q_ref[rows, g * Q_PER_KV * HEAD_DIM:(g + 1) * Q_PER_KV * HEAD_DIM].astype(F32)
            q_heads = [
                _head_norm(qg[:, r * HEAD_DIM:(r + 1) * HEAD_DIM], qw).astype(BF16)
                for r in range(Q_PER_KV)
            ]
            qs = jnp.concatenate(q_heads, axis=0)
            s = lax.dot_general(qs, k_ctx, (((1,), (1,)), ((), ())),
                                preferred_element_type=F32)
            probs, inv = [], []
            for r in range(Q_PER_KV):
                sink = sink_ref[g * Q_PER_KV + r]
                sr = jnp.where(mask, s[r * WINDOW:(r + 1) * WINDOW], -1e30)
                mx = jnp.maximum(jnp.max(sr, axis=-1, keepdims=True), sink)
                p = jnp.exp(sr - mx)
                den = jnp.sum(p, axis=-1, keepdims=True) + jnp.exp(sink - mx)
                probs.append(p.astype(BF16))
                inv.append(1.0 / den)
            o = _dot(jnp.concatenate(probs, axis=0), v_ctx)
            outs = [o[r * WINDOW:(r + 1) * WINDOW] * inv[r] for r in range(Q_PER_KV)]
            og = jnp.concatenate(outs, axis=1)
            gcols = slice(g * Q_PER_KV * HEAD_DIM, (g + 1) * Q_PER_KV * HEAD_DIM)
            gate = g_ref[rows, gcols].astype(F32)
            o_ref[rows, gcols] = (og * (gate * jax.nn.sigmoid(gate))).astype(BF16)
            k_prev, v_prev = k_cur, v_cur


def _swa(proj, sinks, q_norm_w, k_norm_w, batch, seq, tq=512):
    n_sub = tq // WINDOW
    tiles = seq // tq
    m = batch * seq

    def cur_rows(b, i):
        return b * tiles + i

    def prev_rows(b, i):
        return jnp.maximum((b * tiles + i) * n_sub - 1, 0)

    kernel = functools.partial(_swa_kernel, n_sub=n_sub)
    return pl.pallas_call(
        kernel,
        grid=(batch, tiles),
        in_specs=[
            pl.BlockSpec(memory_space=pltpu.SMEM),
            pl.BlockSpec((tq, ATTN_W), lambda b, i: (cur_rows(b, i), COL_Q // ATTN_W)),
            pl.BlockSpec((tq, ATTN_W), lambda b, i: (cur_rows(b, i), COL_GATE // ATTN_W)),
            pl.BlockSpec((tq, KV_W), lambda b, i: (cur_rows(b, i), COL_K // KV_W)),
            pl.BlockSpec((tq, KV_W), lambda b, i: (cur_rows(b, i), COL_V // KV_W)),
            pl.BlockSpec((WINDOW, KV_W), lambda b, i: (prev_rows(b, i), COL_K // KV_W)),
            pl.BlockSpec((WINDOW, KV_W), lambda b, i: (prev_rows(b, i), COL_V // KV_W)),
            pl.BlockSpec((1, HEAD_DIM), lambda b, i: (0, 0)),
            pl.BlockSpec((1, HEAD_DIM), lambda b, i: (0, 0)),
        ],
        out_specs=pl.BlockSpec((tq, ATTN_W), lambda b, i: (cur_rows(b, i), 0)),
        out_shape=jax.ShapeDtypeStruct((m, ATTN_W), BF16),
        compiler_params=pltpu.CompilerParams(
            dimension_semantics=("arbitrary", "arbitrary"), vmem_limit_bytes=VMEM_LIMIT),
        name="swa",
    )(sinks, proj, proj, proj, proj, proj, proj,
      q_norm_w.reshape(1, HEAD_DIM), k_norm_w.reshape(1, HEAD_DIM))


def _ssm_weights(A_re, A_im, log_dt, B_re, B_im, C_re, C_im, D_skip):
    dt = jnp.exp(log_dt)[:, None]
    lag = jnp.arange(CHUNK + 1, dtype=F32)[:, None, None]
    mag = jnp.exp(lag * (dt * A_re)[None])
    ang = lag * (dt * A_im)[None]
    pw_re, pw_im = mag * jnp.cos(ang), mag * jnp.sin(ang)
    ab_re, ab_im = pw_re[1], pw_im[1]
    den = A_re * A_re + A_im * A_im
    num_re, num_im = ab_re - 1.0, ab_im
    cf_re = (num_re * A_re + num_im * A_im) / den
    cf_im = (num_im * A_re - num_re * A_im) / den
    bb_re = cf_re[:, :, None] * B_re - cf_im[:, :, None] * B_im
    bb_im = cf_re[:, :, None] * B_im + cf_im[:, :, None] * B_re

    ab_b_re = pw_re[..., None] * bb_re[None] - pw_im[..., None] * bb_im[None]
    ab_b_im = pw_re[..., None] * bb_im[None] + pw_im[..., None] * bb_re[None]
    taps = (jnp.einsum('ghp,lgpk->lgkh', C_re, ab_b_re[:CHUNK])
            - jnp.einsum('ghp,lgpk->lgkh', C_im, ab_b_im[:CHUNK]))
    s_idx = jnp.arange(CHUNK)[:, None]
    t_idx = jnp.arange(CHUNK)[None, :]
    lag_idx = jnp.clip(t_idx - s_idx, 0, CHUNK - 1)
    m_full = taps[lag_idx]
    m_full = jnp.where((t_idx >= s_idx)[:, :, None, None, None], m_full, 0.0)
    eye_t = jnp.eye(CHUNK, dtype=F32)[:, :, None, None, None]
    eye_h = jnp.eye(GROUP, dtype=F32)[None, None, None]
    m_full = m_full + eye_t * eye_h * D_skip[None, None, :, None, :]
    m_w = m_full.transpose(2, 0, 3, 1, 4).reshape(N_GROUPS, CHUNK_W, CHUNK_W)

    rev = ab_b_re[CHUNK - 1::-1][:CHUNK], ab_b_im[CHUNK - 1::-1][:CHUNK]
    p_re = rev[0].transpose(1, 0, 3, 2).reshape(N_GROUPS, CHUNK_W, STATE)
    p_im = rev[1].transpose(1, 0, 3, 2).reshape(N_GROUPS, CHUNK_W, STATE)
    ca_re = C_re[None] * pw_re[1:, :, None, :] - C_im[None] * pw_im[1:, :, None, :]
    ca_im = C_re[None] * pw_im[1:, :, None, :] + C_im[None] * pw_re[1:, :, None, :]
    q_re = ca_re.transpose(1, 3, 0, 2).reshape(N_GROUPS, STATE, CHUNK_W)
    q_im = (-ca_im).transpose(1, 3, 0, 2).reshape(N_GROUPS, STATE, CHUNK_W)

    odd = (jnp.arange(N_GROUPS) % 2 == 1)[:, None, None]

    def lane_half(w):
        z = jnp.zeros_like(w)
        return jnp.where(odd, jnp.concatenate([z, w], -1), jnp.concatenate([w, z], -1))

    def row_half(w):
        z = jnp.zeros_like(w)
        return jnp.where(odd, jnp.concatenate([z, w], 1), jnp.concatenate([w, z], 1))

    a_re = pw_re[CHUNK].reshape(N_GROUPS // 2, LANES)
    a_im = pw_im[CHUNK].reshape(N_GROUPS // 2, LANES)
    return (m_w.astype(BF16), lane_half(p_re).astype(BF16), lane_half(p_im).astype(BF16),
            row_half(q_re).astype(BF16), row_half(q_im).astype(BF16), a_re, a_im)


def _ssm_kernel(u_ref, m_ref, pre_ref, pim_ref, qre_ref, qim_ref, are_ref, aim_ref, y_ref,
                xre_s, xim_s, sre_s, sim_s, *, batch, gt, n_chunks):
    pairs = gt // 2
    rows = batch * pairs

    for b in range(batch):
        for j in range(pairs):
            u0, u1 = u_ref[b, 2 * j], u_ref[b, 2 * j + 1]
            r = b * pairs + j
            xre_s[pl.ds(r, n_chunks, stride=rows), :] = (
                _dot(u0, pre_ref[2 * j]) + _dot(u1, pre_ref[2 * j + 1]))
            xim_s[pl.ds(r, n_chunks, stride=rows), :] = (
                _dot(u0, pim_ref[2 * j]) + _dot(u1, pim_ref[2 * j + 1]))

    a_re = jnp.concatenate([are_ref[0]] * batch, axis=0)
    a_im = jnp.concatenate([aim_ref[0]] * batch, axis=0)

    def step(c, carry):
        s_re, s_im = carry
        off = pl.multiple_of(c * rows, rows)
        sre_s[pl.ds(off, rows), :] = s_re
        sim_s[pl.ds(off, rows), :] = s_im
        x_re = xre_s[pl.ds(off, rows), :]
        x_im = xim_s[pl.ds(off, rows), :]
        return (a_re * s_re - a_im * s_im + x_re, a_re * s_im + a_im * s_re + x_im)

    zero = jnp.zeros((rows, LANES), F32)
    lax.fori_loop(0, n_chunks, step, (zero, zero))

    for b in range(batch):
        for g in range(gt):
            r = b * pairs + g // 2
            sp_re = sre_s[pl.ds(r, n_chunks, stride=rows), :].astype(BF16)
            sp_im = sim_s[pl.ds(r, n_chunks, stride=rows), :].astype(BF16)
            y = (_dot(u_ref[b, g], m_ref[g]) + _dot(sp_re, qre_ref[g]) + _dot(sp_im, qim_ref[g]))
            y_ref[b, g] = jax.nn.gelu(y).astype(BF16)


def _ssm(u_chunks, weights, gt=16):
    m_w, p_re, p_im, q_re, q_im, a_re, a_im = weights
    batch, _, n_chunks, _ = u_chunks.shape
    pairs = gt // 2
    rows = batch * pairs
    a_re = a_re.reshape(N_GROUPS // gt, pairs, LANES)
    a_im = a_im.reshape(N_GROUPS // gt, pairs, LANES)
    kernel = functools.partial(_ssm_kernel, batch=batch, gt=gt, n_chunks=n_chunks)
    grp = lambda i: (i, 0, 0)
    return pl.pallas_call(
        kernel,
        grid=(N_GROUPS // gt,),
        in_specs=[
            pl.BlockSpec((batch, gt, n_chunks, CHUNK_W), lambda i: (0, i, 0, 0)),
            pl.BlockSpec((gt, CHUNK_W, CHUNK_W), grp),
            pl.BlockSpec((gt, CHUNK_W, LANES), grp),
            pl.BlockSpec((gt, CHUNK_W, LANES), grp),
            pl.BlockSpec((gt, LANES, CHUNK_W), grp),
            pl.BlockSpec((gt, LANES, CHUNK_W), grp),
            pl.BlockSpec((1, pairs, LANES), grp),
            pl.BlockSpec((1, pairs, LANES), grp),
        ],
        out_specs=pl.BlockSpec((batch, gt, n_chunks, CHUNK_W), lambda i: (0, i, 0, 0)),
        out_shape=jax.ShapeDtypeStruct(u_chunks.shape, BF16),
        scratch_shapes=[pltpu.VMEM((n_chunks * rows, LANES), F32) for _ in range(4)],
        compiler_params=pltpu.CompilerParams(
            dimension_semantics=("arbitrary",), vmem_limit_bytes=VMEM_LIMIT),
        name="ssm",
    )(u_chunks, m_w, p_re, p_im, q_re, q_im, a_re, a_im)


def _tail_kernel(ag_ref, yg_ref, z_ref, ga_ref, gs_ref, x_ref, wa_ref, wg_ref, bg_ref, ws_ref,
                 wo_ref, o_ref):
    y_a = _dot(ag_ref[...], wa_ref[...])
    glu = _dot(yg_ref[...], wg_ref[...]) + bg_ref[...]
    z = z_ref[...].astype(F32)
    t = glu[:, :SSM_W] * jax.nn.sigmoid(glu[:, SSM_W:]) * (z * jax.nn.sigmoid(z))
    y_s = _dot(t.astype(BF16), ws_ref[...])
    merged = (jax.nn.sigmoid(ga_ref[...].astype(F32)) * y_a
              + jax.nn.sigmoid(gs_ref[...].astype(F32)) * y_s)
    o_ref[...] = x_ref[...] + _dot(merged.astype(BF16), wo_ref[...])


def _tail(attn_g, y_g, proj, x2, w_attn, w_glu, b_glu, w_ssm, w_out, tm=256):
    m = x2.shape[0]
    row = lambda blk: (lambda i: (i, blk))
    const = lambda i: (0, 0)
    resident = functools.partial(pl.BlockSpec, index_map=const, pipeline_mode=pl.Buffered(1))
    return pl.pallas_call(
        _tail_kernel,
        grid=(m // tm,),
        in_specs=[
            pl.BlockSpec((tm, ATTN_W), row(0)),
            pl.BlockSpec((tm, SSM_W), row(0)),
            pl.BlockSpec((tm, SSM_W), row(COL_Z // SSM_W)),
            pl.BlockSpec((tm, D_MODEL), row(COL_GA // D_MODEL)),
            pl.BlockSpec((tm, D_MODEL), row(COL_GS // D_MODEL)),
            pl.BlockSpec((tm, D_MODEL), row(0)),
            resident((ATTN_W, D_MODEL)),
            resident((SSM_W, 2 * SSM_W)),
            resident((1, 2 * SSM_W)),
            resident((SSM_W, D_MODEL)),
            resident((D_MODEL, D_MODEL)),
        ],
        out_specs=pl.BlockSpec((tm, D_MODEL), row(0)),
        out_shape=jax.ShapeDtypeStruct((m, D_MODEL), F32),
        compiler_params=pltpu.CompilerParams(
            dimension_semantics=("arbitrary",), vmem_limit_bytes=VMEM_LIMIT),
        name="tail",
    )(attn_g, y_g, proj, proj, proj, x2, w_attn, w_glu, b_glu.reshape(1, -1), w_ssm, w_out)


def kernel(x, norm_w, w_in, q_norm_w, k_norm_w, sinks, w_attn_proj, A_re, A_im, log_dt, B_re, B_im,
           C_re, C_im, D_skip, w_glu, b_glu, w_ssm_proj, w_out):
    batch, seq, _ = x.shape
    m = batch * seq
    x2 = x.reshape(m, D_MODEL)

    w_in_bf = jnp.concatenate(
        [w_in[:, :ATTN_W], w_in[:, ATTN_W + 2 * KV_W:], w_in[:, ATTN_W:ATTN_W + 2 * KV_W]],
        axis=1).astype(BF16)

    proj = _in_proj(x2, norm_w, w_in_bf)

    attn_g = _swa(proj, sinks, q_norm_w, k_norm_w, batch, seq)

    n_chunks = seq // CHUNK
    u = proj[:, COL_U:COL_U + SSM_W].reshape(batch, n_chunks, CHUNK, N_GROUPS, GROUP)
    u_chunks = u.transpose(0, 3, 1, 2, 4).reshape(batch, N_GROUPS, n_chunks, CHUNK_W)
    y_chunks = _ssm(u_chunks, _ssm_weights(A_re, A_im, log_dt, B_re, B_im, C_re, C_im, D_skip))
    y_g = (y_chunks.reshape(batch, N_GROUPS, n_chunks, CHUNK, GROUP)
           .transpose(0, 2, 3, 1, 4).reshape(m, SSM_W))

    out = _tail(attn_g, y_g, proj, x2, w_attn_proj.astype(BF16), w_glu.astype(BF16), b_glu,
                w_ssm_proj.astype(BF16), w_out.astype(BF16))
    return out.reshape(batch, seq, D_MODEL)
```

```python
import functools
import math

import jax
import jax.numpy as jnp
from jax import lax
from jax.experimental import pallas as pl
from jax.experimental.pallas import tpu as pltpu

F32 = jnp.float32
BF16 = jnp.bfloat16

D_MODEL = 2048
HEAD_DIM = 64
N_Q_HEADS = 16
N_KV_HEADS = 4
Q_PER_KV = 4
ATTN_W = N_Q_HEADS * HEAD_DIM
KV_W = N_KV_HEADS * HEAD_DIM
WINDOW = 128
SSM_W = D_MODEL // 2
GROUP = 16
N_GROUPS = SSM_W // GROUP
STATE = 64
NORM_EPS = 1e-6

COL_Q, COL_GATE, COL_U, COL_Z = 0, 1024, 2048, 3072
COL_GA, COL_GS, COL_K, COL_V = 4096, 6144, 8192, 8448
IN_W = 8704

LANES = 128
MXU = 256
CHUNK = 16
SLAB_G = LANES // GROUP
N_SLABS = N_GROUPS // SLAB_G
PAIRS = SLAB_G // 2
CHUNK_W = CHUNK * LANES
VMEM_LIMIT = 48 * 1024 * 1024


def _dot(a, b):
    return jnp.dot(a, b, preferred_element_type=F32)


def _in_proj_kernel(x_ref, nw_ref, w_ref, o_ref, uc_ref, h_ref, us_ref, *, tm, tn, ju0):
    j = pl.program_id(1)

    @pl.when(j == 0)
    def _():
        x = x_ref[...]
        ms = jnp.mean(x * x, axis=-1, keepdims=True)
        h_ref[...] = (x * lax.rsqrt(ms + NORM_EPS) * nw_ref[...]).astype(BF16)

    acc = _dot(h_ref[...], w_ref[...])
    o_ref[...] = acc.astype(BF16)

    n_sl = tn // LANES
    rows = tm // CHUNK

    @pl.when((j >= ju0) & (j < ju0 + SSM_W // tn))
    def _():
        for s in range(n_sl):
            us_ref[s * tm:(s + 1) * tm, :] = acc[:, s * LANES:(s + 1) * LANES]
        for s in range(n_sl):
            for t in range(CHUNK):
                piece = us_ref[pl.ds(s * tm + t, rows, stride=CHUNK), :]
                uc_ref[s, :, t * LANES:(t + 1) * LANES] = piece.astype(BF16)


def _in_proj(x2, norm_w, w_in_bf, tm=1024, tn=512):
    m = x2.shape[0]
    ju0 = COL_U // tn
    n_sl = tn // LANES
    kernel = functools.partial(_in_proj_kernel, tm=tm, tn=tn, ju0=ju0)
    return pl.pallas_call(
        kernel,
        grid=(m // tm, IN_W // tn),
        in_specs=[
            pl.BlockSpec((tm, D_MODEL), lambda i, j: (i, 0)),
            pl.BlockSpec((1, D_MODEL), lambda i, j: (0, 0)),
            pl.BlockSpec((D_MODEL, tn), lambda i, j: (0, j)),
        ],
        out_specs=[
            pl.BlockSpec((tm, tn), lambda i, j: (i, j)),
            pl.BlockSpec((n_sl, tm // CHUNK, CHUNK_W),
                         lambda i, j: (jnp.clip(j - ju0, 0, SSM_W // tn - 1), i, 0)),
        ],
        out_shape=[
            jax.ShapeDtypeStruct((m, IN_W), BF16),
            jax.ShapeDtypeStruct((N_SLABS, m // CHUNK, CHUNK_W), BF16),
        ],
        scratch_shapes=[pltpu.VMEM((tm, D_MODEL), BF16), pltpu.VMEM((n_sl * tm, LANES), F32)],
        compiler_params=pltpu.CompilerParams(
            dimension_semantics=("arbitrary", "arbitrary"), vmem_limit_bytes=VMEM_LIMIT),
        name="in_proj",
    )(x2, norm_w.reshape(1, D_MODEL), w_in_bf)


def _head_norm(t, w):
    ms = jnp.mean(t * t, axis=-1, keepdims=True)
    return t * lax.rsqrt(ms + NORM_EPS) * w


def _swa_kernel(sink_ref, q_ref, g_ref, kc_ref, vc_ref, kp_ref, vp_ref, qw_ref, kw_ref, o_ref,
                *, n_sub):
    qw = qw_ref[...] * (1.0 / math.sqrt(HEAD_DIM))
    kw = kw_ref[...]

    row = lax.broadcasted_iota(jnp.int32, (WINDOW, 2 * WINDOW), 0)
    col = lax.broadcasted_iota(jnp.int32, (WINDOW, 2 * WINDOW), 1)
    rel = col - row
    band = (rel >= 1) & (rel <= WINDOW)
    first_key = jnp.where(pl.program_id(1) == 0, WINDOW, 0)
    band_first = band & (col >= first_key)

    for g in range(N_KV_HEADS):
        kcol = slice(g * HEAD_DIM, (g + 1) * HEAD_DIM)
        k_prev = _head_norm(kp_ref[:, kcol].astype(F32), kw).astype(BF16)
        v_prev = vp_ref[:, kcol]
        for n in range(n_sub):
            rows = slice(n * WINDOW, (n + 1) * WINDOW)
            k_cur = _head_norm(kc_ref[rows, kcol].astype(F32), kw).astype(BF16)
            v_cur = vc_ref[rows, kcol]
            k_ctx = jnp.concatenate([k_prev, k_cur], axis=0)
            v_ctx = jnp.concatenate([v_prev, v_cur], axis=0)
            mask = band_first if n == 0 else band

            qg = q_ref[rows, g * Q_PER_KV * HEAD_DIM:(g + 1) * Q_PER_KV * HEAD_DIM].astype(F32)
            q_heads = [
                _head_norm(qg[:, r * HEAD_DIM:(r + 1) * HEAD_DIM], qw).astype(BF16)
                for r in range(Q_PER_KV)
            ]
            qs = jnp.concatenate(q_heads, axis=0)
            s = lax.dot_general(qs, k_ctx, (((1,), (1,)), ((), ())),
                                preferred_element_type=F32)
            probs, inv = [], []
            for r in range(Q_PER_KV):
                sink = sink_ref[g * Q_PER_KV + r]
                sr = jnp.where(mask, s[r * WINDOW:(r + 1) * WINDOW], -1e30)
                mx = jnp.maximum(jnp.max(sr, axis=-1, keepdims=True), sink)
                p = jnp.exp(sr - mx)
                den = jnp.sum(p, axis=-1, keepdims=True) + jnp.exp(sink - mx)
                probs.append(p.astype(BF16))
                inv.append(1.0 / den)
            o = _dot(jnp.concatenate(probs, axis=0), v_ctx)
            outs = [o[r * WINDOW:(r + 1) * WINDOW] * inv[r] for r in range(Q_PER_KV)]
            og = jnp.concatenate(outs, axis=1)
            gcols = slice(g * Q_PER_KV * HEAD_DIM, (g + 1) * Q_PER_KV * HEAD_DIM)
            gate = g_ref[rows, gcols].astype(F32)
            o_ref[rows, gcols] = (og * (gate * jax.nn.sigmoid(gate))).astype(BF16)
            k_prev, v_prev = k_cur, v_cur


def _swa(proj, sinks, q_norm_w, k_norm_w, batch, seq, tq=512):
    n_sub = tq // WINDOW
    tiles = seq // tq
    m = batch * seq

    def cur_rows(b, i):
        return b * tiles + i

    def prev_rows(b, i):
        return jnp.maximum((b * tiles + i) * n_sub - 1, 0)

    kernel = functools.partial(_swa_kernel, n_sub=n_sub)
    return pl.pallas_call(
        kernel,
        grid=(batch, tiles),
        in_specs=[
            pl.BlockSpec(memory_space=pltpu.SMEM),
            pl.BlockSpec((tq, ATTN_W), lambda b, i: (cur_rows(b, i), COL_Q // ATTN_W)),
            pl.BlockSpec((tq, ATTN_W), lambda b, i: (cur_rows(b, i), COL_GATE // ATTN_W)),
            pl.BlockSpec((tq, KV_W), lambda b, i: (cur_rows(b, i), COL_K // KV_W)),
            pl.BlockSpec((tq, KV_W), lambda b, i: (cur_rows(b, i), COL_V // KV_W)),
            pl.BlockSpec((WINDOW, KV_W), lambda b, i: (prev_rows(b, i), COL_K // KV_W)),
            pl.BlockSpec((WINDOW, KV_W), lambda b, i: (prev_rows(b, i), COL_V // KV_W)),
            pl.BlockSpec((1, HEAD_DIM), lambda b, i: (0, 0)),
            pl.BlockSpec((1, HEAD_DIM), lambda b, i: (0, 0)),
        ],
        out_specs=pl.BlockSpec((tq, ATTN_W), lambda b, i: (cur_rows(b, i), 0)),
        out_shape=jax.ShapeDtypeStruct((m, ATTN_W), BF16),
        compiler_params=pltpu.CompilerParams(
            dimension_semantics=("arbitrary", "arbitrary"), vmem_limit_bytes=VMEM_LIMIT),
        name="swa",
    )(sinks, proj, proj, proj, proj, proj, proj,
      q_norm_w.reshape(1, HEAD_DIM), k_norm_w.reshape(1, HEAD_DIM))


def _ssm_weights(A_re, A_im, log_dt, B_re, B_im, C_re, C_im, D_skip):
    dt = jnp.exp(log_dt)[:, None]
    lag = jnp.arange(CHUNK + 1, dtype=F32)[:, None, None]
    mag = jnp.exp(lag * (dt * A_re)[None])
    ang = lag * (dt * A_im)[None]
    pw_re, pw_im = mag * jnp.cos(ang), mag * jnp.sin(ang)
    ab_re, ab_im = pw_re[1], pw_im[1]
    den = A_re * A_re + A_im * A_im
    num_re, num_im = ab_re - 1.0, ab_im
    cf_re = (num_re * A_re + num_im * A_im) / den
    cf_im = (num_im * A_re - num_re * A_im) / den
    bb_re = cf_re[:, :, None] * B_re - cf_im[:, :, None] * B_im
    bb_im = cf_re[:, :, None] * B_im + cf_im[:, :, None] * B_re

    l_re, l_im = pw_re[:CHUNK, :, :, None], pw_im[:CHUNK, :, :, None]
    ab_b_re = l_re * bb_re[None] - l_im * bb_im[None]
    ab_b_im = l_re * bb_im[None] + l_im * bb_re[None]

    taps = (jnp.einsum('ghp,lgpk->lgkh', C_re, ab_b_re)
            - jnp.einsum('ghp,lgpk->lgkh', C_im, ab_b_im))
    taps = taps.at[0].add(jnp.eye(GROUP, dtype=F32)[None] * D_skip[:, None, :])
    eye_g = jnp.eye(SLAB_G, dtype=F32)
    kblk = jnp.einsum('ljakh,ab->ljakbh', taps.reshape(CHUNK, N_SLABS, SLAB_G, GROUP, GROUP),
                      eye_g).reshape(CHUNK, N_SLABS, LANES, LANES)
    kext = jnp.concatenate([jnp.zeros_like(kblk[:1]), kblk], axis=0)
    k_same, k_next, k_prev = kext[1:CHUNK + 1:2], kext[2:CHUNK + 1:2], kext[0:CHUNK:2]
    w_tiles = jnp.concatenate([jnp.concatenate([k_same, k_next], axis=-1),
                               jnp.concatenate([k_prev, k_same], axis=-1)], axis=-2)
    w_tiles = w_tiles.transpose(1, 0, 2, 3).astype(BF16)

    parity = (jnp.arange(SLAB_G)[:, None] % 2 == jnp.arange(2)[None, :]).astype(F32)

    pb = jnp.stack([ab_b_re[::-1], ab_b_im[::-1]], axis=0)
    pb = pb.reshape(2, CHUNK, N_SLABS, SLAB_G, STATE, GROUP).transpose(2, 1, 3, 5, 0, 4)
    p_w = (pb[:, :, :, :, :, None, :] * parity[None, None, :, None, None, :, None])
    p_w = p_w.reshape(N_SLABS, CHUNK_W, 2 * LANES)

    c_re, c_im = C_re[None], C_im[None]
    e_re, e_im = pw_re[1:, :, None, :], pw_im[1:, :, None, :]
    qc = jnp.stack([c_re * e_re - c_im * e_im, -(c_re * e_im + c_im * e_re)], axis=0)
    qc = qc.reshape(2, CHUNK, N_SLABS, SLAB_G, GROUP, STATE).transpose(2, 0, 5, 1, 3, 4)
    q_w = qc[:, :, None] * parity.T[None, None, :, None, None, :, None]
    q_w = q_w.reshape(N_SLABS, 2 * LANES, CHUNK_W)

    a_re = pw_re[CHUNK].reshape(N_SLABS, PAIRS, LANES)
    a_im = pw_im[CHUNK].reshape(N_SLABS, PAIRS, LANES)
    return w_tiles, p_w, q_w, a_re, a_im


def _ssm_kernel(u_ref, p_ref, w_ref, q_ref, are_ref, aim_ref, y_ref, xre_s, xim_s,
                *, batch, n_chunks):
    step = pl.program_id(0)
    rows8 = batch * PAIRS
    slab_rows = n_chunks * rows8
    n_tiles = CHUNK_W // MXU

    def state_rows(slab, b, k):
        return pl.ds(slab * slab_rows + b * PAIRS + k, n_chunks, stride=rows8)

    @pl.when(step < N_SLABS)
    def _():
        u = u_ref[...]
        p = p_ref[...]
        pair_of_row = (lax.broadcasted_iota(jnp.int32, p.shape, 0) % LANES) // (2 * GROUP)
        for k in range(PAIRS):
            x = _dot(u, jnp.where(pair_of_row == k, p, 0.0).astype(BF16))
            for b in range(batch):
                xb = x[b * n_chunks:(b + 1) * n_chunks]
                xre_s[state_rows(step, b, k), :] = xb[:, :LANES]
                xim_s[state_rows(step, b, k), :] = xb[:, LANES:]

    @pl.when(step == N_SLABS)
    def _():
        a_re = jnp.concatenate([are_ref[...]] * batch, axis=1).reshape(N_SLABS * rows8, LANES)
        a_im = jnp.concatenate([aim_ref[...]] * batch, axis=1).reshape(N_SLABS * rows8, LANES)

        def body(c, carry):
            s_re, s_im = carry
            x_re, x_im = [], []
            for sl in range(N_SLABS):
                off = pl.multiple_of(sl * slab_rows + c * rows8, rows8)
                x_re.append(xre_s[pl.ds(off, rows8), :])
                x_im.append(xim_s[pl.ds(off, rows8), :])
                xre_s[pl.ds(off, rows8), :] = s_re[sl * rows8:(sl + 1) * rows8]
                xim_s[pl.ds(off, rows8), :] = s_im[sl * rows8:(sl + 1) * rows8]
            x_re = jnp.concatenate(x_re, axis=0)
            x_im = jnp.concatenate(x_im, axis=0)
            return (a_re * s_re - a_im * s_im + x_re, a_re * s_im + a_im * s_re + x_im)

        zero = jnp.zeros((N_SLABS * rows8, LANES), F32)
        lax.fori_loop(0, n_chunks, body, (zero, zero))

    @pl.when(step > N_SLABS)
    def _():
        slab = step - (N_SLABS + 1)
        u = u_ref[...]
        q = q_ref[...]
        pair_of_col = (lax.broadcasted_iota(jnp.int32, q.shape, 1) % LANES) // (2 * GROUP)
        y_state = None
        for k in range(PAIRS):
            s_in = jnp.concatenate(
                [jnp.concatenate([xre_s[state_rows(slab, b, k), :], xim_s[state_rows(slab, b, k), :]],
                                 axis=1) for b in range(batch)], axis=0).astype(BF16)
            part = _dot(s_in, jnp.where(pair_of_col == k, q, 0.0).astype(BF16))
            y_state = part if y_state is None else y_state + part
        for t2 in range(n_tiles):
            acc = y_state[:, t2 * MXU:(t2 + 1) * MXU]
            for t1 in range(t2 + 1):
                acc = acc + _dot(u[:, t1 * MXU:(t1 + 1) * MXU], w_ref[t2 - t1])
            y_ref[:, t2 * MXU:(t2 + 1) * MXU] = jax.nn.gelu(acc).astype(BF16)


def _ssm(u_chunks, weights, batch):
    w_tiles, p_w, q_w, a_re, a_im = weights
    _, rows, _ = u_chunks.shape
    n_chunks = rows // batch
    n_steps = 2 * N_SLABS + 1
    kernel = functools.partial(_ssm_kernel, batch=batch, n_chunks=n_chunks)

    def slab_in(s):
        return jnp.where(s <= N_SLABS, jnp.minimum(s, N_SLABS - 1), s - (N_SLABS + 1))

    def slab_a(s):
        return jnp.minimum(s, N_SLABS - 1)

    def slab_c(s):
        return jnp.maximum(s - (N_SLABS + 1), 0)

    return pl.pallas_call(
        kernel,
        grid=(n_steps,),
        in_specs=[
            pl.BlockSpec((None, rows, CHUNK_W), lambda s: (slab_in(s), 0, 0)),
            pl.BlockSpec((None, CHUNK_W, 2 * LANES), lambda s: (slab_a(s), 0, 0)),
            pl.BlockSpec((None, CHUNK_W // MXU, MXU, MXU), lambda s: (slab_c(s), 0, 0, 0)),
            pl.BlockSpec((None, 2 * LANES, CHUNK_W), lambda s: (slab_c(s), 0, 0)),
            pl.BlockSpec((N_SLABS, PAIRS, LANES), lambda s: (0, 0, 0)),
            pl.BlockSpec((N_SLABS, PAIRS, LANES), lambda s: (0, 0, 0)),
        ],
        out_specs=pl.BlockSpec((None, rows, CHUNK_W), lambda s: (slab_c(s), 0, 0)),
        out_shape=jax.ShapeDtypeStruct(u_chunks.shape, BF16),
        scratch_shapes=[pltpu.VMEM((N_SLABS * n_chunks * batch * PAIRS, LANES), F32)
                        for _ in range(2)],
        compiler_params=pltpu.CompilerParams(
            dimension_semantics=("arbitrary",), vmem_limit_bytes=VMEM_LIMIT),
        name="ssm",
    )(u_chunks, p_w, w_tiles, q_w, a_re, a_im)


def _tail_kernel(ag_ref, yc_ref, z_ref, ga_ref, gs_ref, x_ref, wa_ref, wg_ref, bg_ref, ws_ref,
                 wo_ref, o_ref, ys_ref, *, tm):
    rows = tm // CHUNK
    for s in range(N_SLABS):
        for t in range(CHUNK):
            ys_ref[pl.ds(s * tm + t, rows, stride=CHUNK), :] = (
                yc_ref[s, :, t * LANES:(t + 1) * LANES].astype(F32))
    y_g = jnp.concatenate([ys_ref[s * tm:(s + 1) * tm, :] for s in range(N_SLABS)],
                          axis=1).astype(BF16)

    y_a = _dot(ag_ref[...], wa_ref[...])
    glu = _dot(y_g, wg_ref[...]) + bg_ref[...]
    z = z_ref[...].astype(F32)
    t = glu[:, :SSM_W] * jax.nn.sigmoid(glu[:, SSM_W:]) * (z * jax.nn.sigmoid(z))
    y_s = _dot(t.astype(BF16), ws_ref[...])
    merged = (jax.nn.sigmoid(ga_ref[...].astype(F32)) * y_a
              + jax.nn.sigmoid(gs_ref[...].astype(F32)) * y_s)
    o_ref[...] = x_ref[...] + _dot(merged.astype(BF16), wo_ref[...])


def _tail(attn_g, y_chunks, proj, x2, w_attn, w_glu, b_glu, w_ssm, w_out, tm=256):
    m = x2.shape[0]
    row = lambda blk: (lambda i: (i, blk))
    const = lambda i: (0, 0)
    resident = functools.partial(pl.BlockSpec, index_map=const, pipeline_mode=pl.Buffered(1))
    kernel = functools.partial(_tail_kernel, tm=tm)
    return pl.pallas_call(
        kernel,
        grid=(m // tm,),
        in_specs=[
            pl.BlockSpec((tm, ATTN_W), row(0)),
            pl.BlockSpec((N_SLABS, tm // CHUNK, CHUNK_W), lambda i: (0, i, 0)),
            pl.BlockSpec((tm, SSM_W), row(COL_Z // SSM_W)),
            pl.BlockSpec((tm, D_MODEL), row(COL_GA // D_MODEL)),
            pl.BlockSpec((tm, D_MODEL), row(COL_GS // D_MODEL)),
            pl.BlockSpec((tm, D_MODEL), row(0)),
            resident((ATTN_W, D_MODEL)),
            resident((SSM_W, 2 * SSM_W)),
            resident((1, 2 * SSM_W)),
            resident((SSM_W, D_MODEL)),
            resident((D_MODEL, D_MODEL)),
        ],
        out_specs=pl.BlockSpec((tm, D_MODEL), row(0)),
        out_shape=jax.ShapeDtypeStruct((m, D_MODEL), F32),
        scratch_shapes=[pltpu.VMEM((N_SLABS * tm, LANES), F32)],
        compiler_params=pltpu.CompilerParams(
            dimension_semantics=("arbitrary",), vmem_limit_bytes=VMEM_LIMIT),
        name="tail",
    )(attn_g, y_chunks, proj, proj, proj, x2, w_attn, w_glu, b_glu.reshape(1, -1), w_ssm, w_out)


def kernel(x, norm_w, w_in, q_norm_w, k_norm_w, sinks, w_attn_proj, A_re, A_im, log_dt, B_re, B_im,
           C_re, C_im, D_skip, w_glu, b_glu, w_ssm_proj, w_out):
    batch, seq, _ = x.shape
    m = batch * seq
    x2 = x.reshape(m, D_MODEL)

    w_in_bf = jnp.concatenate(
        [w_in[:, :ATTN_W], w_in[:, ATTN_W + 2 * KV_W:], w_in[:, ATTN_W:ATTN_W + 2 * KV_W]],
        axis=1).astype(BF16)

    proj, u_chunks = _in_proj(x2, norm_w, w_in_bf)

    attn_g = _swa(proj, sinks, q_norm_w, k_norm_w, batch, seq)

    y_chunks = _ssm(u_chunks, _ssm_weights(A_re, A_im, log_dt, B_re, B_im, C_re, C_im, D_skip),
                    batch)

    out = _tail(attn_g, y_chunks, proj, x2, w_attn_proj.astype(BF16), w_glu.astype(BF16), b_glu,
                w_ssm_proj.astype(BF16), w_out.astype(BF16))
    return out.reshape(batch, seq, D_MODEL)
```

```python
import functools
import math

import jax
import jax.numpy as jnp
from jax import lax
from jax.experimental import pallas as pl
from jax.experimental.pallas import tpu as pltpu

F32 = jnp.float32
BF16 = jnp.bfloat16

D_MODEL = 2048
HEAD_DIM = 64
N_Q_HEADS = 16
N_KV_HEADS = 4
Q_PER_KV = 4
ATTN_W = N_Q_HEADS * HEAD_DIM
KV_W = N_KV_HEADS * HEAD_DIM
WINDOW = 128
SSM_W = D_MODEL // 2
GROUP = 16
N_GROUPS = SSM_W // GROUP
STATE = 64
NORM_EPS = 1e-6

COL_Q, COL_GATE, COL_U, COL_Z = 0, 1024, 2048, 3072
COL_GA, COL_GS, COL_K, COL_V = 4096, 6144, 8192, 8448
IN_W = 8704

LANES = 128
MXU = 256
CHUNK = 16
SLAB_G = LANES // GROUP
N_SLABS = N_GROUPS // SLAB_G
PAIRS = SLAB_G // 2
CHUNK_W = CHUNK * LANES
N_TILES = CHUNK_W // MXU
VMEM_LIMIT = 48 * 1024 * 1024


def _dot(a, b):
    return jnp.dot(a, b, preferred_element_type=F32)


def _in_proj_kernel(x_ref, nw_ref, w_ref, o_ref, uc_ref, h_ref, us_ref, *, tm, tn, ju0):
    j = pl.program_id(1)

    @pl.when(j == 0)
    def _():
        x = x_ref[...]
        ms = jnp.mean(x * x, axis=-1, keepdims=True)
        h_ref[...] = (x * lax.rsqrt(ms + NORM_EPS) * nw_ref[...]).astype(BF16)

    acc = _dot(h_ref[...], w_ref[...])
    o_ref[...] = acc.astype(BF16)

    n_sl = tn // LANES
    rows = tm // CHUNK

    @pl.when((j >= ju0) & (j < ju0 + SSM_W // tn))
    def _():
        for s in range(n_sl):
            us_ref[s * tm:(s + 1) * tm, :] = acc[:, s * LANES:(s + 1) * LANES]
        for s in range(n_sl):
            for t in range(CHUNK):
                piece = us_ref[pl.ds(s * tm + t, rows, stride=CHUNK), :]
                uc_ref[s, :, t * LANES:(t + 1) * LANES] = piece.astype(BF16)


def _in_proj(x2, norm_w, w_in_bf, tm=1024, tn=512):
    m = x2.shape[0]
    ju0 = COL_U // tn
    n_sl = tn // LANES
    kernel = functools.partial(_in_proj_kernel, tm=tm, tn=tn, ju0=ju0)
    return pl.pallas_call(
        kernel,
        grid=(m // tm, IN_W // tn),
        in_specs=[
            pl.BlockSpec((tm, D_MODEL), lambda i, j: (i, 0)),
            pl.BlockSpec((1, D_MODEL), lambda i, j: (0, 0)),
            pl.BlockSpec((D_MODEL, tn), lambda i, j: (0, j)),
        ],
        out_specs=[
            pl.BlockSpec((tm, tn), lambda i, j: (i, j)),
            pl.BlockSpec((n_sl, tm // CHUNK, CHUNK_W),
                         lambda i, j: (jnp.clip(j - ju0, 0, SSM_W // tn - 1), i, 0)),
        ],
        out_shape=[
            jax.ShapeDtypeStruct((m, IN_W), BF16),
            jax.ShapeDtypeStruct((N_SLABS, m // CHUNK, CHUNK_W), BF16),
        ],
        scratch_shapes=[pltpu.VMEM((tm, D_MODEL), BF16), pltpu.VMEM((n_sl * tm, LANES), F32)],
        compiler_params=pltpu.CompilerParams(
            dimension_semantics=("arbitrary", "arbitrary"), vmem_limit_bytes=VMEM_LIMIT),
        name="in_proj",
    )(x2, norm_w.reshape(1, D_MODEL), w_in_bf)


def _head_norm(t, w):
    ms = jnp.mean(t * t, axis=-1, keepdims=True)
    return t * lax.rsqrt(ms + NORM_EPS) * w


def _swa_kernel(sink_ref, q_ref, g_ref, kc_ref, vc_ref, kp_ref, vp_ref, qw_ref, kw_ref, o_ref,
                *, n_sub):
    qw = qw_ref[...] * (1.0 / math.sqrt(HEAD_DIM))
    kw = kw_ref[...]

    row = lax.broadcasted_iota(jnp.int32, (WINDOW, 2 * WINDOW), 0)
    col = lax.broadcasted_iota(jnp.int32, (WINDOW, 2 * WINDOW), 1)
    rel = col - row
    band = (rel >= 1) & (rel <= WINDOW)
    first_key = jnp.where(pl.program_id(1) == 0, WINDOW, 0)
    band_first = band & (col >= first_key)

    for g in range(N_KV_HEADS):
        kcol = slice(g * HEAD_DIM, (g + 1) * HEAD_DIM)
        k_prev = _head_norm(kp_ref[:, kcol].astype(F32), kw).astype(BF16)
        v_prev = vp_ref[:, kcol]
        for n in range(n_sub):
            rows = slice(n * WINDOW, (n + 1) * WINDOW)
            k_cur = _head_norm(kc_ref[rows, kcol].astype(F32), kw).astype(BF16)
            v_cur = vc_ref[rows, kcol]
            k_ctx = jnp.concatenate([k_prev, k_cur], axis=0)
            v_ctx = jnp.concatenate([v_prev, v_cur], axis=0)
            mask = band_first if n == 0 else band

            qg = q_ref[rows, g * Q_PER_KV * HEAD_DIM:(g + 1) * Q_PER_KV * HEAD_DIM].astype(F32)
            q_heads = [
                _head_norm(qg[:, r * HEAD_DIM:(r + 1) * HEAD_DIM], qw).astype(BF16)
                for r in range(Q_PER_KV)
            ]
            qs = jnp.concatenate(q_heads, axis=0)
            s = lax.dot_general(qs, k_ctx, (((1,), (1,)), ((), ())),
                                preferred_element_type=F32)
            probs, inv = [], []
            for r in range(Q_PER_KV):
                sink = sink_ref[g * Q_PER_KV + r]
                sr = jnp.where(mask, s[r * WINDOW:(r + 1) * WINDOW], -1e30)
                mx = jnp.maximum(jnp.max(sr, axis=-1, keepdims=True), sink)
                p = jnp.exp(sr - mx)
                den = jnp.sum(p, axis=-1, keepdims=True) + jnp.exp(sink - mx)
                probs.append(p.astype(BF16))
                inv.append(1.0 / den)
            o = _dot(jnp.concatenate(probs, axis=0), v_ctx)
            outs = [o[r * WINDOW:(r + 1) * WINDOW] * inv[r] for r in range(Q_PER_KV)]
            og = jnp.concatenate(outs, axis=1)
            gcols = slice(g * Q_PER_KV * HEAD_DIM, (g + 1) * Q_PER_KV * HEAD_DIM)
            gate = g_ref[rows, gcols].astype(F32)
            o_ref[rows, gcols] = (og * (gate * jax.nn.sigmoid(gate))).astype(BF16)
            k_prev, v_prev = k_cur, v_cur


def _swa(proj, sinks, q_norm_w, k_norm_w, batch, seq, tq=512):
    n_sub = tq // WINDOW
    tiles = seq // tq
    m = batch * seq

    def cur_rows(b, i):
        return b * tiles + i

    def prev_rows(b, i):
        return jnp.maximum((b * tiles + i) * n_sub - 1, 0)

    kernel = functools.partial(_swa_kernel, n_sub=n_sub)
    return pl.pallas_call(
        kernel,
        grid=(batch, tiles),
        in_specs=[
            pl.BlockSpec(memory_space=pltpu.SMEM),
            pl.BlockSpec((tq, ATTN_W), lambda b, i: (cur_rows(b, i), COL_Q // ATTN_W)),
            pl.BlockSpec((tq, ATTN_W), lambda b, i: (cur_rows(b, i), COL_GATE // ATTN_W)),
            pl.BlockSpec((tq, KV_W), lambda b, i: (cur_rows(b, i), COL_K // KV_W)),
            pl.BlockSpec((tq, KV_W), lambda b, i: (cur_rows(b, i), COL_V // KV_W)),
            pl.BlockSpec((WINDOW, KV_W), lambda b, i: (prev_rows(b, i), COL_K // KV_W)),
            pl.BlockSpec((WINDOW, KV_W), lambda b, i: (prev_rows(b, i), COL_V // KV_W)),
            pl.BlockSpec((1, HEAD_DIM), lambda b, i: (0, 0)),
            pl.BlockSpec((1, HEAD_DIM), lambda b, i: (0, 0)),
        ],
        out_specs=pl.BlockSpec((tq, ATTN_W), lambda b, i: (cur_rows(b, i), 0)),
        out_shape=jax.ShapeDtypeStruct((m, ATTN_W), BF16),
        compiler_params=pltpu.CompilerParams(
            dimension_semantics=("arbitrary", "arbitrary"), vmem_limit_bytes=VMEM_LIMIT),
        name="swa",
    )(sinks, proj, proj, proj, proj, proj, proj,
      q_norm_w.reshape(1, HEAD_DIM), k_norm_w.reshape(1, HEAD_DIM))


def _slab_powers(ar, ai, dt, btr, bti):
    dta_re, dta_im = dt * ar, dt * ai
    mag = jnp.exp(dta_re)
    ab_re, ab_im = mag * jnp.cos(dta_im), mag * jnp.sin(dta_im)
    pw = [(jnp.ones_like(ar), jnp.zeros_like(ar))]
    for _ in range(CHUNK):
        pr, pi = pw[-1]
        pw.append((pr * ab_re - pi * ab_im, pr * ab_im + pi * ab_re))
    den = ar * ar + ai * ai
    num_re, num_im = ab_re - 1.0, ab_im
    cf_re = (num_re * ar + num_im * ai) / den
    cf_im = (num_im * ar - num_re * ai) / den
    bb_re = cf_re * btr - cf_im * bti
    bb_im = cf_re * bti + cf_im * btr
    abb = [(pr * bb_re - pi * bb_im, pr * bb_im + pi * bb_re) for pr, pi in pw[:CHUNK]]
    return pw, abb


def _build_state_in(abb, p_s):
    row_g = lax.broadcasted_iota(jnp.int32, (LANES, STATE), 0) // GROUP
    even = row_g % 2 == 0
    pair_of_row = lax.broadcasted_iota(jnp.int32, (LANES, 2 * LANES), 0) // (2 * GROUP)
    for lag in range(CHUNK):
        t = CHUNK - 1 - lag
        re, im = abb[lag]
        blk = jnp.concatenate([jnp.where(even, re, 0.0), jnp.where(even, 0.0, re),
                               jnp.where(even, im, 0.0), jnp.where(even, 0.0, im)], axis=1)
        for k in range(PAIRS):
            p_s[k, t * LANES:(t + 1) * LANES, :] = jnp.where(pair_of_row == k, blk, 0.0).astype(BF16)


def _build_state_out(pw, cr, ci, q_s):
    lane_g = lax.broadcasted_iota(jnp.int32, (STATE, LANES), 1) // GROUP
    for t in range(CHUNK):
        pr, pi = pw[t + 1]
        cat = jnp.concatenate([cr * pr - ci * pi, -(cr * pi + ci * pr)], axis=1)
        cat_t = cat.T
        for k in range(PAIRS):
            for ri in range(2):
                for half in range(2):
                    piece = jnp.where(lane_g == 2 * k + half, cat_t[ri * STATE:(ri + 1) * STATE], 0.0)
                    r0 = ri * LANES + half * STATE
                    q_s[k, r0:r0 + STATE, t * LANES:(t + 1) * LANES] = piece.astype(BF16)


def _build_toeplitz(abb, cr, ci, d_row, w_s):
    rhs = jnp.concatenate([cr, -ci], axis=1)
    row = lax.broadcasted_iota(jnp.int32, (LANES, LANES), 0)
    col = lax.broadcasted_iota(jnp.int32, (LANES, LANES), 1)
    same_group = (row // GROUP) == (col // GROUP)
    taps = []
    for lag in range(CHUNK):
        lhs = jnp.concatenate(abb[lag], axis=1)
        k = lax.dot_general(lhs, rhs, (((1,), (1,)), ((), ())), precision=lax.Precision.HIGHEST,
                            preferred_element_type=F32)
        k = jnp.where(same_group, k, 0.0)
        if lag == 0:
            k = k + jnp.where(row == col, d_row, 0.0)
        taps.append(k.astype(BF16))
    zero = jnp.zeros((LANES, LANES), BF16)
    for d in range(N_TILES):
        w_s[d, :LANES, :LANES] = taps[2 * d]
        w_s[d, :LANES, LANES:] = taps[2 * d + 1]
        w_s[d, LANES:, :LANES] = taps[2 * d - 1] if d > 0 else zero
        w_s[d, LANES:, LANES:] = taps[2 * d]


def _ssm_kernel(u_ref, ar_ref, ai_ref, dt_ref, btr_ref, bti_ref, cr_ref, ci_ref, d_ref,
                dtare_ref, dtaim_ref, y_ref, xre_s, xim_s, p_s, q_s, w_s, *, batch, n_chunks):
    step = pl.program_id(0)
    rows8 = batch * PAIRS
    slab_rows = n_chunks * rows8

    def state_rows(slab, b, k):
        return pl.ds(slab * slab_rows + b * PAIRS + k, n_chunks, stride=rows8)

    def powers():
        return _slab_powers(ar_ref[...], ai_ref[...], dt_ref[...], btr_ref[...], bti_ref[...])

    @pl.when(step < N_SLABS)
    def _():
        _, abb = powers()
        _build_state_in(abb, p_s)
        u = u_ref[...]
        for k in range(PAIRS):
            x = _dot(u, p_s[k])
            for b in range(batch):
                xb = x[b * n_chunks:(b + 1) * n_chunks]
                xre_s[state_rows(step, b, k), :] = xb[:, :LANES]
                xim_s[state_rows(step, b, k), :] = xb[:, LANES:]

    @pl.when(step == N_SLABS)
    def _():
        dta_re = jnp.concatenate([dtare_ref[...]] * batch, axis=1).reshape(N_SLABS * rows8, LANES)
        dta_im = jnp.concatenate([dtaim_ref[...]] * batch, axis=1).reshape(N_SLABS * rows8, LANES)
        mag = jnp.exp(CHUNK * dta_re)
        a_re, a_im = mag * jnp.cos(CHUNK * dta_im), mag * jnp.sin(CHUNK * dta_im)

        def body(c, carry):
            s_re, s_im = carry
            x_re, x_im = [], []
            for sl in range(N_SLABS):
                off = pl.multiple_of(sl * slab_rows + c * rows8, rows8)
                x_re.append(xre_s[pl.ds(off, rows8), :])
                x_im.append(xim_s[pl.ds(off, rows8), :])
                xre_s[pl.ds(off, rows8), :] = s_re[sl * rows8:(sl + 1) * rows8]
                xim_s[pl.ds(off, rows8), :] = s_im[sl * rows8:(sl + 1) * rows8]
            x_re = jnp.concatenate(x_re, axis=0)
            x_im = jnp.concatenate(x_im, axis=0)
            return (a_re * s_re - a_im * s_im + x_re, a_re * s_im + a_im * s_re + x_im)

        zero = jnp.zeros((N_SLABS * rows8, LANES), F32)
        lax.fori_loop(0, n_chunks, body, (zero, zero))

    @pl.when(step > N_SLABS)
    def _():
        slab = step - (N_SLABS + 1)
        pw, abb = powers()
        cr, ci = cr_ref[...], ci_ref[...]
        _build_state_out(pw, cr, ci, q_s)
        _build_toeplitz(abb, cr, ci, d_ref[...], w_s)
        u = u_ref[...]
        y_state = None
        for k in range(PAIRS):
            s_in = jnp.concatenate(
                [jnp.concatenate([xre_s[state_rows(slab, b, k), :], xim_s[state_rows(slab, b, k), :]],
                                 axis=1) for b in range(batch)], axis=0).astype(BF16)
            part = _dot(s_in, q_s[k])
            y_state = part if y_state is None else y_state + part
        for t2 in range(N_TILES):
            acc = y_state[:, t2 * MXU:(t2 + 1) * MXU]
            for t1 in range(t2 + 1):
                acc = acc + _dot(u[:, t1 * MXU:(t1 + 1) * MXU], w_s[t2 - t1])
            y_ref[:, t2 * MXU:(t2 + 1) * MXU] = jax.nn.gelu(acc).astype(BF16)


def _ssm(u_chunks, A_re, A_im, log_dt, B_re, B_im, C_re, C_im, D_skip, batch):
    _, rows, _ = u_chunks.shape
    n_chunks = rows // batch
    n_steps = 2 * N_SLABS + 1
    kernel = functools.partial(_ssm_kernel, batch=batch, n_chunks=n_chunks)

    dt = jnp.exp(log_dt)[:, None]
    per_row = lambda v: jnp.repeat(v, GROUP, axis=0)
    ar, ai = per_row(A_re), per_row(A_im)
    dtr = per_row(jnp.broadcast_to(dt, A_re.shape))
    btr = B_re.transpose(0, 2, 1).reshape(SSM_W, STATE)
    bti = B_im.transpose(0, 2, 1).reshape(SSM_W, STATE)
    cr, ci = C_re.reshape(SSM_W, STATE), C_im.reshape(SSM_W, STATE)
    d_rows = D_skip.reshape(N_SLABS, 1, LANES)
    dta_re = (dt * A_re).reshape(N_SLABS, PAIRS, LANES)
    dta_im = (dt * A_im).reshape(N_SLABS, PAIRS, LANES)

    def slab_in(s):
        return jnp.where(s <= N_SLABS, jnp.minimum(s, N_SLABS - 1), s - (N_SLABS + 1))

    def slab_out(s):
        return jnp.maximum(s - (N_SLABS + 1), 0)

    coeff = pl.BlockSpec((LANES, STATE), lambda s: (slab_in(s), 0))
    whole = pl.BlockSpec((N_SLABS, PAIRS, LANES), lambda s: (0, 0, 0))
    return pl.pallas_call(
        kernel,
        grid=(n_steps,),
        in_specs=[
            pl.BlockSpec((None, rows, CHUNK_W), lambda s: (slab_in(s), 0, 0)),
            coeff, coeff, coeff, coeff, coeff, coeff, coeff,
            pl.BlockSpec((None, 1, LANES), lambda s: (slab_in(s), 0, 0)),
            whole, whole,
        ],
        out_specs=pl.BlockSpec((None, rows, CHUNK_W), lambda s: (slab_out(s), 0, 0)),
        out_shape=jax.ShapeDtypeStruct(u_chunks.shape, BF16),
        scratch_shapes=[
            pltpu.VMEM((N_SLABS * n_chunks * batch * PAIRS, LANES), F32),
            pltpu.VMEM((N_SLABS * n_chunks * batch * PAIRS, LANES), F32),
            pltpu.VMEM((PAIRS, CHUNK_W, 2 * LANES), BF16),
            pltpu.VMEM((PAIRS, 2 * LANES, CHUNK_W), BF16),
            pltpu.VMEM((N_TILES, MXU, MXU), BF16),
        ],
        compiler_params=pltpu.CompilerParams(
            dimension_semantics=("arbitrary",), vmem_limit_bytes=VMEM_LIMIT),
        name="ssm",
    )(u_chunks, ar, ai, dtr, btr, bti, cr, ci, d_rows, dta_re, dta_im)


def _tail_kernel(ag_ref, yc_ref, z_ref, ga_ref, gs_ref, x_ref, wa_ref, wg_ref, bg_ref, ws_ref,
                 wo_ref, o_ref, ys_ref, *, tm):
    rows = tm // CHUNK
    for s in range(N_SLABS):
        for t in range(CHUNK):
            ys_ref[pl.ds(s * tm + t, rows, stride=CHUNK), :] = (
                yc_ref[s, :, t * LANES:(t + 1) * LANES].astype(F32))
    y_g = jnp.concatenate([ys_ref[s * tm:(s + 1) * tm, :] for s in range(N_SLABS)],
                          axis=1).astype(BF16)

    y_a = _dot(ag_ref[...], wa_ref[...])
    glu = _dot(y_g, wg_ref[...]) + bg_ref[...]
    z = z_ref[...].astype(F32)
    t = glu[:, :SSM_W] * jax.nn.sigmoid(glu[:, SSM_W:]) * (z * jax.nn.sigmoid(z))
    y_s = _dot(t.astype(BF16), ws_ref[...])
    merged = (jax.nn.sigmoid(ga_ref[...].astype(F32)) * y_a
              + jax.nn.sigmoid(gs_ref[...].astype(F32)) * y_s)
    o_ref[...] = x_ref[...] + _dot(merged.astype(BF16), wo_ref[...])


def _tail(attn_g, y_chunks, proj, x2, w_attn, w_glu, b_glu, w_ssm, w_out, tm=256):
    m = x2.shape[0]
    row = lambda blk: (lambda i: (i, blk))
    const = lambda i: (0, 0)
    resident = functools.partial(pl.BlockSpec, index_map=const, pipeline_mode=pl.Buffered(1))
    kernel = functools.partial(_tail_kernel, tm=tm)
    return pl.pallas_call(
        kernel,
        grid=(m // tm,),
        in_specs=[
            pl.BlockSpec((tm, ATTN_W), row(0)),
            pl.BlockSpec((N_SLABS, tm // CHUNK, CHUNK_W), lambda i: (0, i, 0)),
            pl.BlockSpec((tm, SSM_W), row(COL_Z // SSM_W)),
            pl.BlockSpec((tm, D_MODEL), row(COL_GA // D_MODEL)),
            pl.BlockSpec((tm, D_MODEL), row(COL_GS // D_MODEL)),
            pl.BlockSpec((tm, D_MODEL), row(0)),
            resident((ATTN_W, D_MODEL)),
            resident((SSM_W, 2 * SSM_W)),
            resident((1, 2 * SSM_W)),
            resident((SSM_W, D_MODEL)),
            resident((D_MODEL, D_MODEL)),
        ],
        out_specs=pl.BlockSpec((tm, D_MODEL), row(0)),
        out_shape=jax.ShapeDtypeStruct((m, D_MODEL), F32),
        scratch_shapes=[pltpu.VMEM((N_SLABS * tm, LANES), F32)],
        compiler_params=pltpu.CompilerParams(
            dimension_semantics=("arbitrary",), vmem_limit_bytes=VMEM_LIMIT),
        name="tail",
    )(attn_g, y_chunks, proj, proj, proj, x2, w_attn, w_glu, b_glu.reshape(1, -1), w_ssm, w_out)


def kernel(x, norm_w, w_in, q_norm_w, k_norm_w, sinks, w_attn_proj, A_re, A_im, log_dt, B_re, B_im,
           C_re, C_im, D_skip, w_glu, b_glu, w_ssm_proj, w_out):
    batch, seq, _ = x.shape
    m = batch * seq
    x2 = x.reshape(m, D_MODEL)

    w_in_bf = jnp.concatenate(
        [w_in[:, :ATTN_W], w_in[:, ATTN_W + 2 * KV_W:], w_in[:, ATTN_W:ATTN_W + 2 * KV_W]],
        axis=1).astype(BF16)

    proj, u_chunks = _in_proj(x2, norm_w, w_in_bf)

    attn_g = _swa(proj, sinks, q_norm_w, k_norm_w, batch, seq)

    y_chunks = _ssm(u_chunks, A_re, A_im, log_dt, B_re, B_im, C_re, C_im, D_skip, batch)

    out = _tail(attn_g, y_chunks, proj, x2, w_attn_proj.astype(BF16), w_glu.astype(BF16), b_glu,
                w_ssm_proj.astype(BF16), w_out.astype(BF16))
    return out.reshape(batch, seq, D_MODEL)
```

```python
import functools
import math

import jax
import jax.numpy as jnp
from jax import lax
from jax.experimental import pallas as pl
from jax.experimental.pallas import tpu as pltpu

F32 = jnp.float32
BF16 = jnp.bfloat16

D_MODEL = 2048
HEAD_DIM = 64
N_Q_HEADS = 16
N_KV_HEADS = 4
Q_PER_KV = 4
ATTN_W = N_Q_HEADS * HEAD_DIM
KV_W = N_KV_HEADS * HEAD_DIM
WINDOW = 128
SSM_W = D_MODEL // 2
GROUP = 16
N_GROUPS = SSM_W // GROUP
STATE = 64
NORM_EPS = 1e-6

COL_Q, COL_GATE, COL_U, COL_Z = 0, 1024, 2048, 3072
COL_GA, COL_GS, COL_K, COL_V = 4096, 6144, 8192, 8448
IN_W = 8704

LANES = 128
MXU = 256
CHUNK = 16
SLAB_G = LANES // GROUP
N_SLABS = N_GROUPS // SLAB_G
PAIRS = SLAB_G // 2
CHUNK_W = CHUNK * LANES
N_TILES = CHUNK_W // MXU
VMEM_LIMIT = 48 * 1024 * 1024


def _dot(a, b):
    return jnp.dot(a, b, preferred_element_type=F32)


def _in_proj_kernel(x_ref, nw_ref, w_ref, o_ref, uc_ref, h_ref, us_ref, *, tm, tn, ju0):
    j = pl.program_id(1)

    @pl.when(j == 0)
    def _():
        x = x_ref[...]
        ms = jnp.mean(x * x, axis=-1, keepdims=True)
        h_ref[...] = (x * lax.rsqrt(ms + NORM_EPS) * nw_ref[...]).astype(BF16)

    acc = _dot(h_ref[...], w_ref[...])
    o_ref[...] = acc.astype(BF16)

    n_sl = tn // LANES
    rows = tm // CHUNK

    @pl.when((j >= ju0) & (j < ju0 + SSM_W // tn))
    def _():
        for s in range(n_sl):
            us_ref[s * tm:(s + 1) * tm, :] = acc[:, s * LANES:(s + 1) * LANES]
        for s in range(n_sl):
            for t in range(CHUNK):
                piece = us_ref[pl.ds(s * tm + t, rows, stride=CHUNK), :]
                uc_ref[s, :, t * LANES:(t + 1) * LANES] = piece.astype(BF16)


def _in_proj(x2, norm_w, w_in_bf, tm=1024, tn=512):
    m = x2.shape[0]
    ju0 = COL_U // tn
    n_sl = tn // LANES
    kernel = functools.partial(_in_proj_kernel, tm=tm, tn=tn, ju0=ju0)
    return pl.pallas_call(
        kernel,
        grid=(m // tm, IN_W // tn),
        in_specs=[
            pl.BlockSpec((tm, D_MODEL), lambda i, j: (i, 0)),
            pl.BlockSpec((1, D_MODEL), lambda i, j: (0, 0)),
            pl.BlockSpec((D_MODEL, tn), lambda i, j: (0, j)),
        ],
        out_specs=[
            pl.BlockSpec((tm, tn), lambda i, j: (i, j)),
            pl.BlockSpec((n_sl, tm // CHUNK, CHUNK_W),
                         lambda i, j: (jnp.clip(j - ju0, 0, SSM_W // tn - 1), i, 0)),
        ],
        out_shape=[
            jax.ShapeDtypeStruct((m, IN_W), BF16),
            jax.ShapeDtypeStruct((N_SLABS, m // CHUNK, CHUNK_W), BF16),
        ],
        scratch_shapes=[pltpu.VMEM((tm, D_MODEL), BF16), pltpu.VMEM((n_sl * tm, LANES), F32)],
        compiler_params=pltpu.CompilerParams(
            dimension_semantics=("arbitrary", "arbitrary"), vmem_limit_bytes=VMEM_LIMIT),
        name="in_proj",
    )(x2, norm_w.reshape(1, D_MODEL), w_in_bf)


def _head_norm(t, w):
    ms = jnp.mean(t * t, axis=-1, keepdims=True)
    return t * lax.rsqrt(ms + NORM_EPS) * w


_NT = (((1,), (1,)), ((), ()))
_TN = (((0,), (0,)), ((), ()))


def _swa_kernel(sink_ref, q_ref, g_ref, kc_ref, vc_ref, kp_ref, vp_ref, qw_ref, kw_ref, o_ref,
                *, n_sub):
    kqw = kw_ref[...] * qw_ref[...] * (1.0 / math.sqrt(HEAD_DIM))
    n_col = Q_PER_KV * WINDOW
    key = lax.broadcasted_iota(jnp.int32, (WINDOW, n_col), 0)
    qry = lax.broadcasted_iota(jnp.int32, (WINDOW, n_col), 1) % WINDOW
    from_prev = key > qry
    no_prev = jnp.where(pl.program_id(1) == 0, -1e30, 0.0)
    head_of_col = lax.broadcasted_iota(jnp.int32, (1, n_col), 1) // WINDOW
    ones = jnp.ones((8, 2 * HEAD_DIM), BF16)

    for g in range(N_KV_HEADS):
        kcol = slice(g * HEAD_DIM, (g + 1) * HEAD_DIM)
        gcols = slice(g * Q_PER_KV * HEAD_DIM, (g + 1) * Q_PER_KV * HEAD_DIM)
        sink = jnp.zeros((1, n_col), F32)
        for r in range(Q_PER_KV):
            sink = jnp.where(head_of_col == r, sink_ref[g * Q_PER_KV + r], sink)
        k_prev = _head_norm(kp_ref[:, kcol].astype(F32), kqw).astype(BF16)
        v_prev = vp_ref[:, kcol]
        for n in range(n_sub):
            rows = slice(n * WINDOW, (n + 1) * WINDOW)
            k_cur = _head_norm(kc_ref[rows, kcol].astype(F32), kqw).astype(BF16)
            v_cur = vc_ref[rows, kcol]
            k_ctx = jnp.concatenate([k_prev, k_cur], axis=0)
            v_ctx = jnp.concatenate([v_prev, v_cur], axis=0)

            qg = q_ref[rows, gcols]
            qs = jnp.concatenate([qg[:, r * HEAD_DIM:(r + 1) * HEAD_DIM] for r in range(Q_PER_KV)],
                                 axis=0)
            qf = qs.astype(F32)
            q2 = qf * qf
            q2_hi = q2.astype(BF16)
            q2_lo = (q2 - q2_hi.astype(F32)).astype(BF16)
            ssq = lax.dot_general(ones, jnp.concatenate([q2_hi, q2_lo], axis=1), _NT,
                                  preferred_element_type=F32)[:1]
            rms_q = lax.rsqrt(ssq * (1.0 / HEAD_DIM) + NORM_EPS)

            s = lax.dot_general(k_ctx, qs, _NT, preferred_element_type=F32)
            s_prev = s[:WINDOW] + no_prev if n == 0 else s[:WINDOW]
            s = jnp.where(from_prev, s_prev, s[WINDOW:]) * rms_q
            mx = jnp.maximum(jnp.max(s, axis=0, keepdims=True), sink)
            p = jnp.exp(s - mx)
            den = jnp.sum(p, axis=0, keepdims=True) + jnp.exp(sink - mx)
            p_ctx = jnp.concatenate([jnp.where(from_prev, p, 0.0), jnp.where(from_prev, 0.0, p)],
                                    axis=0).astype(BF16)
            o_t = lax.dot_general(v_ctx, p_ctx, _TN, preferred_element_type=F32) * (1.0 / den)
            halves = [jnp.concatenate([o_t[:, (2 * h) * WINDOW:(2 * h + 1) * WINDOW],
                                       o_t[:, (2 * h + 1) * WINDOW:(2 * h + 2) * WINDOW]], axis=0).T
                      for h in range(Q_PER_KV // 2)]
            og = jnp.concatenate(halves, axis=1)
            gate = g_ref[rows, gcols].astype(F32)
            o_ref[rows, gcols] = (og * (gate * jax.nn.sigmoid(gate))).astype(BF16)
            k_prev, v_prev = k_cur, v_cur


def _swa(proj, sinks, q_norm_w, k_norm_w, batch, seq, tq=512):
    n_sub = tq // WINDOW
    tiles = seq // tq
    m = batch * seq

    def cur_rows(b, i):
        return b * tiles + i

    def prev_rows(b, i):
        return jnp.maximum((b * tiles + i) * n_sub - 1, 0)

    kernel = functools.partial(_swa_kernel, n_sub=n_sub)
    return pl.pallas_call(
        kernel,
        grid=(batch, tiles),
        in_specs=[
            pl.BlockSpec(memory_space=pltpu.SMEM),
            pl.BlockSpec((tq, ATTN_W), lambda b, i: (cur_rows(b, i), COL_Q // ATTN_W)),
            pl.BlockSpec((tq, ATTN_W), lambda b, i: (cur_rows(b, i), COL_GATE // ATTN_W)),
            pl.BlockSpec((tq, KV_W), lambda b, i: (cur_rows(b, i), COL_K // KV_W)),
            pl.BlockSpec((tq, KV_W), lambda b, i: (cur_rows(b, i), COL_V // KV_W)),
            pl.BlockSpec((WINDOW, KV_W), lambda b, i: (prev_rows(b, i), COL_K // KV_W)),
            pl.BlockSpec((WINDOW, KV_W), lambda b, i: (prev_rows(b, i), COL_V // KV_W)),
            pl.BlockSpec((1, HEAD_DIM), lambda b, i: (0, 0)),
            pl.BlockSpec((1, HEAD_DIM), lambda b, i: (0, 0)),
        ],
        out_specs=pl.BlockSpec((tq, ATTN_W), lambda b, i: (cur_rows(b, i), 0)),
        out_shape=jax.ShapeDtypeStruct((m, ATTN_W), BF16),
        compiler_params=pltpu.CompilerParams(
            dimension_semantics=("arbitrary", "arbitrary"), vmem_limit_bytes=VMEM_LIMIT),
        name="swa",
    )(sinks, proj, proj, proj, proj, proj, proj,
      q_norm_w.reshape(1, HEAD_DIM), k_norm_w.reshape(1, HEAD_DIM))


def _slab_powers(ar, ai, dt, btr, bti):
    dta_re, dta_im = dt * ar, dt * ai
    mag = jnp.exp(dta_re)
    ab_re, ab_im = mag * jnp.cos(dta_im), mag * jnp.sin(dta_im)
    pw = [(jnp.ones_like(ar), jnp.zeros_like(ar))]
    for _ in range(CHUNK):
        pr, pi = pw[-1]
        pw.append((pr * ab_re - pi * ab_im, pr * ab_im + pi * ab_re))
    den = ar * ar + ai * ai
    num_re, num_im = ab_re - 1.0, ab_im
    cf_re = (num_re * ar + num_im * ai) / den
    cf_im = (num_im * ar - num_re * ai) / den
    bb_re = cf_re * btr - cf_im * bti
    bb_im = cf_re * bti + cf_im * btr
    abb = [(pr * bb_re - pi * bb_im, pr * bb_im + pi * bb_re) for pr, pi in pw[:CHUNK]]
    return pw, abb


def _build_state_in(abb, p_s):
    row_g = lax.broadcasted_iota(jnp.int32, (LANES, STATE), 0) // GROUP
    even = row_g % 2 == 0
    pair_of_row = lax.broadcasted_iota(jnp.int32, (LANES, 2 * LANES), 0) // (2 * GROUP)
    for lag in range(CHUNK):
        t = CHUNK - 1 - lag
        re, im = abb[lag]
        blk = jnp.concatenate([jnp.where(even, re, 0.0), jnp.where(even, 0.0, re),
                               jnp.where(even, im, 0.0), jnp.where(even, 0.0, im)], axis=1)
        for k in range(PAIRS):
            p_s[k, t * LANES:(t + 1) * LANES, :] = jnp.where(pair_of_row == k, blk, 0.0).astype(BF16)


def _build_state_out(pw, cr, ci, q_s):
    lane_g = lax.broadcasted_iota(jnp.int32, (STATE, LANES), 1) // GROUP
    for t in range(CHUNK):
        pr, pi = pw[t + 1]
        cat = jnp.concatenate([cr * pr - ci * pi, -(cr * pi + ci * pr)], axis=1)
        cat_t = cat.T
        for k in range(PAIRS):
            for ri in range(2):
                for half in range(2):
                    piece = jnp.where(lane_g == 2 * k + half, cat_t[ri * STATE:(ri + 1) * STATE], 0.0)
                    r0 = ri * LANES + half * STATE
                    q_s[k, r0:r0 + STATE, t * LANES:(t + 1) * LANES] = piece.astype(BF16)


def _build_toeplitz(abb, cr, ci, d_row, w_s):
    rhs = jnp.concatenate([cr, -ci], axis=1)
    row = lax.broadcasted_iota(jnp.int32, (LANES, LANES), 0)
    col = lax.broadcasted_iota(jnp.int32, (LANES, LANES), 1)
    same_group = (row // GROUP) == (col // GROUP)
    taps = []
    for lag in range(CHUNK):
        lhs = jnp.concatenate(abb[lag], axis=1)
        k = lax.dot_general(lhs, rhs, (((1,), (1,)), ((), ())), precision=lax.Precision.HIGHEST,
                            preferred_element_type=F32)
        k = jnp.where(same_group, k, 0.0)
        if lag == 0:
            k = k + jnp.where(row == col, d_row, 0.0)
        taps.append(k.astype(BF16))
    zero = jnp.zeros((LANES, LANES), BF16)
    for d in range(N_TILES):
        w_s[d, :LANES, :LANES] = taps[2 * d]
        w_s[d, :LANES, LANES:] = taps[2 * d + 1]
        w_s[d, LANES:, :LANES] = taps[2 * d - 1] if d > 0 else zero
        w_s[d, LANES:, LANES:] = taps[2 * d]


def _ssm_kernel(u_ref, ar_ref, ai_ref, dt_ref, btr_ref, bti_ref, cr_ref, ci_ref, d_ref,
                dtare_ref, dtaim_ref, y_ref, xre_s, xim_s, p_s, q_s, w_s, *, batch, n_chunks):
    step = pl.program_id(0)
    rows8 = batch * PAIRS
    slab_rows = n_chunks * rows8

    def state_rows(slab, b, k):
        return pl.ds(slab * slab_rows + b * PAIRS + k, n_chunks, stride=rows8)

    def powers():
        return _slab_powers(ar_ref[...], ai_ref[...], dt_ref[...], btr_ref[...], bti_ref[...])

    @pl.when(step < N_SLABS)
    def _():
        _, abb = powers()
        _build_state_in(abb, p_s)
        u = u_ref[...]
        for k in range(PAIRS):
            x = _dot(u, p_s[k])
            for b in range(batch):
                xb = x[b * n_chunks:(b + 1) * n_chunks]
                xre_s[state_rows(step, b, k), :] = xb[:, :LANES]
                xim_s[state_rows(step, b, k), :] = xb[:, LANES:]

    @pl.when(step == N_SLABS)
    def _():
        dta_re = jnp.concatenate([dtare_ref[...]] * batch, axis=1).reshape(N_SLABS * rows8, LANES)
        dta_im = jnp.concatenate([dtaim_ref[...]] * batch, axis=1).reshape(N_SLABS * rows8, LANES)
        mag = jnp.exp(CHUNK * dta_re)
        a_re, a_im = mag * jnp.cos(CHUNK * dta_im), mag * jnp.sin(CHUNK * dta_im)

        def body(c, carry):
            s_re, s_im = carry
            x_re, x_im = [], []
            for sl in range(N_SLABS):
                off = pl.multiple_of(sl * slab_rows + c * rows8, rows8)
                x_re.append(xre_s[pl.ds(off, rows8), :])
                x_im.append(xim_s[pl.ds(off, rows8), :])
                xre_s[pl.ds(off, rows8), :] = s_re[sl * rows8:(sl + 1) * rows8]
                xim_s[pl.ds(off, rows8), :] = s_im[sl * rows8:(sl + 1) * rows8]
            x_re = jnp.concatenate(x_re, axis=0)
            x_im = jnp.concatenate(x_im, axis=0)
            return (a_re * s_re - a_im * s_im + x_re, a_re * s_im + a_im * s_re + x_im)

        zero = jnp.zeros((N_SLABS * rows8, LANES), F32)
        lax.fori_loop(0, n_chunks, body, (zero, zero))

    @pl.when(step > N_SLABS)
    def _():
        slab = step - (N_SLABS + 1)
        pw, abb = powers()
        cr, ci = cr_ref[...], ci_ref[...]
        _build_state_out(pw, cr, ci, q_s)
        _build_toeplitz(abb, cr, ci, d_ref[...], w_s)
        u = u_ref[...]
        y_state = None
        for k in range(PAIRS):
            s_in = jnp.concatenate(
                [jnp.concatenate([xre_s[state_rows(slab, b, k), :], xim_s[state_rows(slab, b, k), :]],
                                 axis=1) for b in range(batch)], axis=0).astype(BF16)
            part = _dot(s_in, q_s[k])
            y_state = part if y_state is None else y_state + part
        for t2 in range(N_TILES):
            acc = y_state[:, t2 * MXU:(t2 + 1) * MXU]
            for t1 in range(t2 + 1):
                acc = acc + _dot(u[:, t1 * MXU:(t1 + 1) * MXU], w_s[t2 - t1])
            y_ref[:, t2 * MXU:(t2 + 1) * MXU] = jax.nn.gelu(acc).astype(BF16)


def _ssm(u_chunks, A_re, A_im, log_dt, B_re, B_im, C_re, C_im, D_skip, batch):
    _, rows, _ = u_chunks.shape
    n_chunks = rows // batch
    n_steps = 2 * N_SLABS + 1
    kernel = functools.partial(_ssm_kernel, batch=batch, n_chunks=n_chunks)

    dt = jnp.exp(log_dt)[:, None]
    per_row = lambda v: jnp.repeat(v, GROUP, axis=0)
    ar, ai = per_row(A_re), per_row(A_im)
    dtr = per_row(jnp.broadcast_to(dt, A_re.shape))
    btr = B_re.transpose(0, 2, 1).reshape(SSM_W, STATE)
    bti = B_im.transpose(0, 2, 1).reshape(SSM_W, STATE)
    cr, ci = C_re.reshape(SSM_W, STATE), C_im.reshape(SSM_W, STATE)
    d_rows = D_skip.reshape(N_SLABS, 1, LANES)
    dta_re = (dt * A_re).reshape(N_SLABS, PAIRS, LANES)
    dta_im = (dt * A_im).reshape(N_SLABS, PAIRS, LANES)

    def slab_in(s):
        return jnp.where(s <= N_SLABS, jnp.minimum(s, N_SLABS - 1), s - (N_SLABS + 1))

    def slab_out(s):
        return jnp.maximum(s - (N_SLABS + 1), 0)

    coeff = pl.BlockSpec((LANES, STATE), lambda s: (slab_in(s), 0))
    whole = pl.BlockSpec((N_SLABS, PAIRS, LANES), lambda s: (0, 0, 0))
    return pl.pallas_call(
        kernel,
        grid=(n_steps,),
        in_specs=[
            pl.BlockSpec((None, rows, CHUNK_W), lambda s: (slab_in(s), 0, 0)),
            coeff, coeff, coeff, coeff, coeff, coeff, coeff,
            pl.BlockSpec((None, 1, LANES), lambda s: (slab_in(s), 0, 0)),
            whole, whole,
        ],
        out_specs=pl.BlockSpec((None, rows, CHUNK_W), lambda s: (slab_out(s), 0, 0)),
        out_shape=jax.ShapeDtypeStruct(u_chunks.shape, BF16),
        scratch_shapes=[
            pltpu.VMEM((N_SLABS * n_chunks * batch * PAIRS, LANES), F32),
            pltpu.VMEM((N_SLABS * n_chunks * batch * PAIRS, LANES), F32),
            pltpu.VMEM((PAIRS, CHUNK_W, 2 * LANES), BF16),
            pltpu.VMEM((PAIRS, 2 * LANES, CHUNK_W), BF16),
            pltpu.VMEM((N_TILES, MXU, MXU), BF16),
        ],
        compiler_params=pltpu.CompilerParams(
            dimension_semantics=("arbitrary",), vmem_limit_bytes=VMEM_LIMIT),
        name="ssm",
    )(u_chunks, ar, ai, dtr, btr, bti, cr, ci, d_rows, dta_re, dta_im)


def _tail_kernel(ag_ref, yc_ref, z_ref, ga_ref, gs_ref, x_ref, wa_ref, wg_ref, bg_ref, ws_ref,
                 wo_ref, o_ref, ys_ref, *, tm):
    rows = tm // CHUNK
    for s in range(N_SLABS):
        for t in range(CHUNK):
            ys_ref[pl.ds(s * tm + t, rows, stride=CHUNK), :] = (
                yc_ref[s, :, t * LANES:(t + 1) * LANES].astype(F32))
    y_g = jnp.concatenate([ys_ref[s * tm:(s + 1) * tm, :] for s in range(N_SLABS)],
                          axis=1).astype(BF16)

    y_a = _dot(ag_ref[...], wa_ref[...])
    glu = _dot(y_g, wg_ref[...]) + bg_ref[...]
    z = z_ref[...].astype(F32)
    t = glu[:, :SSM_W] * jax.nn.sigmoid(glu[:, SSM_W:]) * (z * jax.nn.sigmoid(z))
    y_s = _dot(t.astype(BF16), ws_ref[...])
    merged = (jax.nn.sigmoid(ga_ref[...].astype(F32)) * y_a
              + jax.nn.sigmoid(gs_ref[...].astype(F32)) * y_s)
    o_ref[...] = x_ref[...] + _dot(merged.astype(BF16), wo_ref[...])


def _tail(attn_g, y_chunks, proj, x2, w_attn, w_glu, b_glu, w_ssm, w_out, tm=256):
    m = x2.shape[0]
    row = lambda blk: (lambda i: (i, blk))
    const = lambda i: (0, 0)
    resident = functools.partial(pl.BlockSpec, index_map=const, pipeline_mode=pl.Buffered(1))
    kernel = functools.partial(_tail_kernel, tm=tm)
    return pl.pallas_call(
        kernel,
        grid=(m // tm,),
        in_specs=[
            pl.BlockSpec((tm, ATTN_W), row(0)),
            pl.BlockSpec((N_SLABS, tm // CHUNK, CHUNK_W), lambda i: (0, i, 0)),
            pl.BlockSpec((tm, SSM_W), row(COL_Z // SSM_W)),
            pl.BlockSpec((tm, D_MODEL), row(COL_GA // D_MODEL)),
            pl.BlockSpec((tm, D_MODEL), row(COL_GS // D_MODEL)),
            pl.BlockSpec((tm, D_MODEL), row(0)),
            resident((ATTN_W, D_MODEL)),
            resident((SSM_W, 2 * SSM_W)),
            resident((1, 2 * SSM_W)),
            resident((SSM_W, D_MODEL)),
            resident((D_MODEL, D_MODEL)),
        ],
        out_specs=pl.BlockSpec((tm, D_MODEL), row(0)),
        out_shape=jax.ShapeDtypeStruct((m, D_MODEL), F32),
        scratch_shapes=[pltpu.VMEM((N_SLABS * tm, LANES), F32)],
        compiler_params=pltpu.CompilerParams(
            dimension_semantics=("arbitrary",), vmem_limit_bytes=VMEM_LIMIT),
        name="tail",
    )(attn_g, y_chunks, proj, proj, proj, x2, w_attn, w_glu, b_glu.reshape(1, -1), w_ssm, w_out)


def kernel(x, norm_w, w_in, q_norm_w, k_norm_w, sinks, w_attn_proj, A_re, A_im, log_dt, B_re, B_im,
           C_re, C_im, D_skip, w_glu, b_glu, w_ssm_proj, w_out):
    batch, seq, _ = x.shape
    m = batch * seq
    x2 = x.reshape(m, D_MODEL)

    w_in_bf = jnp.concatenate(
        [w_in[:, :ATTN_W], w_in[:, ATTN_W + 2 * KV_W:], w_in[:, ATTN_W:ATTN_W + 2 * KV_W]],
        axis=1).astype(BF16)

    proj, u_chunks = _in_proj(x2, norm_w, w_in_bf)

    attn_g = _swa(proj, sinks, q_norm_w, k_norm_w, batch, seq)

    y_chunks = _ssm(u_chunks, A_re, A_im, log_dt, B_re, B_im, C_re, C_im, D_skip, batch)

    out = _tail(attn_g, y_chunks, proj, x2, w_attn_proj.astype(BF16), w_glu.astype(BF16), b_glu,
                w_ssm_proj.astype(BF16), w_out.astype(BF16))
    return out.reshape(batch, seq, D_MODEL)
```

```python
import functools
import math

import jax
import jax.numpy as jnp
from jax import lax
from jax.experimental import pallas as pl
from jax.experimental.pallas import tpu as pltpu

F32 = jnp.float32
BF16 = jnp.bfloat16

D_MODEL = 2048
HEAD_DIM = 64
N_Q_HEADS = 16
N_KV_HEADS = 4
Q_PER_KV = 4
ATTN_W = N_Q_HEADS * HEAD_DIM
KV_W = N_KV_HEADS * HEAD_DIM
WINDOW = 128
SSM_W = D_MODEL // 2
GROUP = 16
N_GROUPS = SSM_W // GROUP
STATE = 64
NORM_EPS = 1e-6

COL_Q, COL_GATE, COL_U, COL_Z = 0, 1024, 2048, 3072
COL_GA, COL_GS, COL_K, COL_V = 4096, 6144, 8192, 8448
IN_W = 8704

LANES = 128
MXU = 256
CHUNK = 16
SLAB_G = LANES // GROUP
N_SLABS = N_GROUPS // SLAB_G
PAIRS = SLAB_G // 2
CHUNK_W = CHUNK * LANES
N_TILES = CHUNK_W // MXU
VMEM_LIMIT = 48 * 1024 * 1024


def _dot(a, b):
    return jnp.dot(a, b, preferred_element_type=F32)


IN_TN = 512
N_IN_TILES = IN_W // IN_TN
SRC_KV_TILE = (ATTN_W) // IN_TN
SRC_U_TILE0 = (ATTN_W + 2 * KV_W + ATTN_W) // IN_TN


def _dst_tile(src_tile):
    if src_tile < SRC_KV_TILE:
        return src_tile
    if src_tile == SRC_KV_TILE:
        return N_IN_TILES - 1
    return src_tile - 1


def _in_proj_kernel(x_ref, nw_ref, w_hbm, o_hbm, uc_ref, h_ref, w_buf, o_buf, us_ref, w_sem, o_sem,
                    *, tm):
    i = pl.program_id(0)
    rows = tm // CHUNK
    n_sl = IN_TN // LANES

    def w_copy(t):
        return pltpu.make_async_copy(w_hbm.at[:, pl.ds(t * IN_TN, IN_TN)], w_buf.at[t % 2],
                                     w_sem.at[t % 2])

    def o_copy(t):
        dst = o_hbm.at[pl.ds(pl.multiple_of(i * tm, tm), tm), pl.ds(_dst_tile(t) * IN_TN, IN_TN)]
        return pltpu.make_async_copy(o_buf.at[t % 2], dst, o_sem.at[t % 2])

    w_copy(0).start()
    x = x_ref[...]
    ms = jnp.mean(x * x, axis=-1, keepdims=True)
    h_ref[...] = (x * lax.rsqrt(ms + NORM_EPS) * nw_ref[...]).astype(BF16)

    def finish(t, acc):
        o_buf[t % 2] = acc.astype(BF16)
        if SRC_U_TILE0 <= t < SRC_U_TILE0 + SSM_W // IN_TN:
            slab0 = (t - SRC_U_TILE0) * n_sl
            for s in range(n_sl):
                us_ref[s * tm:(s + 1) * tm, :] = acc[:, s * LANES:(s + 1) * LANES]
            for s in range(n_sl):
                for tok in range(CHUNK):
                    piece = us_ref[pl.ds(s * tm + tok, rows, stride=CHUNK), :]
                    uc_ref[slab0 + s, :, tok * LANES:(tok + 1) * LANES] = piece.astype(BF16)
        o_copy(t).start()

    for t in range(N_IN_TILES):
        if t + 1 < N_IN_TILES:
            w_copy(t + 1).start()
        w_copy(t).wait()
        if t >= 2:
            o_copy(t - 2).wait()
        acc = None
        for k in range(D_MODEL // MXU):
            ks = slice(k * MXU, (k + 1) * MXU)
            part = _dot(h_ref[:, ks], w_buf[t % 2, ks, :].astype(BF16))
            acc = part if acc is None else acc + part
        finish(t, acc)

    o_copy(N_IN_TILES - 2).wait()
    o_copy(N_IN_TILES - 1).wait()


def _in_proj(x2, norm_w, w_in, tm=1024):
    m = x2.shape[0]
    kernel = functools.partial(_in_proj_kernel, tm=tm)
    return pl.pallas_call(
        kernel,
        grid=(m // tm,),
        in_specs=[
            pl.BlockSpec((tm, D_MODEL), lambda i: (i, 0)),
            pl.BlockSpec((1, D_MODEL), lambda i: (0, 0)),
            pl.BlockSpec(memory_space=pl.ANY),
        ],
        out_specs=[
            pl.BlockSpec(memory_space=pl.ANY),
            pl.BlockSpec((N_SLABS, tm // CHUNK, CHUNK_W), lambda i: (0, i, 0)),
        ],
        out_shape=[
            jax.ShapeDtypeStruct((m, IN_W), BF16),
            jax.ShapeDtypeStruct((N_SLABS, m // CHUNK, CHUNK_W), BF16),
        ],
        scratch_shapes=[
            pltpu.VMEM((tm, D_MODEL), BF16),
            pltpu.VMEM((2, D_MODEL, IN_TN), F32),
            pltpu.VMEM((2, tm, IN_TN), BF16),
            pltpu.VMEM((IN_TN // LANES * tm, LANES), F32),
            pltpu.SemaphoreType.DMA((2,)),
            pltpu.SemaphoreType.DMA((2,)),
        ],
        compiler_params=pltpu.CompilerParams(
            dimension_semantics=("arbitrary",), vmem_limit_bytes=VMEM_LIMIT),
        name="in_proj",
    )(x2, norm_w.reshape(1, D_MODEL), w_in)


def _head_norm(t, w):
    ms = jnp.mean(t * t, axis=-1, keepdims=True)
    return t * lax.rsqrt(ms + NORM_EPS) * w


_NT = (((1,), (1,)), ((), ()))
_TN = (((0,), (0,)), ((), ()))


def _swa_kernel(sink_ref, q_ref, g_ref, kc_ref, vc_ref, kp_ref, vp_ref, qw_ref, kw_ref, o_ref,
                *, n_sub):
    kqw = kw_ref[...] * qw_ref[...] * (1.0 / math.sqrt(HEAD_DIM))
    n_col = Q_PER_KV * WINDOW
    key = lax.broadcasted_iota(jnp.int32, (WINDOW, n_col), 0)
    qry = lax.broadcasted_iota(jnp.int32, (WINDOW, n_col), 1) % WINDOW
    from_prev = key > qry
    no_prev = jnp.where(pl.program_id(1) == 0, -1e30, 0.0)
    head_of_col = lax.broadcasted_iota(jnp.int32, (1, n_col), 1) // WINDOW
    ones = jnp.ones((8, 2 * HEAD_DIM), BF16)

    for g in range(N_KV_HEADS):
        kcol = slice(g * HEAD_DIM, (g + 1) * HEAD_DIM)
        gcols = slice(g * Q_PER_KV * HEAD_DIM, (g + 1) * Q_PER_KV * HEAD_DIM)
        sink = jnp.zeros((1, n_col), F32)
        for r in range(Q_PER_KV):
            sink = jnp.where(head_of_col == r, sink_ref[g * Q_PER_KV + r], sink)
        k_prev = _head_norm(kp_ref[:, kcol].astype(F32), kqw).astype(BF16)
        v_prev = vp_ref[:, kcol]
        for n in range(n_sub):
            rows = slice(n * WINDOW, (n + 1) * WINDOW)
            k_cur = _head_norm(kc_ref[rows, kcol].astype(F32), kqw).astype(BF16)
            v_cur = vc_ref[rows, kcol]
            k_ctx = jnp.concatenate([k_prev, k_cur], axis=0)
            v_ctx = jnp.concatenate([v_prev, v_cur], axis=0)

            qg = q_ref[rows, gcols]
            qs = jnp.concatenate([qg[:, r * HEAD_DIM:(r + 1) * HEAD_DIM] for r in range(Q_PER_KV)],
                                 axis=0)
            qf = qs.astype(F32)
            q2 = qf * qf
            q2_hi = q2.astype(BF16)
            q2_lo = (q2 - q2_hi.astype(F32)).astype(BF16)
            ssq = lax.dot_general(ones, jnp.concatenate([q2_hi, q2_lo], axis=1), _NT,
                                  preferred_element_type=F32)[:1]
            rms_q = lax.rsqrt(ssq * (1.0 / HEAD_DIM) + NORM_EPS)

            s = lax.dot_general(k_ctx, qs, _NT, preferred_element_type=F32)
            s_prev = s[:WINDOW] + no_prev if n == 0 else s[:WINDOW]
            s = jnp.where(from_prev, s_prev, s[WINDOW:]) * rms_q
            mx = jnp.maximum(jnp.max(s, axis=0, keepdims=True), sink)
            p = jnp.exp(s - mx)
            den = jnp.sum(p, axis=0, keepdims=True) + jnp.exp(sink - mx)
            p_ctx = jnp.concatenate([jnp.where(from_prev, p, 0.0), jnp.where(from_prev, 0.0, p)],
                                    axis=0).astype(BF16)
            o_t = lax.dot_general(v_ctx, p_ctx, _TN, preferred_element_type=F32) * (1.0 / den)
            halves = [jnp.concatenate([o_t[:, (2 * h) * WINDOW:(2 * h + 1) * WINDOW],
                                       o_t[:, (2 * h + 1) * WINDOW:(2 * h + 2) * WINDOW]], axis=0).T
                      for h in range(Q_PER_KV // 2)]
            og = jnp.concatenate(halves, axis=1)
            gate = g_ref[rows, gcols].astype(F32)
            o_ref[rows, gcols] = (og * (gate * jax.nn.sigmoid(gate))).astype(BF16)
            k_prev, v_prev = k_cur, v_cur


def _swa(proj, sinks, q_norm_w, k_norm_w, batch, seq, tq=512):
    n_sub = tq // WINDOW
    tiles = seq // tq
    m = batch * seq

    def cur_rows(b, i):
        return b * tiles + i

    def prev_rows(b, i):
        return jnp.maximum((b * tiles + i) * n_sub - 1, 0)

    kernel = functools.partial(_swa_kernel, n_sub=n_sub)
    return pl.pallas_call(
        kernel,
        grid=(batch, tiles),
        in_specs=[
            pl.BlockSpec(memory_space=pltpu.SMEM),
            pl.BlockSpec((tq, ATTN_W), lambda b, i: (cur_rows(b, i), COL_Q // ATTN_W)),
            pl.BlockSpec((tq, ATTN_W), lambda b, i: (cur_rows(b, i), COL_GATE // ATTN_W)),
            pl.BlockSpec((tq, KV_W), lambda b, i: (cur_rows(b, i), COL_K // KV_W)),
            pl.BlockSpec((tq, KV_W), lambda b, i: (cur_rows(b, i), COL_V // KV_W)),
            pl.BlockSpec((WINDOW, KV_W), lambda b, i: (prev_rows(b, i), COL_K // KV_W)),
            pl.BlockSpec((WINDOW, KV_W), lambda b, i: (prev_rows(b, i), COL_V // KV_W)),
            pl.BlockSpec((1, HEAD_DIM), lambda b, i: (0, 0)),
            pl.BlockSpec((1, HEAD_DIM), lambda b, i: (0, 0)),
        ],
        out_specs=pl.BlockSpec((tq, ATTN_W), lambda b, i: (cur_rows(b, i), 0)),
        out_shape=jax.ShapeDtypeStruct((m, ATTN_W), BF16),
        compiler_params=pltpu.CompilerParams(
            dimension_semantics=("arbitrary", "arbitrary"), vmem_limit_bytes=VMEM_LIMIT),
        name="swa",
    )(sinks, proj, proj, proj, proj, proj, proj,
      q_norm_w.reshape(1, HEAD_DIM), k_norm_w.reshape(1, HEAD_DIM))


def _slab_powers(ar, ai, dt, btr, bti):
    dta_re, dta_im = dt * ar, dt * ai
    mag = jnp.exp(dta_re)
    ab_re, ab_im = mag * jnp.cos(dta_im), mag * jnp.sin(dta_im)
    pw = [(jnp.ones_like(ar), jnp.zeros_like(ar))]
    for _ in range(CHUNK):
        pr, pi = pw[-1]
        pw.append((pr * ab_re - pi * ab_im, pr * ab_im + pi * ab_re))
    den = ar * ar + ai * ai
    num_re, num_im = ab_re - 1.0, ab_im
    cf_re = (num_re * ar + num_im * ai) / den
    cf_im = (num_im * ar - num_re * ai) / den
    bb_re = cf_re * btr - cf_im * bti
    bb_im = cf_re * bti + cf_im * btr
    abb = [(pr * bb_re - pi * bb_im, pr * bb_im + pi * bb_re) for pr, pi in pw[:CHUNK]]
    return pw, abb


def _build_state_in(abb, p_s):
    row_g = lax.broadcasted_iota(jnp.int32, (LANES, STATE), 0) // GROUP
    even = row_g % 2 == 0
    pair_of_row = lax.broadcasted_iota(jnp.int32, (LANES, 2 * LANES), 0) // (2 * GROUP)
    for lag in range(CHUNK):
        t = CHUNK - 1 - lag
        re, im = abb[lag]
        blk = jnp.concatenate([jnp.where(even, re, 0.0), jnp.where(even, 0.0, re),
                               jnp.where(even, im, 0.0), jnp.where(even, 0.0, im)], axis=1)
        for k in range(PAIRS):
            p_s[k, t * LANES:(t + 1) * LANES, :] = jnp.where(pair_of_row == k, blk, 0.0).astype(BF16)


def _build_state_out(pw, cr, ci, q_s):
    lane_g = lax.broadcasted_iota(jnp.int32, (STATE, LANES), 1) // GROUP
    for t in range(CHUNK):
        pr, pi = pw[t + 1]
        cat = jnp.concatenate([cr * pr - ci * pi, -(cr * pi + ci * pr)], axis=1)
        cat_t = cat.T
        for k in range(PAIRS):
            for ri in range(2):
                for half in range(2):
                    piece = jnp.where(lane_g == 2 * k + half, cat_t[ri * STATE:(ri + 1) * STATE], 0.0)
                    r0 = ri * LANES + half * STATE
                    q_s[k, r0:r0 + STATE, t * LANES:(t + 1) * LANES] = piece.astype(BF16)


def _build_toeplitz(abb, cr, ci, d_row, w_s):
    rhs = jnp.concatenate([cr, -ci], axis=1)
    row = lax.broadcasted_iota(jnp.int32, (LANES, LANES), 0)
    col = lax.broadcasted_iota(jnp.int32, (LANES, LANES), 1)
    same_group = (row // GROUP) == (col // GROUP)
    taps = []
    for lag in range(CHUNK):
        lhs = jnp.concatenate(abb[lag], axis=1)
        k = lax.dot_general(lhs, rhs, (((1,), (1,)), ((), ())), precision=lax.Precision.HIGHEST,
                            preferred_element_type=F32)
        k = jnp.where(same_group, k, 0.0)
        if lag == 0:
            k = k + jnp.where(row == col, d_row, 0.0)
        taps.append(k.astype(BF16))
    zero = jnp.zeros((LANES, LANES), BF16)
    for d in range(N_TILES):
        w_s[d, :LANES, :LANES] = taps[2 * d]
        w_s[d, :LANES, LANES:] = taps[2 * d + 1]
        w_s[d, LANES:, :LANES] = taps[2 * d - 1] if d > 0 else zero
        w_s[d, LANES:, LANES:] = taps[2 * d]


def _ssm_kernel(u_ref, ar_ref, ai_ref, dt_ref, btr_ref, bti_ref, cr_ref, ci_ref, d_ref,
                dtare_ref, dtaim_ref, y_ref, xre_s, xim_s, p_s, q_s, w_s, *, batch, n_chunks):
    step = pl.program_id(0)
    rows8 = batch * PAIRS
    slab_rows = n_chunks * rows8

    def state_rows(slab, b, k):
        return pl.ds(slab * slab_rows + b * PAIRS + k, n_chunks, stride=rows8)

    def powers():
        return _slab_powers(ar_ref[...], ai_ref[...], dt_ref[...], btr_ref[...], bti_ref[...])

    @pl.when(step < N_SLABS)
    def _():
        _, abb = powers()
        _build_state_in(abb, p_s)
        u = u_ref[...]
        for k in range(PAIRS):
            x = _dot(u, p_s[k])
            for b in range(batch):
                xb = x[b * n_chunks:(b + 1) * n_chunks]
                xre_s[state_rows(step, b, k), :] = xb[:, :LANES]
                xim_s[state_rows(step, b, k), :] = xb[:, LANES:]

    @pl.when(step == N_SLABS)
    def _():
        dta_re = jnp.concatenate([dtare_ref[...]] * batch, axis=1).reshape(N_SLABS * rows8, LANES)
        dta_im = jnp.concatenate([dtaim_ref[...]] * batch, axis=1).reshape(N_SLABS * rows8, LANES)
        mag = jnp.exp(CHUNK * dta_re)
        a_re, a_im = mag * jnp.cos(CHUNK * dta_im), mag * jnp.sin(CHUNK * dta_im)

        def body(c, carry):
            s_re, s_im = carry
            x_re, x_im = [], []
            for sl in range(N_SLABS):
                off = pl.multiple_of(sl * slab_rows + c * rows8, rows8)
                x_re.append(xre_s[pl.ds(off, rows8), :])
                x_im.append(xim_s[pl.ds(off, rows8), :])
                xre_s[pl.ds(off, rows8), :] = s_re[sl * rows8:(sl + 1) * rows8]
                xim_s[pl.ds(off, rows8), :] = s_im[sl * rows8:(sl + 1) * rows8]
            x_re = jnp.concatenate(x_re, axis=0)
            x_im = jnp.concatenate(x_im, axis=0)
            return (a_re * s_re - a_im * s_im + x_re, a_re * s_im + a_im * s_re + x_im)

        zero = jnp.zeros((N_SLABS * rows8, LANES), F32)
        lax.fori_loop(0, n_chunks, body, (zero, zero))

    @pl.when(step > N_SLABS)
    def _():
        slab = step - (N_SLABS + 1)
        pw, abb = powers()
        cr, ci = cr_ref[...], ci_ref[...]
        _build_state_out(pw, cr, ci, q_s)
        _build_toeplitz(abb, cr, ci, d_ref[...], w_s)
        u = u_ref[...]
        y_state = None
        for k in range(PAIRS):
            s_in = jnp.concatenate(
                [jnp.concatenate([xre_s[state_rows(slab, b, k), :], xim_s[state_rows(slab, b, k), :]],
                                 axis=1) for b in range(batch)], axis=0).astype(BF16)
            part = _dot(s_in, q_s[k])
            y_state = part if y_state is None else y_state + part
        for t2 in range(N_TILES):
            acc = y_state[:, t2 * MXU:(t2 + 1) * MXU]
            for t1 in range(t2 + 1):
                acc = acc + _dot(u[:, t1 * MXU:(t1 + 1) * MXU], w_s[t2 - t1])
            y_ref[:, t2 * MXU:(t2 + 1) * MXU] = jax.nn.gelu(acc).astype(BF16)


def _ssm(u_chunks, A_re, A_im, log_dt, B_re, B_im, C_re, C_im, D_skip, batch):
    _, rows, _ = u_chunks.shape
    n_chunks = rows // batch
    n_steps = 2 * N_SLABS + 1
    kernel = functools.partial(_ssm_kernel, batch=batch, n_chunks=n_chunks)

    dt = jnp.exp(log_dt)[:, None]
    per_row = lambda v: jnp.repeat(v, GROUP, axis=0)
    ar, ai = per_row(A_re), per_row(A_im)
    dtr = per_row(jnp.broadcast_to(dt, A_re.shape))
    btr = B_re.transpose(0, 2, 1).reshape(SSM_W, STATE)
    bti = B_im.transpose(0, 2, 1).reshape(SSM_W, STATE)
    cr, ci = C_re.reshape(SSM_W, STATE), C_im.reshape(SSM_W, STATE)
    d_rows = D_skip.reshape(N_SLABS, 1, LANES)
    dta_re = (dt * A_re).reshape(N_SLABS, PAIRS, LANES)
    dta_im = (dt * A_im).reshape(N_SLABS, PAIRS, LANES)

    def slab_in(s):
        return jnp.where(s <= N_SLABS, jnp.minimum(s, N_SLABS - 1), s - (N_SLABS + 1))

    def slab_out(s):
        return jnp.maximum(s - (N_SLABS + 1), 0)

    coeff = pl.BlockSpec((LANES, STATE), lambda s: (slab_in(s), 0))
    whole = pl.BlockSpec((N_SLABS, PAIRS, LANES), lambda s: (0, 0, 0))
    return pl.pallas_call(
        kernel,
        grid=(n_steps,),
        in_specs=[
            pl.BlockSpec((None, rows, CHUNK_W), lambda s: (slab_in(s), 0, 0)),
            coeff, coeff, coeff, coeff, coeff, coeff, coeff,
            pl.BlockSpec((None, 1, LANES), lambda s: (slab_in(s), 0, 0)),
            whole, whole,
        ],
        out_specs=pl.BlockSpec((None, rows, CHUNK_W), lambda s: (slab_out(s), 0, 0)),
        out_shape=jax.ShapeDtypeStruct(u_chunks.shape, BF16),
        scratch_shapes=[
            pltpu.VMEM((N_SLABS * n_chunks * batch * PAIRS, LANES), F32),
            pltpu.VMEM((N_SLABS * n_chunks * batch * PAIRS, LANES), F32),
            pltpu.VMEM((PAIRS, CHUNK_W, 2 * LANES), BF16),
            pltpu.VMEM((PAIRS, 2 * LANES, CHUNK_W), BF16),
            pltpu.VMEM((N_TILES, MXU, MXU), BF16),
        ],
        compiler_params=pltpu.CompilerParams(
            dimension_semantics=("arbitrary",), vmem_limit_bytes=VMEM_LIMIT),
        name="ssm",
    )(u_chunks, ar, ai, dtr, btr, bti, cr, ci, d_rows, dta_re, dta_im)


def _tail_kernel(ag_ref, yc_ref, z_ref, ga_ref, gs_ref, x_ref, wa_ref, wg_ref, bg_ref, ws_ref,
                 wo_ref, o_ref, ys_ref, *, tm):
    rows = tm // CHUNK
    for s in range(N_SLABS):
        for t in range(CHUNK):
            ys_ref[pl.ds(s * tm + t, rows, stride=CHUNK), :] = (
                yc_ref[s, :, t * LANES:(t + 1) * LANES].astype(F32))
    y_g = jnp.concatenate([ys_ref[s * tm:(s + 1) * tm, :] for s in range(N_SLABS)],
                          axis=1).astype(BF16)

    y_a = _dot(ag_ref[...], wa_ref[...])
    glu = _dot(y_g, wg_ref[...]) + bg_ref[...]
    z = z_ref[...].astype(F32)
    t = glu[:, :SSM_W] * jax.nn.sigmoid(glu[:, SSM_W:]) * (z * jax.nn.sigmoid(z))
    y_s = _dot(t.astype(BF16), ws_ref[...])
    merged = (jax.nn.sigmoid(ga_ref[...].astype(F32)) * y_a
              + jax.nn.sigmoid(gs_ref[...].astype(F32)) * y_s)
    o_ref[...] = x_ref[...] + _dot(merged.astype(BF16), wo_ref[...])


def _tail(attn_g, y_chunks, proj, x2, w_attn, w_glu, b_glu, w_ssm, w_out, tm=256):
    m = x2.shape[0]
    row = lambda blk: (lambda i: (i, blk))
    const = lambda i: (0, 0)
    resident = functools.partial(pl.BlockSpec, index_map=const, pipeline_mode=pl.Buffered(1))
    kernel = functools.partial(_tail_kernel, tm=tm)
    return pl.pallas_call(
        kernel,
        grid=(m // tm,),
        in_specs=[
            pl.BlockSpec((tm, ATTN_W), row(0)),
            pl.BlockSpec((N_SLABS, tm // CHUNK, CHUNK_W), lambda i: (0, i, 0)),
            pl.BlockSpec((tm, SSM_W), row(COL_Z // SSM_W)),
            pl.BlockSpec((tm, D_MODEL), row(COL_GA // D_MODEL)),
            pl.BlockSpec((tm, D_MODEL), row(COL_GS // D_MODEL)),
            pl.BlockSpec((tm, D_MODEL), row(0)),
            resident((ATTN_W, D_MODEL)),
            resident((SSM_W, 2 * SSM_W)),
            resident((1, 2 * SSM_W)),
            resident((SSM_W, D_MODEL)),
            resident((D_MODEL, D_MODEL)),
        ],
        out_specs=pl.BlockSpec((tm, D_MODEL), row(0)),
        out_shape=jax.ShapeDtypeStruct((m, D_MODEL), F32),
        scratch_shapes=[pltpu.VMEM((N_SLABS * tm, LANES), F32)],
        compiler_params=pltpu.CompilerParams(
            dimension_semantics=("arbitrary",), vmem_limit_bytes=VMEM_LIMIT),
        name="tail",
    )(attn_g, y_chunks, proj, proj, proj, x2, w_attn, w_glu, b_glu.reshape(1, -1), w_ssm, w_out)


def kernel(x, norm_w, w_in, q_norm_w, k_norm_w, sinks, w_attn_proj, A_re, A_im, log_dt, B_re, B_im,
           C_re, C_im, D_skip, w_glu, b_glu, w_ssm_proj, w_out):
    batch, seq, _ = x.shape
    m = batch * seq
    x2 = x.reshape(m, D_MODEL)

    proj, u_chunks = _in_proj(x2, norm_w, w_in)

    attn_g = _swa(proj, sinks, q_norm_w, k_norm_w, batch, seq)

    y_chunks = _ssm(u_chunks, A_re, A_im, log_dt, B_re, B_im, C_re, C_im, D_skip, batch)

    out = _tail(attn_g, y_chunks, proj, x2, w_attn_proj.astype(BF16), w_glu.astype(BF16), b_glu,
                w_ssm_proj.astype(BF16), w_out.astype(BF16))
    return out.reshape(batch, seq, D_MODEL)
```

```python
import functools
import math

import jax
import jax.numpy as jnp
from jax import lax
from jax.experimental import pallas as pl
from jax.experimental.pallas import tpu as pltpu

F32 = jnp.float32
BF16 = jnp.bfloat16

D_MODEL = 2048
HEAD_DIM = 64
N_Q_HEADS = 16
N_KV_HEADS = 4
Q_PER_KV = 4
ATTN_W = N_Q_HEADS * HEAD_DIM
KV_W = N_KV_HEADS * HEAD_DIM
WINDOW = 128
SSM_W = D_MODEL // 2
GROUP = 16
N_GROUPS = SSM_W // GROUP
STATE = 64
NORM_EPS = 1e-6

COL_Q, COL_GATE, COL_U, COL_Z = 0, 1024, 2048, 3072
COL_GA, COL_GS, COL_K, COL_V = 4096, 6144, 8192, 8448
IN_W = 8704

LANES = 128
MXU = 256
CHUNK = 16
SLAB_G = LANES // GROUP
N_SLABS = N_GROUPS // SLAB_G
PAIRS = SLAB_G // 2
CHUNK_W = CHUNK * LANES
N_TILES = CHUNK_W // MXU
VMEM_LIMIT = 48 * 1024 * 1024
IN_PROJ_VMEM_LIMIT = 58 * 1024 * 1024


def _dot(a, b):
    return jnp.dot(a, b, preferred_element_type=F32)


IN_TN = 512
N_IN_TILES = IN_W // IN_TN
SRC_KV_TILE = (ATTN_W) // IN_TN
SRC_U_TILE0 = (ATTN_W + 2 * KV_W + ATTN_W) // IN_TN


def _dst_tile(src_tile):
    if src_tile < SRC_KV_TILE:
        return src_tile
    if src_tile == SRC_KV_TILE:
        return N_IN_TILES - 1
    return src_tile - 1


def _in_proj_kernel(nw_ref, x_hbm, w_hbm, o_hbm, uc_ref, h_ref, x_buf, w_buf, o_buf, us_ref, x_sem,
                    w_sem, o_sem, *, tm, n_steps):
    i = pl.program_id(0)
    rows = tm // CHUNK
    n_sl = IN_TN // LANES
    half = tm // 2

    def x_copy(step, s):
        src = x_hbm.at[pl.ds(pl.multiple_of(step * tm + s * half, half), half), :]
        return pltpu.make_async_copy(src, x_buf.at[s], x_sem.at[s])

    def w_copy(t):
        return pltpu.make_async_copy(w_hbm.at[:, pl.ds(t * IN_TN, IN_TN)], w_buf.at[t % 2],
                                     w_sem.at[t % 2])

    def o_copy(t):
        dst = o_hbm.at[pl.ds(pl.multiple_of(i * tm, tm), tm), pl.ds(_dst_tile(t) * IN_TN, IN_TN)]
        return pltpu.make_async_copy(o_buf.at[t % 2], dst, o_sem.at[t % 2])

    @pl.when(i == 0)
    def _():
        x_copy(0, 0).start()
        x_copy(0, 1).start()

    w_copy(0).start()
    for s in range(2):
        x_copy(i, s).wait()
        x = x_buf[s]
        ms = jnp.mean(x * x, axis=-1, keepdims=True)
        h_ref[s * half:(s + 1) * half, :] = (x * lax.rsqrt(ms + NORM_EPS) * nw_ref[...]).astype(BF16)

    @pl.when(i + 1 < n_steps)
    def _():
        x_copy(i + 1, 0).start()
        x_copy(i + 1, 1).start()

    def finish(t, acc):
        o_buf[t % 2] = acc.astype(BF16)
        if SRC_U_TILE0 <= t < SRC_U_TILE0 + SSM_W // IN_TN:
            slab0 = (t - SRC_U_TILE0) * n_sl
            for s in range(n_sl):
                us_ref[s * tm:(s + 1) * tm, :] = acc[:, s * LANES:(s + 1) * LANES]
            for s in range(n_sl):
                for tok in range(CHUNK):
                    piece = us_ref[pl.ds(s * tm + tok, rows, stride=CHUNK), :]
                    uc_ref[slab0 + s, :, tok * LANES:(tok + 1) * LANES] = piece.astype(BF16)
        o_copy(t).start()

    for t in range(N_IN_TILES):
        if t + 1 < N_IN_TILES:
            w_copy(t + 1).start()
        w_copy(t).wait()
        if t >= 2:
            o_copy(t - 2).wait()
        acc = None
        for k in range(D_MODEL // MXU):
            ks = slice(k * MXU, (k + 1) * MXU)
            part = _dot(h_ref[:, ks], w_buf[t % 2, ks, :].astype(BF16))
            acc = part if acc is None else acc + part
        finish(t, acc)

    o_copy(N_IN_TILES - 2).wait()
    o_copy(N_IN_TILES - 1).wait()


def _in_proj(x2, norm_w, w_in, tm=2048):
    m = x2.shape[0]
    n_steps = m // tm
    kernel = functools.partial(_in_proj_kernel, tm=tm, n_steps=n_steps)
    return pl.pallas_call(
        kernel,
        grid=(n_steps,),
        in_specs=[
            pl.BlockSpec((1, D_MODEL), lambda i: (0, 0)),
            pl.BlockSpec(memory_space=pl.ANY),
            pl.BlockSpec(memory_space=pl.ANY),
        ],
        out_specs=[
            pl.BlockSpec(memory_space=pl.ANY),
            pl.BlockSpec((N_SLABS, tm // CHUNK, CHUNK_W), lambda i: (0, i, 0)),
        ],
        out_shape=[
            jax.ShapeDtypeStruct((m, IN_W), BF16),
            jax.ShapeDtypeStruct((N_SLABS, m // CHUNK, CHUNK_W), BF16),
        ],
        scratch_shapes=[
            pltpu.VMEM((tm, D_MODEL), BF16),
            pltpu.VMEM((2, tm // 2, D_MODEL), F32),
            pltpu.VMEM((2, D_MODEL, IN_TN), F32),
            pltpu.VMEM((2, tm, IN_TN), BF16),
            pltpu.VMEM((IN_TN // LANES * tm, LANES), F32),
            pltpu.SemaphoreType.DMA((2,)),
            pltpu.SemaphoreType.DMA((2,)),
            pltpu.SemaphoreType.DMA((2,)),
        ],
        compiler_params=pltpu.CompilerParams(
            dimension_semantics=("arbitrary",), vmem_limit_bytes=IN_PROJ_VMEM_LIMIT),
        name="in_proj",
    )(norm_w.reshape(1, D_MODEL), x2, w_in)


def _head_norm(t, w):
    ms = jnp.mean(t * t, axis=-1, keepdims=True)
    return t * lax.rsqrt(ms + NORM_EPS) * w


_NT = (((1,), (1,)), ((), ()))
_TN = (((0,), (0,)), ((), ()))


def _swa_kernel(sink_ref, q_ref, g_ref, kc_ref, vc_ref, kp_ref, vp_ref, qw_ref, kw_ref, o_ref,
                *, n_sub):
    log2e = math.log2(math.e)
    kqw = kw_ref[...] * qw_ref[...] * (log2e / math.sqrt(HEAD_DIM))
    n_col = Q_PER_KV * WINDOW
    key = lax.broadcasted_iota(jnp.int32, (WINDOW, n_col), 0)
    qry = lax.broadcasted_iota(jnp.int32, (WINDOW, n_col), 1) % WINDOW
    from_prev = key > qry
    no_prev = jnp.where(pl.program_id(1) == 0, -1e30, 0.0)
    head_of_col = lax.broadcasted_iota(jnp.int32, (1, n_col), 1) // WINDOW
    gw = Q_PER_KV * HEAD_DIM
    sel_r = lax.broadcasted_iota(jnp.int32, (8, 2 * gw), 0)
    sel_l = (lax.broadcasted_iota(jnp.int32, (8, 2 * gw), 1) % gw) // HEAD_DIM
    head_sel = jnp.where(sel_r == sel_l, 1.0, 0.0).astype(BF16)

    for g in range(N_KV_HEADS):
        kcol = slice(g * HEAD_DIM, (g + 1) * HEAD_DIM)
        gcols = slice(g * Q_PER_KV * HEAD_DIM, (g + 1) * Q_PER_KV * HEAD_DIM)
        sink = jnp.zeros((1, n_col), F32)
        for r in range(Q_PER_KV):
            sink = jnp.where(head_of_col == r, sink_ref[g * Q_PER_KV + r] * log2e, sink)
        k_prev = _head_norm(kp_ref[:, kcol].astype(F32), kqw).astype(BF16)
        v_prev = vp_ref[:, kcol]
        for n in range(n_sub):
            rows = slice(n * WINDOW, (n + 1) * WINDOW)
            k_cur = _head_norm(kc_ref[rows, kcol].astype(F32), kqw).astype(BF16)
            v_cur = vc_ref[rows, kcol]
            k_ctx = jnp.concatenate([k_prev, k_cur], axis=0)
            v_ctx = jnp.concatenate([v_prev, v_cur], axis=0)

            qg = q_ref[rows, gcols]
            qs = jnp.concatenate([qg[:, r * HEAD_DIM:(r + 1) * HEAD_DIM] for r in range(Q_PER_KV)],
                                 axis=0)
            qf = qg.astype(F32)
            q2 = qf * qf
            q2_hi = q2.astype(BF16)
            q2_lo = (q2 - q2_hi.astype(F32)).astype(BF16)
            ssq = lax.dot_general(head_sel, jnp.concatenate([q2_hi, q2_lo], axis=1), _NT,
                                  preferred_element_type=F32)
            rms = lax.rsqrt(ssq * (1.0 / HEAD_DIM) + NORM_EPS)
            rms_q = jnp.concatenate([rms[r:r + 1] for r in range(Q_PER_KV)], axis=1)

            s = lax.dot_general(k_ctx, qs, _NT, preferred_element_type=F32)
            s_prev = s[:WINDOW] + no_prev if n == 0 else s[:WINDOW]
            s = jnp.where(from_prev, s_prev, s[WINDOW:]) * rms_q
            mx = jnp.maximum(jnp.max(s, axis=0, keepdims=True), sink)
            p = jnp.exp2(s - mx)
            den = jnp.sum(p, axis=0, keepdims=True) + jnp.exp2(sink - mx)
            p_ctx = jnp.concatenate([jnp.where(from_prev, p, 0.0), jnp.where(from_prev, 0.0, p)],
                                    axis=0).astype(BF16)
            o_t = lax.dot_general(v_ctx, p_ctx, _TN, preferred_element_type=F32) * (1.0 / den)
            halves = [jnp.concatenate([o_t[:, (2 * h) * WINDOW:(2 * h + 1) * WINDOW],
                                       o_t[:, (2 * h + 1) * WINDOW:(2 * h + 2) * WINDOW]], axis=0).T
                      for h in range(Q_PER_KV // 2)]
            og = jnp.concatenate(halves, axis=1)
            gate = g_ref[rows, gcols].astype(F32)
            o_ref[rows, gcols] = (og * (gate * jax.nn.sigmoid(gate))).astype(BF16)
            k_prev, v_prev = k_cur, v_cur


def _swa(proj, sinks, q_norm_w, k_norm_w, batch, seq, tq=512):
    n_sub = tq // WINDOW
    tiles = seq // tq
    m = batch * seq

    def cur_rows(b, i):
        return b * tiles + i

    def prev_rows(b, i):
        return jnp.maximum((b * tiles + i) * n_sub - 1, 0)

    kernel = functools.partial(_swa_kernel, n_sub=n_sub)
    return pl.pallas_call(
        kernel,
        grid=(batch, tiles),
        in_specs=[
            pl.BlockSpec(memory_space=pltpu.SMEM),
            pl.BlockSpec((tq, ATTN_W), lambda b, i: (cur_rows(b, i), COL_Q // ATTN_W)),
            pl.BlockSpec((tq, ATTN_W), lambda b, i: (cur_rows(b, i), COL_GATE // ATTN_W)),
            pl.BlockSpec((tq, KV_W), lambda b, i: (cur_rows(b, i), COL_K // KV_W)),
            pl.BlockSpec((tq, KV_W), lambda b, i: (cur_rows(b, i), COL_V // KV_W)),
            pl.BlockSpec((WINDOW, KV_W), lambda b, i: (prev_rows(b, i), COL_K // KV_W)),
            pl.BlockSpec((WINDOW, KV_W), lambda b, i: (prev_rows(b, i), COL_V // KV_W)),
            pl.BlockSpec((1, HEAD_DIM), lambda b, i: (0, 0)),
            pl.BlockSpec((1, HEAD_DIM), lambda b, i: (0, 0)),
        ],
        out_specs=pl.BlockSpec((tq, ATTN_W), lambda b, i: (cur_rows(b, i), 0)),
        out_shape=jax.ShapeDtypeStruct((m, ATTN_W), BF16),
        compiler_params=pltpu.CompilerParams(
            dimension_semantics=("arbitrary", "arbitrary"), vmem_limit_bytes=VMEM_LIMIT),
        name="swa",
    )(sinks, proj, proj, proj, proj, proj, proj,
      q_norm_w.reshape(1, HEAD_DIM), k_norm_w.reshape(1, HEAD_DIM))


def _slab_powers(ar, ai, dt, btr, bti):
    dta_re, dta_im = dt * ar, dt * ai
    mag = jnp.exp(dta_re)
    ab_re, ab_im = mag * jnp.cos(dta_im), mag * jnp.sin(dta_im)
    pw = [(jnp.ones_like(ar), jnp.zeros_like(ar))]
    for _ in range(CHUNK):
        pr, pi = pw[-1]
        pw.append((pr * ab_re - pi * ab_im, pr * ab_im + pi * ab_re))
    den = ar * ar + ai * ai
    num_re, num_im = ab_re - 1.0, ab_im
    cf_re = (num_re * ar + num_im * ai) / den
    cf_im = (num_im * ar - num_re * ai) / den
    bb_re = cf_re * btr - cf_im * bti
    bb_im = cf_re * bti + cf_im * btr
    abb = [(pr * bb_re - pi * bb_im, pr * bb_im + pi * bb_re) for pr, pi in pw[:CHUNK]]
    return pw, abb


def _build_state_in(abb, p_s):
    row_g = lax.broadcasted_iota(jnp.int32, (LANES, STATE), 0) // GROUP
    even = row_g % 2 == 0
    pair_of_row = lax.broadcasted_iota(jnp.int32, (LANES, 2 * LANES), 0) // (2 * GROUP)
    for lag in range(CHUNK):
        t = CHUNK - 1 - lag
        re, im = abb[lag]
        blk = jnp.concatenate([jnp.where(even, re, 0.0), jnp.where(even, 0.0, re),
                               jnp.where(even, im, 0.0), jnp.where(even, 0.0, im)], axis=1)
        for k in range(PAIRS):
            p_s[k, t * LANES:(t + 1) * LANES, :] = jnp.where(pair_of_row == k, blk, 0.0).astype(BF16)


def _build_state_out(pw, cr, ci, q_s):
    lane_g = lax.broadcasted_iota(jnp.int32, (STATE, LANES), 1) // GROUP
    for t in range(CHUNK):
        pr, pi = pw[t + 1]
        cat = jnp.concatenate([cr * pr - ci * pi, -(cr * pi + ci * pr)], axis=1)
        cat_t = cat.T
        for k in range(PAIRS):
            for ri in range(2):
                for half in range(2):
                    piece = jnp.where(lane_g == 2 * k + half, cat_t[ri * STATE:(ri + 1) * STATE], 0.0)
                    r0 = ri * LANES + half * STATE
                    q_s[k, r0:r0 + STATE, t * LANES:(t + 1) * LANES] = piece.astype(BF16)


def _build_toeplitz(abb, cr, ci, d_row, w_s):
    rhs = jnp.concatenate([cr, -ci], axis=1)
    row = lax.broadcasted_iota(jnp.int32, (LANES, LANES), 0)
    col = lax.broadcasted_iota(jnp.int32, (LANES, LANES), 1)
    same_group = (row // GROUP) == (col // GROUP)
    taps = []
    for lag in range(CHUNK):
        lhs = jnp.concatenate(abb[lag], axis=1)
        k = lax.dot_general(lhs, rhs, (((1,), (1,)), ((), ())), precision=lax.Precision.HIGHEST,
                            preferred_element_type=F32)
        k = jnp.where(same_group, k, 0.0)
        if lag == 0:
            k = k + jnp.where(row == col, d_row, 0.0)
        taps.append(k.astype(BF16))
    zero = jnp.zeros((LANES, LANES), BF16)
    for d in range(N_TILES):
        w_s[d, :LANES, :LANES] = taps[2 * d]
        w_s[d, :LANES, LANES:] = taps[2 * d + 1]
        w_s[d, LANES:, :LANES] = taps[2 * d - 1] if d > 0 else zero
        w_s[d, LANES:, LANES:] = taps[2 * d]


def _ssm_kernel(u_ref, ar_ref, ai_ref, dt_ref, btr_ref, bti_ref, cr_ref, ci_ref, d_ref,
                dtare_ref, dtaim_ref, y_ref, xre_s, xim_s, p_s, q_s, w_s, *, batch, n_chunks):
    step = pl.program_id(0)
    rows8 = batch * PAIRS
    slab_rows = n_chunks * rows8

    def state_rows(slab, b, k):
        return pl.ds(slab * slab_rows + b * PAIRS + k, n_chunks, stride=rows8)

    def powers():
        return _slab_powers(ar_ref[...], ai_ref[...], dt_ref[...], btr_ref[...], bti_ref[...])

    @pl.when(step < N_SLABS)
    def _():
        _, abb = powers()
        _build_state_in(abb, p_s)
        u = u_ref[...]
        for k in range(PAIRS):
            x = _dot(u, p_s[k])
            for b in range(batch):
                xb = x[b * n_chunks:(b + 1) * n_chunks]
                xre_s[state_rows(step, b, k), :] = xb[:, :LANES]
                xim_s[state_rows(step, b, k), :] = xb[:, LANES:]

    @pl.when(step == N_SLABS)
    def _():
        dta_re = jnp.concatenate([dtare_ref[...]] * batch, axis=1).reshape(N_SLABS * rows8, LANES)
        dta_im = jnp.concatenate([dtaim_ref[...]] * batch, axis=1).reshape(N_SLABS * rows8, LANES)
        mag = jnp.exp(CHUNK * dta_re)
        a_re, a_im = mag * jnp.cos(CHUNK * dta_im), mag * jnp.sin(CHUNK * dta_im)

        def body(c, carry):
            s_re, s_im = carry
            x_re, x_im = [], []
            for sl in range(N_SLABS):
                off = pl.multiple_of(sl * slab_rows + c * rows8, rows8)
                x_re.append(xre_s[pl.ds(off, rows8), :])
                x_im.append(xim_s[pl.ds(off, rows8), :])
                xre_s[pl.ds(off, rows8), :] = s_re[sl * rows8:(sl + 1) * rows8]
                xim_s[pl.ds(off, rows8), :] = s_im[sl * rows8:(sl + 1) * rows8]
            x_re = jnp.concatenate(x_re, axis=0)
            x_im = jnp.concatenate(x_im, axis=0)
            return (a_re * s_re - a_im * s_im + x_re, a_re * s_im + a_im * s_re + x_im)

        zero = jnp.zeros((N_SLABS * rows8, LANES), F32)
        lax.fori_loop(0, n_chunks, body, (zero, zero))

    @pl.when(step > N_SLABS)
    def _():
        slab = step - (N_SLABS + 1)
        pw, abb = powers()
        cr, ci = cr_ref[...], ci_ref[...]
        _build_state_out(pw, cr, ci, q_s)
        _build_toeplitz(abb, cr, ci, d_ref[...], w_s)
        u = u_ref[...]
        y_state = None
        for k in range(PAIRS):
            s_in = jnp.concatenate(
                [jnp.concatenate([xre_s[state_rows(slab, b, k), :], xim_s[state_rows(slab, b, k), :]],
                                 axis=1) for b in range(batch)], axis=0).astype(BF16)
            part = _dot(s_in, q_s[k])
            y_state = part if y_state is None else y_state + part
        for t2 in range(N_TILES):
            acc = y_state[:, t2 * MXU:(t2 + 1) * MXU]
            for t1 in range(t2 + 1):
                acc = acc + _dot(u[:, t1 * MXU:(t1 + 1) * MXU], w_s[t2 - t1])
            y_ref[:, t2 * MXU:(t2 + 1) * MXU] = jax.nn.gelu(acc).astype(BF16)


def _ssm(u_chunks, A_re, A_im, log_dt, B_re, B_im, C_re, C_im, D_skip, batch):
    _, rows, _ = u_chunks.shape
    n_chunks = rows // batch
    n_steps = 2 * N_SLABS + 1
    kernel = functools.partial(_ssm_kernel, batch=batch, n_chunks=n_chunks)

    dt = jnp.exp(log_dt)[:, None]
    per_row = lambda v: jnp.repeat(v, GROUP, axis=0)
    ar, ai = per_row(A_re), per_row(A_im)
    dtr = per_row(jnp.broadcast_to(dt, A_re.shape))
    btr = B_re.transpose(0, 2, 1).reshape(SSM_W, STATE)
    bti = B_im.transpose(0, 2, 1).reshape(SSM_W, STATE)
    cr, ci = C_re.reshape(SSM_W, STATE), C_im.reshape(SSM_W, STATE)
    d_rows = D_skip.reshape(N_SLABS, 1, LANES)
    dta_re = (dt * A_re).reshape(N_SLABS, PAIRS, LANES)
    dta_im = (dt * A_im).reshape(N_SLABS, PAIRS, LANES)

    def slab_in(s):
        return jnp.where(s <= N_SLABS, jnp.minimum(s, N_SLABS - 1), s - (N_SLABS + 1))

    def slab_out(s):
        return jnp.maximum(s - (N_SLABS + 1), 0)

    coeff = pl.BlockSpec((LANES, STATE), lambda s: (slab_in(s), 0))
    whole = pl.BlockSpec((N_SLABS, PAIRS, LANES), lambda s: (0, 0, 0))
    return pl.pallas_call(
        kernel,
        grid=(n_steps,),
        in_specs=[
            pl.BlockSpec((None, rows, CHUNK_W), lambda s: (slab_in(s), 0, 0)),
            coeff, coeff, coeff, coeff, coeff, coeff, coeff,
            pl.BlockSpec((None, 1, LANES), lambda s: (slab_in(s), 0, 0)),
            whole, whole,
        ],
        out_specs=pl.BlockSpec((None, rows, CHUNK_W), lambda s: (slab_out(s), 0, 0)),
        out_shape=jax.ShapeDtypeStruct(u_chunks.shape, BF16),
        scratch_shapes=[
            pltpu.VMEM((N_SLABS * n_chunks * batch * PAIRS, LANES), F32),
            pltpu.VMEM((N_SLABS * n_chunks * batch * PAIRS, LANES), F32),
            pltpu.VMEM((PAIRS, CHUNK_W, 2 * LANES), BF16),
            pltpu.VMEM((PAIRS, 2 * LANES, CHUNK_W), BF16),
            pltpu.VMEM((N_TILES, MXU, MXU), BF16),
        ],
        compiler_params=pltpu.CompilerParams(
            dimension_semantics=("arbitrary",), vmem_limit_bytes=VMEM_LIMIT),
        name="ssm",
    )(u_chunks, ar, ai, dtr, btr, bti, cr, ci, d_rows, dta_re, dta_im)


def _tail_kernel(ag_ref, yc_ref, z_ref, ga_ref, gs_ref, x_ref, wa_ref, wg_ref, bg_ref, ws_ref,
                 wo_ref, o_ref, ys_ref, *, tm):
    rows = tm // CHUNK
    for s in range(N_SLABS):
        for t in range(CHUNK):
            ys_ref[pl.ds(s * tm + t, rows, stride=CHUNK), :] = (
                yc_ref[s, :, t * LANES:(t + 1) * LANES].astype(F32))
    y_g = jnp.concatenate([ys_ref[s * tm:(s + 1) * tm, :] for s in range(N_SLABS)],
                          axis=1).astype(BF16)

    y_a = _dot(ag_ref[...], wa_ref[...])
    glu = _dot(y_g, wg_ref[...]) + bg_ref[...]
    z = z_ref[...].astype(F32)
    t = glu[:, :SSM_W] * jax.nn.sigmoid(glu[:, SSM_W:]) * (z * jax.nn.sigmoid(z))
    y_s = _dot(t.astype(BF16), ws_ref[...])
    merged = (jax.nn.sigmoid(ga_ref[...].astype(F32)) * y_a
              + jax.nn.sigmoid(gs_ref[...].astype(F32)) * y_s)
    o_ref[...] = x_ref[...] + _dot(merged.astype(BF16), wo_ref[...])


def _tail(attn_g, y_chunks, proj, x2, w_attn, w_glu, b_glu, w_ssm, w_out, tm=256):
    m = x2.shape[0]
    row = lambda blk: (lambda i: (i, blk))
    const = lambda i: (0, 0)
    resident = functools.partial(pl.BlockSpec, index_map=const, pipeline_mode=pl.Buffered(1))
    kernel = functools.partial(_tail_kernel, tm=tm)
    return pl.pallas_call(
        kernel,
        grid=(m // tm,),
        in_specs=[
            pl.BlockSpec((tm, ATTN_W), row(0)),
            pl.BlockSpec((N_SLABS, tm // CHUNK, CHUNK_W), lambda i: (0, i, 0)),
            pl.BlockSpec((tm, SSM_W), row(COL_Z // SSM_W)),
            pl.BlockSpec((tm, D_MODEL), row(COL_GA // D_MODEL)),
            pl.BlockSpec((tm, D_MODEL), row(COL_GS // D_MODEL)),
            pl.BlockSpec((tm, D_MODEL), row(0)),
            resident((ATTN_W, D_MODEL)),
            resident((SSM_W, 2 * SSM_W)),
            resident((1, 2 * SSM_W)),
            resident((SSM_W, D_MODEL)),
            resident((D_MODEL, D_MODEL)),
        ],
        out_specs=pl.BlockSpec((tm, D_MODEL), row(0)),
        out_shape=jax.ShapeDtypeStruct((m, D_MODEL), F32),
        scratch_shapes=[pltpu.VMEM((N_SLABS * tm, LANES), F32)],
        compiler_params=pltpu.CompilerParams(
            dimension_semantics=("arbitrary",), vmem_limit_bytes=VMEM_LIMIT),
        name="tail",
    )(attn_g, y_chunks, proj, proj, proj, x2, w_attn, w_glu, b_glu.reshape(1, -1), w_ssm, w_out)


def kernel(x, norm_w, w_in, q_norm_w, k_norm_w, sinks, w_attn_proj, A_re, A_im, log_dt, B_re, B_im,
           C_re, C_im, D_skip, w_glu, b_glu, w_ssm_proj, w_out):
    batch, seq, _ = x.shape
    m = batch * seq
    x2 = x.reshape(m, D_MODEL)

    proj, u_chunks = _in_proj(x2, norm_w, w_in)

    attn_g = _swa(proj, sinks, q_norm_w, k_norm_w, batch, seq)

    y_chunks = _ssm(u_chunks, A_re, A_im, log_dt, B_re, B_im, C_re, C_im, D_skip, batch)

    out = _tail(attn_g, y_chunks, proj, x2, w_attn_proj.astype(BF16), w_glu.astype(BF16), b_glu,
                w_ssm_proj.astype(BF16), w_out.astype(BF16))
    return out.reshape(batch, seq, D_MODEL)
```

```python
import functools
import math

import jax
import jax.numpy as jnp
from jax import lax
from jax.experimental import pallas as pl
from jax.experimental.pallas import tpu as pltpu

F32 = jnp.float32
BF16 = jnp.bfloat16

D_MODEL = 2048
HEAD_DIM = 64
N_Q_HEADS = 16
N_KV_HEADS = 4
Q_PER_KV = 4
ATTN_W = N_Q_HEADS * HEAD_DIM
KV_W = N_KV_HEADS * HEAD_DIM
WINDOW = 128
SSM_W = D_MODEL // 2
GROUP = 16
N_GROUPS = SSM_W // GROUP
STATE = 64
NORM_EPS = 1e-6

COL_Q, COL_GATE, COL_U, COL_Z = 0, 1024, 2048, 3072
COL_GA, COL_GS, COL_K, COL_V = 4096, 6144, 8192, 8448
IN_W = 8704

LANES = 128
MXU = 256
CHUNK = 16
SLAB_G = LANES // GROUP
N_SLABS = N_GROUPS // SLAB_G
PAIRS = SLAB_G // 2
CHUNK_W = CHUNK * LANES
N_TILES = CHUNK_W // MXU
VMEM_LIMIT = 48 * 1024 * 1024
IN_PROJ_VMEM_LIMIT = 58 * 1024 * 1024


def _dot(a, b):
    return jnp.dot(a, b, preferred_element_type=F32)


IN_TN = 512
N_IN_TILES = IN_W // IN_TN
SRC_KV_TILE = (ATTN_W) // IN_TN
SRC_U_TILE0 = (ATTN_W + 2 * KV_W + ATTN_W) // IN_TN


X_PREFETCH_TILES = (4, 10)


def _dst_tile(src_tile):
    if src_tile < SRC_KV_TILE:
        return src_tile
    if src_tile == SRC_KV_TILE:
        return N_IN_TILES - 1
    return src_tile - 1


def _in_proj_kernel(nw_ref, x_hbm, w_hbm, o_hbm, uc_ref, h_ref, x_buf, w_buf, o_buf, us_ref, x_sem,
                    w_sem, o_sem, *, tm, n_steps):
    i = pl.program_id(0)
    rows = tm // CHUNK
    n_sl = IN_TN // LANES
    half = tm // 2

    def x_copy(step, s):
        src = x_hbm.at[pl.ds(pl.multiple_of(step * tm + s * half, half), half), :]
        return pltpu.make_async_copy(src, x_buf.at[s], x_sem.at[s])

    def w_copy(t):
        return pltpu.make_async_copy(w_hbm.at[:, pl.ds(t * IN_TN, IN_TN)], w_buf.at[t % 2],
                                     w_sem.at[t % 2])

    def o_copy(t):
        dst = o_hbm.at[pl.ds(pl.multiple_of(i * tm, tm), tm), pl.ds(_dst_tile(t) * IN_TN, IN_TN)]
        return pltpu.make_async_copy(o_buf.at[t % 2], dst, o_sem.at[t % 2])

    @pl.when(i == 0)
    def _():
        x_copy(0, 0).start()
        x_copy(0, 1).start()

    w_copy(0).start()
    for s in range(2):
        x_copy(i, s).wait()
        x = x_buf[s]
        ms = jnp.mean(x * x, axis=-1, keepdims=True)
        h_ref[s * half:(s + 1) * half, :] = (x * lax.rsqrt(ms + NORM_EPS) * nw_ref[...]).astype(BF16)


    def finish(t, acc):
        o_buf[t % 2] = acc.astype(BF16)
        if SRC_U_TILE0 <= t < SRC_U_TILE0 + SSM_W // IN_TN:
            slab0 = (t - SRC_U_TILE0) * n_sl
            for s in range(n_sl):
                us_ref[s * tm:(s + 1) * tm, :] = acc[:, s * LANES:(s + 1) * LANES]
            for s in range(n_sl):
                for tok in range(CHUNK):
                    piece = us_ref[pl.ds(s * tm + tok, rows, stride=CHUNK), :]
                    uc_ref[slab0 + s, :, tok * LANES:(tok + 1) * LANES] = piece.astype(BF16)
        o_copy(t).start()

    for t in range(N_IN_TILES):
        if t + 1 < N_IN_TILES:
            w_copy(t + 1).start()
        w_copy(t).wait()
        if t >= 2:
            o_copy(t - 2).wait()
        if t in X_PREFETCH_TILES:
            @pl.when(i + 1 < n_steps)
            def _(s=X_PREFETCH_TILES.index(t)):
                x_copy(i + 1, s).start()
        acc = None
        for k in range(D_MODEL // MXU):
            ks = slice(k * MXU, (k + 1) * MXU)
            part = _dot(h_ref[:, ks], w_buf[t % 2, ks, :].astype(BF16))
            acc = part if acc is None else acc + part
        finish(t, acc)

    o_copy(N_IN_TILES - 2).wait()
    o_copy(N_IN_TILES - 1).wait()


def _in_proj(x2, norm_w, w_in, tm=2048):
    m = x2.shape[0]
    n_steps = m // tm
    kernel = functools.partial(_in_proj_kernel, tm=tm, n_steps=n_steps)
    return pl.pallas_call(
        kernel,
        grid=(n_steps,),
        in_specs=[
            pl.BlockSpec((1, D_MODEL), lambda i: (0, 0)),
            pl.BlockSpec(memory_space=pl.ANY),
            pl.BlockSpec(memory_space=pl.ANY),
        ],
        out_specs=[
            pl.BlockSpec(memory_space=pl.ANY),
            pl.BlockSpec((N_SLABS, tm // CHUNK, CHUNK_W), lambda i: (0, i, 0)),
        ],
        out_shape=[
            jax.ShapeDtypeStruct((m, IN_W), BF16),
            jax.ShapeDtypeStruct((N_SLABS, m // CHUNK, CHUNK_W), BF16),
        ],
        scratch_shapes=[
            pltpu.VMEM((tm, D_MODEL), BF16),
            pltpu.VMEM((2, tm // 2, D_MODEL), F32),
            pltpu.VMEM((2, D_MODEL, IN_TN), F32),
            pltpu.VMEM((2, tm, IN_TN), BF16),
            pltpu.VMEM((IN_TN // LANES * tm, LANES), F32),
            pltpu.SemaphoreType.DMA((2,)),
            pltpu.SemaphoreType.DMA((2,)),
            pltpu.SemaphoreType.DMA((2,)),
        ],
        compiler_params=pltpu.CompilerParams(
            dimension_semantics=("arbitrary",), vmem_limit_bytes=IN_PROJ_VMEM_LIMIT),
        name="in_proj",
    )(norm_w.reshape(1, D_MODEL), x2, w_in)


def _head_norm(t, w):
    ms = jnp.mean(t * t, axis=-1, keepdims=True)
    return t * lax.rsqrt(ms + NORM_EPS) * w


_NT = (((1,), (1,)), ((), ()))
_TN = (((0,), (0,)), ((), ()))


def _swa_kernel(sink_ref, q_ref, g_ref, kc_ref, vc_ref, kp_ref, vp_ref, qw_ref, kw_ref, o_ref,
                *, n_sub):
    log2e = math.log2(math.e)
    kqw = kw_ref[...] * qw_ref[...] * (log2e / math.sqrt(HEAD_DIM))
    n_col = Q_PER_KV * WINDOW
    key = lax.broadcasted_iota(jnp.int32, (WINDOW, n_col), 0)
    qry = lax.broadcasted_iota(jnp.int32, (WINDOW, n_col), 1) % WINDOW
    from_prev = key > qry
    no_prev = jnp.where(pl.program_id(1) == 0, -1e30, 0.0)
    head_of_col = lax.broadcasted_iota(jnp.int32, (1, n_col), 1) // WINDOW
    gw = Q_PER_KV * HEAD_DIM
    sel_r = lax.broadcasted_iota(jnp.int32, (8, 2 * gw), 0)
    sel_l = (lax.broadcasted_iota(jnp.int32, (8, 2 * gw), 1) % gw) // HEAD_DIM
    head_sel = jnp.where(sel_r == sel_l, 1.0, 0.0).astype(BF16)

    for g in range(N_KV_HEADS):
        kcol = slice(g * HEAD_DIM, (g + 1) * HEAD_DIM)
        gcols = slice(g * Q_PER_KV * HEAD_DIM, (g + 1) * Q_PER_KV * HEAD_DIM)
        sink = jnp.zeros((1, n_col), F32)
        for r in range(Q_PER_KV):
            sink = jnp.where(head_of_col == r, sink_ref[g * Q_PER_KV + r] * log2e, sink)
        k_prev = _head_norm(kp_ref[:, kcol].astype(F32), kqw).astype(BF16)
        v_prev = vp_ref[:, kcol]
        for n in range(n_sub):
            rows = slice(n * WINDOW, (n + 1) * WINDOW)
            k_cur = _head_norm(kc_ref[rows, kcol].astype(F32), kqw).astype(BF16)
            v_cur = vc_ref[rows, kcol]
            k_ctx = jnp.concatenate([k_prev, k_cur], axis=0)
            v_ctx = jnp.concatenate([v_prev, v_cur], axis=0)

            qg = q_ref[rows, gcols]
            qs = jnp.concatenate([qg[:, r * HEAD_DIM:(r + 1) * HEAD_DIM] for r in range(Q_PER_KV)],
                                 axis=0)
            qf = qg.astype(F32)
            q2 = qf * qf
            q2_hi = q2.astype(BF16)
            q2_lo = (q2 - q2_hi.astype(F32)).astype(BF16)
            ssq = lax.dot_general(head_sel, jnp.concatenate([q2_hi, q2_lo], axis=1), _NT,
                                  preferred_element_type=F32)
            rms = lax.rsqrt(ssq * (1.0 / HEAD_DIM) + NORM_EPS)
            rms_q = jnp.concatenate([rms[r:r + 1] for r in range(Q_PER_KV)], axis=1)

            s = lax.dot_general(k_ctx, qs, _NT, preferred_element_type=F32)
            s_prev = s[:WINDOW] + no_prev if n == 0 else s[:WINDOW]
            s = jnp.where(from_prev, s_prev, s[WINDOW:]) * rms_q
            mx = jnp.maximum(jnp.max(s, axis=0, keepdims=True), sink)
            p = jnp.exp2(s - mx)
            den = jnp.sum(p, axis=0, keepdims=True) + jnp.exp2(sink - mx)
            p_ctx = jnp.concatenate([jnp.where(from_prev, p, 0.0), jnp.where(from_prev, 0.0, p)],
                                    axis=0).astype(BF16)
            o_t = lax.dot_general(v_ctx, p_ctx, _TN, preferred_element_type=F32) * (1.0 / den)
            halves = [jnp.concatenate([o_t[:, (2 * h) * WINDOW:(2 * h + 1) * WINDOW],
                                       o_t[:, (2 * h + 1) * WINDOW:(2 * h + 2) * WINDOW]], axis=0).T
                      for h in range(Q_PER_KV // 2)]
            og = jnp.concatenate(halves, axis=1)
            gate = g_ref[rows, gcols].astype(F32)
            o_ref[rows, gcols] = (og * (gate * jax.nn.sigmoid(gate))).astype(BF16)
            k_prev, v_prev = k_cur, v_cur


def _swa(proj, sinks, q_norm_w, k_norm_w, batch, seq, tq=512):
    n_sub = tq // WINDOW
    tiles = seq // tq
    m = batch * seq

    def cur_rows(b, i):
        return b * tiles + i

    def prev_rows(b, i):
        return jnp.maximum((b * tiles + i) * n_sub - 1, 0)

    kernel = functools.partial(_swa_kernel, n_sub=n_sub)
    return pl.pallas_call(
        kernel,
        grid=(batch, tiles),
        in_specs=[
            pl.BlockSpec(memory_space=pltpu.SMEM),
            pl.BlockSpec((tq, ATTN_W), lambda b, i: (cur_rows(b, i), COL_Q // ATTN_W)),
            pl.BlockSpec((tq, ATTN_W), lambda b, i: (cur_rows(b, i), COL_GATE // ATTN_W)),
            pl.BlockSpec((tq, KV_W), lambda b, i: (cur_rows(b, i), COL_K // KV_W)),
            pl.BlockSpec((tq, KV_W), lambda b, i: (cur_rows(b, i), COL_V // KV_W)),
            pl.BlockSpec((WINDOW, KV_W), lambda b, i: (prev_rows(b, i), COL_K // KV_W)),
            pl.BlockSpec((WINDOW, KV_W), lambda b, i: (prev_rows(b, i), COL_V // KV_W)),
            pl.BlockSpec((1, HEAD_DIM), lambda b, i: (0, 0)),
            pl.BlockSpec((1, HEAD_DIM), lambda b, i: (0, 0)),
        ],
        out_specs=pl.BlockSpec((tq, ATTN_W), lambda b, i: (cur_rows(b, i), 0)),
        out_shape=jax.ShapeDtypeStruct((m, ATTN_W), BF16),
        compiler_params=pltpu.CompilerParams(
            dimension_semantics=("arbitrary", "arbitrary"), vmem_limit_bytes=VMEM_LIMIT),
        name="swa",
    )(sinks, proj, proj, proj, proj, proj, proj,
      q_norm_w.reshape(1, HEAD_DIM), k_norm_w.reshape(1, HEAD_DIM))


def _slab_powers(ar, ai, dt, btr, bti):
    dta_re, dta_im = dt * ar, dt * ai
    mag = jnp.exp(dta_re)
    ab_re, ab_im = mag * jnp.cos(dta_im), mag * jnp.sin(dta_im)
    pw = [(jnp.ones_like(ar), jnp.zeros_like(ar))]
    for _ in range(CHUNK):
        pr, pi = pw[-1]
        pw.append((pr * ab_re - pi * ab_im, pr * ab_im + pi * ab_re))
    den = ar * ar + ai * ai
    num_re, num_im = ab_re - 1.0, ab_im
    cf_re = (num_re * ar + num_im * ai) / den
    cf_im = (num_im * ar - num_re * ai) / den
    bb_re = cf_re * btr - cf_im * bti
    bb_im = cf_re * bti + cf_im * btr
    abb = [(pr * bb_re - pi * bb_im, pr * bb_im + pi * bb_re) for pr, pi in pw[:CHUNK]]
    return pw, abb


def _build_state_in(abb, p_s):
    row_g = lax.broadcasted_iota(jnp.int32, (LANES, STATE), 0) // GROUP
    even = row_g % 2 == 0
    pair_of_row = lax.broadcasted_iota(jnp.int32, (LANES, 2 * LANES), 0) // (2 * GROUP)
    for lag in range(CHUNK):
        t = CHUNK - 1 - lag
        re, im = abb[lag]
        blk = jnp.concatenate([jnp.where(even, re, 0.0), jnp.where(even, 0.0, re),
                               jnp.where(even, im, 0.0), jnp.where(even, 0.0, im)], axis=1)
        for k in range(PAIRS):
            p_s[k, t * LANES:(t + 1) * LANES, :] = jnp.where(pair_of_row == k, blk, 0.0).astype(BF16)


def _build_state_out(pw, cr, ci, q_s):
    lane_g = lax.broadcasted_iota(jnp.int32, (STATE, LANES), 1) // GROUP
    for t in range(CHUNK):
        pr, pi = pw[t + 1]
        cat = jnp.concatenate([cr * pr - ci * pi, -(cr * pi + ci * pr)], axis=1)
        cat_t = cat.T
        for k in range(PAIRS):
            for ri in range(2):
                for half in range(2):
                    piece = jnp.where(lane_g == 2 * k + half, cat_t[ri * STATE:(ri + 1) * STATE], 0.0)
                    r0 = ri * LANES + half * STATE
                    q_s[k, r0:r0 + STATE, t * LANES:(t + 1) * LANES] = piece.astype(BF16)


def _build_toeplitz(abb, cr, ci, d_row, w_s):
    rhs = jnp.concatenate([cr, -ci], axis=1)
    row = lax.broadcasted_iota(jnp.int32, (LANES, LANES), 0)
    col = lax.broadcasted_iota(jnp.int32, (LANES, LANES), 1)
    same_group = (row // GROUP) == (col // GROUP)
    taps = []
    for lag in range(CHUNK):
        lhs = jnp.concatenate(abb[lag], axis=1)
        k = lax.dot_general(lhs, rhs, (((1,), (1,)), ((), ())), precision=lax.Precision.HIGHEST,
                            preferred_element_type=F32)
        k = jnp.where(same_group, k, 0.0)
        if lag == 0:
            k = k + jnp.where(row == col, d_row, 0.0)
        taps.append(k.astype(BF16))
    zero = jnp.zeros((LANES, LANES), BF16)
    for d in range(N_TILES):
        w_s[d, :LANES, :LANES] = taps[2 * d]
        w_s[d, :LANES, LANES:] = taps[2 * d + 1]
        w_s[d, LANES:, :LANES] = taps[2 * d - 1] if d > 0 else zero
        w_s[d, LANES:, LANES:] = taps[2 * d]


def _ssm_kernel(u_ref, ar_ref, ai_ref, dt_ref, btr_ref, bti_ref, cr_ref, ci_ref, d_ref,
                dtare_ref, dtaim_ref, y_ref, xre_s, xim_s, p_s, q_s, w_s, *, batch, n_chunks):
    step = pl.program_id(0)
    rows8 = batch * PAIRS
    slab_rows = n_chunks * rows8

    def state_rows(slab, b, k):
        return pl.ds(slab * slab_rows + b * PAIRS + k, n_chunks, stride=rows8)

    def powers():
        return _slab_powers(ar_ref[...], ai_ref[...], dt_ref[...], btr_ref[...], bti_ref[...])

    @pl.when(step < N_SLABS)
    def _():
        _, abb = powers()
        _build_state_in(abb, p_s)
        u = u_ref[...]
        for k in range(PAIRS):
            x = _dot(u, p_s[k])
            for b in range(batch):
                xb = x[b * n_chunks:(b + 1) * n_chunks]
                xre_s[state_rows(step, b, k), :] = xb[:, :LANES]
                xim_s[state_rows(step, b, k), :] = xb[:, LANES:]

    @pl.when(step == N_SLABS)
    def _():
        dta_re = jnp.concatenate([dtare_ref[...]] * batch, axis=1).reshape(N_SLABS * rows8, LANES)
        dta_im = jnp.concatenate([dtaim_ref[...]] * batch, axis=1).reshape(N_SLABS * rows8, LANES)
        mag = jnp.exp(CHUNK * dta_re)
        a_re, a_im = mag * jnp.cos(CHUNK * dta_im), mag * jnp.sin(CHUNK * dta_im)

        def body(c, carry):
            s_re, s_im = carry
            x_re, x_im = [], []
            for sl in range(N_SLABS):
                off = pl.multiple_of(sl * slab_rows + c * rows8, rows8)
                x_re.append(xre_s[pl.ds(off, rows8), :])
                x_im.append(xim_s[pl.ds(off, rows8), :])
                xre_s[pl.ds(off, rows8), :] = s_re[sl * rows8:(sl + 1) * rows8]
                xim_s[pl.ds(off, rows8), :] = s_im[sl * rows8:(sl + 1) * rows8]
            x_re = jnp.concatenate(x_re, axis=0)
            x_im = jnp.concatenate(x_im, axis=0)
            return (a_re * s_re - a_im * s_im + x_re, a_re * s_im + a_im * s_re + x_im)

        zero = jnp.zeros((N_SLABS * rows8, LANES), F32)
        lax.fori_loop(0, n_chunks, body, (zero, zero))

    @pl.when(step > N_SLABS)
    def _():
        slab = step - (N_SLABS + 1)
        pw, abb = powers()
        cr, ci = cr_ref[...], ci_ref[...]
        _build_state_out(pw, cr, ci, q_s)
        _build_toeplitz(abb, cr, ci, d_ref[...], w_s)
        u = u_ref[...]
        y_state = None
        for k in range(PAIRS):
            s_in = jnp.concatenate(
                [jnp.concatenate([xre_s[state_rows(slab, b, k), :], xim_s[state_rows(slab, b, k), :]],
                                 axis=1) for b in range(batch)], axis=0).astype(BF16)
            part = _dot(s_in, q_s[k])
            y_state = part if y_state is None else y_state + part
        for t2 in range(N_TILES):
            acc = y_state[:, t2 * MXU:(t2 + 1) * MXU]
            for t1 in range(t2 + 1):
                acc = acc + _dot(u[:, t1 * MXU:(t1 + 1) * MXU], w_s[t2 - t1])
            y_ref[:, t2 * MXU:(t2 + 1) * MXU] = jax.nn.gelu(acc).astype(BF16)


def _ssm(u_chunks, A_re, A_im, log_dt, B_re, B_im, C_re, C_im, D_skip, batch):
    _, rows, _ = u_chunks.shape
    n_chunks = rows // batch
    n_steps = 2 * N_SLABS + 1
    kernel = functools.partial(_ssm_kernel, batch=batch, n_chunks=n_chunks)

    dt = jnp.exp(log_dt)[:, None]
    per_row = lambda v: jnp.repeat(v, GROUP, axis=0)
    ar, ai = per_row(A_re), per_row(A_im)
    dtr = per_row(jnp.broadcast_to(dt, A_re.shape))
    btr = B_re.transpose(0, 2, 1).reshape(SSM_W, STATE)
    bti = B_im.transpose(0, 2, 1).reshape(SSM_W, STATE)
    cr, ci = C_re.reshape(SSM_W, STATE), C_im.reshape(SSM_W, STATE)
    d_rows = D_skip.reshape(N_SLABS, 1, LANES)
    dta_re = (dt * A_re).reshape(N_SLABS, PAIRS, LANES)
    dta_im = (dt * A_im).reshape(N_SLABS, PAIRS, LANES)

    def slab_in(s):
        return jnp.where(s <= N_SLABS, jnp.minimum(s, N_SLABS - 1), s - (N_SLABS + 1))

    def slab_out(s):
        return jnp.maximum(s - (N_SLABS + 1), 0)

    coeff = pl.BlockSpec((LANES, STATE), lambda s: (slab_in(s), 0))
    whole = pl.BlockSpec((N_SLABS, PAIRS, LANES), lambda s: (0, 0, 0))
    return pl.pallas_call(
        kernel,
        grid=(n_steps,),
        in_specs=[
            pl.BlockSpec((None, rows, CHUNK_W), lambda s: (slab_in(s), 0, 0)),
            coeff, coeff, coeff, coeff, coeff, coeff, coeff,
            pl.BlockSpec((None, 1, LANES), lambda s: (slab_in(s), 0, 0)),
            whole, whole,
        ],
        out_specs=pl.BlockSpec((None, rows, CHUNK_W), lambda s: (slab_out(s), 0, 0)),
        out_shape=jax.ShapeDtypeStruct(u_chunks.shape, BF16),
        scratch_shapes=[
            pltpu.VMEM((N_SLABS * n_chunks * batch * PAIRS, LANES), F32),
            pltpu.VMEM((N_SLABS * n_chunks * batch * PAIRS, LANES), F32),
            pltpu.VMEM((PAIRS, CHUNK_W, 2 * LANES), BF16),
            pltpu.VMEM((PAIRS, 2 * LANES, CHUNK_W), BF16),
            pltpu.VMEM((N_TILES, MXU, MXU), BF16),
        ],
        compiler_params=pltpu.CompilerParams(
            dimension_semantics=("arbitrary",), vmem_limit_bytes=VMEM_LIMIT),
        name="ssm",
    )(u_chunks, ar, ai, dtr, btr, bti, cr, ci, d_rows, dta_re, dta_im)


def _tail_kernel(ag_ref, yc_ref, z_ref, ga_ref, gs_ref, x_ref, wa_ref, wg_ref, bg_ref, ws_ref,
                 wo_ref, o_ref, ys_ref, *, tm):
    rows = tm // CHUNK
    for s in range(N_SLABS):
        for t in range(CHUNK):
            ys_ref[pl.ds(s * tm + t, rows, stride=CHUNK), :] = (
                yc_ref[s, :, t * LANES:(t + 1) * LANES].astype(F32))
    y_g = jnp.concatenate([ys_ref[s * tm:(s + 1) * tm, :] for s in range(N_SLABS)],
                          axis=1).astype(BF16)

    y_a = _dot(ag_ref[...], wa_ref[...])
    glu = _dot(y_g, wg_ref[...]) + bg_ref[...]
    z = z_ref[...].astype(F32)
    t = glu[:, :SSM_W] * jax.nn.sigmoid(glu[:, SSM_W:]) * (z * jax.nn.sigmoid(z))
    y_s = _dot(t.astype(BF16), ws_ref[...])
    merged = (jax.nn.sigmoid(ga_ref[...].astype(F32)) * y_a
              + jax.nn.sigmoid(gs_ref[...].astype(F32)) * y_s)
    o_ref[...] = x_ref[...] + _dot(merged.astype(BF16), wo_ref[...])


def _tail(attn_g, y_chunks, proj, x2, w_attn, w_glu, b_glu, w_ssm, w_out, tm=256):
    m = x2.shape[0]
    row = lambda blk: (lambda i: (i, blk))
    const = lambda i: (0, 0)
    resident = functools.partial(pl.BlockSpec, index_map=const, pipeline_mode=pl.Buffered(1))
    kernel = functools.partial(_tail_kernel, tm=tm)
    return pl.pallas_call(
        kernel,
        grid=(m // tm,),
        in_specs=[
            pl.BlockSpec((tm, ATTN_W), row(0)),
            pl.BlockSpec((N_SLABS, tm // CHUNK, CHUNK_W), lambda i: (0, i, 0)),
            pl.BlockSpec((tm, SSM_W), row(COL_Z // SSM_W)),
            pl.BlockSpec((tm, D_MODEL), row(COL_GA // D_MODEL)),
            pl.BlockSpec((tm, D_MODEL), row(COL_GS // D_MODEL)),
            pl.BlockSpec((tm, D_MODEL), row(0)),
            resident((ATTN_W, D_MODEL)),
            resident((SSM_W, 2 * SSM_W)),
            resident((1, 2 * SSM_W)),
            resident((SSM_W, D_MODEL)),
            resident((D_MODEL, D_MODEL)),
        ],
        out_specs=pl.BlockSpec((tm, D_MODEL), row(0)),
        out_shape=jax.ShapeDtypeStruct((m, D_MODEL), F32),
        scratch_shapes=[pltpu.VMEM((N_SLABS * tm, LANES), F32)],
        compiler_params=pltpu.CompilerParams(
            dimension_semantics=("arbitrary",), vmem_limit_bytes=VMEM_LIMIT),
        name="tail",
    )(attn_g, y_chunks, proj, proj, proj, x2, w_attn, w_glu, b_glu.reshape(1, -1), w_ssm, w_out)


def kernel(x, norm_w, w_in, q_norm_w, k_norm_w, sinks, w_attn_proj, A_re, A_im, log_dt, B_re, B_im,
           C_re, C_im, D_skip, w_glu, b_glu, w_ssm_proj, w_out):
    batch, seq, _ = x.shape
    m = batch * seq
    x2 = x.reshape(m, D_MODEL)

    proj, u_chunks = _in_proj(x2, norm_w, w_in)

    attn_g = _swa(proj, sinks, q_norm_w, k_norm_w, batch, seq)

    y_chunks = _ssm(u_chunks, A_re, A_im, log_dt, B_re, B_im, C_re, C_im, D_skip, batch)

    out = _tail(attn_g, y_chunks, proj, x2, w_attn_proj.astype(BF16), w_glu.astype(BF16), b_glu,
                w_ssm_proj.astype(BF16), w_out.astype(BF16))
    return out.reshape(batch, seq, D_MODEL)
```

```python
import functools
import math

import jax
import jax.numpy as jnp
from jax import lax
from jax.experimental import pallas as pl
from jax.experimental.pallas import tpu as pltpu

F32 = jnp.float32
BF16 = jnp.bfloat16

D_MODEL = 2048
HEAD_DIM = 64
N_Q_HEADS = 16
N_KV_HEADS = 4
Q_PER_KV = 4
ATTN_W = N_Q_HEADS * HEAD_DIM
KV_W = N_KV_HEADS * HEAD_DIM
WINDOW = 128
SSM_W = D_MODEL // 2
GROUP = 16
N_GROUPS = SSM_W // GROUP
STATE = 64
NORM_EPS = 1e-6

COL_Q, COL_GATE, COL_U, COL_Z = 0, 1024, 2048, 3072
COL_GA, COL_GS, COL_K, COL_V = 4096, 6144, 8192, 8448
IN_W = 8704

LANES = 128
MXU = 256
CHUNK = 16
SLAB_G = LANES // GROUP
N_SLABS = N_GROUPS // SLAB_G
PAIRS = SLAB_G // 2
CHUNK_W = CHUNK * LANES
N_TILES = CHUNK_W // MXU
VMEM_LIMIT = 48 * 1024 * 1024


def _dot(a, b):
    return jnp.dot(a, b, preferred_element_type=F32)


IN_TN = 512
N_IN_TILES = IN_W // IN_TN
SRC_KV_TILE = (ATTN_W) // IN_TN
SRC_U_TILE0 = (ATTN_W + 2 * KV_W + ATTN_W) // IN_TN


def _dst_tile(src_tile):
    if src_tile < SRC_KV_TILE:
        return src_tile
    if src_tile == SRC_KV_TILE:
        return N_IN_TILES - 1
    return src_tile - 1


def _in_proj_kernel(x_ref, nw_ref, w_hbm, o_hbm, uc_ref, h_ref, w_buf, o_buf, us_ref, w_sem, o_sem,
                    *, tm):
    i = pl.program_id(0)
    rows = tm // CHUNK
    n_sl = IN_TN // LANES

    def w_copy(t):
        return pltpu.make_async_copy(w_hbm.at[:, pl.ds(t * IN_TN, IN_TN)], w_buf.at[t % 2],
                                     w_sem.at[t % 2])

    def o_copy(t):
        dst = o_hbm.at[pl.ds(pl.multiple_of(i * tm, tm), tm), pl.ds(_dst_tile(t) * IN_TN, IN_TN)]
        return pltpu.make_async_copy(o_buf.at[t % 2], dst, o_sem.at[t % 2])

    w_copy(0).start()
    x = x_ref[...]
    ms = jnp.mean(x * x, axis=-1, keepdims=True)
    h_ref[...] = (x * lax.rsqrt(ms + NORM_EPS) * nw_ref[...]).astype(BF16)


    def finish(t, acc):
        o_buf[t % 2] = acc.astype(BF16)
        if SRC_U_TILE0 <= t < SRC_U_TILE0 + SSM_W // IN_TN:
            slab0 = (t - SRC_U_TILE0) * n_sl
            for s in range(n_sl):
                us_ref[s * tm:(s + 1) * tm, :] = acc[:, s * LANES:(s + 1) * LANES]
            for s in range(n_sl):
                for tok in range(CHUNK):
                    piece = us_ref[pl.ds(s * tm + tok, rows, stride=CHUNK), :]
                    uc_ref[slab0 + s, :, tok * LANES:(tok + 1) * LANES] = piece.astype(BF16)
        o_copy(t).start()

    for t in range(N_IN_TILES):
        if t + 1 < N_IN_TILES:
            w_copy(t + 1).start()
        w_copy(t).wait()
        if t >= 2:
            o_copy(t - 2).wait()
        acc = None
        for k in range(D_MODEL // MXU):
            ks = slice(k * MXU, (k + 1) * MXU)
            part = _dot(h_ref[:, ks], w_buf[t % 2, ks, :].astype(BF16))
            acc = part if acc is None else acc + part
        finish(t, acc)

    o_copy(N_IN_TILES - 2).wait()
    o_copy(N_IN_TILES - 1).wait()


def _in_proj(x2, norm_w, w_in, tm=1024):
    m = x2.shape[0]
    kernel = functools.partial(_in_proj_kernel, tm=tm)
    return pl.pallas_call(
        kernel,
        grid=(m // tm,),
        in_specs=[
            pl.BlockSpec((tm, D_MODEL), lambda i: (i, 0)),
            pl.BlockSpec((1, D_MODEL), lambda i: (0, 0)),
            pl.BlockSpec(memory_space=pl.ANY),
        ],
        out_specs=[
            pl.BlockSpec(memory_space=pl.ANY),
            pl.BlockSpec((N_SLABS, tm // CHUNK, CHUNK_W), lambda i: (0, i, 0)),
        ],
        out_shape=[
            jax.ShapeDtypeStruct((m, IN_W), BF16),
            jax.ShapeDtypeStruct((N_SLABS, m // CHUNK, CHUNK_W), BF16),
        ],
        scratch_shapes=[
            pltpu.VMEM((tm, D_MODEL), BF16),
            pltpu.VMEM((2, D_MODEL, IN_TN), F32),
            pltpu.VMEM((2, tm, IN_TN), BF16),
            pltpu.VMEM((IN_TN // LANES * tm, LANES), F32),
            pltpu.SemaphoreType.DMA((2,)),
            pltpu.SemaphoreType.DMA((2,)),
        ],
        compiler_params=pltpu.CompilerParams(
            dimension_semantics=("arbitrary",), vmem_limit_bytes=VMEM_LIMIT),
        name="in_proj",
    )(x2, norm_w.reshape(1, D_MODEL), w_in)


def _head_norm(t, w):
    ms = jnp.mean(t * t, axis=-1, keepdims=True)
    return t * lax.rsqrt(ms + NORM_EPS) * w


_NT = (((1,), (1,)), ((), ()))
_TN = (((0,), (0,)), ((), ()))


def _swa_kernel(sink_ref, q_ref, g_ref, kc_ref, vc_ref, kp_ref, vp_ref, qw_ref, kw_ref, o_ref,
                *, n_sub):
    log2e = math.log2(math.e)
    kqw = kw_ref[...] * qw_ref[...] * (log2e / math.sqrt(HEAD_DIM))
    n_col = Q_PER_KV * WINDOW
    key = lax.broadcasted_iota(jnp.int32, (WINDOW, n_col), 0)
    qry = lax.broadcasted_iota(jnp.int32, (WINDOW, n_col), 1) % WINDOW
    from_prev = key > qry
    no_prev = jnp.where(pl.program_id(1) == 0, -1e30, 0.0)
    head_of_col = lax.broadcasted_iota(jnp.int32, (1, n_col), 1) // WINDOW
    gw = Q_PER_KV * HEAD_DIM
    sel_r = lax.broadcasted_iota(jnp.int32, (8, 2 * gw), 0)
    sel_l = (lax.broadcasted_iota(jnp.int32, (8, 2 * gw), 1) % gw) // HEAD_DIM
    head_sel = jnp.where(sel_r == sel_l, 1.0, 0.0).astype(BF16)

    for g in range(N_KV_HEADS):
        kcol = slice(g * HEAD_DIM, (g + 1) * HEAD_DIM)
        gcols = slice(g * Q_PER_KV * HEAD_DIM, (g + 1) * Q_PER_KV * HEAD_DIM)
        sink = jnp.zeros((1, n_col), F32)
        for r in range(Q_PER_KV):
            sink = jnp.where(head_of_col == r, sink_ref[g * Q_PER_KV + r] * log2e, sink)
        k_prev = _head_norm(kp_ref[:, kcol].astype(F32), kqw).astype(BF16)
        v_prev = vp_ref[:, kcol]
        for n in range(n_sub):
            rows = slice(n * WINDOW, (n + 1) * WINDOW)
            k_cur = _head_norm(kc_ref[rows, kcol].astype(F32), kqw).astype(BF16)
            v_cur = vc_ref[rows, kcol]
            k_ctx = jnp.concatenate([k_prev, k_cur], axis=0)
            v_ctx = jnp.concatenate([v_prev, v_cur], axis=0)

            qg = q_ref[rows, gcols]
            qs = jnp.concatenate([qg[:, r * HEAD_DIM:(r + 1) * HEAD_DIM] for r in range(Q_PER_KV)],
                                 axis=0)
            qf = qg.astype(F32)
            q2 = qf * qf
            q2_hi = q2.astype(BF16)
            q2_lo = (q2 - q2_hi.astype(F32)).astype(BF16)
            ssq = lax.dot_general(head_sel, jnp.concatenate([q2_hi, q2_lo], axis=1), _NT,
                                  preferred_element_type=F32)
            rms = lax.rsqrt(ssq * (1.0 / HEAD_DIM) + NORM_EPS)
            rms_q = jnp.concatenate([rms[r:r + 1] for r in range(Q_PER_KV)], axis=1)

            s = lax.dot_general(k_ctx, qs, _NT, preferred_element_type=F32)
            s_prev = s[:WINDOW] + no_prev if n == 0 else s[:WINDOW]
            s = jnp.where(from_prev, s_prev, s[WINDOW:]) * rms_q
            mx = jnp.maximum(jnp.max(s, axis=0, keepdims=True), sink)
            p = jnp.exp2(s - mx)
            den = jnp.sum(p, axis=0, keepdims=True) + jnp.exp2(sink - mx)
            p_ctx = jnp.concatenate([jnp.where(from_prev, p, 0.0), jnp.where(from_prev, 0.0, p)],
                                    axis=0).astype(BF16)
            o_t = lax.dot_general(v_ctx, p_ctx, _TN, preferred_element_type=F32) * (1.0 / den)
            halves = [jnp.concatenate([o_t[:, (2 * h) * WINDOW:(2 * h + 1) * WINDOW],
                                       o_t[:, (2 * h + 1) * WINDOW:(2 * h + 2) * WINDOW]], axis=0).T
                      for h in range(Q_PER_KV // 2)]
            og = jnp.concatenate(halves, axis=1)
            gate = g_ref[rows, gcols].astype(F32)
            o_ref[rows, gcols] = (og * (gate * jax.nn.sigmoid(gate))).astype(BF16)
            k_prev, v_prev = k_cur, v_cur


def _swa(proj, sinks, q_norm_w, k_norm_w, batch, seq, tq=1024):
    n_sub = tq // WINDOW
    tiles = seq // tq
    m = batch * seq

    def cur_rows(b, i):
        return b * tiles + i

    def prev_rows(b, i):
        return jnp.maximum((b * tiles + i) * n_sub - 1, 0)

    kernel = functools.partial(_swa_kernel, n_sub=n_sub)
    return pl.pallas_call(
        kernel,
        grid=(batch, tiles),
        in_specs=[
            pl.BlockSpec(memory_space=pltpu.SMEM),
            pl.BlockSpec((tq, ATTN_W), lambda b, i: (cur_rows(b, i), COL_Q // ATTN_W)),
            pl.BlockSpec((tq, ATTN_W), lambda b, i: (cur_rows(b, i), COL_GATE // ATTN_W)),
            pl.BlockSpec((tq, KV_W), lambda b, i: (cur_rows(b, i), COL_K // KV_W)),
            pl.BlockSpec((tq, KV_W), lambda b, i: (cur_rows(b, i), COL_V // KV_W)),
            pl.BlockSpec((WINDOW, KV_W), lambda b, i: (prev_rows(b, i), COL_K // KV_W)),
            pl.BlockSpec((WINDOW, KV_W), lambda b, i: (prev_rows(b, i), COL_V // KV_W)),
            pl.BlockSpec((1, HEAD_DIM), lambda b, i: (0, 0)),
            pl.BlockSpec((1, HEAD_DIM), lambda b, i: (0, 0)),
        ],
        out_specs=pl.BlockSpec((tq, ATTN_W), lambda b, i: (cur_rows(b, i), 0)),
        out_shape=jax.ShapeDtypeStruct((m, ATTN_W), BF16),
        compiler_params=pltpu.CompilerParams(
            dimension_semantics=("arbitrary", "arbitrary"), vmem_limit_bytes=VMEM_LIMIT),
        name="swa",
    )(sinks, proj, proj, proj, proj, proj, proj,
      q_norm_w.reshape(1, HEAD_DIM), k_norm_w.reshape(1, HEAD_DIM))


def _slab_powers(ar, ai, dt, btr, bti):
    dta_re, dta_im = dt * ar, dt * ai
    mag = jnp.exp(dta_re)
    ab_re, ab_im = mag * jnp.cos(dta_im), mag * jnp.sin(dta_im)
    pw = [(jnp.ones_like(ar), jnp.zeros_like(ar))]
    for _ in range(CHUNK):
        pr, pi = pw[-1]
        pw.append((pr * ab_re - pi * ab_im, pr * ab_im + pi * ab_re))
    den = ar * ar + ai * ai
    num_re, num_im = ab_re - 1.0, ab_im
    cf_re = (num_re * ar + num_im * ai) / den
    cf_im = (num_im * ar - num_re * ai) / den
    bb_re = cf_re * btr - cf_im * bti
    bb_im = cf_re * bti + cf_im * btr
    abb = [(pr * bb_re - pi * bb_im, pr * bb_im + pi * bb_re) for pr, pi in pw[:CHUNK]]
    return pw, abb


def _build_state_in(abb, p_s):
    row_g = lax.broadcasted_iota(jnp.int32, (LANES, STATE), 0) // GROUP
    even = row_g % 2 == 0
    pair_of_row = lax.broadcasted_iota(jnp.int32, (LANES, 2 * LANES), 0) // (2 * GROUP)
    for lag in range(CHUNK):
        t = CHUNK - 1 - lag
        re, im = abb[lag]
        blk = jnp.concatenate([jnp.where(even, re, 0.0), jnp.where(even, 0.0, re),
                               jnp.where(even, im, 0.0), jnp.where(even, 0.0, im)], axis=1)
        for k in range(PAIRS):
            p_s[k, t * LANES:(t + 1) * LANES, :] = jnp.where(pair_of_row == k, blk, 0.0).astype(BF16)


def _build_state_out(pw, cr, ci, q_s):
    lane_g = lax.broadcasted_iota(jnp.int32, (STATE, LANES), 1) // GROUP
    for t in range(CHUNK):
        pr, pi = pw[t + 1]
        cat = jnp.concatenate([cr * pr - ci * pi, -(cr * pi + ci * pr)], axis=1)
        cat_t = cat.T
        for k in range(PAIRS):
            for ri in range(2):
                for half in range(2):
                    piece = jnp.where(lane_g == 2 * k + half, cat_t[ri * STATE:(ri + 1) * STATE], 0.0)
                    r0 = ri * LANES + half * STATE
                    q_s[k, r0:r0 + STATE, t * LANES:(t + 1) * LANES] = piece.astype(BF16)


def _build_toeplitz(abb, cr, ci, d_row, w_s):
    def split(v):
        hi = v.astype(BF16)
        return hi, (v - hi.astype(F32)).astype(BF16)

    rhs_hi, rhs_lo = split(jnp.concatenate([cr, -ci], axis=1))
    rhs = jnp.concatenate([rhs_hi, rhs_lo, rhs_hi], axis=1)
    row = lax.broadcasted_iota(jnp.int32, (LANES, LANES), 0)
    col = lax.broadcasted_iota(jnp.int32, (LANES, LANES), 1)
    same_group = (row // GROUP) == (col // GROUP)
    taps = []
    for lag in range(CHUNK):
        lhs_hi, lhs_lo = split(jnp.concatenate(abb[lag], axis=1))
        k = lax.dot_general(jnp.concatenate([lhs_hi, lhs_hi, lhs_lo], axis=1), rhs,
                            (((1,), (1,)), ((), ())), preferred_element_type=F32)
        k = jnp.where(same_group, k, 0.0)
        if lag == 0:
            k = k + jnp.where(row == col, d_row, 0.0)
        taps.append(k.astype(BF16))
    zero = jnp.zeros((LANES, LANES), BF16)
    for d in range(N_TILES):
        w_s[d, :LANES, :LANES] = taps[2 * d]
        w_s[d, :LANES, LANES:] = taps[2 * d + 1]
        w_s[d, LANES:, :LANES] = taps[2 * d - 1] if d > 0 else zero
        w_s[d, LANES:, LANES:] = taps[2 * d]


def _ssm_kernel(u_ref, ar_ref, ai_ref, dt_ref, btr_ref, bti_ref, cr_ref, ci_ref, d_ref,
                dtare_ref, dtaim_ref, y_ref, xre_s, xim_s, p_s, q_s, w_s, *, batch, n_chunks):
    step = pl.program_id(0)
    rows8 = batch * PAIRS
    slab_rows = n_chunks * rows8

    def state_rows(slab, b, k):
        return pl.ds(slab * slab_rows + b * PAIRS + k, n_chunks, stride=rows8)

    def powers():
        return _slab_powers(ar_ref[...], ai_ref[...], dt_ref[...], btr_ref[...], bti_ref[...])

    @pl.when(step < N_SLABS)
    def _():
        _, abb = powers()
        _build_state_in(abb, p_s)
        u = u_ref[...]
        for k in range(PAIRS):
            x = _dot(u, p_s[k])
            for b in range(batch):
                xb = x[b * n_chunks:(b + 1) * n_chunks]
                xre_s[state_rows(step, b, k), :] = xb[:, :LANES]
                xim_s[state_rows(step, b, k), :] = xb[:, LANES:]

    @pl.when(step == N_SLABS)
    def _():
        dta_re = jnp.concatenate([dtare_ref[...]] * batch, axis=1).reshape(N_SLABS * rows8, LANES)
        dta_im = jnp.concatenate([dtaim_ref[...]] * batch, axis=1).reshape(N_SLABS * rows8, LANES)
        mag = jnp.exp(CHUNK * dta_re)
        a_re, a_im = mag * jnp.cos(CHUNK * dta_im), mag * jnp.sin(CHUNK * dta_im)

        def body(c, carry):
            s_re, s_im = carry
            x_re, x_im = [], []
            for sl in range(N_SLABS):
                off = pl.multiple_of(sl * slab_rows + c * rows8, rows8)
                x_re.append(xre_s[pl.ds(off, rows8), :])
                x_im.append(xim_s[pl.ds(off, rows8), :])
                xre_s[pl.ds(off, rows8), :] = s_re[sl * rows8:(sl + 1) * rows8]
                xim_s[pl.ds(off, rows8), :] = s_im[sl * rows8:(sl + 1) * rows8]
            x_re = jnp.concatenate(x_re, axis=0)
            x_im = jnp.concatenate(x_im, axis=0)
            return (a_re * s_re - a_im * s_im + x_re, a_re * s_im + a_im * s_re + x_im)

        zero = jnp.zeros((N_SLABS * rows8, LANES), F32)
        lax.fori_loop(0, n_chunks, body, (zero, zero))

    @pl.when(step > N_SLABS)
    def _():
        slab = step - (N_SLABS + 1)
        pw, abb = powers()
        cr, ci = cr_ref[...], ci_ref[...]
        _build_state_out(pw, cr, ci, q_s)
        _build_toeplitz(abb, cr, ci, d_ref[...], w_s)
        u = u_ref[...]
        y_state = None
        for k in range(PAIRS):
            s_in = jnp.concatenate(
                [jnp.concatenate([xre_s[state_rows(slab, b, k), :], xim_s[state_rows(slab, b, k), :]],
                                 axis=1) for b in range(batch)], axis=0).astype(BF16)
            part = _dot(s_in, q_s[k])
            y_state = part if y_state is None else y_state + part
        for t2 in range(N_TILES):
            acc = y_state[:, t2 * MXU:(t2 + 1) * MXU]
            for t1 in range(t2 + 1):
                acc = acc + _dot(u[:, t1 * MXU:(t1 + 1) * MXU], w_s[t2 - t1])
            y_ref[:, t2 * MXU:(t2 + 1) * MXU] = jax.nn.gelu(acc).astype(BF16)


def _ssm(u_chunks, A_re, A_im, log_dt, B_re, B_im, C_re, C_im, D_skip, batch):
    _, rows, _ = u_chunks.shape
    n_chunks = rows // batch
    n_steps = 2 * N_SLABS + 1
    kernel = functools.partial(_ssm_kernel, batch=batch, n_chunks=n_chunks)

    dt = jnp.exp(log_dt)[:, None]
    per_row = lambda v: jnp.repeat(v, GROUP, axis=0)
    ar, ai = per_row(A_re), per_row(A_im)
    dtr = per_row(jnp.broadcast_to(dt, A_re.shape))
    btr = B_re.transpose(0, 2, 1).reshape(SSM_W, STATE)
    bti = B_im.transpose(0, 2, 1).reshape(SSM_W, STATE)
    cr, ci = C_re.reshape(SSM_W, STATE), C_im.reshape(SSM_W, STATE)
    d_rows = D_skip.reshape(N_SLABS, 1, LANES)
    dta_re = (dt * A_re).reshape(N_SLABS, PAIRS, LANES)
    dta_im = (dt * A_im).reshape(N_SLABS, PAIRS, LANES)

    def slab_in(s):
        return jnp.where(s <= N_SLABS, jnp.minimum(s, N_SLABS - 1), s - (N_SLABS + 1))

    def slab_out(s):
        return jnp.maximum(s - (N_SLABS + 1), 0)

    coeff = pl.BlockSpec((LANES, STATE), lambda s: (slab_in(s), 0))
    whole = pl.BlockSpec((N_SLABS, PAIRS, LANES), lambda s: (0, 0, 0))
    return pl.pallas_call(
        kernel,
        grid=(n_steps,),
        in_specs=[
            pl.BlockSpec((None, rows, CHUNK_W), lambda s: (slab_in(s), 0, 0)),
            coeff, coeff, coeff, coeff, coeff, coeff, coeff,
            pl.BlockSpec((None, 1, LANES), lambda s: (slab_in(s), 0, 0)),
            whole, whole,
        ],
        out_specs=pl.BlockSpec((None, rows, CHUNK_W), lambda s: (slab_out(s), 0, 0)),
        out_shape=jax.ShapeDtypeStruct(u_chunks.shape, BF16),
        scratch_shapes=[
            pltpu.VMEM((N_SLABS * n_chunks * batch * PAIRS, LANES), F32),
            pltpu.VMEM((N_SLABS * n_chunks * batch * PAIRS, LANES), F32),
            pltpu.VMEM((PAIRS, CHUNK_W, 2 * LANES), BF16),
            pltpu.VMEM((PAIRS, 2 * LANES, CHUNK_W), BF16),
            pltpu.VMEM((N_TILES, MXU, MXU), BF16),
        ],
        compiler_params=pltpu.CompilerParams(
            dimension_semantics=("arbitrary",), vmem_limit_bytes=VMEM_LIMIT),
        name="ssm",
    )(u_chunks, ar, ai, dtr, btr, bti, cr, ci, d_rows, dta_re, dta_im)


def _tail_kernel(ag_ref, yc_ref, z_ref, ga_ref, gs_ref, x_ref, wa_ref, wg_ref, bg_ref, ws_ref,
                 wo_ref, o_ref, ys_ref, *, tm):
    rows = tm // CHUNK
    for s in range(N_SLABS):
        for t in range(CHUNK):
            ys_ref[pl.ds(s * tm + t, rows, stride=CHUNK), :] = (
                yc_ref[s, :, t * LANES:(t + 1) * LANES].astype(F32))
    y_g = jnp.concatenate([ys_ref[s * tm:(s + 1) * tm, :] for s in range(N_SLABS)],
                          axis=1).astype(BF16)

    y_a = _dot(ag_ref[...], wa_ref[...])
    glu = _dot(y_g, wg_ref[...]) + bg_ref[...]
    z = z_ref[...].astype(F32)
    t = glu[:, :SSM_W] * jax.nn.sigmoid(glu[:, SSM_W:]) * (z * jax.nn.sigmoid(z))
    y_s = _dot(t.astype(BF16), ws_ref[...])
    merged = (jax.nn.sigmoid(ga_ref[...].astype(F32)) * y_a
              + jax.nn.sigmoid(gs_ref[...].astype(F32)) * y_s)
    o_ref[...] = x_ref[...] + _dot(merged.astype(BF16), wo_ref[...])


def _tail(attn_g, y_chunks, proj, x2, w_attn, w_glu, b_glu, w_ssm, w_out, tm=256):
    m = x2.shape[0]
    row = lambda blk: (lambda i: (i, blk))
    const = lambda i: (0, 0)
    resident = functools.partial(pl.BlockSpec, index_map=const, pipeline_mode=pl.Buffered(1))
    kernel = functools.partial(_tail_kernel, tm=tm)
    return pl.pallas_call(
        kernel,
        grid=(m // tm,),
        in_specs=[
            pl.BlockSpec((tm, ATTN_W), row(0)),
            pl.BlockSpec((N_SLABS, tm // CHUNK, CHUNK_W), lambda i: (0, i, 0)),
            pl.BlockSpec((tm, SSM_W), row(COL_Z // SSM_W)),
            pl.BlockSpec((tm, D_MODEL), row(COL_GA // D_MODEL)),
            pl.BlockSpec((tm, D_MODEL), row(COL_GS // D_MODEL)),
            pl.BlockSpec((tm, D_MODEL), row(0)),
            resident((ATTN_W, D_MODEL)),
            resident((SSM_W, 2 * SSM_W)),
            resident((1, 2 * SSM_W)),
            resident((SSM_W, D_MODEL)),
            resident((D_MODEL, D_MODEL)),
        ],
        out_specs=pl.BlockSpec((tm, D_MODEL), row(0)),
        out_shape=jax.ShapeDtypeStruct((m, D_MODEL), F32),
        scratch_shapes=[pltpu.VMEM((N_SLABS * tm, LANES), F32)],
        compiler_params=pltpu.CompilerParams(
            dimension_semantics=("arbitrary",), vmem_limit_bytes=VMEM_LIMIT),
        name="tail",
    )(attn_g, y_chunks, proj, proj, proj, x2, w_attn, w_glu, b_glu.reshape(1, -1), w_ssm, w_out)


def kernel(x, norm_w, w_in, q_norm_w, k_norm_w, sinks, w_attn_proj, A_re, A_im, log_dt, B_re, B_im,
           C_re, C_im, D_skip, w_glu, b_glu, w_ssm_proj, w_out):
    batch, seq, _ = x.shape
    m = batch * seq
    x2 = x.reshape(m, D_MODEL)

    proj, u_chunks = _in_proj(x2, norm_w, w_in)

    attn_g = _swa(proj, sinks, q_norm_w, k_norm_w, batch, seq)

    y_chunks = _ssm(u_chunks, A_re, A_im, log_dt, B_re, B_im, C_re, C_im, D_skip, batch)

    out = _tail(attn_g, y_chunks, proj, x2, w_attn_proj.astype(BF16), w_glu.astype(BF16), b_glu,
                w_ssm_proj.astype(BF16), w_out.astype(BF16))
    return out.reshape(batch, seq, D_MODEL)
```

```python
import functools
import math

import jax
import jax.numpy as jnp
from jax import lax
from jax.experimental import pallas as pl
from jax.experimental.pallas import tpu as pltpu

F32 = jnp.float32
BF16 = jnp.bfloat16

D_MODEL = 2048
HEAD_DIM = 64
N_Q_HEADS = 16
N_KV_HEADS = 4
Q_PER_KV = 4
ATTN_W = N_Q_HEADS * HEAD_DIM
KV_W = N_KV_HEADS * HEAD_DIM
WINDOW = 128
SSM_W = D_MODEL // 2
GROUP = 16
N_GROUPS = SSM_W // GROUP
STATE = 64
NORM_EPS = 1e-6

COL_Q, COL_GATE, COL_U, COL_Z = 0, 1024, 2048, 3072
COL_GA, COL_GS, COL_K, COL_V = 4096, 6144, 8192, 8448
IN_W = 8704

LANES = 128
MXU = 256
CHUNK = 16
SLAB_G = LANES // GROUP
N_SLABS = N_GROUPS // SLAB_G
PAIRS = SLAB_G // 2
CHUNK_W = CHUNK * LANES
N_TILES = CHUNK_W // MXU
VMEM_LIMIT = 48 * 1024 * 1024


def _dot(a, b):
    return jnp.dot(a, b, preferred_element_type=F32)


IN_TN = 512
N_IN_TILES = IN_W // IN_TN
SRC_KV_TILE = (ATTN_W) // IN_TN
SRC_U_TILE0 = (ATTN_W + 2 * KV_W + ATTN_W) // IN_TN


def _dst_tile(src_tile):
    if src_tile < SRC_KV_TILE:
        return src_tile
    if src_tile == SRC_KV_TILE:
        return N_IN_TILES - 1
    return src_tile - 1


def _in_proj_kernel(x_ref, nw_ref, w_hbm, o_hbm, uc_ref, h_ref, w_buf, o_buf, us_ref, w_sem, o_sem,
                    *, tm):
    i = pl.program_id(0)
    rows = tm // CHUNK
    n_sl = IN_TN // LANES

    def w_copy(t):
        return pltpu.make_async_copy(w_hbm.at[:, pl.ds(t * IN_TN, IN_TN)], w_buf.at[t % 2],
                                     w_sem.at[t % 2])

    def o_copy(t):
        dst = o_hbm.at[pl.ds(pl.multiple_of(i * tm, tm), tm), pl.ds(_dst_tile(t) * IN_TN, IN_TN)]
        return pltpu.make_async_copy(o_buf.at[t % 2], dst, o_sem.at[t % 2])

    w_copy(0).start()
    x = x_ref[...]
    ms = jnp.mean(x * x, axis=-1, keepdims=True)
    h_ref[...] = (x * lax.rsqrt(ms + NORM_EPS) * nw_ref[...]).astype(BF16)


    def finish(t, acc):
        o_buf[t % 2] = acc.astype(BF16)
        if SRC_U_TILE0 <= t < SRC_U_TILE0 + SSM_W // IN_TN:
            slab0 = (t - SRC_U_TILE0) * n_sl
            for s in range(n_sl):
                us_ref[s * tm:(s + 1) * tm, :] = acc[:, s * LANES:(s + 1) * LANES]
            for s in range(n_sl):
                for tok in range(CHUNK):
                    piece = us_ref[pl.ds(s * tm + tok, rows, stride=CHUNK), :]
                    uc_ref[slab0 + s, :, tok * LANES:(tok + 1) * LANES] = piece.astype(BF16)
        o_copy(t).start()

    for t in range(N_IN_TILES):
        if t + 1 < N_IN_TILES:
            w_copy(t + 1).start()
        w_copy(t).wait()
        if t >= 2:
            o_copy(t - 2).wait()
        acc = None
        for k in range(D_MODEL // MXU):
            ks = slice(k * MXU, (k + 1) * MXU)
            part = _dot(h_ref[:, ks], w_buf[t % 2, ks, :].astype(BF16))
            acc = part if acc is None else acc + part
        finish(t, acc)

    o_copy(N_IN_TILES - 2).wait()
    o_copy(N_IN_TILES - 1).wait()


def _in_proj(x2, norm_w, w_in, tm=1024):
    m = x2.shape[0]
    kernel = functools.partial(_in_proj_kernel, tm=tm)
    return pl.pallas_call(
        kernel,
        grid=(m // tm,),
        in_specs=[
            pl.BlockSpec((tm, D_MODEL), lambda i: (i, 0)),
            pl.BlockSpec((1, D_MODEL), lambda i: (0, 0)),
            pl.BlockSpec(memory_space=pl.ANY),
        ],
        out_specs=[
            pl.BlockSpec(memory_space=pl.ANY),
            pl.BlockSpec((N_SLABS, tm // CHUNK, CHUNK_W), lambda i: (0, i, 0)),
        ],
        out_shape=[
            jax.ShapeDtypeStruct((m, IN_W), BF16),
            jax.ShapeDtypeStruct((N_SLABS, m // CHUNK, CHUNK_W), BF16),
        ],
        scratch_shapes=[
            pltpu.VMEM((tm, D_MODEL), BF16),
            pltpu.VMEM((2, D_MODEL, IN_TN), F32),
            pltpu.VMEM((2, tm, IN_TN), BF16),
            pltpu.VMEM((IN_TN // LANES * tm, LANES), F32),
            pltpu.SemaphoreType.DMA((2,)),
            pltpu.SemaphoreType.DMA((2,)),
        ],
        compiler_params=pltpu.CompilerParams(
            dimension_semantics=("arbitrary",), vmem_limit_bytes=VMEM_LIMIT),
        name="in_proj",
    )(x2, norm_w.reshape(1, D_MODEL), w_in)


def _head_norm(t, w):
    ms = jnp.mean(t * t, axis=-1, keepdims=True)
    return t * lax.rsqrt(ms + NORM_EPS) * w


_NT = (((1,), (1,)), ((), ()))
_TN = (((0,), (0,)), ((), ()))


def _swa_kernel(sink_ref, q_ref, g_ref, kc_ref, vc_ref, kp_ref, vp_ref, qw_ref, kw_ref, o_ref,
                *, n_sub):
    log2e = math.log2(math.e)
    kqw = kw_ref[...] * qw_ref[...] * (log2e / math.sqrt(HEAD_DIM))
    n_col = Q_PER_KV * WINDOW
    key = lax.broadcasted_iota(jnp.int32, (WINDOW, n_col), 0)
    qry = lax.broadcasted_iota(jnp.int32, (WINDOW, n_col), 1) % WINDOW
    from_prev = key > qry
    no_prev = jnp.where(pl.program_id(1) == 0, -1e30, 0.0)
    head_of_col = lax.broadcasted_iota(jnp.int32, (1, n_col), 1) // WINDOW
    gw = Q_PER_KV * HEAD_DIM
    sel_r = lax.broadcasted_iota(jnp.int32, (8, 2 * gw), 0)
    sel_l = (lax.broadcasted_iota(jnp.int32, (8, 2 * gw), 1) % gw) // HEAD_DIM
    head_sel = jnp.where(sel_r == sel_l, 1.0, 0.0).astype(BF16)

    for g in range(N_KV_HEADS):
        kcol = slice(g * HEAD_DIM, (g + 1) * HEAD_DIM)
        gcols = slice(g * Q_PER_KV * HEAD_DIM, (g + 1) * Q_PER_KV * HEAD_DIM)
        sink = jnp.zeros((1, n_col), F32)
        for r in range(Q_PER_KV):
            sink = jnp.where(head_of_col == r, sink_ref[g * Q_PER_KV + r] * log2e, sink)
        k_prev = _head_norm(kp_ref[:, kcol].astype(F32), kqw).astype(BF16)
        v_prev = vp_ref[:, kcol]
        for n in range(n_sub):
            rows = slice(n * WINDOW, (n + 1) * WINDOW)
            k_cur = _head_norm(kc_ref[rows, kcol].astype(F32), kqw).astype(BF16)
            v_cur = vc_ref[rows, kcol]
            k_ctx = jnp.concatenate([k_prev, k_cur], axis=0)
            v_ctx = jnp.concatenate([v_prev, v_cur], axis=0)

            qg = q_ref[rows, gcols]
            qs = jnp.concatenate([qg[:, r * HEAD_DIM:(r + 1) * HEAD_DIM] for r in range(Q_PER_KV)],
                                 axis=0)
            qf = qg.astype(F32)
            q2 = qf * qf
            q2_hi = q2.astype(BF16)
            q2_lo = (q2 - q2_hi.astype(F32)).astype(BF16)
            ssq = lax.dot_general(head_sel, jnp.concatenate([q2_hi, q2_lo], axis=1), _NT,
                                  preferred_element_type=F32)
            rms = lax.rsqrt(ssq * (1.0 / HEAD_DIM) + NORM_EPS)
            rms_q = jnp.concatenate([rms[r:r + 1] for r in range(Q_PER_KV)], axis=1)

            s = lax.dot_general(k_ctx, qs, _NT, preferred_element_type=F32)
            s_prev = s[:WINDOW] + no_prev if n == 0 else s[:WINDOW]
            s = jnp.where(from_prev, s_prev, s[WINDOW:]) * rms_q
            mx = jnp.maximum(jnp.max(s, axis=0, keepdims=True), sink)
            p = jnp.exp2(s - mx)
            den = jnp.sum(p, axis=0, keepdims=True) + jnp.exp2(sink - mx)
            p_ctx = jnp.concatenate([jnp.where(from_prev, p, 0.0), jnp.where(from_prev, 0.0, p)],
                                    axis=0).astype(BF16)
            o_t = lax.dot_general(v_ctx, p_ctx, _TN, preferred_element_type=F32) * (1.0 / den)
            halves = [jnp.concatenate([o_t[:, (2 * h) * WINDOW:(2 * h + 1) * WINDOW],
                                       o_t[:, (2 * h + 1) * WINDOW:(2 * h + 2) * WINDOW]], axis=0).T
                      for h in range(Q_PER_KV // 2)]
            og = jnp.concatenate(halves, axis=1)
            gate = g_ref[rows, gcols].astype(F32)
            o_ref[rows, gcols] = (og * (gate * jax.nn.sigmoid(gate))).astype(BF16)
            k_prev, v_prev = k_cur, v_cur


def _swa(proj, sinks, q_norm_w, k_norm_w, batch, seq, tq=1024):
    n_sub = tq // WINDOW
    tiles = seq // tq
    m = batch * seq

    def cur_rows(b, i):
        return b * tiles + i

    def prev_rows(b, i):
        return jnp.maximum((b * tiles + i) * n_sub - 1, 0)

    kernel = functools.partial(_swa_kernel, n_sub=n_sub)
    return pl.pallas_call(
        kernel,
        grid=(batch, tiles),
        in_specs=[
            pl.BlockSpec(memory_space=pltpu.SMEM),
            pl.BlockSpec((tq, ATTN_W), lambda b, i: (cur_rows(b, i), COL_Q // ATTN_W)),
            pl.BlockSpec((tq, ATTN_W), lambda b, i: (cur_rows(b, i), COL_GATE // ATTN_W)),
            pl.BlockSpec((tq, KV_W), lambda b, i: (cur_rows(b, i), COL_K // KV_W)),
            pl.BlockSpec((tq, KV_W), lambda b, i: (cur_rows(b, i), COL_V // KV_W)),
            pl.BlockSpec((WINDOW, KV_W), lambda b, i: (prev_rows(b, i), COL_K // KV_W)),
            pl.BlockSpec((WINDOW, KV_W), lambda b, i: (prev_rows(b, i), COL_V // KV_W)),
            pl.BlockSpec((1, HEAD_DIM), lambda b, i: (0, 0)),
            pl.BlockSpec((1, HEAD_DIM), lambda b, i: (0, 0)),
        ],
        out_specs=pl.BlockSpec((tq, ATTN_W), lambda b, i: (cur_rows(b, i), 0)),
        out_shape=jax.ShapeDtypeStruct((m, ATTN_W), BF16),
        compiler_params=pltpu.CompilerParams(
            dimension_semantics=("arbitrary", "arbitrary"), vmem_limit_bytes=VMEM_LIMIT),
        name="swa",
    )(sinks, proj, proj, proj, proj, proj, proj,
      q_norm_w.reshape(1, HEAD_DIM), k_norm_w.reshape(1, HEAD_DIM))


def _slab_powers(ar, ai, dt, btr, bti):
    dta_re, dta_im = dt * ar, dt * ai
    mag = jnp.exp(dta_re)
    ab_re, ab_im = mag * jnp.cos(dta_im), mag * jnp.sin(dta_im)
    pw = [(jnp.ones_like(ar), jnp.zeros_like(ar))]
    for _ in range(CHUNK):
        pr, pi = pw[-1]
        pw.append((pr * ab_re - pi * ab_im, pr * ab_im + pi * ab_re))
    den = ar * ar + ai * ai
    num_re, num_im = ab_re - 1.0, ab_im
    cf_re = (num_re * ar + num_im * ai) / den
    cf_im = (num_im * ar - num_re * ai) / den
    bb_re = cf_re * btr - cf_im * bti
    bb_im = cf_re * bti + cf_im * btr
    abb = [(pr * bb_re - pi * bb_im, pr * bb_im + pi * bb_re) for pr, pi in pw[:CHUNK]]
    return pw, abb


def _build_state_in(abb, p_s):
    row_g = lax.broadcasted_iota(jnp.int32, (LANES, STATE), 0) // GROUP
    even = row_g % 2 == 0
    pair_of_row = lax.broadcasted_iota(jnp.int32, (LANES, 2 * LANES), 0) // (2 * GROUP)
    for lag in range(CHUNK):
        t = CHUNK - 1 - lag
        re, im = abb[lag]
        blk = jnp.concatenate([jnp.where(even, re, 0.0), jnp.where(even, 0.0, re),
                               jnp.where(even, im, 0.0), jnp.where(even, 0.0, im)], axis=1)
        for k in range(PAIRS):
            p_s[k, t * LANES:(t + 1) * LANES, :] = jnp.where(pair_of_row == k, blk, 0.0).astype(BF16)


def _build_state_out(pw, cr, ci, q_s):
    lane_g = lax.broadcasted_iota(jnp.int32, (STATE, LANES), 1) // GROUP
    for t in range(CHUNK):
        pr, pi = pw[t + 1]
        cat = jnp.concatenate([cr * pr - ci * pi, -(cr * pi + ci * pr)], axis=1)
        cat_t = cat.T
        for k in range(PAIRS):
            for ri in range(2):
                for half in range(2):
                    piece = jnp.where(lane_g == 2 * k + half, cat_t[ri * STATE:(ri + 1) * STATE], 0.0)
                    r0 = ri * LANES + half * STATE
                    q_s[k, r0:r0 + STATE, t * LANES:(t + 1) * LANES] = piece.astype(BF16)


def _build_toeplitz(abb, cr, ci, d_row, w_s):
    def split(v):
        hi = v.astype(BF16)
        return hi, (v - hi.astype(F32)).astype(BF16)

    rhs_hi, rhs_lo = split(jnp.concatenate([cr, -ci], axis=1))
    rhs = jnp.concatenate([rhs_hi, rhs_lo, rhs_hi], axis=1)
    row = lax.broadcasted_iota(jnp.int32, (LANES, LANES), 0)
    col = lax.broadcasted_iota(jnp.int32, (LANES, LANES), 1)
    same_group = (row // GROUP) == (col // GROUP)
    taps = []
    for lag in range(CHUNK):
        lhs_hi, lhs_lo = split(jnp.concatenate(abb[lag], axis=1))
        k = lax.dot_general(jnp.concatenate([lhs_hi, lhs_hi, lhs_lo], axis=1), rhs,
                            (((1,), (1,)), ((), ())), preferred_element_type=F32)
        k = jnp.where(same_group, k, 0.0)
        if lag == 0:
            k = k + jnp.where(row == col, d_row, 0.0)
        taps.append(k.astype(BF16))
    zero = jnp.zeros((LANES, LANES), BF16)
    for d in range(N_TILES):
        w_s[d, :LANES, :LANES] = taps[2 * d]
        w_s[d, :LANES, LANES:] = taps[2 * d + 1]
        w_s[d, LANES:, :LANES] = taps[2 * d - 1] if d > 0 else zero
        w_s[d, LANES:, LANES:] = taps[2 * d]


def _ssm_kernel(u_ref, ar_ref, ai_ref, dt_ref, btr_ref, bti_ref, cr_ref, ci_ref, d_ref,
                dtare_ref, dtaim_ref, w0_ref, w1_ref, w2_ref, w3_ref, y_ref, c0_ref, c1_ref, c2_ref,
                c3_ref, xre_s, xim_s, p_s, q_s, w_s, *, batch, n_chunks):
    step = pl.program_id(0)
    for src, dst in ((w0_ref, c0_ref), (w1_ref, c1_ref), (w2_ref, c2_ref), (w3_ref, c3_ref)):
        dst[...] = src[...].astype(BF16)
    rows8 = batch * PAIRS
    slab_rows = n_chunks * rows8

    def state_rows(slab, b, k):
        return pl.ds(slab * slab_rows + b * PAIRS + k, n_chunks, stride=rows8)

    def powers():
        return _slab_powers(ar_ref[...], ai_ref[...], dt_ref[...], btr_ref[...], bti_ref[...])

    @pl.when(step < N_SLABS)
    def _():
        _, abb = powers()
        _build_state_in(abb, p_s)
        u = u_ref[...]
        for k in range(PAIRS):
            x = _dot(u, p_s[k])
            for b in range(batch):
                xb = x[b * n_chunks:(b + 1) * n_chunks]
                xre_s[state_rows(step, b, k), :] = xb[:, :LANES]
                xim_s[state_rows(step, b, k), :] = xb[:, LANES:]

    @pl.when(step == N_SLABS)
    def _():
        dta_re = jnp.concatenate([dtare_ref[...]] * batch, axis=1).reshape(N_SLABS * rows8, LANES)
        dta_im = jnp.concatenate([dtaim_ref[...]] * batch, axis=1).reshape(N_SLABS * rows8, LANES)
        mag = jnp.exp(CHUNK * dta_re)
        a_re, a_im = mag * jnp.cos(CHUNK * dta_im), mag * jnp.sin(CHUNK * dta_im)

        def body(c, carry):
            s_re, s_im = carry
            x_re, x_im = [], []
            for sl in range(N_SLABS):
                off = pl.multiple_of(sl * slab_rows + c * rows8, rows8)
                x_re.append(xre_s[pl.ds(off, rows8), :])
                x_im.append(xim_s[pl.ds(off, rows8), :])
                xre_s[pl.ds(off, rows8), :] = s_re[sl * rows8:(sl + 1) * rows8]
                xim_s[pl.ds(off, rows8), :] = s_im[sl * rows8:(sl + 1) * rows8]
            x_re = jnp.concatenate(x_re, axis=0)
            x_im = jnp.concatenate(x_im, axis=0)
            return (a_re * s_re - a_im * s_im + x_re, a_re * s_im + a_im * s_re + x_im)

        zero = jnp.zeros((N_SLABS * rows8, LANES), F32)
        lax.fori_loop(0, n_chunks, body, (zero, zero))

    @pl.when(step > N_SLABS)
    def _():
        slab = step - (N_SLABS + 1)
        pw, abb = powers()
        cr, ci = cr_ref[...], ci_ref[...]
        _build_state_out(pw, cr, ci, q_s)
        _build_toeplitz(abb, cr, ci, d_ref[...], w_s)
        u = u_ref[...]
        y_state = None
        for k in range(PAIRS):
            s_in = jnp.concatenate(
                [jnp.concatenate([xre_s[state_rows(slab, b, k), :], xim_s[state_rows(slab, b, k), :]],
                                 axis=1) for b in range(batch)], axis=0).astype(BF16)
            part = _dot(s_in, q_s[k])
            y_state = part if y_state is None else y_state + part
        for t2 in range(N_TILES):
            acc = y_state[:, t2 * MXU:(t2 + 1) * MXU]
            for t1 in range(t2 + 1):
                acc = acc + _dot(u[:, t1 * MXU:(t1 + 1) * MXU], w_s[t2 - t1])
            y_ref[:, t2 * MXU:(t2 + 1) * MXU] = jax.nn.gelu(acc).astype(BF16)


N_CAST = 16


def _ssm(u_chunks, A_re, A_im, log_dt, B_re, B_im, C_re, C_im, D_skip, batch, tail_weights):
    _, rows, _ = u_chunks.shape
    n_chunks = rows // batch
    n_steps = 2 * N_SLABS + 1
    kernel = functools.partial(_ssm_kernel, batch=batch, n_chunks=n_chunks)

    dt = jnp.exp(log_dt)[:, None]
    per_row = lambda v: jnp.repeat(v, GROUP, axis=0)
    ar, ai = per_row(A_re), per_row(A_im)
    dtr = per_row(jnp.broadcast_to(dt, A_re.shape))
    btr = B_re.transpose(0, 2, 1).reshape(SSM_W, STATE)
    bti = B_im.transpose(0, 2, 1).reshape(SSM_W, STATE)
    cr, ci = C_re.reshape(SSM_W, STATE), C_im.reshape(SSM_W, STATE)
    d_rows = D_skip.reshape(N_SLABS, 1, LANES)
    dta_re = (dt * A_re).reshape(N_SLABS, PAIRS, LANES)
    dta_im = (dt * A_im).reshape(N_SLABS, PAIRS, LANES)

    def slab_in(s):
        return jnp.where(s <= N_SLABS, jnp.minimum(s, N_SLABS - 1), s - (N_SLABS + 1))

    def slab_out(s):
        return jnp.maximum(s - (N_SLABS + 1), 0)

    def cast_spec(w):
        return pl.BlockSpec((w.shape[0] // N_CAST, w.shape[1]),
                            lambda s: (jnp.minimum(s, N_CAST - 1), 0))

    coeff = pl.BlockSpec((LANES, STATE), lambda s: (slab_in(s), 0))
    whole = pl.BlockSpec((N_SLABS, PAIRS, LANES), lambda s: (0, 0, 0))
    return pl.pallas_call(
        kernel,
        grid=(n_steps,),
        in_specs=[
            pl.BlockSpec((None, rows, CHUNK_W), lambda s: (slab_in(s), 0, 0)),
            coeff, coeff, coeff, coeff, coeff, coeff, coeff,
            pl.BlockSpec((None, 1, LANES), lambda s: (slab_in(s), 0, 0)),
            whole, whole,
        ] + [cast_spec(w) for w in tail_weights],
        out_specs=[pl.BlockSpec((None, rows, CHUNK_W), lambda s: (slab_out(s), 0, 0))]
        + [cast_spec(w) for w in tail_weights],
        out_shape=[jax.ShapeDtypeStruct(u_chunks.shape, BF16)]
        + [jax.ShapeDtypeStruct(w.shape, BF16) for w in tail_weights],
        scratch_shapes=[
            pltpu.VMEM((N_SLABS * n_chunks * batch * PAIRS, LANES), F32),
            pltpu.VMEM((N_SLABS * n_chunks * batch * PAIRS, LANES), F32),
            pltpu.VMEM((PAIRS, CHUNK_W, 2 * LANES), BF16),
            pltpu.VMEM((PAIRS, 2 * LANES, CHUNK_W), BF16),
            pltpu.VMEM((N_TILES, MXU, MXU), BF16),
        ],
        compiler_params=pltpu.CompilerParams(
            dimension_semantics=("arbitrary",), vmem_limit_bytes=VMEM_LIMIT),
        name="ssm",
    )(u_chunks, ar, ai, dtr, btr, bti, cr, ci, d_rows, dta_re, dta_im, *tail_weights)


def _tail_kernel(ag_ref, yc_ref, z_ref, ga_ref, gs_ref, x_ref, wa_ref, wg_ref, bg_ref, ws_ref,
                 wo_ref, o_ref, ys_ref, *, tm):
    rows = tm // CHUNK
    for s in range(N_SLABS):
        for t in range(CHUNK):
            ys_ref[pl.ds(s * tm + t, rows, stride=CHUNK), :] = (
                yc_ref[s, :, t * LANES:(t + 1) * LANES].astype(F32))
    y_g = jnp.concatenate([ys_ref[s * tm:(s + 1) * tm, :] for s in range(N_SLABS)],
                          axis=1).astype(BF16)

    y_a = _dot(ag_ref[...], wa_ref[...])
    glu = _dot(y_g, wg_ref[...]) + bg_ref[...]
    z = z_ref[...].astype(F32)
    t = glu[:, :SSM_W] * jax.nn.sigmoid(glu[:, SSM_W:]) * (z * jax.nn.sigmoid(z))
    y_s = _dot(t.astype(BF16), ws_ref[...])
    merged = (jax.nn.sigmoid(ga_ref[...].astype(F32)) * y_a
              + jax.nn.sigmoid(gs_ref[...].astype(F32)) * y_s)
    o_ref[...] = x_ref[...] + _dot(merged.astype(BF16), wo_ref[...])


def _tail(attn_g, y_chunks, proj, x2, w_attn, w_glu, b_glu, w_ssm, w_out, tm=256):
    m = x2.shape[0]
    row = lambda blk: (lambda i: (i, blk))
    const = lambda i: (0, 0)
    resident = functools.partial(pl.BlockSpec, index_map=const, pipeline_mode=pl.Buffered(1))
    kernel = functools.partial(_tail_kernel, tm=tm)
    return pl.pallas_call(
        kernel,
        grid=(m // tm,),
        in_specs=[
            pl.BlockSpec((tm, ATTN_W), row(0)),
            pl.BlockSpec((N_SLABS, tm // CHUNK, CHUNK_W), lambda i: (0, i, 0)),
            pl.BlockSpec((tm, SSM_W), row(COL_Z // SSM_W)),
            pl.BlockSpec((tm, D_MODEL), row(COL_GA // D_MODEL)),
            pl.BlockSpec((tm, D_MODEL), row(COL_GS // D_MODEL)),
            pl.BlockSpec((tm, D_MODEL), row(0)),
            resident((ATTN_W, D_MODEL)),
            resident((SSM_W, 2 * SSM_W)),
            resident((1, 2 * SSM_W)),
            resident((SSM_W, D_MODEL)),
            resident((D_MODEL, D_MODEL)),
        ],
        out_specs=pl.BlockSpec((tm, D_MODEL), row(0)),
        out_shape=jax.ShapeDtypeStruct((m, D_MODEL), F32),
        scratch_shapes=[pltpu.VMEM((N_SLABS * tm, LANES), F32)],
        compiler_params=pltpu.CompilerParams(
            dimension_semantics=("arbitrary",), vmem_limit_bytes=VMEM_LIMIT),
        name="tail",
    )(attn_g, y_chunks, proj, proj, proj, x2, w_attn, w_glu, b_glu.reshape(1, -1), w_ssm, w_out)


def kernel(x, norm_w, w_in, q_norm_w, k_norm_w, sinks, w_attn_proj, A_re, A_im, log_dt, B_re, B_im,
           C_re, C_im, D_skip, w_glu, b_glu, w_ssm_proj, w_out):
    batch, seq, _ = x.shape
    m = batch * seq
    x2 = x.reshape(m, D_MODEL)

    proj, u_chunks = _in_proj(x2, norm_w, w_in)

    attn_g = _swa(proj, sinks, q_norm_w, k_norm_w, batch, seq)

    y_chunks, w_attn_bf, w_glu_bf, w_ssm_bf, w_out_bf = _ssm(
        u_chunks, A_re, A_im, log_dt, B_re, B_im, C_re, C_im, D_skip, batch,
        (w_attn_proj, w_glu, w_ssm_proj, w_out))

    out = _tail(attn_g, y_chunks, proj, x2, w_attn_bf, w_glu_bf, b_glu, w_ssm_bf, w_out_bf)
    return out.reshape(batch, seq, D_MODEL)
```

```python
import functools
import math

import jax
import jax.numpy as jnp
from jax import lax
from jax.experimental import pallas as pl
from jax.experimental.pallas import tpu as pltpu

F32 = jnp.float32
BF16 = jnp.bfloat16

D_MODEL = 2048
HEAD_DIM = 64
N_Q_HEADS = 16
N_KV_HEADS = 4
Q_PER_KV = 4
ATTN_W = N_Q_HEADS * HEAD_DIM
KV_W = N_KV_HEADS * HEAD_DIM
WINDOW = 128
SSM_W = D_MODEL // 2
GROUP = 16
N_GROUPS = SSM_W // GROUP
STATE = 64
NORM_EPS = 1e-6

COL_Q, COL_GATE, COL_U, COL_Z = 0, 1024, 2048, 3072
COL_GA, COL_GS, COL_K, COL_V = 4096, 6144, 8192, 8448
IN_W = 8704

LANES = 128
MXU = 256
CHUNK = 16
SLAB_G = LANES // GROUP
N_SLABS = N_GROUPS // SLAB_G
PAIRS = SLAB_G // 2
CHUNK_W = CHUNK * LANES
N_TILES = CHUNK_W // MXU
VMEM_LIMIT = 48 * 1024 * 1024


def _dot(a, b):
    return jnp.dot(a, b, preferred_element_type=F32)


IN_TN = 512
N_IN_TILES = IN_W // IN_TN
SRC_KV_TILE = (ATTN_W) // IN_TN
SRC_U_TILE0 = (ATTN_W + 2 * KV_W + ATTN_W) // IN_TN


US_PITCH = CHUNK + 8
IN_PER_ROUND = 2
IN_RING = 2 * IN_PER_ROUND
IN_PROJ_VMEM_LIMIT = 56 * 1024 * 1024


def _dst_tile(src_tile):
    if src_tile < SRC_KV_TILE:
        return src_tile
    if src_tile == SRC_KV_TILE:
        return N_IN_TILES - 1
    return src_tile - 1


def _in_proj_kernel(x_ref, nw_ref, w_hbm, o_hbm, uc_ref, h_ref, w_buf, o_buf, us_ref, w_sem, o_sem,
                    *, tm):
    i = pl.program_id(0)
    rows = tm // CHUNK
    n_sl = IN_TN // LANES

    def w_copy(t):
        return pltpu.make_async_copy(w_hbm.at[:, pl.ds(t * IN_TN, IN_TN)], w_buf.at[t % IN_RING],
                                     w_sem.at[t % IN_RING])

    def o_copy(t):
        dst = o_hbm.at[pl.ds(pl.multiple_of(i * tm, tm), tm), pl.ds(_dst_tile(t) * IN_TN, IN_TN)]
        return pltpu.make_async_copy(o_buf.at[t % IN_RING], dst, o_sem.at[t % IN_RING])

    def round_tiles(r):
        return [t for t in range(r * IN_PER_ROUND, (r + 1) * IN_PER_ROUND) if 0 <= t < N_IN_TILES]

    for t in round_tiles(0):
        w_copy(t).start()
    x = x_ref[...]
    ms = jnp.mean(x * x, axis=-1, keepdims=True)
    h_ref[...] = (x * lax.rsqrt(ms + NORM_EPS) * nw_ref[...]).astype(BF16)

    def finish(t, acc):
        o_buf[t % IN_RING] = acc.astype(BF16)
        if SRC_U_TILE0 <= t < SRC_U_TILE0 + SSM_W // IN_TN:
            slab0 = (t - SRC_U_TILE0) * n_sl
            for s in range(n_sl):
                for c in range(rows):
                    r0 = (s * rows + c) * US_PITCH
                    us_ref[r0:r0 + CHUNK, :] = acc[c * CHUNK:(c + 1) * CHUNK, s * LANES:(s + 1) * LANES]
            for s in range(n_sl):
                for tok in range(CHUNK):
                    piece = us_ref[pl.ds(s * rows * US_PITCH + tok, rows, stride=US_PITCH), :]
                    uc_ref[slab0 + s, :, tok * LANES:(tok + 1) * LANES] = piece.astype(BF16)

    n_rounds = pl.cdiv(N_IN_TILES, IN_PER_ROUND)
    for r in range(n_rounds):
        for t in round_tiles(r + 1):
            w_copy(t).start()
        for t in round_tiles(r):
            w_copy(t).wait()
        for t in round_tiles(r - 2):
            o_copy(t).wait()
        for t in round_tiles(r - 1):
            o_copy(t).start()
        for t in round_tiles(r):
            acc = None
            for k in range(D_MODEL // MXU):
                ks = slice(k * MXU, (k + 1) * MXU)
                part = _dot(h_ref[:, ks], w_buf[t % IN_RING, ks, :].astype(BF16))
                acc = part if acc is None else acc + part
            finish(t, acc)

    for t in round_tiles(n_rounds - 1):
        o_copy(t).start()
    for t in round_tiles(n_rounds - 2) + round_tiles(n_rounds - 1):
        o_copy(t).wait()


def _in_proj(x2, norm_w, w_in, tm=1024):
    m = x2.shape[0]
    kernel = functools.partial(_in_proj_kernel, tm=tm)
    return pl.pallas_call(
        kernel,
        grid=(m // tm,),
        in_specs=[
            pl.BlockSpec((tm, D_MODEL), lambda i: (i, 0)),
            pl.BlockSpec((1, D_MODEL), lambda i: (0, 0)),
            pl.BlockSpec(memory_space=pl.ANY),
        ],
        out_specs=[
            pl.BlockSpec(memory_space=pl.ANY),
            pl.BlockSpec((N_SLABS, tm // CHUNK, CHUNK_W), lambda i: (0, i, 0)),
        ],
        out_shape=[
            jax.ShapeDtypeStruct((m, IN_W), BF16),
            jax.ShapeDtypeStruct((N_SLABS, m // CHUNK, CHUNK_W), BF16),
        ],
        scratch_shapes=[
            pltpu.VMEM((tm, D_MODEL), BF16),
            pltpu.VMEM((IN_RING, D_MODEL, IN_TN), F32),
            pltpu.VMEM((IN_RING, tm, IN_TN), BF16),
            pltpu.VMEM((IN_TN // LANES * (tm // CHUNK) * US_PITCH, LANES), F32),
            pltpu.SemaphoreType.DMA((IN_RING,)),
            pltpu.SemaphoreType.DMA((IN_RING,)),
        ],
        compiler_params=pltpu.CompilerParams(
            dimension_semantics=("arbitrary",), vmem_limit_bytes=IN_PROJ_VMEM_LIMIT),
        name="in_proj",
    )(x2, norm_w.reshape(1, D_MODEL), w_in)


def _head_norm(t, w):
    ms = jnp.mean(t * t, axis=-1, keepdims=True)
    return t * lax.rsqrt(ms + NORM_EPS) * w


_NT = (((1,), (1,)), ((), ()))
_TN = (((0,), (0,)), ((), ()))


def _swa_kernel(sink_ref, q_ref, g_ref, kc_ref, vc_ref, kp_ref, vp_ref, qw_ref, kw_ref, o_ref,
                *, n_sub):
    log2e = math.log2(math.e)
    kqw = kw_ref[...] * qw_ref[...] * (log2e / math.sqrt(HEAD_DIM))
    n_col = Q_PER_KV * WINDOW
    key = lax.broadcasted_iota(jnp.int32, (WINDOW, n_col), 0)
    qry = lax.broadcasted_iota(jnp.int32, (WINDOW, n_col), 1) % WINDOW
    from_prev = key > qry
    no_prev = jnp.where(pl.program_id(1) == 0, -1e30, 0.0)
    head_of_col = lax.broadcasted_iota(jnp.int32, (1, n_col), 1) // WINDOW
    gw = Q_PER_KV * HEAD_DIM
    sel_r = lax.broadcasted_iota(jnp.int32, (8, 2 * gw), 0)
    sel_l = (lax.broadcasted_iota(jnp.int32, (8, 2 * gw), 1) % gw) // HEAD_DIM
    head_sel = jnp.where(sel_r == sel_l, 1.0, 0.0).astype(BF16)

    for g in range(N_KV_HEADS):
        kcol = slice(g * HEAD_DIM, (g + 1) * HEAD_DIM)
        gcols = slice(g * Q_PER_KV * HEAD_DIM, (g + 1) * Q_PER_KV * HEAD_DIM)
        sink = jnp.zeros((1, n_col), F32)
        for r in range(Q_PER_KV):
            sink = jnp.where(head_of_col == r, sink_ref[g * Q_PER_KV + r] * log2e, sink)
        k_prev = _head_norm(kp_ref[:, kcol].astype(F32), kqw).astype(BF16)
        v_prev = vp_ref[:, kcol]
        for n in range(n_sub):
            rows = slice(n * WINDOW, (n + 1) * WINDOW)
            k_cur = _head_norm(kc_ref[rows, kcol].astype(F32), kqw).astype(BF16)
            v_cur = vc_ref[rows, kcol]
            k_ctx = jnp.concatenate([k_prev, k_cur], axis=0)
            v_ctx = jnp.concatenate([v_prev, v_cur], axis=0)

            qg = q_ref[rows, gcols]
            qs = jnp.concatenate([qg[:, r * HEAD_DIM:(r + 1) * HEAD_DIM] for r in range(Q_PER_KV)],
                                 axis=0)
            qf = qg.astype(F32)
            q2 = qf * qf
            q2_hi = q2.astype(BF16)
            q2_lo = (q2 - q2_hi.astype(F32)).astype(BF16)
            ssq = lax.dot_general(head_sel, jnp.concatenate([q2_hi, q2_lo], axis=1), _NT,
                                  preferred_element_type=F32)
            rms = lax.rsqrt(ssq * (1.0 / HEAD_DIM) + NORM_EPS)
            rms_q = jnp.concatenate([rms[r:r + 1] for r in range(Q_PER_KV)], axis=1)

            s = lax.dot_general(k_ctx, qs, _NT, preferred_element_type=F32)
            s_prev = s[:WINDOW] + no_prev if n == 0 else s[:WINDOW]
            s = jnp.where(from_prev, s_prev, s[WINDOW:]) * rms_q
            mx = jnp.maximum(jnp.max(s, axis=0, keepdims=True), sink)
            p = jnp.exp2(s - mx)
            den = jnp.sum(p, axis=0, keepdims=True) + jnp.exp2(sink - mx)
            p_ctx = jnp.concatenate([jnp.where(from_prev, p, 0.0), jnp.where(from_prev, 0.0, p)],
                                    axis=0).astype(BF16)
            o_t = lax.dot_general(v_ctx, p_ctx, _TN, preferred_element_type=F32) * (1.0 / den)
            halves = [jnp.concatenate([o_t[:, (2 * h) * WINDOW:(2 * h + 1) * WINDOW],
                                       o_t[:, (2 * h + 1) * WINDOW:(2 * h + 2) * WINDOW]], axis=0).T
                      for h in range(Q_PER_KV // 2)]
            og = jnp.concatenate(halves, axis=1)
            gate = g_ref[rows, gcols].astype(F32)
            o_ref[rows, gcols] = (og * (gate * jax.nn.sigmoid(gate))).astype(BF16)
            k_prev, v_prev = k_cur, v_cur


def _swa(proj, sinks, q_norm_w, k_norm_w, batch, seq, tq=1024):
    n_sub = tq // WINDOW
    tiles = seq // tq
    m = batch * seq

    def cur_rows(b, i):
        return b * tiles + i

    def prev_rows(b, i):
        return jnp.maximum((b * tiles + i) * n_sub - 1, 0)

    kernel = functools.partial(_swa_kernel, n_sub=n_sub)
    return pl.pallas_call(
        kernel,
        grid=(batch, tiles),
        in_specs=[
            pl.BlockSpec(memory_space=pltpu.SMEM),
            pl.BlockSpec((tq, ATTN_W), lambda b, i: (cur_rows(b, i), COL_Q // ATTN_W)),
            pl.BlockSpec((tq, ATTN_W), lambda b, i: (cur_rows(b, i), COL_GATE // ATTN_W)),
            pl.BlockSpec((tq, KV_W), lambda b, i: (cur_rows(b, i), COL_K // KV_W)),
            pl.BlockSpec((tq, KV_W), lambda b, i: (cur_rows(b, i), COL_V // KV_W)),
            pl.BlockSpec((WINDOW, KV_W), lambda b, i: (prev_rows(b, i), COL_K // KV_W)),
            pl.BlockSpec((WINDOW, KV_W), lambda b, i: (prev_rows(b, i), COL_V // KV_W)),
            pl.BlockSpec((1, HEAD_DIM), lambda b, i: (0, 0)),
            pl.BlockSpec((1, HEAD_DIM), lambda b, i: (0, 0)),
        ],
        out_specs=pl.BlockSpec((tq, ATTN_W), lambda b, i: (cur_rows(b, i), 0)),
        out_shape=jax.ShapeDtypeStruct((m, ATTN_W), BF16),
        compiler_params=pltpu.CompilerParams(
            dimension_semantics=("arbitrary", "arbitrary"), vmem_limit_bytes=VMEM_LIMIT),
        name="swa",
    )(sinks, proj, proj, proj, proj, proj, proj,
      q_norm_w.reshape(1, HEAD_DIM), k_norm_w.reshape(1, HEAD_DIM))


def _slab_powers(ar, ai, dt, btr, bti):
    dta_re, dta_im = dt * ar, dt * ai
    mag = jnp.exp(dta_re)
    ab_re, ab_im = mag * jnp.cos(dta_im), mag * jnp.sin(dta_im)
    pw = [(jnp.ones_like(ar), jnp.zeros_like(ar))]
    for _ in range(CHUNK):
        pr, pi = pw[-1]
        pw.append((pr * ab_re - pi * ab_im, pr * ab_im + pi * ab_re))
    den = ar * ar + ai * ai
    num_re, num_im = ab_re - 1.0, ab_im
    cf_re = (num_re * ar + num_im * ai) / den
    cf_im = (num_im * ar - num_re * ai) / den
    bb_re = cf_re * btr - cf_im * bti
    bb_im = cf_re * bti + cf_im * btr
    abb = [(pr * bb_re - pi * bb_im, pr * bb_im + pi * bb_re) for pr, pi in pw[:CHUNK]]
    return pw, abb


def _build_state_in(abb, p_s):
    row_g = lax.broadcasted_iota(jnp.int32, (LANES, STATE), 0) // GROUP
    even = row_g % 2 == 0
    pair_of_row = lax.broadcasted_iota(jnp.int32, (LANES, 2 * LANES), 0) // (2 * GROUP)
    for lag in range(CHUNK):
        t = CHUNK - 1 - lag
        re, im = abb[lag]
        blk = jnp.concatenate([jnp.where(even, re, 0.0), jnp.where(even, 0.0, re),
                               jnp.where(even, im, 0.0), jnp.where(even, 0.0, im)], axis=1)
        for k in range(PAIRS):
            p_s[k, t * LANES:(t + 1) * LANES, :] = jnp.where(pair_of_row == k, blk, 0.0).astype(BF16)


def _build_state_out(pw, cr, ci, q_s):
    lane_g = lax.broadcasted_iota(jnp.int32, (STATE, LANES), 1) // GROUP
    for t in range(CHUNK):
        pr, pi = pw[t + 1]
        cat = jnp.concatenate([cr * pr - ci * pi, -(cr * pi + ci * pr)], axis=1)
        cat_t = cat.T
        for k in range(PAIRS):
            for ri in range(2):
                for half in range(2):
                    piece = jnp.where(lane_g == 2 * k + half, cat_t[ri * STATE:(ri + 1) * STATE], 0.0)
                    r0 = ri * LANES + half * STATE
                    q_s[k, r0:r0 + STATE, t * LANES:(t + 1) * LANES] = piece.astype(BF16)


def _build_toeplitz(abb, cr, ci, d_row, w_s):
    def split(v):
        hi = v.astype(BF16)
        return hi, (v - hi.astype(F32)).astype(BF16)

    rhs_hi, rhs_lo = split(jnp.concatenate([cr, -ci], axis=1))
    rhs = jnp.concatenate([rhs_hi, rhs_lo, rhs_hi], axis=1)
    row = lax.broadcasted_iota(jnp.int32, (LANES, LANES), 0)
    col = lax.broadcasted_iota(jnp.int32, (LANES, LANES), 1)
    same_group = (row // GROUP) == (col // GROUP)
    taps = []
    for lag in range(CHUNK):
        lhs_hi, lhs_lo = split(jnp.concatenate(abb[lag], axis=1))
        k = lax.dot_general(jnp.concatenate([lhs_hi, lhs_hi, lhs_lo], axis=1), rhs,
                            (((1,), (1,)), ((), ())), preferred_element_type=F32)
        k = jnp.where(same_group, k, 0.0)
        if lag == 0:
            k = k + jnp.where(row == col, d_row, 0.0)
        taps.append(k.astype(BF16))
    zero = jnp.zeros((LANES, LANES), BF16)
    for d in range(N_TILES):
        w_s[d, :LANES, :LANES] = taps[2 * d]
        w_s[d, :LANES, LANES:] = taps[2 * d + 1]
        w_s[d, LANES:, :LANES] = taps[2 * d - 1] if d > 0 else zero
        w_s[d, LANES:, LANES:] = taps[2 * d]


def _ssm_kernel(u_ref, ar_ref, ai_ref, dt_ref, btr_ref, bti_ref, cr_ref, ci_ref, d_ref,
                dtare_ref, dtaim_ref, w0_ref, w1_ref, w2_ref, w3_ref, y_ref, c0_ref, c1_ref, c2_ref,
                c3_ref, xre_s, xim_s, p_s, q_s, w_s, *, batch, n_chunks):
    step = pl.program_id(0)
    for src, dst in ((w0_ref, c0_ref), (w1_ref, c1_ref), (w2_ref, c2_ref), (w3_ref, c3_ref)):
        dst[...] = src[...].astype(BF16)
    rows8 = batch * PAIRS
    slab_rows = n_chunks * rows8

    def state_rows(slab, b, k):
        return pl.ds(slab * slab_rows + b * PAIRS + k, n_chunks, stride=rows8)

    def powers():
        return _slab_powers(ar_ref[...], ai_ref[...], dt_ref[...], btr_ref[...], bti_ref[...])

    @pl.when(step < N_SLABS)
    def _():
        _, abb = powers()
        _build_state_in(abb, p_s)
        u = u_ref[...]
        for k in range(PAIRS):
            x = _dot(u, p_s[k])
            for b in range(batch):
                xb = x[b * n_chunks:(b + 1) * n_chunks]
                xre_s[state_rows(step, b, k), :] = xb[:, :LANES]
                xim_s[state_rows(step, b, k), :] = xb[:, LANES:]

    @pl.when(step == N_SLABS)
    def _():
        dta_re = jnp.concatenate([dtare_ref[...]] * batch, axis=1).reshape(N_SLABS * rows8, LANES)
        dta_im = jnp.concatenate([dtaim_ref[...]] * batch, axis=1).reshape(N_SLABS * rows8, LANES)
        mag = jnp.exp(CHUNK * dta_re)
        a_re, a_im = mag * jnp.cos(CHUNK * dta_im), mag * jnp.sin(CHUNK * dta_im)

        def body(c, carry):
            s_re, s_im = carry
            x_re, x_im = [], []
            for sl in range(N_SLABS):
                off = pl.multiple_of(sl * slab_rows + c * rows8, rows8)
                x_re.append(xre_s[pl.ds(off, rows8), :])
                x_im.append(xim_s[pl.ds(off, rows8), :])
                xre_s[pl.ds(off, rows8), :] = s_re[sl * rows8:(sl + 1) * rows8]
                xim_s[pl.ds(off, rows8), :] = s_im[sl * rows8:(sl + 1) * rows8]
            x_re = jnp.concatenate(x_re, axis=0)
            x_im = jnp.concatenate(x_im, axis=0)
            return (a_re * s_re - a_im * s_im + x_re, a_re * s_im + a_im * s_re + x_im)

        zero = jnp.zeros((N_SLABS * rows8, LANES), F32)
        lax.fori_loop(0, n_chunks, body, (zero, zero))

    @pl.when(step > N_SLABS)
    def _():
        slab = step - (N_SLABS + 1)
        pw, abb = powers()
        cr, ci = cr_ref[...], ci_ref[...]
        _build_state_out(pw, cr, ci, q_s)
        _build_toeplitz(abb, cr, ci, d_ref[...], w_s)
        u = u_ref[...]
        y_state = None
        for k in range(PAIRS):
            s_in = jnp.concatenate(
                [jnp.concatenate([xre_s[state_rows(slab, b, k), :], xim_s[state_rows(slab, b, k), :]],
                                 axis=1) for b in range(batch)], axis=0).astype(BF16)
            part = _dot(s_in, q_s[k])
            y_state = part if y_state is None else y_state + part
        for t2 in range(N_TILES):
            acc = y_state[:, t2 * MXU:(t2 + 1) * MXU]
            for t1 in range(t2 + 1):
                acc = acc + _dot(u[:, t1 * MXU:(t1 + 1) * MXU], w_s[t2 - t1])
            y_ref[:, t2 * MXU:(t2 + 1) * MXU] = jax.nn.gelu(acc).astype(BF16)


N_CAST = 16


def _ssm(u_chunks, A_re, A_im, log_dt, B_re, B_im, C_re, C_im, D_skip, batch, tail_weights):
    _, rows, _ = u_chunks.shape
    n_chunks = rows // batch
    n_steps = 2 * N_SLABS + 1
    kernel = functools.partial(_ssm_kernel, batch=batch, n_chunks=n_chunks)

    dt = jnp.exp(log_dt)[:, None]
    per_row = lambda v: jnp.repeat(v, GROUP, axis=0)
    ar, ai = per_row(A_re), per_row(A_im)
    dtr = per_row(jnp.broadcast_to(dt, A_re.shape))
    btr = B_re.transpose(0, 2, 1).reshape(SSM_W, STATE)
    bti = B_im.transpose(0, 2, 1).reshape(SSM_W, STATE)
    cr, ci = C_re.reshape(SSM_W, STATE), C_im.reshape(SSM_W, STATE)
    d_rows = D_skip.reshape(N_SLABS, 1, LANES)
    dta_re = (dt * A_re).reshape(N_SLABS, PAIRS, LANES)
    dta_im = (dt * A_im).reshape(N_SLABS, PAIRS, LANES)

    def slab_in(s):
        return jnp.where(s <= N_SLABS, jnp.minimum(s, N_SLABS - 1), s - (N_SLABS + 1))

    def slab_out(s):
        return jnp.maximum(s - (N_SLABS + 1), 0)

    def cast_spec(w):
        return pl.BlockSpec((w.shape[0] // N_CAST, w.shape[1]),
                            lambda s: (jnp.minimum(s, N_CAST - 1), 0))

    coeff = pl.BlockSpec((LANES, STATE), lambda s: (slab_in(s), 0))
    whole = pl.BlockSpec((N_SLABS, PAIRS, LANES), lambda s: (0, 0, 0))
    return pl.pallas_call(
        kernel,
        grid=(n_steps,),
        in_specs=[
            pl.BlockSpec((None, rows, CHUNK_W), lambda s: (slab_in(s), 0, 0)),
            coeff, coeff, coeff, coeff, coeff, coeff, coeff,
            pl.BlockSpec((None, 1, LANES), lambda s: (slab_in(s), 0, 0)),
            whole, whole,
        ] + [cast_spec(w) for w in tail_weights],
        out_specs=[pl.BlockSpec((None, rows, CHUNK_W), lambda s: (slab_out(s), 0, 0))]
        + [cast_spec(w) for w in tail_weights],
        out_shape=[jax.ShapeDtypeStruct(u_chunks.shape, BF16)]
        + [jax.ShapeDtypeStruct(w.shape, BF16) for w in tail_weights],
        scratch_shapes=[
            pltpu.VMEM((N_SLABS * n_chunks * batch * PAIRS, LANES), F32),
            pltpu.VMEM((N_SLABS * n_chunks * batch * PAIRS, LANES), F32),
            pltpu.VMEM((PAIRS, CHUNK_W, 2 * LANES), BF16),
            pltpu.VMEM((PAIRS, 2 * LANES, CHUNK_W), BF16),
            pltpu.VMEM((N_TILES, MXU, MXU), BF16),
        ],
        compiler_params=pltpu.CompilerParams(
            dimension_semantics=("arbitrary",), vmem_limit_bytes=VMEM_LIMIT),
        name="ssm",
    )(u_chunks, ar, ai, dtr, btr, bti, cr, ci, d_rows, dta_re, dta_im, *tail_weights)


def _tail_kernel(ag_ref, yc_ref, z_ref, ga_ref, gs_ref, x_ref, wa_ref, wg_ref, bg_ref, ws_ref,
                 wo_ref, o_ref, ys_ref, *, tm):
    rows = tm // CHUNK
    for s in range(N_SLABS):
        for t in range(CHUNK):
            ys_ref[pl.ds(s * tm + t, rows, stride=CHUNK), :] = (
                yc_ref[s, :, t * LANES:(t + 1) * LANES].astype(F32))
    y_g = jnp.concatenate([ys_ref[s * tm:(s + 1) * tm, :] for s in range(N_SLABS)],
                          axis=1).astype(BF16)

    y_a = _dot(ag_ref[...], wa_ref[...])
    glu = _dot(y_g, wg_ref[...]) + bg_ref[...]
    z = z_ref[...].astype(F32)
    t = glu[:, :SSM_W] * jax.nn.sigmoid(glu[:, SSM_W:]) * (z * jax.nn.sigmoid(z))
    y_s = _dot(t.astype(BF16), ws_ref[...])
    merged = (jax.nn.sigmoid(ga_ref[...].astype(F32)) * y_a
              + jax.nn.sigmoid(gs_ref[...].astype(F32)) * y_s)
    o_ref[...] = x_ref[...] + _dot(merged.astype(BF16), wo_ref[...])


def _tail(attn_g, y_chunks, proj, x2, w_attn, w_glu, b_glu, w_ssm, w_out, tm=256):
    m = x2.shape[0]
    row = lambda blk: (lambda i: (i, blk))
    const = lambda i: (0, 0)
    resident = functools.partial(pl.BlockSpec, index_map=const, pipeline_mode=pl.Buffered(1))
    kernel = functools.partial(_tail_kernel, tm=tm)
    return pl.pallas_call(
        kernel,
        grid=(m // tm,),
        in_specs=[
            pl.BlockSpec((tm, ATTN_W), row(0)),
            pl.BlockSpec((N_SLABS, tm // CHUNK, CHUNK_W), lambda i: (0, i, 0)),
            pl.BlockSpec((tm, SSM_W), row(COL_Z // SSM_W)),
            pl.BlockSpec((tm, D_MODEL), row(COL_GA // D_MODEL)),
            pl.BlockSpec((tm, D_MODEL), row(COL_GS // D_MODEL)),
            pl.BlockSpec((tm, D_MODEL), row(0)),
            resident((ATTN_W, D_MODEL)),
            resident((SSM_W, 2 * SSM_W)),
            resident((1, 2 * SSM_W)),
            resident((SSM_W, D_MODEL)),
            resident((D_MODEL, D_MODEL)),
        ],
        out_specs=pl.BlockSpec((tm, D_MODEL), row(0)),
        out_shape=jax.ShapeDtypeStruct((m, D_MODEL), F32),
        scratch_shapes=[pltpu.VMEM((N_SLABS * tm, LANES), F32)],
        compiler_params=pltpu.CompilerParams(
            dimension_semantics=("arbitrary",), vmem_limit_bytes=VMEM_LIMIT),
        name="tail",
    )(attn_g, y_chunks, proj, proj, proj, x2, w_attn, w_glu, b_glu.reshape(1, -1), w_ssm, w_out)


def kernel(x, norm_w, w_in, q_norm_w, k_norm_w, sinks, w_attn_proj, A_re, A_im, log_dt, B_re, B_im,
           C_re, C_im, D_skip, w_glu, b_glu, w_ssm_proj, w_out):
    batch, seq, _ = x.shape
    m = batch * seq
    x2 = x.reshape(m, D_MODEL)

    proj, u_chunks = _in_proj(x2, norm_w, w_in)

    attn_g = _swa(proj, sinks, q_norm_w, k_norm_w, batch, seq)

    y_chunks, w_attn_bf, w_glu_bf, w_ssm_bf, w_out_bf = _ssm(
        u_chunks, A_re, A_im, log_dt, B_re, B_im, C_re, C_im, D_skip, batch,
        (w_attn_proj, w_glu, w_ssm_proj, w_out))

    out = _tail(attn_g, y_chunks, proj, x2, w_attn_bf, w_glu_bf, b_glu, w_ssm_bf, w_out_bf)
    return out.reshape(batch, seq, D_MODEL)
```

```python
import functools
import math

import jax
import jax.numpy as jnp
from jax import lax
from jax.experimental import pallas as pl
from jax.experimental.pallas import tpu as pltpu

F32 = jnp.float32
BF16 = jnp.bfloat16

D_MODEL = 2048
HEAD_DIM = 64
N_Q_HEADS = 16
N_KV_HEADS = 4
Q_PER_KV = 4
ATTN_W = N_Q_HEADS * HEAD_DIM
KV_W = N_KV_HEADS * HEAD_DIM
WINDOW = 128
SSM_W = D_MODEL // 2
GROUP = 16
N_GROUPS = SSM_W // GROUP
STATE = 64
NORM_EPS = 1e-6

COL_Q, COL_GATE, COL_U, COL_Z = 0, 1024, 2048, 3072
COL_GA, COL_GS, COL_K, COL_V = 4096, 6144, 8192, 8448
IN_W = 8704

LANES = 128
MXU = 256
CHUNK = 16
SLAB_G = LANES // GROUP
N_SLABS = N_GROUPS // SLAB_G
PAIRS = SLAB_G // 2
CHUNK_W = CHUNK * LANES
N_TILES = CHUNK_W // MXU
VMEM_LIMIT = 48 * 1024 * 1024
BIG_VMEM_LIMIT = 56 * 1024 * 1024


def _dot(a, b):
    return jnp.dot(a, b, preferred_element_type=F32)


IN_TN = 512
N_IN_TILES = IN_W // IN_TN
SRC_KV_TILE = (ATTN_W) // IN_TN
SRC_U_TILE0 = (ATTN_W + 2 * KV_W + ATTN_W) // IN_TN


US_PITCH = CHUNK + 8
IN_PER_ROUND = 2
IN_RING = 2 * IN_PER_ROUND


def _dst_tile(src_tile):
    if src_tile < SRC_KV_TILE:
        return src_tile
    if src_tile == SRC_KV_TILE:
        return N_IN_TILES - 1
    return src_tile - 1


def _in_proj_kernel(x_ref, nw_ref, w_hbm, o_hbm, uc_ref, h_ref, w_buf, o_buf, us_ref, w_sem, o_sem,
                    *, tm):
    i = pl.program_id(0)
    rows = tm // CHUNK
    n_sl = IN_TN // LANES

    def w_copy(t):
        return pltpu.make_async_copy(w_hbm.at[:, pl.ds(t * IN_TN, IN_TN)], w_buf.at[t % IN_RING],
                                     w_sem.at[t % IN_RING])

    def o_copy(t):
        dst = o_hbm.at[pl.ds(pl.multiple_of(i * tm, tm), tm), pl.ds(_dst_tile(t) * IN_TN, IN_TN)]
        return pltpu.make_async_copy(o_buf.at[t % IN_RING], dst, o_sem.at[t % IN_RING])

    def round_tiles(r):
        return [t for t in range(r * IN_PER_ROUND, (r + 1) * IN_PER_ROUND) if 0 <= t < N_IN_TILES]

    for t in round_tiles(0):
        w_copy(t).start()
    x = x_ref[...]
    ms = jnp.mean(x * x, axis=-1, keepdims=True)
    h_ref[...] = (x * lax.rsqrt(ms + NORM_EPS) * nw_ref[...]).astype(BF16)

    def finish(t, acc):
        o_buf[t % IN_RING] = acc.astype(BF16)
        if SRC_U_TILE0 <= t < SRC_U_TILE0 + SSM_W // IN_TN:
            slab0 = (t - SRC_U_TILE0) * n_sl
            for s in range(n_sl):
                for c in range(rows):
                    r0 = (s * rows + c) * US_PITCH
                    us_ref[r0:r0 + CHUNK, :] = acc[c * CHUNK:(c + 1) * CHUNK, s * LANES:(s + 1) * LANES]
            for s in range(n_sl):
                for tok in range(CHUNK):
                    piece = us_ref[pl.ds(s * rows * US_PITCH + tok, rows, stride=US_PITCH), :]
                    uc_ref[slab0 + s, :, tok * LANES:(tok + 1) * LANES] = piece.astype(BF16)

    n_rounds = pl.cdiv(N_IN_TILES, IN_PER_ROUND)
    for r in range(n_rounds):
        for t in round_tiles(r + 1):
            w_copy(t).start()
        for t in round_tiles(r):
            w_copy(t).wait()
        for t in round_tiles(r - 2):
            o_copy(t).wait()
        for t in round_tiles(r - 1):
            o_copy(t).start()
        for t in round_tiles(r):
            acc = None
            for k in range(D_MODEL // MXU):
                ks = slice(k * MXU, (k + 1) * MXU)
                part = _dot(h_ref[:, ks], w_buf[t % IN_RING, ks, :].astype(BF16))
                acc = part if acc is None else acc + part
            finish(t, acc)

    for t in round_tiles(n_rounds - 1):
        o_copy(t).start()
    for t in round_tiles(n_rounds - 2) + round_tiles(n_rounds - 1):
        o_copy(t).wait()


def _in_proj(x2, norm_w, w_in, tm=1024):
    m = x2.shape[0]
    kernel = functools.partial(_in_proj_kernel, tm=tm)
    return pl.pallas_call(
        kernel,
        grid=(m // tm,),
        in_specs=[
            pl.BlockSpec((tm, D_MODEL), lambda i: (i, 0)),
            pl.BlockSpec((1, D_MODEL), lambda i: (0, 0)),
            pl.BlockSpec(memory_space=pl.ANY),
        ],
        out_specs=[
            pl.BlockSpec(memory_space=pl.ANY),
            pl.BlockSpec((N_SLABS, tm // CHUNK, CHUNK_W), lambda i: (0, i, 0)),
        ],
        out_shape=[
            jax.ShapeDtypeStruct((m, IN_W), BF16),
            jax.ShapeDtypeStruct((N_SLABS, m // CHUNK, CHUNK_W), BF16),
        ],
        scratch_shapes=[
            pltpu.VMEM((tm, D_MODEL), BF16),
            pltpu.VMEM((IN_RING, D_MODEL, IN_TN), F32),
            pltpu.VMEM((IN_RING, tm, IN_TN), BF16),
            pltpu.VMEM((IN_TN // LANES * (tm // CHUNK) * US_PITCH, LANES), F32),
            pltpu.SemaphoreType.DMA((IN_RING,)),
            pltpu.SemaphoreType.DMA((IN_RING,)),
        ],
        compiler_params=pltpu.CompilerParams(
            dimension_semantics=("arbitrary",), vmem_limit_bytes=BIG_VMEM_LIMIT),
        name="in_proj",
    )(x2, norm_w.reshape(1, D_MODEL), w_in)


def _head_norm(t, w):
    ms = jnp.mean(t * t, axis=-1, keepdims=True)
    return t * lax.rsqrt(ms + NORM_EPS) * w


_NT = (((1,), (1,)), ((), ()))
_TN = (((0,), (0,)), ((), ()))


SWA_TQ = 512
SWA_SUB = SWA_TQ // WINDOW


def _swa_tile(sink_ref, q_ref, g_ref, kc_ref, vc_ref, kp_ref, vp_ref, qw_ref, kw_ref, o_ref,
              first_tile):
    log2e = math.log2(math.e)
    kqw = kw_ref[...] * qw_ref[...] * (log2e / math.sqrt(HEAD_DIM))
    n_col = Q_PER_KV * WINDOW
    key = lax.broadcasted_iota(jnp.int32, (WINDOW, n_col), 0)
    qry = lax.broadcasted_iota(jnp.int32, (WINDOW, n_col), 1) % WINDOW
    from_prev = key > qry
    no_prev = jnp.where(first_tile, -1e30, 0.0)
    head_of_col = lax.broadcasted_iota(jnp.int32, (1, n_col), 1) // WINDOW
    gw = Q_PER_KV * HEAD_DIM
    sel_r = lax.broadcasted_iota(jnp.int32, (8, 2 * gw), 0)
    sel_l = (lax.broadcasted_iota(jnp.int32, (8, 2 * gw), 1) % gw) // HEAD_DIM
    head_sel = jnp.where(sel_r == sel_l, 1.0, 0.0).astype(BF16)

    parts = []
    carry = {}

    def block(g, n):
        kcol = slice(g * HEAD_DIM, (g + 1) * HEAD_DIM)
        gcols = slice(g * Q_PER_KV * HEAD_DIM, (g + 1) * Q_PER_KV * HEAD_DIM)
        if n == 0:
            sink = jnp.zeros((1, n_col), F32)
            for r in range(Q_PER_KV):
                sink = jnp.where(head_of_col == r, sink_ref[g * Q_PER_KV + r] * log2e, sink)
            carry['sink'] = sink
            carry['k'] = _head_norm(kp_ref[:, kcol].astype(F32), kqw).astype(BF16)
            carry['v'] = vp_ref[:, kcol]
        sink, k_prev, v_prev = carry['sink'], carry['k'], carry['v']
        rows = slice(n * WINDOW, (n + 1) * WINDOW)
        k_cur = _head_norm(kc_ref[rows, kcol].astype(F32), kqw).astype(BF16)
        v_cur = vc_ref[rows, kcol]
        k_ctx = jnp.concatenate([k_prev, k_cur], axis=0)
        v_ctx = jnp.concatenate([v_prev, v_cur], axis=0)

        qg = q_ref[rows, gcols]
        qs = jnp.concatenate([qg[:, r * HEAD_DIM:(r + 1) * HEAD_DIM] for r in range(Q_PER_KV)],
                             axis=0)
        qf = qg.astype(F32)
        q2 = qf * qf
        q2_hi = q2.astype(BF16)
        q2_lo = (q2 - q2_hi.astype(F32)).astype(BF16)
        ssq = lax.dot_general(head_sel, jnp.concatenate([q2_hi, q2_lo], axis=1), _NT,
                              preferred_element_type=F32)
        rms = lax.rsqrt(ssq * (1.0 / HEAD_DIM) + NORM_EPS)
        rms_q = jnp.concatenate([rms[r:r + 1] for r in range(Q_PER_KV)], axis=1)

        s = lax.dot_general(k_ctx, qs, _NT, preferred_element_type=F32)
        s_prev = s[:WINDOW] + no_prev if n == 0 else s[:WINDOW]
        s = jnp.where(from_prev, s_prev, s[WINDOW:]) * rms_q
        mx = jnp.maximum(jnp.max(s, axis=0, keepdims=True), sink)
        p = jnp.exp2(s - mx)
        den = jnp.sum(p, axis=0, keepdims=True) + jnp.exp2(sink - mx)
        p_ctx = jnp.concatenate([jnp.where(from_prev, p, 0.0), jnp.where(from_prev, 0.0, p)],
                                axis=0).astype(BF16)
        o_t = lax.dot_general(v_ctx, p_ctx, _TN, preferred_element_type=F32) * (1.0 / den)
        halves = [jnp.concatenate([o_t[:, (2 * h) * WINDOW:(2 * h + 1) * WINDOW],
                                   o_t[:, (2 * h + 1) * WINDOW:(2 * h + 2) * WINDOW]], axis=0).T
                  for h in range(Q_PER_KV // 2)]
        og = jnp.concatenate(halves, axis=1)
        gate = g_ref[rows, gcols].astype(F32)
        o_ref[rows, gcols] = (og * (gate * jax.nn.sigmoid(gate))).astype(BF16)
        carry['k'], carry['v'] = k_cur, v_cur

    for g in range(N_KV_HEADS):
        for n in range(SWA_SUB):
            parts.append(functools.partial(block, g, n))
    return parts


def _swa_specs(tile_of_step):
    cur = lambda col: (lambda s: (tile_of_step(s), col))
    prev = lambda col: (lambda s: (jnp.maximum(tile_of_step(s) * SWA_SUB - 1, 0), col))
    in_specs = [
        pl.BlockSpec(memory_space=pltpu.SMEM),
        pl.BlockSpec((SWA_TQ, ATTN_W), cur(COL_Q // ATTN_W)),
        pl.BlockSpec((SWA_TQ, ATTN_W), cur(COL_GATE // ATTN_W)),
        pl.BlockSpec((SWA_TQ, KV_W), cur(COL_K // KV_W)),
        pl.BlockSpec((SWA_TQ, KV_W), cur(COL_V // KV_W)),
        pl.BlockSpec((WINDOW, KV_W), prev(COL_K // KV_W)),
        pl.BlockSpec((WINDOW, KV_W), prev(COL_V // KV_W)),
        pl.BlockSpec((1, HEAD_DIM), lambda s: (0, 0)),
        pl.BlockSpec((1, HEAD_DIM), lambda s: (0, 0)),
    ]
    return in_specs, pl.BlockSpec((SWA_TQ, ATTN_W), cur(0))


def _slab_powers(ar, ai, dt, btr, bti):
    dta_re, dta_im = dt * ar, dt * ai
    mag = jnp.exp(dta_re)
    ab_re, ab_im = mag * jnp.cos(dta_im), mag * jnp.sin(dta_im)
    pw = [(jnp.ones_like(ar), jnp.zeros_like(ar))]
    for _ in range(CHUNK):
        pr, pi = pw[-1]
        pw.append((pr * ab_re - pi * ab_im, pr * ab_im + pi * ab_re))
    den = ar * ar + ai * ai
    num_re, num_im = ab_re - 1.0, ab_im
    cf_re = (num_re * ar + num_im * ai) / den
    cf_im = (num_im * ar - num_re * ai) / den
    bb_re = cf_re * btr - cf_im * bti
    bb_im = cf_re * bti + cf_im * btr
    abb = [(pr * bb_re - pi * bb_im, pr * bb_im + pi * bb_re) for pr, pi in pw[:CHUNK]]
    return pw, abb


def _build_state_in(abb, p_s):
    row_g = lax.broadcasted_iota(jnp.int32, (LANES, STATE), 0) // GROUP
    even = row_g % 2 == 0
    pair_of_row = lax.broadcasted_iota(jnp.int32, (LANES, 2 * LANES), 0) // (2 * GROUP)
    for lag in range(CHUNK):
        t = CHUNK - 1 - lag
        re, im = abb[lag]
        blk = jnp.concatenate([jnp.where(even, re, 0.0), jnp.where(even, 0.0, re),
                               jnp.where(even, im, 0.0), jnp.where(even, 0.0, im)], axis=1)
        for k in range(PAIRS):
            p_s[k, t * LANES:(t + 1) * LANES, :] = jnp.where(pair_of_row == k, blk, 0.0).astype(BF16)


def _build_state_out(pw, cr, ci, q_s):
    lane_g = lax.broadcasted_iota(jnp.int32, (STATE, LANES), 1) // GROUP
    for t in range(CHUNK):
        pr, pi = pw[t + 1]
        cat = jnp.concatenate([cr * pr - ci * pi, -(cr * pi + ci * pr)], axis=1)
        cat_t = cat.T
        for k in range(PAIRS):
            for ri in range(2):
                for half in range(2):
                    piece = jnp.where(lane_g == 2 * k + half, cat_t[ri * STATE:(ri + 1) * STATE], 0.0)
                    r0 = ri * LANES + half * STATE
                    q_s[k, r0:r0 + STATE, t * LANES:(t + 1) * LANES] = piece.astype(BF16)


def _build_toeplitz(abb, cr, ci, d_row, w_s):
    def split(v):
        hi = v.astype(BF16)
        return hi, (v - hi.astype(F32)).astype(BF16)

    rhs_hi, rhs_lo = split(jnp.concatenate([cr, -ci], axis=1))
    rhs = jnp.concatenate([rhs_hi, rhs_lo, rhs_hi], axis=1)
    row = lax.broadcasted_iota(jnp.int32, (LANES, LANES), 0)
    col = lax.broadcasted_iota(jnp.int32, (LANES, LANES), 1)
    same_group = (row // GROUP) == (col // GROUP)
    taps = []
    for lag in range(CHUNK):
        lhs_hi, lhs_lo = split(jnp.concatenate(abb[lag], axis=1))
        k = lax.dot_general(jnp.concatenate([lhs_hi, lhs_hi, lhs_lo], axis=1), rhs,
                            (((1,), (1,)), ((), ())), preferred_element_type=F32)
        k = jnp.where(same_group, k, 0.0)
        if lag == 0:
            k = k + jnp.where(row == col, d_row, 0.0)
        taps.append(k.astype(BF16))
    zero = jnp.zeros((LANES, LANES), BF16)
    for d in range(N_TILES):
        w_s[d, :LANES, :LANES] = taps[2 * d]
        w_s[d, :LANES, LANES:] = taps[2 * d + 1]
        w_s[d, LANES:, :LANES] = taps[2 * d - 1] if d > 0 else zero
        w_s[d, LANES:, LANES:] = taps[2 * d]


def _ssm_swa_kernel(u_ref, ar_ref, ai_ref, dt_ref, btr_ref, bti_ref, cr_ref, ci_ref, d_ref,
                    dtare_ref, dtaim_ref, w0_ref, w1_ref, w2_ref, w3_ref,
                    sink_ref, q_ref, g_ref, kc_ref, vc_ref, kp_ref, vp_ref, qw_ref, kw_ref,
                    y_ref, c0_ref, c1_ref, c2_ref, c3_ref, attn_ref,
                    xre_s, xim_s, p_s, q_s, w_s, *, batch, n_chunks, tiles_per_seq, n_tiles):
    step = pl.program_id(0)
    for src, dst in ((w0_ref, c0_ref), (w1_ref, c1_ref), (w2_ref, c2_ref), (w3_ref, c3_ref)):
        dst[...] = src[...].astype(BF16)

    def attention_parts():
        tile = jnp.minimum(step, n_tiles - 1)
        return _swa_tile(sink_ref, q_ref, g_ref, kc_ref, vc_ref, kp_ref, vp_ref, qw_ref, kw_ref,
                         attn_ref, tile % tiles_per_seq == 0)
    rows8 = batch * PAIRS
    slab_rows = n_chunks * rows8

    def state_rows(slab, b, k):
        return pl.ds(slab * slab_rows + b * PAIRS + k, n_chunks, stride=rows8)

    def powers():
        return _slab_powers(ar_ref[...], ai_ref[...], dt_ref[...], btr_ref[...], bti_ref[...])

    @pl.when(step < N_SLABS)
    def _():
        attn = attention_parts()
        per_k = len(attn) // PAIRS
        _, abb = powers()
        _build_state_in(abb, p_s)
        u = u_ref[...]
        for k in range(PAIRS):
            x = _dot(u, p_s[k])
            for b in range(batch):
                xb = x[b * n_chunks:(b + 1) * n_chunks]
                xre_s[state_rows(step, b, k), :] = xb[:, :LANES]
                xim_s[state_rows(step, b, k), :] = xb[:, LANES:]
            for part in attn[k * per_k:(k + 1) * per_k]:
                part()

    @pl.when(step == N_SLABS)
    def _():
        for part in attention_parts():
            part()
        dta_re =jnp.concatenate([dtare_ref[...]] * batch, axis=1).reshape(N_SLABS * rows8, LANES)
        dta_im = jnp.concatenate([dtaim_ref[...]] * batch, axis=1).reshape(N_SLABS * rows8, LANES)
        mag = jnp.exp(CHUNK * dta_re)
        a_re, a_im = mag * jnp.cos(CHUNK * dta_im), mag * jnp.sin(CHUNK * dta_im)

        def body(c, carry):
            s_re, s_im = carry
            x_re, x_im = [], []
            for sl in range(N_SLABS):
                off = pl.multiple_of(sl * slab_rows + c * rows8, rows8)
                x_re.append(xre_s[pl.ds(off, rows8), :])
                x_im.append(xim_s[pl.ds(off, rows8), :])
                xre_s[pl.ds(off, rows8), :] = s_re[sl * rows8:(sl + 1) * rows8]
                xim_s[pl.ds(off, rows8), :] = s_im[sl * rows8:(sl + 1) * rows8]
            x_re = jnp.concatenate(x_re, axis=0)
            x_im = jnp.concatenate(x_im, axis=0)
            return (a_re * s_re - a_im * s_im + x_re, a_re * s_im + a_im * s_re + x_im)

        zero = jnp.zeros((N_SLABS * rows8, LANES), F32)
        lax.fori_loop(0, n_chunks, body, (zero, zero))

    @pl.when(step > N_SLABS)
    def _():
        attn = attention_parts()
        slab = step - (N_SLABS + 1)
        pw, abb = powers()
        cr, ci = cr_ref[...], ci_ref[...]
        _build_state_out(pw, cr, ci, q_s)
        _build_toeplitz(abb, cr, ci, d_ref[...], w_s)
        u = u_ref[...]
        y_state = None
        for k in range(PAIRS):
            s_in = jnp.concatenate(
                [jnp.concatenate([xre_s[state_rows(slab, b, k), :], xim_s[state_rows(slab, b, k), :]],
                                 axis=1) for b in range(batch)], axis=0).astype(BF16)
            part = _dot(s_in, q_s[k])
            y_state = part if y_state is None else y_state + part
            attn[2 * k]()
            attn[2 * k + 1]()
        for t2 in range(N_TILES):
            acc = y_state[:, t2 * MXU:(t2 + 1) * MXU]
            for t1 in range(t2 + 1):
                acc = acc + _dot(u[:, t1 * MXU:(t1 + 1) * MXU], w_s[t2 - t1])
            y_ref[:, t2 * MXU:(t2 + 1) * MXU] = jax.nn.gelu(acc).astype(BF16)
            attn[2 * PAIRS + t2]()


N_CAST = 16


def _ssm_swa(u_chunks, A_re, A_im, log_dt, B_re, B_im, C_re, C_im, D_skip, tail_weights,
             proj, sinks, q_norm_w, k_norm_w, batch, seq):
    _, rows, _ = u_chunks.shape
    n_chunks = rows // batch
    n_steps = 2 * N_SLABS + 1
    m = batch * seq
    n_tiles = m // SWA_TQ
    assert n_tiles <= n_steps
    kernel = functools.partial(_ssm_swa_kernel, batch=batch, n_chunks=n_chunks,
                               tiles_per_seq=seq // SWA_TQ, n_tiles=n_tiles)
    swa_in_specs, swa_out_spec = _swa_specs(lambda s: jnp.minimum(s, n_tiles - 1))

    dt = jnp.exp(log_dt)[:, None]
    per_row = lambda v: jnp.repeat(v, GROUP, axis=0)
    ar, ai = per_row(A_re), per_row(A_im)
    dtr = per_row(jnp.broadcast_to(dt, A_re.shape))
    btr = B_re.transpose(0, 2, 1).reshape(SSM_W, STATE)
    bti = B_im.transpose(0, 2, 1).reshape(SSM_W, STATE)
    cr, ci = C_re.reshape(SSM_W, STATE), C_im.reshape(SSM_W, STATE)
    d_rows = D_skip.reshape(N_SLABS, 1, LANES)
    dta_re = (dt * A_re).reshape(N_SLABS, PAIRS, LANES)
    dta_im = (dt * A_im).reshape(N_SLABS, PAIRS, LANES)

    def slab_in(s):
        return jnp.where(s <= N_SLABS, jnp.minimum(s, N_SLABS - 1), s - (N_SLABS + 1))

    def slab_out(s):
        return jnp.maximum(s - (N_SLABS + 1), 0)

    def cast_spec(w):
        return pl.BlockSpec((w.shape[0] // N_CAST, w.shape[1]),
                            lambda s: (jnp.minimum(s, N_CAST - 1), 0))

    coeff = pl.BlockSpec((LANES, STATE), lambda s: (slab_in(s), 0))
    whole = pl.BlockSpec((N_SLABS, PAIRS, LANES), lambda s: (0, 0, 0))
    return pl.pallas_call(
        kernel,
        grid=(n_steps,),
        in_specs=[
            pl.BlockSpec((None, rows, CHUNK_W), lambda s: (slab_in(s), 0, 0)),
            coeff, coeff, coeff, coeff, coeff, coeff, coeff,
            pl.BlockSpec((None, 1, LANES), lambda s: (slab_in(s), 0, 0)),
            whole, whole,
        ] + [cast_spec(w) for w in tail_weights] + swa_in_specs,
        out_specs=[pl.BlockSpec((None, rows, CHUNK_W), lambda s: (slab_out(s), 0, 0))]
        + [cast_spec(w) for w in tail_weights] + [swa_out_spec],
        out_shape=[jax.ShapeDtypeStruct(u_chunks.shape, BF16)]
        + [jax.ShapeDtypeStruct(w.shape, BF16) for w in tail_weights]
        + [jax.ShapeDtypeStruct((m, ATTN_W), BF16)],
        scratch_shapes=[
            pltpu.VMEM((N_SLABS * n_chunks * batch * PAIRS, LANES), F32),
            pltpu.VMEM((N_SLABS * n_chunks * batch * PAIRS, LANES), F32),
            pltpu.VMEM((PAIRS, CHUNK_W, 2 * LANES), BF16),
            pltpu.VMEM((PAIRS, 2 * LANES, CHUNK_W), BF16),
            pltpu.VMEM((N_TILES, MXU, MXU), BF16),
        ],
        compiler_params=pltpu.CompilerParams(
            dimension_semantics=("arbitrary",), vmem_limit_bytes=BIG_VMEM_LIMIT),
        name="ssm_swa",
    )(u_chunks, ar, ai, dtr, btr, bti, cr, ci, d_rows, dta_re, dta_im, *tail_weights,
      sinks, proj, proj, proj, proj, proj, proj,
      q_norm_w.reshape(1, HEAD_DIM), k_norm_w.reshape(1, HEAD_DIM))


def _tail_kernel(ag_ref, yc_ref, z_ref, ga_ref, gs_ref, x_ref, wa_ref, wg_ref, bg_ref, ws_ref,
                 wo_ref, o_ref, ys_ref, *, tm):
    rows = tm // CHUNK
    for s in range(N_SLABS):
        for t in range(CHUNK):
            ys_ref[pl.ds(s * tm + t, rows, stride=CHUNK), :] = (
                yc_ref[s, :, t * LANES:(t + 1) * LANES].astype(F32))
    y_g = jnp.concatenate([ys_ref[s * tm:(s + 1) * tm, :] for s in range(N_SLABS)],
                          axis=1).astype(BF16)

    y_a = _dot(ag_ref[...], wa_ref[...])
    glu = _dot(y_g, wg_ref[...]) + bg_ref[...]
    z = z_ref[...].astype(F32)
    t = glu[:, :SSM_W] * jax.nn.sigmoid(glu[:, SSM_W:]) * (z * jax.nn.sigmoid(z))
    y_s = _dot(t.astype(BF16), ws_ref[...])
    merged = (jax.nn.sigmoid(ga_ref[...].astype(F32)) * y_a
              + jax.nn.sigmoid(gs_ref[...].astype(F32)) * y_s)
    o_ref[...] = x_ref[...] + _dot(merged.astype(BF16), wo_ref[...])


def _tail(attn_g, y_chunks, proj, x2, w_attn, w_glu, b_glu, w_ssm, w_out, tm=256):
    m = x2.shape[0]
    row = lambda blk: (lambda i: (i, blk))
    const = lambda i: (0, 0)
    resident = functools.partial(pl.BlockSpec, index_map=const, pipeline_mode=pl.Buffered(1))
    kernel = functools.partial(_tail_kernel, tm=tm)
    return pl.pallas_call(
        kernel,
        grid=(m // tm,),
        in_specs=[
            pl.BlockSpec((tm, ATTN_W), row(0)),
            pl.BlockSpec((N_SLABS, tm // CHUNK, CHUNK_W), lambda i: (0, i, 0)),
            pl.BlockSpec((tm, SSM_W), row(COL_Z // SSM_W)),
            pl.BlockSpec((tm, D_MODEL), row(COL_GA // D_MODEL)),
            pl.BlockSpec((tm, D_MODEL), row(COL_GS // D_MODEL)),
            pl.BlockSpec((tm, D_MODEL), row(0)),
            resident((ATTN_W, D_MODEL)),
            resident((SSM_W, 2 * SSM_W)),
            resident((1, 2 * SSM_W)),
            resident((SSM_W, D_MODEL)),
            resident((D_MODEL, D_MODEL)),
        ],
        out_specs=pl.BlockSpec((tm, D_MODEL), row(0)),
        out_shape=jax.ShapeDtypeStruct((m, D_MODEL), F32),
        scratch_shapes=[pltpu.VMEM((N_SLABS * tm, LANES), F32)],
        compiler_params=pltpu.CompilerParams(
            dimension_semantics=("arbitrary",), vmem_limit_bytes=VMEM_LIMIT),
        name="tail",
    )(attn_g, y_chunks, proj, proj, proj, x2, w_attn, w_glu, b_glu.reshape(1, -1), w_ssm, w_out)


def kernel(x, norm_w, w_in, q_norm_w, k_norm_w, sinks, w_attn_proj, A_re, A_im, log_dt, B_re, B_im,
           C_re, C_im, D_skip, w_glu, b_glu, w_ssm_proj, w_out):
    batch, seq, _ = x.shape
    m = batch * seq
    x2 = x.reshape(m, D_MODEL)

    proj, u_chunks = _in_proj(x2, norm_w, w_in)

    y_chunks, w_attn_bf, w_glu_bf, w_ssm_bf, w_out_bf, attn_g = _ssm_swa(
        u_chunks, A_re, A_im, log_dt, B_re, B_im, C_re, C_im, D_skip,
        (w_attn_proj, w_glu, w_ssm_proj, w_out), proj, sinks, q_norm_w, k_norm_w, batch, seq)

    out = _tail(attn_g, y_chunks, proj, x2, w_attn_bf, w_glu_bf, b_glu, w_ssm_bf, w_out_bf)
    return out.reshape(batch, seq, D_MODEL)
```

```python
import functools
import math

import jax
import jax.numpy as jnp
from jax import lax
from jax.experimental import pallas as pl
from jax.experimental.pallas import tpu as pltpu

F32 = jnp.float32
BF16 = jnp.bfloat16

D_MODEL = 2048
HEAD_DIM = 64
N_Q_HEADS = 16
N_KV_HEADS = 4
Q_PER_KV = 4
ATTN_W = N_Q_HEADS * HEAD_DIM
KV_W = N_KV_HEADS * HEAD_DIM
WINDOW = 128
SSM_W = D_MODEL // 2
GROUP = 16
N_GROUPS = SSM_W // GROUP
STATE = 64
NORM_EPS = 1e-6

COL_Q, COL_GATE, COL_U, COL_Z = 0, 1024, 2048, 3072
COL_GA, COL_GS, COL_K, COL_V = 4096, 6144, 8192, 8448
IN_W = 8704

LANES = 128
MXU = 256
CHUNK = 16
SLAB_G = LANES // GROUP
N_SLABS = N_GROUPS // SLAB_G
PAIRS = SLAB_G // 2
CHUNK_W = CHUNK * LANES
N_TILES = CHUNK_W // MXU
VMEM_LIMIT = 48 * 1024 * 1024
BIG_VMEM_LIMIT = 56 * 1024 * 1024


def _dot(a, b):
    return jnp.dot(a, b, preferred_element_type=F32)


IN_TN = 512
N_IN_TILES = IN_W // IN_TN
SRC_KV_TILE = (ATTN_W) // IN_TN
SRC_U_TILE0 = (ATTN_W + 2 * KV_W + ATTN_W) // IN_TN


US_PITCH = CHUNK + 8
IN_PER_ROUND = 2
IN_RING = 2 * IN_PER_ROUND


def _dst_tile(src_tile):
    if src_tile < SRC_KV_TILE:
        return src_tile
    if src_tile == SRC_KV_TILE:
        return N_IN_TILES - 1
    return src_tile - 1


def _in_proj_kernel(x_ref, nw_ref, w_hbm, o_hbm, uc_ref, h_ref, w_buf, o_buf, us_ref, w_sem, o_sem,
                    *, tm):
    i = pl.program_id(0)
    rows = tm // CHUNK
    n_sl = IN_TN // LANES

    def w_copy(t):
        return pltpu.make_async_copy(w_hbm.at[:, pl.ds(t * IN_TN, IN_TN)], w_buf.at[t % IN_RING],
                                     w_sem.at[t % IN_RING])

    def o_copy(t):
        dst = o_hbm.at[pl.ds(pl.multiple_of(i * tm, tm), tm), pl.ds(_dst_tile(t) * IN_TN, IN_TN)]
        return pltpu.make_async_copy(o_buf.at[t % IN_RING], dst, o_sem.at[t % IN_RING])

    def round_tiles(r):
        return [t for t in range(r * IN_PER_ROUND, (r + 1) * IN_PER_ROUND) if 0 <= t < N_IN_TILES]

    for t in round_tiles(0):
        w_copy(t).start()
    x = x_ref[...]
    ms = jnp.mean(x * x, axis=-1, keepdims=True)
    h_ref[...] = (x * lax.rsqrt(ms + NORM_EPS) * nw_ref[...]).astype(BF16)

    def finish(t, acc):
        o_buf[t % IN_RING] = acc.astype(BF16)
        if SRC_U_TILE0 <= t < SRC_U_TILE0 + SSM_W // IN_TN:
            slab0 = (t - SRC_U_TILE0) * n_sl
            for s in range(n_sl):
                for c in range(rows):
                    r0 = (s * rows + c) * US_PITCH
                    us_ref[r0:r0 + CHUNK, :] = acc[c * CHUNK:(c + 1) * CHUNK, s * LANES:(s + 1) * LANES]
            for s in range(n_sl):
                for tok in range(CHUNK):
                    piece = us_ref[pl.ds(s * rows * US_PITCH + tok, rows, stride=US_PITCH), :]
                    uc_ref[slab0 + s, :, tok * LANES:(tok + 1) * LANES] = piece.astype(BF16)

    n_rounds = pl.cdiv(N_IN_TILES, IN_PER_ROUND)
    for r in range(n_rounds):
        for t in round_tiles(r + 1):
            w_copy(t).start()
        for t in round_tiles(r):
            w_copy(t).wait()
        for t in round_tiles(r - 2):
            o_copy(t).wait()
        for t in round_tiles(r - 1):
            o_copy(t).start()
        for t in round_tiles(r):
            acc = None
            for k in range(D_MODEL // MXU):
                ks = slice(k * MXU, (k + 1) * MXU)
                part = _dot(h_ref[:, ks], w_buf[t % IN_RING, ks, :].astype(BF16))
                acc = part if acc is None else acc + part
            finish(t, acc)

    for t in round_tiles(n_rounds - 1):
        o_copy(t).start()
    for t in round_tiles(n_rounds - 2) + round_tiles(n_rounds - 1):
        o_copy(t).wait()


def _in_proj(x2, norm_w, w_in, tm=1024):
    m = x2.shape[0]
    kernel = functools.partial(_in_proj_kernel, tm=tm)
    return pl.pallas_call(
        kernel,
        grid=(m // tm,),
        in_specs=[
            pl.BlockSpec((tm, D_MODEL), lambda i: (i, 0)),
            pl.BlockSpec((1, D_MODEL), lambda i: (0, 0)),
            pl.BlockSpec(memory_space=pl.ANY),
        ],
        out_specs=[
            pl.BlockSpec(memory_space=pl.ANY),
            pl.BlockSpec((N_SLABS, tm // CHUNK, CHUNK_W), lambda i: (0, i, 0)),
        ],
        out_shape=[
            jax.ShapeDtypeStruct((m, IN_W), BF16),
            jax.ShapeDtypeStruct((N_SLABS, m // CHUNK, CHUNK_W), BF16),
        ],
        scratch_shapes=[
            pltpu.VMEM((tm, D_MODEL), BF16),
            pltpu.VMEM((IN_RING, D_MODEL, IN_TN), F32),
            pltpu.VMEM((IN_RING, tm, IN_TN), BF16),
            pltpu.VMEM((IN_TN // LANES * (tm // CHUNK) * US_PITCH, LANES), F32),
            pltpu.SemaphoreType.DMA((IN_RING,)),
            pltpu.SemaphoreType.DMA((IN_RING,)),
        ],
        compiler_params=pltpu.CompilerParams(
            dimension_semantics=("arbitrary",), vmem_limit_bytes=BIG_VMEM_LIMIT),
        name="in_proj",
    )(x2, norm_w.reshape(1, D_MODEL), w_in)


def _head_norm(t, w):
    ms = jnp.mean(t * t, axis=-1, keepdims=True)
    return t * lax.rsqrt(ms + NORM_EPS) * w


_NT = (((1,), (1,)), ((), ()))
_TN = (((0,), (0,)), ((), ()))


SWA_TQ = 1024
SWA_SUB = SWA_TQ // WINDOW


def _swa_tile(sink_ref, q_ref, g_ref, kc_ref, vc_ref, kp_ref, vp_ref, qw_ref, kw_ref, o_ref,
              first_tile):
    log2e = math.log2(math.e)
    kqw = kw_ref[...] * qw_ref[...] * (log2e / math.sqrt(HEAD_DIM))
    n_col = Q_PER_KV * WINDOW
    key = lax.broadcasted_iota(jnp.int32, (WINDOW, n_col), 0)
    qry = lax.broadcasted_iota(jnp.int32, (WINDOW, n_col), 1) % WINDOW
    from_prev = key > qry
    no_prev = jnp.where(first_tile, -1e30, 0.0)
    head_of_col = lax.broadcasted_iota(jnp.int32, (1, n_col), 1) // WINDOW
    gw = Q_PER_KV * HEAD_DIM
    sel_r = lax.broadcasted_iota(jnp.int32, (8, 2 * gw), 0)
    sel_l = (lax.broadcasted_iota(jnp.int32, (8, 2 * gw), 1) % gw) // HEAD_DIM
    head_sel = jnp.where(sel_r == sel_l, 1.0, 0.0).astype(BF16)

    parts = []
    carry = {}

    def block(g, n):
        kcol = slice(g * HEAD_DIM, (g + 1) * HEAD_DIM)
        gcols = slice(g * Q_PER_KV * HEAD_DIM, (g + 1) * Q_PER_KV * HEAD_DIM)
        if n == 0:
            sink = jnp.zeros((1, n_col), F32)
            for r in range(Q_PER_KV):
                sink = jnp.where(head_of_col == r, sink_ref[g * Q_PER_KV + r] * log2e, sink)
            carry['sink'] = sink
            carry['k'] = _head_norm(kp_ref[:, kcol].astype(F32), kqw).astype(BF16)
            carry['v'] = vp_ref[:, kcol]
        sink, k_prev, v_prev = carry['sink'], carry['k'], carry['v']
        rows = slice(n * WINDOW, (n + 1) * WINDOW)
        k_cur = _head_norm(kc_ref[rows, kcol].astype(F32), kqw).astype(BF16)
        v_cur = vc_ref[rows, kcol]
        k_ctx = jnp.concatenate([k_prev, k_cur], axis=0)
        v_ctx = jnp.concatenate([v_prev, v_cur], axis=0)

        qg = q_ref[rows, gcols]
        qf = qg.astype(F32)
        q_t = qf.T.astype(BF16)
        q_t = jnp.concatenate([q_t[r * HEAD_DIM:(r + 1) * HEAD_DIM] for r in range(Q_PER_KV)],
                              axis=1)
        q2 = qf * qf
        q2_hi = q2.astype(BF16)
        q2_lo = (q2 - q2_hi.astype(F32)).astype(BF16)
        ssq = lax.dot_general(head_sel, jnp.concatenate([q2_hi, q2_lo], axis=1), _NT,
                              preferred_element_type=F32)
        rms = lax.rsqrt(ssq * (1.0 / HEAD_DIM) + NORM_EPS)
        rms_q = jnp.concatenate([rms[r:r + 1] for r in range(Q_PER_KV)], axis=1)

        s = _dot(k_ctx, q_t)
        s_prev = s[:WINDOW] + no_prev if n == 0 else s[:WINDOW]
        s = jnp.where(from_prev, s_prev, s[WINDOW:]) * rms_q
        mx = jnp.maximum(jnp.max(s, axis=0, keepdims=True), sink)
        p = jnp.exp2(s - mx)
        den = jnp.sum(p, axis=0, keepdims=True) + jnp.exp2(sink - mx)
        p_ctx = jnp.concatenate([jnp.where(from_prev, p, 0.0), jnp.where(from_prev, 0.0, p)],
                                axis=0).astype(BF16)
        o_t = lax.dot_general(v_ctx, p_ctx, _TN, preferred_element_type=F32) * (1.0 / den)
        halves = [jnp.concatenate([o_t[:, (2 * h) * WINDOW:(2 * h + 1) * WINDOW],
                                   o_t[:, (2 * h + 1) * WINDOW:(2 * h + 2) * WINDOW]], axis=0).T
                  for h in range(Q_PER_KV // 2)]
        og = jnp.concatenate(halves, axis=1)
        gate = g_ref[rows, gcols].astype(F32)
        o_ref[rows, gcols] = (og * (gate * jax.nn.sigmoid(gate))).astype(BF16)
        carry['k'], carry['v'] = k_cur, v_cur

    for g in range(N_KV_HEADS):
        for n in range(SWA_SUB):
            parts.append(functools.partial(block, g, n))
    return parts


def _swa_kernel(*refs, tiles_per_seq):
    for block in _swa_tile(*refs, pl.program_id(0) % tiles_per_seq == 0):
        block()


def _swa(proj, sinks, q_norm_w, k_norm_w, batch, seq):
    m = batch * seq
    cur = lambda col: (lambda s: (s, col))
    prev = lambda col: (lambda s: (jnp.maximum(s * SWA_SUB - 1, 0), col))
    kernel = functools.partial(_swa_kernel, tiles_per_seq=seq // SWA_TQ)
    return pl.pallas_call(
        kernel,
        grid=(m // SWA_TQ,),
        in_specs=[
            pl.BlockSpec(memory_space=pltpu.SMEM),
            pl.BlockSpec((SWA_TQ, ATTN_W), cur(COL_Q // ATTN_W)),
            pl.BlockSpec((SWA_TQ, ATTN_W), cur(COL_GATE // ATTN_W)),
            pl.BlockSpec((SWA_TQ, KV_W), cur(COL_K // KV_W)),
            pl.BlockSpec((SWA_TQ, KV_W), cur(COL_V // KV_W)),
            pl.BlockSpec((WINDOW, KV_W), prev(COL_K // KV_W)),
            pl.BlockSpec((WINDOW, KV_W), prev(COL_V // KV_W)),
            pl.BlockSpec((1, HEAD_DIM), lambda s: (0, 0)),
            pl.BlockSpec((1, HEAD_DIM), lambda s: (0, 0)),
        ],
        out_specs=pl.BlockSpec((SWA_TQ, ATTN_W), cur(0)),
        out_shape=jax.ShapeDtypeStruct((m, ATTN_W), BF16),
        compiler_params=pltpu.CompilerParams(
            dimension_semantics=("arbitrary",), vmem_limit_bytes=VMEM_LIMIT),
        name="swa",
    )(sinks, proj, proj, proj, proj, proj, proj,
      q_norm_w.reshape(1, HEAD_DIM), k_norm_w.reshape(1, HEAD_DIM))


def _slab_powers(ar, ai, dt, btr, bti):
    dta_re, dta_im = dt * ar, dt * ai
    mag = jnp.exp(dta_re)
    ab_re, ab_im = mag * jnp.cos(dta_im), mag * jnp.sin(dta_im)
    pw = [(jnp.ones_like(ar), jnp.zeros_like(ar))]
    for _ in range(CHUNK):
        pr, pi = pw[-1]
        pw.append((pr * ab_re - pi * ab_im, pr * ab_im + pi * ab_re))
    den = ar * ar + ai * ai
    num_re, num_im = ab_re - 1.0, ab_im
    cf_re = (num_re * ar + num_im * ai) / den
    cf_im = (num_im * ar - num_re * ai) / den
    bb_re = cf_re * btr - cf_im * bti
    bb_im = cf_re * bti + cf_im * btr
    abb = [(pr * bb_re - pi * bb_im, pr * bb_im + pi * bb_re) for pr, pi in pw[:CHUNK]]
    return pw, abb


def _build_state_in(abb, p_s):
    row_g = lax.broadcasted_iota(jnp.int32, (LANES, STATE), 0) // GROUP
    even = row_g % 2 == 0
    pair_of_row = lax.broadcasted_iota(jnp.int32, (LANES, 2 * LANES), 0) // (2 * GROUP)
    for lag in range(CHUNK):
        t = CHUNK - 1 - lag
        re, im = abb[lag]
        blk = jnp.concatenate([jnp.where(even, re, 0.0), jnp.where(even, 0.0, re),
                               jnp.where(even, im, 0.0), jnp.where(even, 0.0, im)], axis=1)
        for k in range(PAIRS):
            p_s[k, t * LANES:(t + 1) * LANES, :] = jnp.where(pair_of_row == k, blk, 0.0).astype(BF16)


def _build_state_out(pw, cr, ci, q_s):
    lane_g = lax.broadcasted_iota(jnp.int32, (STATE, LANES), 1) // GROUP
    for t in range(CHUNK):
        pr, pi = pw[t + 1]
        cat = jnp.concatenate([cr * pr - ci * pi, -(cr * pi + ci * pr)], axis=1)
        cat_t = cat.T
        for k in range(PAIRS):
            for ri in range(2):
                for half in range(2):
                    piece = jnp.where(lane_g == 2 * k + half, cat_t[ri * STATE:(ri + 1) * STATE], 0.0)
                    r0 = ri * LANES + half * STATE
                    q_s[k, r0:r0 + STATE, t * LANES:(t + 1) * LANES] = piece.astype(BF16)


def _build_toeplitz(abb, cr, ci, d_row, w_s):
    def split(v):
        hi = v.astype(BF16)
        return hi, (v - hi.astype(F32)).astype(BF16)

    rhs_hi, rhs_lo = split(jnp.concatenate([cr, -ci], axis=1))
    rhs = jnp.concatenate([rhs_hi, rhs_lo, rhs_hi], axis=1)
    row = lax.broadcasted_iota(jnp.int32, (LANES, LANES), 0)
    col = lax.broadcasted_iota(jnp.int32, (LANES, LANES), 1)
    same_group = (row // GROUP) == (col // GROUP)
    taps = []
    for lag in range(CHUNK):
        lhs_hi, lhs_lo = split(jnp.concatenate(abb[lag], axis=1))
        k = lax.dot_general(jnp.concatenate([lhs_hi, lhs_hi, lhs_lo], axis=1), rhs,
                            (((1,), (1,)), ((), ())), preferred_element_type=F32)
        k = jnp.where(same_group, k, 0.0)
        if lag == 0:
            k = k + jnp.where(row == col, d_row, 0.0)
        taps.append(k.astype(BF16))
    zero = jnp.zeros((LANES, LANES), BF16)
    for d in range(N_TILES):
        w_s[d, :LANES, :LANES] = taps[2 * d]
        w_s[d, :LANES, LANES:] = taps[2 * d + 1]
        w_s[d, LANES:, :LANES] = taps[2 * d - 1] if d > 0 else zero
        w_s[d, LANES:, LANES:] = taps[2 * d]


def _ssm_kernel(u_ref, ar_ref, ai_ref, dt_ref, btr_ref, bti_ref, cr_ref, ci_ref, d_ref,
                dtare_ref, dtaim_ref, w0_ref, w1_ref, w2_ref, w3_ref, y_ref, c0_ref, c1_ref, c2_ref,
                c3_ref, xre_s, xim_s, p_s, q_s, w_s, *, batch, n_chunks):
    step = pl.program_id(0)
    for src, dst in ((w0_ref, c0_ref), (w1_ref, c1_ref), (w2_ref, c2_ref), (w3_ref, c3_ref)):
        dst[...] = src[...].astype(BF16)
    rows8 = batch * PAIRS
    slab_rows = n_chunks * rows8

    def state_rows(slab, b, k):
        return pl.ds(slab * slab_rows + b * PAIRS + k, n_chunks, stride=rows8)

    def powers():
        return _slab_powers(ar_ref[...], ai_ref[...], dt_ref[...], btr_ref[...], bti_ref[...])

    @pl.when(step < N_SLABS)
    def _():
        _, abb = powers()
        _build_state_in(abb, p_s)
        u = u_ref[...]
        for k in range(PAIRS):
            x = _dot(u, p_s[k])
            for b in range(batch):
                xb = x[b * n_chunks:(b + 1) * n_chunks]
                xre_s[state_rows(step, b, k), :] = xb[:, :LANES]
                xim_s[state_rows(step, b, k), :] = xb[:, LANES:]

    @pl.when(step == N_SLABS)
    def _():
        dta_re = jnp.concatenate([dtare_ref[...]] * batch, axis=1).reshape(N_SLABS * rows8, LANES)
        dta_im = jnp.concatenate([dtaim_ref[...]] * batch, axis=1).reshape(N_SLABS * rows8, LANES)
        mag = jnp.exp(CHUNK * dta_re)
        a_re, a_im = mag * jnp.cos(CHUNK * dta_im), mag * jnp.sin(CHUNK * dta_im)

        def body(c, carry):
            s_re, s_im = carry
            x_re, x_im = [], []
            for sl in range(N_SLABS):
                off = pl.multiple_of(sl * slab_rows + c * rows8, rows8)
                x_re.append(xre_s[pl.ds(off, rows8), :])
                x_im.append(xim_s[pl.ds(off, rows8), :])
                xre_s[pl.ds(off, rows8), :] = s_re[sl * rows8:(sl + 1) * rows8]
                xim_s[pl.ds(off, rows8), :] = s_im[sl * rows8:(sl + 1) * rows8]
            x_re = jnp.concatenate(x_re, axis=0)
            x_im = jnp.concatenate(x_im, axis=0)
            return (a_re * s_re - a_im * s_im + x_re, a_re * s_im + a_im * s_re + x_im)

        zero = jnp.zeros((N_SLABS * rows8, LANES), F32)
        lax.fori_loop(0, n_chunks, body, (zero, zero))

    @pl.when(step > N_SLABS)
    def _():
        slab = step - (N_SLABS + 1)
        pw, abb = powers()
        cr, ci = cr_ref[...], ci_ref[...]
        _build_state_out(pw, cr, ci, q_s)
        _build_toeplitz(abb, cr, ci, d_ref[...], w_s)
        u = u_ref[...]
        y_state = None
        for k in range(PAIRS):
            s_in = jnp.concatenate(
                [jnp.concatenate([xre_s[state_rows(slab, b, k), :], xim_s[state_rows(slab, b, k), :]],
                                 axis=1) for b in range(batch)], axis=0).astype(BF16)
            part = _dot(s_in, q_s[k])
            y_state = part if y_state is None else y_state + part
        for t2 in range(N_TILES):
            acc = y_state[:, t2 * MXU:(t2 + 1) * MXU]
            for t1 in range(t2 + 1):
                acc = acc + _dot(u[:, t1 * MXU:(t1 + 1) * MXU], w_s[t2 - t1])
            y_ref[:, t2 * MXU:(t2 + 1) * MXU] = jax.nn.gelu(acc).astype(BF16)


N_CAST = 16


def _ssm(u_chunks, A_re, A_im, log_dt, B_re, B_im, C_re, C_im, D_skip, batch, tail_weights):
    _, rows, _ = u_chunks.shape
    n_chunks = rows // batch
    n_steps = 2 * N_SLABS + 1
    kernel = functools.partial(_ssm_kernel, batch=batch, n_chunks=n_chunks)

    dt = jnp.exp(log_dt)[:, None]
    per_row = lambda v: jnp.repeat(v, GROUP, axis=0)
    ar, ai = per_row(A_re), per_row(A_im)
    dtr = per_row(jnp.broadcast_to(dt, A_re.shape))
    btr = B_re.transpose(0, 2, 1).reshape(SSM_W, STATE)
    bti = B_im.transpose(0, 2, 1).reshape(SSM_W, STATE)
    cr, ci = C_re.reshape(SSM_W, STATE), C_im.reshape(SSM_W, STATE)
    d_rows = D_skip.reshape(N_SLABS, 1, LANES)
    dta_re = (dt * A_re).reshape(N_SLABS, PAIRS, LANES)
    dta_im = (dt * A_im).reshape(N_SLABS, PAIRS, LANES)

    def slab_in(s):
        return jnp.where(s <= N_SLABS, jnp.minimum(s, N_SLABS - 1), s - (N_SLABS + 1))

    def slab_out(s):
        return jnp.maximum(s - (N_SLABS + 1), 0)

    def cast_spec(w):
        return pl.BlockSpec((w.shape[0] // N_CAST, w.shape[1]),
                            lambda s: (jnp.minimum(s, N_CAST - 1), 0))

    coeff = pl.BlockSpec((LANES, STATE), lambda s: (slab_in(s), 0))
    whole = pl.BlockSpec((N_SLABS, PAIRS, LANES), lambda s: (0, 0, 0))
    return pl.pallas_call(
        kernel,
        grid=(n_steps,),
        in_specs=[
            pl.BlockSpec((None, rows, CHUNK_W), lambda s: (slab_in(s), 0, 0)),
            coeff, coeff, coeff, coeff, coeff, coeff, coeff,
            pl.BlockSpec((None, 1, LANES), lambda s: (slab_in(s), 0, 0)),
            whole, whole,
        ] + [cast_spec(w) for w in tail_weights],
        out_specs=[pl.BlockSpec((None, rows, CHUNK_W), lambda s: (slab_out(s), 0, 0))]
        + [cast_spec(w) for w in tail_weights],
        out_shape=[jax.ShapeDtypeStruct(u_chunks.shape, BF16)]
        + [jax.ShapeDtypeStruct(w.shape, BF16) for w in tail_weights],
        scratch_shapes=[
            pltpu.VMEM((N_SLABS * n_chunks * batch * PAIRS, LANES), F32),
            pltpu.VMEM((N_SLABS * n_chunks * batch * PAIRS, LANES), F32),
            pltpu.VMEM((PAIRS, CHUNK_W, 2 * LANES), BF16),
            pltpu.VMEM((PAIRS, 2 * LANES, CHUNK_W), BF16),
            pltpu.VMEM((N_TILES, MXU, MXU), BF16),
        ],
        compiler_params=pltpu.CompilerParams(
            dimension_semantics=("arbitrary",), vmem_limit_bytes=VMEM_LIMIT),
        name="ssm",
    )(u_chunks, ar, ai, dtr, btr, bti, cr, ci, d_rows, dta_re, dta_im, *tail_weights)


def _tail_kernel(ag_ref, yc_ref, z_ref, ga_ref, gs_ref, x_ref, wa_ref, wg_ref, bg_ref, ws_ref,
                 wo_ref, o_ref, ys_ref, *, tm):
    rows = tm // CHUNK
    for s in range(N_SLABS):
        for t in range(CHUNK):
            ys_ref[pl.ds(s * rows * US_PITCH + t, rows, stride=US_PITCH), :] = (
                yc_ref[s, :, t * LANES:(t + 1) * LANES].astype(F32))
    y_g = jnp.concatenate(
        [jnp.concatenate([ys_ref[(s * rows + c) * US_PITCH:(s * rows + c) * US_PITCH + CHUNK, :]
                          for c in range(rows)], axis=0) for s in range(N_SLABS)],
        axis=1).astype(BF16)

    y_a = _dot(ag_ref[...], wa_ref[...])
    glu = _dot(y_g, wg_ref[...]) + bg_ref[...]
    z = z_ref[...].astype(F32)
    t = glu[:, :SSM_W] * jax.nn.sigmoid(glu[:, SSM_W:]) * (z * jax.nn.sigmoid(z))
    y_s = _dot(t.astype(BF16), ws_ref[...])
    merged = (jax.nn.sigmoid(ga_ref[...].astype(F32)) * y_a
              + jax.nn.sigmoid(gs_ref[...].astype(F32)) * y_s)
    o_ref[...] = x_ref[...] + _dot(merged.astype(BF16), wo_ref[...])


def _tail(attn_g, y_chunks, proj, x2, w_attn, w_glu, b_glu, w_ssm, w_out, tm=256):
    m = x2.shape[0]
    row = lambda blk: (lambda i: (i, blk))
    const = lambda i: (0, 0)
    resident = functools.partial(pl.BlockSpec, index_map=const, pipeline_mode=pl.Buffered(1))
    kernel = functools.partial(_tail_kernel, tm=tm)
    return pl.pallas_call(
        kernel,
        grid=(m // tm,),
        in_specs=[
            pl.BlockSpec((tm, ATTN_W), row(0)),
            pl.BlockSpec((N_SLABS, tm // CHUNK, CHUNK_W), lambda i: (0, i, 0)),
            pl.BlockSpec((tm, SSM_W), row(COL_Z // SSM_W)),
            pl.BlockSpec((tm, D_MODEL), row(COL_GA // D_MODEL)),
            pl.BlockSpec((tm, D_MODEL), row(COL_GS // D_MODEL)),
            pl.BlockSpec((tm, D_MODEL), row(0)),
            resident((ATTN_W, D_MODEL)),
            resident((SSM_W, 2 * SSM_W)),
            resident((1, 2 * SSM_W)),
            resident((SSM_W, D_MODEL)),
            resident((D_MODEL, D_MODEL)),
        ],
        out_specs=pl.BlockSpec((tm, D_MODEL), row(0)),
        out_shape=jax.ShapeDtypeStruct((m, D_MODEL), F32),
        scratch_shapes=[pltpu.VMEM((N_SLABS * (tm // CHUNK) * US_PITCH, LANES), F32)],
        compiler_params=pltpu.CompilerParams(
            dimension_semantics=("arbitrary",), vmem_limit_bytes=VMEM_LIMIT),
        name="tail",
    )(attn_g, y_chunks, proj, proj, proj, x2, w_attn, w_glu, b_glu.reshape(1, -1), w_ssm, w_out)


def kernel(x, norm_w, w_in, q_norm_w, k_norm_w, sinks, w_attn_proj, A_re, A_im, log_dt, B_re, B_im,
           C_re, C_im, D_skip, w_glu, b_glu, w_ssm_proj, w_out):
    batch, seq, _ = x.shape
    m = batch * seq
    x2 = x.reshape(m, D_MODEL)

    proj, u_chunks = _in_proj(x2, norm_w, w_in)

    attn_g = _swa(proj, sinks, q_norm_w, k_norm_w, batch, seq)

    y_chunks, w_attn_bf, w_glu_bf, w_ssm_bf, w_out_bf = _ssm(
        u_chunks, A_re, A_im, log_dt, B_re, B_im, C_re, C_im, D_skip, batch,
        (w_attn_proj, w_glu, w_ssm_proj, w_out))

    out = _tail(attn_g, y_chunks, proj, x2, w_attn_bf, w_glu_bf, b_glu, w_ssm_bf, w_out_bf)
    return out.reshape(batch, seq, D_MODEL)
```

```python
import functools
import math

import jax
import jax.numpy as jnp
from jax import lax
from jax.experimental import pallas as pl
from jax.experimental.pallas import tpu as pltpu

F32 = jnp.float32
BF16 = jnp.bfloat16

D_MODEL = 2048
HEAD_DIM = 64
N_Q_HEADS = 16
N_KV_HEADS = 4
Q_PER_KV = 4
ATTN_W = N_Q_HEADS * HEAD_DIM
KV_W = N_KV_HEADS * HEAD_DIM
WINDOW = 128
SSM_W = D_MODEL // 2
GROUP = 16
N_GROUPS = SSM_W // GROUP
STATE = 64
NORM_EPS = 1e-6

COL_Q, COL_GATE, COL_U, COL_Z = 0, 1024, 2048, 3072
COL_GA, COL_GS, COL_K, COL_V = 4096, 6144, 8192, 8448
IN_W = 8704

LANES = 128
MXU = 256
CHUNK = 16
SLAB_G = LANES // GROUP
N_SLABS = N_GROUPS // SLAB_G
PAIRS = SLAB_G // 2
CHUNK_W = CHUNK * LANES
N_TILES = CHUNK_W // MXU
VMEM_LIMIT = 48 * 1024 * 1024
BIG_VMEM_LIMIT = 56 * 1024 * 1024


def _dot(a, b):
    return jnp.dot(a, b, preferred_element_type=F32)


IN_TN = 512
N_IN_TILES = IN_W // IN_TN
SRC_KV_TILE = (ATTN_W) // IN_TN
SRC_U_TILE0 = (ATTN_W + 2 * KV_W + ATTN_W) // IN_TN


US_PITCH = CHUNK + 8
X_ROWS = 128
IN_PER_ROUND = 2
IN_RING = 2 * IN_PER_ROUND


def _dst_tile(src_tile):
    if src_tile < SRC_KV_TILE:
        return src_tile
    if src_tile == SRC_KV_TILE:
        return N_IN_TILES - 1
    return src_tile - 1


def _in_proj_kernel(nw_ref, x_hbm, w_hbm, o_hbm, uc_ref, h_ref, x_buf, w_buf, o_buf, us_ref, x_sem,
                    w_sem, o_sem, *, tm, n_steps):
    i = pl.program_id(0)
    rows = tm // CHUNK
    n_sl = IN_TN // LANES
    n_xc = tm // X_ROWS
    cur, nxt = i % 2, (i + 1) % 2
    nxt_tile = jnp.minimum(i + 1, n_steps - 1)

    def x_copy(tile, c):
        src = x_hbm.at[pl.ds(pl.multiple_of(tile * tm + c * X_ROWS, X_ROWS), X_ROWS), :]
        return pltpu.make_async_copy(src, x_buf.at[c % 2], x_sem.at[c % 2])

    def norm_chunk(c, dst):
        x = x_buf[c % 2]
        ms = jnp.mean(x * x, axis=-1, keepdims=True)
        h_ref[dst, c * X_ROWS:(c + 1) * X_ROWS, :] = (
            x * lax.rsqrt(ms + NORM_EPS) * nw_ref[...]).astype(BF16)

    def w_copy(t):
        return pltpu.make_async_copy(w_hbm.at[:, pl.ds(t * IN_TN, IN_TN)], w_buf.at[t % IN_RING],
                                     w_sem.at[t % IN_RING])

    def o_copy(t):
        dst = o_hbm.at[pl.ds(pl.multiple_of(i * tm, tm), tm), pl.ds(_dst_tile(t) * IN_TN, IN_TN)]
        return pltpu.make_async_copy(o_buf.at[t % IN_RING], dst, o_sem.at[t % IN_RING])

    def round_tiles(r):
        return [t for t in range(r * IN_PER_ROUND, (r + 1) * IN_PER_ROUND) if 0 <= t < N_IN_TILES]

    for t in round_tiles(0):
        w_copy(t).start()

    @pl.when(i == 0)
    def _():
        x_copy(0, 0).start()
        for c in range(n_xc):
            if c + 1 < n_xc:
                x_copy(0, c + 1).start()
            x_copy(0, c).wait()
            norm_chunk(c, 0)

    def finish(t, acc):
        o_buf[t % IN_RING] = acc.astype(BF16)
        if SRC_U_TILE0 <= t < SRC_U_TILE0 + SSM_W // IN_TN:
            slab0 = (t - SRC_U_TILE0) * n_sl
            for s in range(n_sl):
                for c in range(rows):
                    r0 = (s * rows + c) * US_PITCH
                    us_ref[r0:r0 + CHUNK, :] = acc[c * CHUNK:(c + 1) * CHUNK, s * LANES:(s + 1) * LANES]
            for s in range(n_sl):
                for tok in range(CHUNK):
                    piece = us_ref[pl.ds(s * rows * US_PITCH + tok, rows, stride=US_PITCH), :]
                    uc_ref[slab0 + s, :, tok * LANES:(tok + 1) * LANES] = piece.astype(BF16)

    n_rounds = pl.cdiv(N_IN_TILES, IN_PER_ROUND)
    assert n_xc < n_rounds
    for r in range(n_rounds):
        for t in round_tiles(r + 1):
            w_copy(t).start()
        for t in round_tiles(r):
            w_copy(t).wait()
        for t in round_tiles(r - 2):
            o_copy(t).wait()
        for t in round_tiles(r - 1):
            o_copy(t).start()
        if r < n_xc:
            x_copy(nxt_tile, r).start()
        if 1 <= r <= n_xc:
            x_copy(nxt_tile, r - 1).wait()
        for t in round_tiles(r):
            acc = None
            for k in range(D_MODEL // MXU):
                ks = slice(k * MXU, (k + 1) * MXU)
                part = _dot(h_ref[cur, :, ks], w_buf[t % IN_RING, ks, :].astype(BF16))
                acc = part if acc is None else acc + part
            finish(t, acc)
        if 1 <= r <= n_xc:
            norm_chunk(r - 1, nxt)

    for t in round_tiles(n_rounds - 1):
        o_copy(t).start()
    for t in round_tiles(n_rounds - 2) + round_tiles(n_rounds - 1):
        o_copy(t).wait()


def _in_proj(x2, norm_w, w_in, tm=1024):
    m = x2.shape[0]
    n_steps = m // tm
    kernel = functools.partial(_in_proj_kernel, tm=tm, n_steps=n_steps)
    return pl.pallas_call(
        kernel,
        grid=(n_steps,),
        in_specs=[
            pl.BlockSpec((1, D_MODEL), lambda i: (0, 0)),
            pl.BlockSpec(memory_space=pl.ANY),
            pl.BlockSpec(memory_space=pl.ANY),
        ],
        out_specs=[
            pl.BlockSpec(memory_space=pl.ANY),
            pl.BlockSpec((N_SLABS, tm // CHUNK, CHUNK_W), lambda i: (0, i, 0)),
        ],
        out_shape=[
            jax.ShapeDtypeStruct((m, IN_W), BF16),
            jax.ShapeDtypeStruct((N_SLABS, m // CHUNK, CHUNK_W), BF16),
        ],
        scratch_shapes=[
            pltpu.VMEM((2, tm, D_MODEL), BF16),
            pltpu.VMEM((2, X_ROWS, D_MODEL), F32),
            pltpu.VMEM((IN_RING, D_MODEL, IN_TN), F32),
            pltpu.VMEM((IN_RING, tm, IN_TN), BF16),
            pltpu.VMEM((IN_TN // LANES * (tm // CHUNK) * US_PITCH, LANES), F32),
            pltpu.SemaphoreType.DMA((2,)),
            pltpu.SemaphoreType.DMA((IN_RING,)),
            pltpu.SemaphoreType.DMA((IN_RING,)),
        ],
        compiler_params=pltpu.CompilerParams(
            dimension_semantics=("arbitrary",), vmem_limit_bytes=BIG_VMEM_LIMIT),
        name="in_proj",
    )(norm_w.reshape(1, D_MODEL), x2, w_in)


def _head_norm(t, w):
    ms = jnp.mean(t * t, axis=-1, keepdims=True)
    return t * lax.rsqrt(ms + NORM_EPS) * w


_NT = (((1,), (1,)), ((), ()))
_TN = (((0,), (0,)), ((), ()))


SWA_TQ = 1024
SWA_SUB = SWA_TQ // WINDOW


def _swa_tile(sink_ref, q_ref, g_ref, kc_ref, vc_ref, kp_ref, vp_ref, qw_ref, kw_ref, o_ref,
              first_tile):
    log2e = math.log2(math.e)
    kqw = kw_ref[...] * qw_ref[...] * (log2e / math.sqrt(HEAD_DIM))
    n_col = Q_PER_KV * WINDOW
    key = lax.broadcasted_iota(jnp.int32, (WINDOW, n_col), 0)
    qry = lax.broadcasted_iota(jnp.int32, (WINDOW, n_col), 1) % WINDOW
    from_prev = key > qry
    no_prev = jnp.where(first_tile, -1e30, 0.0)
    head_of_col = lax.broadcasted_iota(jnp.int32, (1, n_col), 1) // WINDOW
    gw = Q_PER_KV * HEAD_DIM
    sel_r = lax.broadcasted_iota(jnp.int32, (8, 2 * gw), 0)
    sel_l = (lax.broadcasted_iota(jnp.int32, (8, 2 * gw), 1) % gw) // HEAD_DIM
    head_sel = jnp.where(sel_r == sel_l, 1.0, 0.0).astype(BF16)

    parts = []
    carry = {}

    def block(g, n):
        kcol = slice(g * HEAD_DIM, (g + 1) * HEAD_DIM)
        gcols = slice(g * Q_PER_KV * HEAD_DIM, (g + 1) * Q_PER_KV * HEAD_DIM)
        if n == 0:
            sink = jnp.zeros((1, n_col), F32)
            for r in range(Q_PER_KV):
                sink = jnp.where(head_of_col == r, sink_ref[g * Q_PER_KV + r] * log2e, sink)
            carry['sink'] = sink
            carry['k'] = _head_norm(kp_ref[:, kcol].astype(F32), kqw).astype(BF16)
            carry['v'] = vp_ref[:, kcol]
        sink, k_prev, v_prev = carry['sink'], carry['k'], carry['v']
        rows = slice(n * WINDOW, (n + 1) * WINDOW)
        k_cur = _head_norm(kc_ref[rows, kcol].astype(F32), kqw).astype(BF16)
        v_cur = vc_ref[rows, kcol]
        k_ctx = jnp.concatenate([k_prev, k_cur], axis=0)
        v_ctx = jnp.concatenate([v_prev, v_cur], axis=0)

        qg = q_ref[rows, gcols]
        qf = qg.astype(F32)
        q_t = qf.T.astype(BF16)
        q_t = jnp.concatenate([q_t[r * HEAD_DIM:(r + 1) * HEAD_DIM] for r in range(Q_PER_KV)],
                              axis=1)
        q2 = qf * qf
        q2_hi = q2.astype(BF16)
        q2_lo = (q2 - q2_hi.astype(F32)).astype(BF16)
        ssq = lax.dot_general(head_sel, jnp.concatenate([q2_hi, q2_lo], axis=1), _NT,
                              preferred_element_type=F32)
        rms = lax.rsqrt(ssq * (1.0 / HEAD_DIM) + NORM_EPS)
        rms_q = jnp.concatenate([rms[r:r + 1] for r in range(Q_PER_KV)], axis=1)

        s = _dot(k_ctx, q_t)
        s_prev = s[:WINDOW] + no_prev if n == 0 else s[:WINDOW]
        s = jnp.where(from_prev, s_prev, s[WINDOW:]) * rms_q
        mx = jnp.maximum(jnp.max(s, axis=0, keepdims=True), sink)
        p = jnp.exp2(s - mx)
        den = jnp.sum(p, axis=0, keepdims=True) + jnp.exp2(sink - mx)
        p_ctx = jnp.concatenate([jnp.where(from_prev, p, 0.0), jnp.where(from_prev, 0.0, p)],
                                axis=0).astype(BF16)
        o_t = lax.dot_general(v_ctx, p_ctx, _TN, preferred_element_type=F32) * (1.0 / den)
        halves = [jnp.concatenate([o_t[:, (2 * h) * WINDOW:(2 * h + 1) * WINDOW],
                                   o_t[:, (2 * h + 1) * WINDOW:(2 * h + 2) * WINDOW]], axis=0).T
                  for h in range(Q_PER_KV // 2)]
        og = jnp.concatenate(halves, axis=1)
        gate = g_ref[rows, gcols].astype(F32)
        o_ref[rows, gcols] = (og * (gate * jax.nn.sigmoid(gate))).astype(BF16)
        carry['k'], carry['v'] = k_cur, v_cur

    for g in range(N_KV_HEADS):
        for n in range(SWA_SUB):
            parts.append(functools.partial(block, g, n))
    return parts


def _swa_kernel(*refs, tiles_per_seq):
    for block in _swa_tile(*refs, pl.program_id(0) % tiles_per_seq == 0):
        block()


def _swa(proj, sinks, q_norm_w, k_norm_w, batch, seq):
    m = batch * seq
    cur = lambda col: (lambda s: (s, col))
    prev = lambda col: (lambda s: (jnp.maximum(s * SWA_SUB - 1, 0), col))
    kernel = functools.partial(_swa_kernel, tiles_per_seq=seq // SWA_TQ)
    return pl.pallas_call(
        kernel,
        grid=(m // SWA_TQ,),
        in_specs=[
            pl.BlockSpec(memory_space=pltpu.SMEM),
            pl.BlockSpec((SWA_TQ, ATTN_W), cur(COL_Q // ATTN_W)),
            pl.BlockSpec((SWA_TQ, ATTN_W), cur(COL_GATE // ATTN_W)),
            pl.BlockSpec((SWA_TQ, KV_W), cur(COL_K // KV_W)),
            pl.BlockSpec((SWA_TQ, KV_W), cur(COL_V // KV_W)),
            pl.BlockSpec((WINDOW, KV_W), prev(COL_K // KV_W)),
            pl.BlockSpec((WINDOW, KV_W), prev(COL_V // KV_W)),
            pl.BlockSpec((1, HEAD_DIM), lambda s: (0, 0)),
            pl.BlockSpec((1, HEAD_DIM), lambda s: (0, 0)),
        ],
        out_specs=pl.BlockSpec((SWA_TQ, ATTN_W), cur(0)),
        out_shape=jax.ShapeDtypeStruct((m, ATTN_W), BF16),
        compiler_params=pltpu.CompilerParams(
            dimension_semantics=("arbitrary",), vmem_limit_bytes=VMEM_LIMIT),
        name="swa",
    )(sinks, proj, proj, proj, proj, proj, proj,
      q_norm_w.reshape(1, HEAD_DIM), k_norm_w.reshape(1, HEAD_DIM))


def _slab_powers(ar, ai, dt, btr, bti):
    dta_re, dta_im = dt * ar, dt * ai
    mag = jnp.exp(dta_re)
    ab_re, ab_im = mag * jnp.cos(dta_im), mag * jnp.sin(dta_im)
    pw = [(jnp.ones_like(ar), jnp.zeros_like(ar))]
    for _ in range(CHUNK):
        pr, pi = pw[-1]
        pw.append((pr * ab_re - pi * ab_im, pr * ab_im + pi * ab_re))
    den = ar * ar + ai * ai
    num_re, num_im = ab_re - 1.0, ab_im
    cf_re = (num_re * ar + num_im * ai) / den
    cf_im = (num_im * ar - num_re * ai) / den
    bb_re = cf_re * btr - cf_im * bti
    bb_im = cf_re * bti + cf_im * btr
    abb = [(pr * bb_re - pi * bb_im, pr * bb_im + pi * bb_re) for pr, pi in pw[:CHUNK]]
    return pw, abb


def _build_state_in(abb, p_s):
    row_g = lax.broadcasted_iota(jnp.int32, (LANES, STATE), 0) // GROUP
    even = row_g % 2 == 0
    pair_of_row = lax.broadcasted_iota(jnp.int32, (LANES, 2 * LANES), 0) // (2 * GROUP)
    for lag in range(CHUNK):
        t = CHUNK - 1 - lag
        re, im = abb[lag]
        blk = jnp.concatenate([jnp.where(even, re, 0.0), jnp.where(even, 0.0, re),
                               jnp.where(even, im, 0.0), jnp.where(even, 0.0, im)], axis=1)
        for k in range(PAIRS):
            p_s[k, t * LANES:(t + 1) * LANES, :] = jnp.where(pair_of_row == k, blk, 0.0).astype(BF16)


def _build_state_out(pw, cr, ci, q_s):
    lane_g = lax.broadcasted_iota(jnp.int32, (STATE, LANES), 1) // GROUP
    for t in range(CHUNK):
        pr, pi = pw[t + 1]
        cat = jnp.concatenate([cr * pr - ci * pi, -(cr * pi + ci * pr)], axis=1)
        cat_t = cat.T
        for k in range(PAIRS):
            for ri in range(2):
                for half in range(2):
                    piece = jnp.where(lane_g == 2 * k + half, cat_t[ri * STATE:(ri + 1) * STATE], 0.0)
                    r0 = ri * LANES + half * STATE
                    q_s[k, r0:r0 + STATE, t * LANES:(t + 1) * LANES] = piece.astype(BF16)


def _build_toeplitz(abb, cr, ci, d_row, w_s):
    def split(v):
        hi = v.astype(BF16)
        return hi, (v - hi.astype(F32)).astype(BF16)

    rhs_hi, rhs_lo = split(jnp.concatenate([cr, -ci], axis=1))
    rhs = jnp.concatenate([rhs_hi, rhs_lo, rhs_hi], axis=1)
    row = lax.broadcasted_iota(jnp.int32, (LANES, LANES), 0)
    col = lax.broadcasted_iota(jnp.int32, (LANES, LANES), 1)
    same_group = (row // GROUP) == (col // GROUP)
    taps = []
    for lag in range(CHUNK):
        lhs_hi, lhs_lo = split(jnp.concatenate(abb[lag], axis=1))
        k = lax.dot_general(jnp.concatenate([lhs_hi, lhs_hi, lhs_lo], axis=1), rhs,
                            (((1,), (1,)), ((), ())), preferred_element_type=F32)
        k = jnp.where(same_group, k, 0.0)
        if lag == 0:
            k = k + jnp.where(row == col, d_row, 0.0)
        taps.append(k.astype(BF16))
    zero = jnp.zeros((LANES, LANES), BF16)
    for d in range(N_TILES):
        w_s[d, :LANES, :LANES] = taps[2 * d]
        w_s[d, :LANES, LANES:] = taps[2 * d + 1]
        w_s[d, LANES:, :LANES] = taps[2 * d - 1] if d > 0 else zero
        w_s[d, LANES:, LANES:] = taps[2 * d]


def _ssm_kernel(u_ref, ar_ref, ai_ref, dt_ref, btr_ref, bti_ref, cr_ref, ci_ref, d_ref,
                dtare_ref, dtaim_ref, w0_ref, w1_ref, w2_ref, w3_ref, y_ref, c0_ref, c1_ref, c2_ref,
                c3_ref, xre_s, xim_s, p_s, q_s, w_s, *, batch, n_chunks):
    step = pl.program_id(0)
    for src, dst in ((w0_ref, c0_ref), (w1_ref, c1_ref), (w2_ref, c2_ref), (w3_ref, c3_ref)):
        dst[...] = src[...].astype(BF16)
    rows8 = batch * PAIRS
    slab_rows = n_chunks * rows8

    def state_rows(slab, b, k):
        return pl.ds(slab * slab_rows + b * PAIRS + k, n_chunks, stride=rows8)

    def powers():
        return _slab_powers(ar_ref[...], ai_ref[...], dt_ref[...], btr_ref[...], bti_ref[...])

    @pl.when(step < N_SLABS)
    def _():
        _, abb = powers()
        _build_state_in(abb, p_s)
        u = u_ref[...]
        for k in range(PAIRS):
            x = _dot(u, p_s[k])
            for b in range(batch):
                xb = x[b * n_chunks:(b + 1) * n_chunks]
                xre_s[state_rows(step, b, k), :] = xb[:, :LANES]
                xim_s[state_rows(step, b, k), :] = xb[:, LANES:]

    @pl.when(step == N_SLABS)
    def _():
        dta_re = jnp.concatenate([dtare_ref[...]] * batch, axis=1).reshape(N_SLABS * rows8, LANES)
        dta_im = jnp.concatenate([dtaim_ref[...]] * batch, axis=1).reshape(N_SLABS * rows8, LANES)
        mag = jnp.exp(CHUNK * dta_re)
        a_re, a_im = mag * jnp.cos(CHUNK * dta_im), mag * jnp.sin(CHUNK * dta_im)

        def body(c, carry):
            s_re, s_im = carry
            x_re, x_im = [], []
            for sl in range(N_SLABS):
                off = pl.multiple_of(sl * slab_rows + c * rows8, rows8)
                x_re.append(xre_s[pl.ds(off, rows8), :])
                x_im.append(xim_s[pl.ds(off, rows8), :])
                xre_s[pl.ds(off, rows8), :] = s_re[sl * rows8:(sl + 1) * rows8]
                xim_s[pl.ds(off, rows8), :] = s_im[sl * rows8:(sl + 1) * rows8]
            x_re = jnp.concatenate(x_re, axis=0)
            x_im = jnp.concatenate(x_im, axis=0)
            return (a_re * s_re - a_im * s_im + x_re, a_re * s_im + a_im * s_re + x_im)

        zero = jnp.zeros((N_SLABS * rows8, LANES), F32)
        lax.fori_loop(0, n_chunks, body, (zero, zero))

    @pl.when(step > N_SLABS)
    def _():
        slab = step - (N_SLABS + 1)
        pw, abb = powers()
        cr, ci = cr_ref[...], ci_ref[...]
        _build_state_out(pw, cr, ci, q_s)
        _build_toeplitz(abb, cr, ci, d_ref[...], w_s)
        u = u_ref[...]
        y_state = None
        for k in range(PAIRS):
            s_in = jnp.concatenate(
                [jnp.concatenate([xre_s[state_rows(slab, b, k), :], xim_s[state_rows(slab, b, k), :]],
                                 axis=1) for b in range(batch)], axis=0).astype(BF16)
            part = _dot(s_in, q_s[k])
            y_state = part if y_state is None else y_state + part
        for t2 in range(N_TILES):
            acc = y_state[:, t2 * MXU:(t2 + 1) * MXU]
            for t1 in range(t2 + 1):
                acc = acc + _dot(u[:, t1 * MXU:(t1 + 1) * MXU], w_s[t2 - t1])
            y_ref[:, t2 * MXU:(t2 + 1) * MXU] = jax.nn.gelu(acc).astype(BF16)


N_CAST = 16


def _ssm(u_chunks, A_re, A_im, log_dt, B_re, B_im, C_re, C_im, D_skip, batch, tail_weights):
    _, rows, _ = u_chunks.shape
    n_chunks = rows // batch
    n_steps = 2 * N_SLABS + 1
    kernel = functools.partial(_ssm_kernel, batch=batch, n_chunks=n_chunks)

    dt = jnp.exp(log_dt)[:, None]
    per_row = lambda v: jnp.repeat(v, GROUP, axis=0)
    ar, ai = per_row(A_re), per_row(A_im)
    dtr = per_row(jnp.broadcast_to(dt, A_re.shape))
    btr = B_re.transpose(0, 2, 1).reshape(SSM_W, STATE)
    bti = B_im.transpose(0, 2, 1).reshape(SSM_W, STATE)
    cr, ci = C_re.reshape(SSM_W, STATE), C_im.reshape(SSM_W, STATE)
    d_rows = D_skip.reshape(N_SLABS, 1, LANES)
    dta_re = (dt * A_re).reshape(N_SLABS, PAIRS, LANES)
    dta_im = (dt * A_im).reshape(N_SLABS, PAIRS, LANES)

    def slab_in(s):
        return jnp.where(s <= N_SLABS, jnp.minimum(s, N_SLABS - 1), s - (N_SLABS + 1))

    def slab_out(s):
        return jnp.maximum(s - (N_SLABS + 1), 0)

    def cast_spec(w):
        return pl.BlockSpec((w.shape[0] // N_CAST, w.shape[1]),
                            lambda s: (jnp.minimum(s, N_CAST - 1), 0))

    coeff = pl.BlockSpec((LANES, STATE), lambda s: (slab_in(s), 0))
    whole = pl.BlockSpec((N_SLABS, PAIRS, LANES), lambda s: (0, 0, 0))
    return pl.pallas_call(
        kernel,
        grid=(n_steps,),
        in_specs=[
            pl.BlockSpec((None, rows, CHUNK_W), lambda s: (slab_in(s), 0, 0)),
            coeff, coeff, coeff, coeff, coeff, coeff, coeff,
            pl.BlockSpec((None, 1, LANES), lambda s: (slab_in(s), 0, 0)),
            whole, whole,
        ] + [cast_spec(w) for w in tail_weights],
        out_specs=[pl.BlockSpec((None, rows, CHUNK_W), lambda s: (slab_out(s), 0, 0))]
        + [cast_spec(w) for w in tail_weights],
        out_shape=[jax.ShapeDtypeStruct(u_chunks.shape, BF16)]
        + [jax.ShapeDtypeStruct(w.shape, BF16) for w in tail_weights],
        scratch_shapes=[
            pltpu.VMEM((N_SLABS * n_chunks * batch * PAIRS, LANES), F32),
            pltpu.VMEM((N_SLABS * n_chunks * batch * PAIRS, LANES), F32),
            pltpu.VMEM((PAIRS, CHUNK_W, 2 * LANES), BF16),
            pltpu.VMEM((PAIRS, 2 * LANES, CHUNK_W), BF16),
            pltpu.VMEM((N_TILES, MXU, MXU), BF16),
        ],
        compiler_params=pltpu.CompilerParams(
            dimension_semantics=("arbitrary",), vmem_limit_bytes=VMEM_LIMIT),
        name="ssm",
    )(u_chunks, ar, ai, dtr, btr, bti, cr, ci, d_rows, dta_re, dta_im, *tail_weights)


def _tail_kernel(ag_ref, yc_ref, z_ref, ga_ref, gs_ref, x_ref, wa_ref, wg_ref, bg_ref, ws_ref,
                 wo_ref, o_ref, ys_ref, *, tm):
    rows = tm // CHUNK
    for s in range(N_SLABS):
        for t in range(CHUNK):
            ys_ref[pl.ds(s * rows * US_PITCH + t, rows, stride=US_PITCH), :] = (
                yc_ref[s, :, t * LANES:(t + 1) * LANES].astype(F32))
    y_g = jnp.concatenate(
        [jnp.concatenate([ys_ref[(s * rows + c) * US_PITCH:(s * rows + c) * US_PITCH + CHUNK, :]
                          for c in range(rows)], axis=0) for s in range(N_SLABS)],
        axis=1).astype(BF16)

    y_a = _dot(ag_ref[...], wa_ref[...])
    glu = _dot(y_g, wg_ref[...]) + bg_ref[...]
    z = z_ref[...].astype(F32)
    t = glu[:, :SSM_W] * jax.nn.sigmoid(glu[:, SSM_W:]) * (z * jax.nn.sigmoid(z))
    y_s = _dot(t.astype(BF16), ws_ref[...])
    merged = (jax.nn.sigmoid(ga_ref[...].astype(F32)) * y_a
              + jax.nn.sigmoid(gs_ref[...].astype(F32)) * y_s)
    o_ref[...] = x_ref[...] + _dot(merged.astype(BF16), wo_ref[...])


def _tail(attn_g, y_chunks, proj, x2, w_attn, w_glu, b_glu, w_ssm, w_out, tm=256):
    m = x2.shape[0]
    row = lambda blk: (lambda i: (i, blk))
    const = lambda i: (0, 0)
    resident = functools.partial(pl.BlockSpec, index_map=const, pipeline_mode=pl.Buffered(1))
    kernel = functools.partial(_tail_kernel, tm=tm)
    return pl.pallas_call(
        kernel,
        grid=(m // tm,),
        in_specs=[
            pl.BlockSpec((tm, ATTN_W), row(0)),
            pl.BlockSpec((N_SLABS, tm // CHUNK, CHUNK_W), lambda i: (0, i, 0)),
            pl.BlockSpec((tm, SSM_W), row(COL_Z // SSM_W)),
            pl.BlockSpec((tm, D_MODEL), row(COL_GA // D_MODEL)),
            pl.BlockSpec((tm, D_MODEL), row(COL_GS // D_MODEL)),
            pl.BlockSpec((tm, D_MODEL), row(0)),
            resident((ATTN_W, D_MODEL)),
            resident((SSM_W, 2 * SSM_W)),
            resident((1, 2 * SSM_W)),
            resident((SSM_W, D_MODEL)),
            resident((D_MODEL, D_MODEL)),
        ],
        out_specs=pl.BlockSpec((tm, D_MODEL), row(0)),
        out_shape=jax.ShapeDtypeStruct((m, D_MODEL), F32),
        scratch_shapes=[pltpu.VMEM((N_SLABS * (tm // CHUNK) * US_PITCH, LANES), F32)],
        compiler_params=pltpu.CompilerParams(
            dimension_semantics=("arbitrary",), vmem_limit_bytes=VMEM_LIMIT),
        name="tail",
    )(attn_g, y_chunks, proj, proj, proj, x2, w_attn, w_glu, b_glu.reshape(1, -1), w_ssm, w_out)


def kernel(x, norm_w, w_in, q_norm_w, k_norm_w, sinks, w_attn_proj, A_re, A_im, log_dt, B_re, B_im,
           C_re, C_im, D_skip, w_glu, b_glu, w_ssm_proj, w_out):
    batch, seq, _ = x.shape
    m = batch * seq
    x2 = x.reshape(m, D_MODEL)

    proj, u_chunks = _in_proj(x2, norm_w, w_in)

    attn_g = _swa(proj, sinks, q_norm_w, k_norm_w, batch, seq)

    y_chunks, w_attn_bf, w_glu_bf, w_ssm_bf, w_out_bf = _ssm(
        u_chunks, A_re, A_im, log_dt, B_re, B_im, C_re, C_im, D_skip, batch,
        (w_attn_proj, w_glu, w_ssm_proj, w_out))

    out = _tail(attn_g, y_chunks, proj, x2, w_attn_bf, w_glu_bf, b_glu, w_ssm_bf, w_out_bf)
    return out.reshape(batch, seq, D_MODEL)
```

```python
import functools
import math

import jax
import jax.numpy as jnp
from jax import lax
from jax.experimental import pallas as pl
from jax.experimental.pallas import tpu as pltpu

F32 = jnp.float32
BF16 = jnp.bfloat16

D_MODEL = 2048
HEAD_DIM = 64
N_Q_HEADS = 16
N_KV_HEADS = 4
Q_PER_KV = 4
ATTN_W = N_Q_HEADS * HEAD_DIM
KV_W = N_KV_HEADS * HEAD_DIM
WINDOW = 128
SSM_W = D_MODEL // 2
GROUP = 16
N_GROUPS = SSM_W // GROUP
STATE = 64
NORM_EPS = 1e-6

COL_Q, COL_GATE, COL_U, COL_Z = 0, 1024, 2048, 3072
COL_GA, COL_GS, COL_K, COL_V = 4096, 6144, 8192, 8448
IN_W = 8704

LANES = 128
MXU = 256
CHUNK = 16
SLAB_G = LANES // GROUP
N_SLABS = N_GROUPS // SLAB_G
PAIRS = SLAB_G // 2
CHUNK_W = CHUNK * LANES
N_TILES = CHUNK_W // MXU
VMEM_LIMIT = 48 * 1024 * 1024
BIG_VMEM_LIMIT = 56 * 1024 * 1024


def _dot(a, b):
    return jnp.dot(a, b, preferred_element_type=F32)


IN_TN = 512
N_IN_TILES = IN_W // IN_TN
SRC_KV_TILE = (ATTN_W) // IN_TN
SRC_U_TILE0 = (ATTN_W + 2 * KV_W + ATTN_W) // IN_TN


US_PITCH = CHUNK + 8
IN_PER_ROUND = 2
OUT_DMA_PRIORITY = 1
IN_RING = 2 * IN_PER_ROUND


def _dst_tile(src_tile):
    if src_tile < SRC_KV_TILE:
        return src_tile
    if src_tile == SRC_KV_TILE:
        return N_IN_TILES - 1
    return src_tile - 1


def _in_proj_kernel(x_ref, nw_ref, w_hbm, o_hbm, uc_ref, h_ref, w_buf, o_buf, us_ref, w_sem, o_sem,
                    *, tm):
    i = pl.program_id(0)
    rows = tm // CHUNK
    n_sl = IN_TN // LANES

    def w_copy(t):
        return pltpu.make_async_copy(w_hbm.at[:, pl.ds(t * IN_TN, IN_TN)], w_buf.at[t % IN_RING],
                                     w_sem.at[t % IN_RING])

    def o_copy(t):
        dst = o_hbm.at[pl.ds(pl.multiple_of(i * tm, tm), tm), pl.ds(_dst_tile(t) * IN_TN, IN_TN)]
        return pltpu.make_async_copy(o_buf.at[t % IN_RING], dst, o_sem.at[t % IN_RING])

    def round_tiles(r):
        return [t for t in range(r * IN_PER_ROUND, (r + 1) * IN_PER_ROUND) if 0 <= t < N_IN_TILES]

    for t in round_tiles(0):
        w_copy(t).start()
    x = x_ref[...]
    ms = jnp.mean(x * x, axis=-1, keepdims=True)
    h_ref[...] = (x * lax.rsqrt(ms + NORM_EPS) * nw_ref[...]).astype(BF16)

    def finish(t, acc):
        o_buf[t % IN_RING] = acc.astype(BF16)
        if SRC_U_TILE0 <= t < SRC_U_TILE0 + SSM_W // IN_TN:
            slab0 = (t - SRC_U_TILE0) * n_sl
            for s in range(n_sl):
                for c in range(rows):
                    r0 = (s * rows + c) * US_PITCH
                    us_ref[r0:r0 + CHUNK, :] = acc[c * CHUNK:(c + 1) * CHUNK, s * LANES:(s + 1) * LANES]
            for s in range(n_sl):
                for tok in range(CHUNK):
                    piece = us_ref[pl.ds(s * rows * US_PITCH + tok, rows, stride=US_PITCH), :]
                    uc_ref[slab0 + s, :, tok * LANES:(tok + 1) * LANES] = piece.astype(BF16)

    n_rounds = pl.cdiv(N_IN_TILES, IN_PER_ROUND)
    for r in range(n_rounds):
        for t in round_tiles(r + 1):
            w_copy(t).start()
        for t in round_tiles(r):
            w_copy(t).wait()
        for t in round_tiles(r - 2):
            o_copy(t).wait()
        for t in round_tiles(r - 1):
            o_copy(t).start(priority=OUT_DMA_PRIORITY)
        for t in round_tiles(r):
            acc = None
            for k in range(D_MODEL // MXU):
                ks = slice(k * MXU, (k + 1) * MXU)
                part = _dot(h_ref[:, ks], w_buf[t % IN_RING, ks, :].astype(BF16))
                acc = part if acc is None else acc + part
            finish(t, acc)

    for t in round_tiles(n_rounds - 1):
        o_copy(t).start(priority=OUT_DMA_PRIORITY)
    for t in round_tiles(n_rounds - 2) + round_tiles(n_rounds - 1):
        o_copy(t).wait()


def _in_proj(x2, norm_w, w_in, tm=1024):
    m = x2.shape[0]
    kernel = functools.partial(_in_proj_kernel, tm=tm)
    return pl.pallas_call(
        kernel,
        grid=(m // tm,),
        in_specs=[
            pl.BlockSpec((tm, D_MODEL), lambda i: (i, 0)),
            pl.BlockSpec((1, D_MODEL), lambda i: (0, 0)),
            pl.BlockSpec(memory_space=pl.ANY),
        ],
        out_specs=[
            pl.BlockSpec(memory_space=pl.ANY),
            pl.BlockSpec((N_SLABS, tm // CHUNK, CHUNK_W), lambda i: (0, i, 0)),
        ],
        out_shape=[
            jax.ShapeDtypeStruct((m, IN_W), BF16),
            jax.ShapeDtypeStruct((N_SLABS, m // CHUNK, CHUNK_W), BF16),
        ],
        scratch_shapes=[
            pltpu.VMEM((tm, D_MODEL), BF16),
            pltpu.VMEM((IN_RING, D_MODEL, IN_TN), F32),
            pltpu.VMEM((IN_RING, tm, IN_TN), BF16),
            pltpu.VMEM((IN_TN // LANES * (tm // CHUNK) * US_PITCH, LANES), F32),
            pltpu.SemaphoreType.DMA((IN_RING,)),
            pltpu.SemaphoreType.DMA((IN_RING,)),
        ],
        compiler_params=pltpu.CompilerParams(
            dimension_semantics=("arbitrary",), vmem_limit_bytes=BIG_VMEM_LIMIT),
        name="in_proj",
    )(x2, norm_w.reshape(1, D_MODEL), w_in)


def _head_norm(t, w):
    ms = jnp.mean(t * t, axis=-1, keepdims=True)
    return t * lax.rsqrt(ms + NORM_EPS) * w


_NT = (((1,), (1,)), ((), ()))
_TN = (((0,), (0,)), ((), ()))


SWA_TQ = 1024
SWA_SUB = SWA_TQ // WINDOW


def _swa_tile(sink_ref, q_ref, g_ref, kc_ref, vc_ref, kp_ref, vp_ref, qw_ref, kw_ref, o_ref,
              first_tile):
    log2e = math.log2(math.e)
    kqw = kw_ref[...] * qw_ref[...] * (log2e / math.sqrt(HEAD_DIM))
    n_col = Q_PER_KV * WINDOW
    key = lax.broadcasted_iota(jnp.int32, (WINDOW, n_col), 0)
    qry = lax.broadcasted_iota(jnp.int32, (WINDOW, n_col), 1) % WINDOW
    from_prev = key > qry
    no_prev = jnp.where(first_tile, -1e30, 0.0)
    head_of_col = lax.broadcasted_iota(jnp.int32, (1, n_col), 1) // WINDOW
    gw = Q_PER_KV * HEAD_DIM
    sel_r = lax.broadcasted_iota(jnp.int32, (8, 2 * gw), 0)
    sel_l = (lax.broadcasted_iota(jnp.int32, (8, 2 * gw), 1) % gw) // HEAD_DIM
    head_sel = jnp.where(sel_r == sel_l, 1.0, 0.0).astype(BF16)

    parts = []
    carry = {}

    def block(g, n):
        kcol = slice(g * HEAD_DIM, (g + 1) * HEAD_DIM)
        gcols = slice(g * Q_PER_KV * HEAD_DIM, (g + 1) * Q_PER_KV * HEAD_DIM)
        if n == 0:
            sink = jnp.zeros((1, n_col), F32)
            for r in range(Q_PER_KV):
                sink = jnp.where(head_of_col == r, sink_ref[g * Q_PER_KV + r] * log2e, sink)
            carry['sink'] = sink
            carry['k'] = _head_norm(kp_ref[:, kcol].astype(F32), kqw).astype(BF16)
            carry['v'] = vp_ref[:, kcol]
        sink, k_prev, v_prev = carry['sink'], carry['k'], carry['v']
        rows = slice(n * WINDOW, (n + 1) * WINDOW)
        k_cur = _head_norm(kc_ref[rows, kcol].astype(F32), kqw).astype(BF16)
        v_cur = vc_ref[rows, kcol]
        k_ctx = jnp.concatenate([k_prev, k_cur], axis=0)
        v_ctx = jnp.concatenate([v_prev, v_cur], axis=0)

        qg = q_ref[rows, gcols]
        qf = qg.astype(F32)
        q_t = qf.T.astype(BF16)
        q_t = jnp.concatenate([q_t[r * HEAD_DIM:(r + 1) * HEAD_DIM] for r in range(Q_PER_KV)],
                              axis=1)
        q2 = qf * qf
        q2_hi = q2.astype(BF16)
        q2_lo = (q2 - q2_hi.astype(F32)).astype(BF16)
        ssq = lax.dot_general(head_sel, jnp.concatenate([q2_hi, q2_lo], axis=1), _NT,
                              preferred_element_type=F32)
        rms = lax.rsqrt(ssq * (1.0 / HEAD_DIM) + NORM_EPS)
        rms_q = jnp.concatenate([rms[r:r + 1] for r in range(Q_PER_KV)], axis=1)

        s = _dot(k_ctx, q_t)
        s_prev = s[:WINDOW] + no_prev if n == 0 else s[:WINDOW]
        s = jnp.where(from_prev, s_prev, s[WINDOW:]) * rms_q
        mx = jnp.maximum(jnp.max(s, axis=0, keepdims=True), sink)
        p = jnp.exp2(s - mx)
        den = jnp.sum(p, axis=0, keepdims=True) + jnp.exp2(sink - mx)
        p_ctx = jnp.concatenate([jnp.where(from_prev, p, 0.0), jnp.where(from_prev, 0.0, p)],
                                axis=0).astype(BF16)
        o_t = lax.dot_general(v_ctx, p_ctx, _TN, preferred_element_type=F32) * (1.0 / den)
        halves = [jnp.concatenate([o_t[:, (2 * h) * WINDOW:(2 * h + 1) * WINDOW],
                                   o_t[:, (2 * h + 1) * WINDOW:(2 * h + 2) * WINDOW]], axis=0).T
                  for h in range(Q_PER_KV // 2)]
        og = jnp.concatenate(halves, axis=1)
        gate = g_ref[rows, gcols].astype(F32)
        o_ref[rows, gcols] = (og * (gate * jax.nn.sigmoid(gate))).astype(BF16)
        carry['k'], carry['v'] = k_cur, v_cur

    for g in range(N_KV_HEADS):
        for n in range(SWA_SUB):
            parts.append(functools.partial(block, g, n))
    return parts


def _swa_kernel(*refs, tiles_per_seq):
    for block in _swa_tile(*refs, pl.program_id(0) % tiles_per_seq == 0):
        block()


def _swa(proj, sinks, q_norm_w, k_norm_w, batch, seq):
    m = batch * seq
    cur = lambda col: (lambda s: (s, col))
    prev = lambda col: (lambda s: (jnp.maximum(s * SWA_SUB - 1, 0), col))
    kernel = functools.partial(_swa_kernel, tiles_per_seq=seq // SWA_TQ)
    return pl.pallas_call(
        kernel,
        grid=(m // SWA_TQ,),
        in_specs=[
            pl.BlockSpec(memory_space=pltpu.SMEM),
            pl.BlockSpec((SWA_TQ, ATTN_W), cur(COL_Q // ATTN_W)),
            pl.BlockSpec((SWA_TQ, ATTN_W), cur(COL_GATE // ATTN_W)),
            pl.BlockSpec((SWA_TQ, KV_W), cur(COL_K // KV_W)),
            pl.BlockSpec((SWA_TQ, KV_W), cur(COL_V // KV_W)),
            pl.BlockSpec((WINDOW, KV_W), prev(COL_K // KV_W)),
            pl.BlockSpec((WINDOW, KV_W), prev(COL_V // KV_W)),
            pl.BlockSpec((1, HEAD_DIM), lambda s: (0, 0)),
            pl.BlockSpec((1, HEAD_DIM), lambda s: (0, 0)),
        ],
        out_specs=pl.BlockSpec((SWA_TQ, ATTN_W), cur(0)),
        out_shape=jax.ShapeDtypeStruct((m, ATTN_W), BF16),
        compiler_params=pltpu.CompilerParams(
            dimension_semantics=("arbitrary",), vmem_limit_bytes=VMEM_LIMIT),
        name="swa",
    )(sinks, proj, proj, proj, proj, proj, proj,
      q_norm_w.reshape(1, HEAD_DIM), k_norm_w.reshape(1, HEAD_DIM))


def _slab_powers(ar, ai, dt, btr, bti):
    dta_re, dta_im = dt * ar, dt * ai
    mag = jnp.exp(dta_re)
    ab_re, ab_im = mag * jnp.cos(dta_im), mag * jnp.sin(dta_im)
    pw = [(jnp.ones_like(ar), jnp.zeros_like(ar))]
    for _ in range(CHUNK):
        pr, pi = pw[-1]
        pw.append((pr * ab_re - pi * ab_im, pr * ab_im + pi * ab_re))
    den = ar * ar + ai * ai
    num_re, num_im = ab_re - 1.0, ab_im
    cf_re = (num_re * ar + num_im * ai) / den
    cf_im = (num_im * ar - num_re * ai) / den
    bb_re = cf_re * btr - cf_im * bti
    bb_im = cf_re * bti + cf_im * btr
    abb = [(pr * bb_re - pi * bb_im, pr * bb_im + pi * bb_re) for pr, pi in pw[:CHUNK]]
    return pw, abb


def _build_state_in(abb, p_s):
    row_g = lax.broadcasted_iota(jnp.int32, (LANES, STATE), 0) // GROUP
    even = row_g % 2 == 0
    pair_of_row = lax.broadcasted_iota(jnp.int32, (LANES, 2 * LANES), 0) // (2 * GROUP)
    for lag in range(CHUNK):
        t = CHUNK - 1 - lag
        re, im = abb[lag]
        blk = jnp.concatenate([jnp.where(even, re, 0.0), jnp.where(even, 0.0, re),
                               jnp.where(even, im, 0.0), jnp.where(even, 0.0, im)], axis=1)
        for k in range(PAIRS):
            p_s[k, t * LANES:(t + 1) * LANES, :] = jnp.where(pair_of_row == k, blk, 0.0).astype(BF16)


def _build_state_out(pw, cr, ci, q_s):
    lane_g = lax.broadcasted_iota(jnp.int32, (STATE, LANES), 1) // GROUP
    for t in range(CHUNK):
        pr, pi = pw[t + 1]
        cat = jnp.concatenate([cr * pr - ci * pi, -(cr * pi + ci * pr)], axis=1)
        cat_t = cat.T
        for k in range(PAIRS):
            for ri in range(2):
                for half in range(2):
                    piece = jnp.where(lane_g == 2 * k + half, cat_t[ri * STATE:(ri + 1) * STATE], 0.0)
                    r0 = ri * LANES + half * STATE
                    q_s[k, r0:r0 + STATE, t * LANES:(t + 1) * LANES] = piece.astype(BF16)


def _build_toeplitz(abb, cr, ci, d_row, w_s):
    def split(v):
        hi = v.astype(BF16)
        return hi, (v - hi.astype(F32)).astype(BF16)

    rhs_hi, rhs_lo = split(jnp.concatenate([cr, -ci], axis=1))
    rhs = jnp.concatenate([rhs_hi, rhs_lo, rhs_hi], axis=1)
    row = lax.broadcasted_iota(jnp.int32, (LANES, LANES), 0)
    col = lax.broadcasted_iota(jnp.int32, (LANES, LANES), 1)
    same_group = (row // GROUP) == (col // GROUP)
    taps = []
    for lag in range(CHUNK):
        lhs_hi, lhs_lo = split(jnp.concatenate(abb[lag], axis=1))
        k = lax.dot_general(jnp.concatenate([lhs_hi, lhs_hi, lhs_lo], axis=1), rhs,
                            (((1,), (1,)), ((), ())), preferred_element_type=F32)
        k = jnp.where(same_group, k, 0.0)
        if lag == 0:
            k = k + jnp.where(row == col, d_row, 0.0)
        taps.append(k.astype(BF16))
    zero = jnp.zeros((LANES, LANES), BF16)
    for d in range(N_TILES):
        w_s[d, :LANES, :LANES] = taps[2 * d]
        w_s[d, :LANES, LANES:] = taps[2 * d + 1]
        w_s[d, LANES:, :LANES] = taps[2 * d - 1] if d > 0 else zero
        w_s[d, LANES:, LANES:] = taps[2 * d]


def _ssm_kernel(u_ref, ar_ref, ai_ref, dt_ref, btr_ref, bti_ref, cr_ref, ci_ref, d_ref,
                dtare_ref, dtaim_ref, w0_ref, w1_ref, w2_ref, w3_ref, y_ref, c0_ref, c1_ref, c2_ref,
                c3_ref, xre_s, xim_s, p_s, q_s, w_s, *, batch, n_chunks):
    step = pl.program_id(0)
    for src, dst in ((w0_ref, c0_ref), (w1_ref, c1_ref), (w2_ref, c2_ref), (w3_ref, c3_ref)):
        dst[...] = src[...].astype(BF16)
    rows8 = batch * PAIRS
    slab_rows = n_chunks * rows8

    def state_rows(slab, b, k):
        return pl.ds(slab * slab_rows + b * PAIRS + k, n_chunks, stride=rows8)

    def powers():
        return _slab_powers(ar_ref[...], ai_ref[...], dt_ref[...], btr_ref[...], bti_ref[...])

    @pl.when(step < N_SLABS)
    def _():
        _, abb = powers()
        _build_state_in(abb, p_s)
        u = u_ref[...]
        for k in range(PAIRS):
            x = _dot(u, p_s[k])
            for b in range(batch):
                xb = x[b * n_chunks:(b + 1) * n_chunks]
                xre_s[state_rows(step, b, k), :] = xb[:, :LANES]
                xim_s[state_rows(step, b, k), :] = xb[:, LANES:]

    @pl.when(step == N_SLABS)
    def _():
        dta_re = jnp.concatenate([dtare_ref[...]] * batch, axis=1).reshape(N_SLABS * rows8, LANES)
        dta_im = jnp.concatenate([dtaim_ref[...]] * batch, axis=1).reshape(N_SLABS * rows8, LANES)
        mag = jnp.exp(CHUNK * dta_re)
        a_re, a_im = mag * jnp.cos(CHUNK * dta_im), mag * jnp.sin(CHUNK * dta_im)

        def body(c, carry):
            s_re, s_im = carry
            x_re, x_im = [], []
            for sl in range(N_SLABS):
                off = pl.multiple_of(sl * slab_rows + c * rows8, rows8)
                x_re.append(xre_s[pl.ds(off, rows8), :])
                x_im.append(xim_s[pl.ds(off, rows8), :])
                xre_s[pl.ds(off, rows8), :] = s_re[sl * rows8:(sl + 1) * rows8]
                xim_s[pl.ds(off, rows8), :] = s_im[sl * rows8:(sl + 1) * rows8]
            x_re = jnp.concatenate(x_re, axis=0)
            x_im = jnp.concatenate(x_im, axis=0)
            return (a_re * s_re - a_im * s_im + x_re, a_re * s_im + a_im * s_re + x_im)

        zero = jnp.zeros((N_SLABS * rows8, LANES), F32)
        lax.fori_loop(0, n_chunks, body, (zero, zero))

    @pl.when(step > N_SLABS)
    def _():
        slab = step - (N_SLABS + 1)
        pw, abb = powers()
        cr, ci = cr_ref[...], ci_ref[...]
        _build_state_out(pw, cr, ci, q_s)
        _build_toeplitz(abb, cr, ci, d_ref[...], w_s)
        u = u_ref[...]
        y_state = None
        for k in range(PAIRS):
            s_in = jnp.concatenate(
                [jnp.concatenate([xre_s[state_rows(slab, b, k), :], xim_s[state_rows(slab, b, k), :]],
                                 axis=1) for b in range(batch)], axis=0).astype(BF16)
            part = _dot(s_in, q_s[k])
            y_state = part if y_state is None else y_state + part
        for t2 in range(N_TILES):
            acc = y_state[:, t2 * MXU:(t2 + 1) * MXU]
            for t1 in range(t2 + 1):
                acc = acc + _dot(u[:, t1 * MXU:(t1 + 1) * MXU], w_s[t2 - t1])
            y_ref[:, t2 * MXU:(t2 + 1) * MXU] = jax.nn.gelu(acc).astype(BF16)


N_CAST = 16


def _ssm(u_chunks, A_re, A_im, log_dt, B_re, B_im, C_re, C_im, D_skip, batch, tail_weights):
    _, rows, _ = u_chunks.shape
    n_chunks = rows // batch
    n_steps = 2 * N_SLABS + 1
    kernel = functools.partial(_ssm_kernel, batch=batch, n_chunks=n_chunks)

    dt = jnp.exp(log_dt)[:, None]
    per_row = lambda v: jnp.repeat(v, GROUP, axis=0)
    ar, ai = per_row(A_re), per_row(A_im)
    dtr = per_row(jnp.broadcast_to(dt, A_re.shape))
    btr = B_re.transpose(0, 2, 1).reshape(SSM_W, STATE)
    bti = B_im.transpose(0, 2, 1).reshape(SSM_W, STATE)
    cr, ci = C_re.reshape(SSM_W, STATE), C_im.reshape(SSM_W, STATE)
    d_rows = D_skip.reshape(N_SLABS, 1, LANES)
    dta_re = (dt * A_re).reshape(N_SLABS, PAIRS, LANES)
    dta_im = (dt * A_im).reshape(N_SLABS, PAIRS, LANES)

    def slab_in(s):
        return jnp.where(s <= N_SLABS, jnp.minimum(s, N_SLABS - 1), s - (N_SLABS + 1))

    def slab_out(s):
        return jnp.maximum(s - (N_SLABS + 1), 0)

    def cast_spec(w):
        return pl.BlockSpec((w.shape[0] // N_CAST, w.shape[1]),
                            lambda s: (jnp.minimum(s, N_CAST - 1), 0))

    coeff = pl.BlockSpec((LANES, STATE), lambda s: (slab_in(s), 0))
    whole = pl.BlockSpec((N_SLABS, PAIRS, LANES), lambda s: (0, 0, 0))
    return pl.pallas_call(
        kernel,
        grid=(n_steps,),
        in_specs=[
            pl.BlockSpec((None, rows, CHUNK_W), lambda s: (slab_in(s), 0, 0)),
            coeff, coeff, coeff, coeff, coeff, coeff, coeff,
            pl.BlockSpec((None, 1, LANES), lambda s: (slab_in(s), 0, 0)),
            whole, whole,
        ] + [cast_spec(w) for w in tail_weights],
        out_specs=[pl.BlockSpec((None, rows, CHUNK_W), lambda s: (slab_out(s), 0, 0))]
        + [cast_spec(w) for w in tail_weights],
        out_shape=[jax.ShapeDtypeStruct(u_chunks.shape, BF16)]
        + [jax.ShapeDtypeStruct(w.shape, BF16) for w in tail_weights],
        scratch_shapes=[
            pltpu.VMEM((N_SLABS * n_chunks * batch * PAIRS, LANES), F32),
            pltpu.VMEM((N_SLABS * n_chunks * batch * PAIRS, LANES), F32),
            pltpu.VMEM((PAIRS, CHUNK_W, 2 * LANES), BF16),
            pltpu.VMEM((PAIRS, 2 * LANES, CHUNK_W), BF16),
            pltpu.VMEM((N_TILES, MXU, MXU), BF16),
        ],
        compiler_params=pltpu.CompilerParams(
            dimension_semantics=("arbitrary",), vmem_limit_bytes=VMEM_LIMIT),
        name="ssm",
    )(u_chunks, ar, ai, dtr, btr, bti, cr, ci, d_rows, dta_re, dta_im, *tail_weights)


def _tail_kernel(ag_ref, yc_ref, z_ref, ga_ref, gs_ref, x_ref, wa_ref, wg_ref, bg_ref, ws_ref,
                 wo_ref, o_ref, ys_ref, *, tm):
    rows = tm // CHUNK
    for s in range(N_SLABS):
        for t in range(CHUNK):
            ys_ref[pl.ds(s * rows * US_PITCH + t, rows, stride=US_PITCH), :] = (
                yc_ref[s, :, t * LANES:(t + 1) * LANES].astype(F32))
    y_g = jnp.concatenate(
        [jnp.concatenate([ys_ref[(s * rows + c) * US_PITCH:(s * rows + c) * US_PITCH + CHUNK, :]
                          for c in range(rows)], axis=0) for s in range(N_SLABS)],
        axis=1).astype(BF16)

    y_a = _dot(ag_ref[...], wa_ref[...])
    glu = _dot(y_g, wg_ref[...]) + bg_ref[...]
    z = z_ref[...].astype(F32)
    t = glu[:, :SSM_W] * jax.nn.sigmoid(glu[:, SSM_W:]) * (z * jax.nn.sigmoid(z))
    y_s = _dot(t.astype(BF16), ws_ref[...])
    merged = (jax.nn.sigmoid(ga_ref[...].astype(F32)) * y_a
              + jax.nn.sigmoid(gs_ref[...].astype(F32)) * y_s)
    o_ref[...] = x_ref[...] + _dot(merged.astype(BF16), wo_ref[...])


def _tail(attn_g, y_chunks, proj, x2, w_attn, w_glu, b_glu, w_ssm, w_out, tm=256):
    m = x2.shape[0]
    row = lambda blk: (lambda i: (i, blk))
    const = lambda i: (0, 0)
    resident = functools.partial(pl.BlockSpec, index_map=const, pipeline_mode=pl.Buffered(1))
    kernel = functools.partial(_tail_kernel, tm=tm)
    return pl.pallas_call(
        kernel,
        grid=(m // tm,),
        in_specs=[
            pl.BlockSpec((tm, ATTN_W), row(0)),
            pl.BlockSpec((N_SLABS, tm // CHUNK, CHUNK_W), lambda i: (0, i, 0)),
            pl.BlockSpec((tm, SSM_W), row(COL_Z // SSM_W)),
            pl.BlockSpec((tm, D_MODEL), row(COL_GA // D_MODEL)),
            pl.BlockSpec((tm, D_MODEL), row(COL_GS // D_MODEL)),
            pl.BlockSpec((tm, D_MODEL), row(0)),
            resident((ATTN_W, D_MODEL)),
            resident((SSM_W, 2 * SSM_W)),
            resident((1, 2 * SSM_W)),
            resident((SSM_W, D_MODEL)),
            resident((D_MODEL, D_MODEL)),
        ],
        out_specs=pl.BlockSpec((tm, D_MODEL), row(0)),
        out_shape=jax.ShapeDtypeStruct((m, D_MODEL), F32),
        scratch_shapes=[pltpu.VMEM((N_SLABS * (tm // CHUNK) * US_PITCH, LANES), F32)],
        compiler_params=pltpu.CompilerParams(
            dimension_semantics=("arbitrary",), vmem_limit_bytes=VMEM_LIMIT),
        name="tail",
    )(attn_g, y_chunks, proj, proj, proj, x2, w_attn, w_glu, b_glu.reshape(1, -1), w_ssm, w_out)


def kernel(x, norm_w, w_in, q_norm_w, k_norm_w, sinks, w_attn_proj, A_re, A_im, log_dt, B_re, B_im,
           C_re, C_im, D_skip, w_glu, b_glu, w_ssm_proj, w_out):
    batch, seq, _ = x.shape
    m = batch * seq
    x2 = x.reshape(m, D_MODEL)

    proj, u_chunks = _in_proj(x2, norm_w, w_in)

    attn_g = _swa(proj, sinks, q_norm_w, k_norm_w, batch, seq)

    y_chunks, w_attn_bf, w_glu_bf, w_ssm_bf, w_out_bf = _ssm(
        u_chunks, A_re, A_im, log_dt, B_re, B_im, C_re, C_im, D_skip, batch,
        (w_attn_proj, w_glu, w_ssm_proj, w_out))

    out = _tail(attn_g, y_chunks, proj, x2, w_attn_bf, w_glu_bf, b_glu, w_ssm_bf, w_out_bf)
    return out.reshape(batch, seq, D_MODEL)
```

```python
import functools
import math

import jax
import jax.numpy as jnp
from jax import lax
from jax.experimental import pallas as pl
from jax.experimental.pallas import tpu as pltpu

F32 = jnp.float32
BF16 = jnp.bfloat16

D_MODEL = 2048
HEAD_DIM = 64
N_Q_HEADS = 16
N_KV_HEADS = 4
Q_PER_KV = 4
ATTN_W = N_Q_HEADS * HEAD_DIM
KV_W = N_KV_HEADS * HEAD_DIM
WINDOW = 128
SSM_W = D_MODEL // 2
GROUP = 16
N_GROUPS = SSM_W // GROUP
STATE = 64
NORM_EPS = 1e-6

COL_Q = 0
COL_GATE = COL_Q + ATTN_W
COL_U = COL_GATE + ATTN_W
COL_Z = COL_U + SSM_W
COL_GA = COL_Z + SSM_W
COL_GS = COL_GA + D_MODEL
COL_K = COL_GS + D_MODEL
COL_V = COL_K + KV_W
IN_W = COL_V + KV_W

LANES = 128
MXU = 256
CHUNK = 16
SLAB_G = LANES // GROUP
N_SLABS = N_GROUPS // SLAB_G
PAIRS = SLAB_G // 2
CHUNK_W = CHUNK * LANES
N_TILES = CHUNK_W // MXU
VMEM_LIMIT = 48 * 1024 * 1024
BIG_VMEM_LIMIT = 56 * 1024 * 1024


def _dot(a, b):
    return jnp.dot(a, b, preferred_element_type=F32)


IN_TN = 512
N_IN_TILES = IN_W // IN_TN
SRC_KV_TILE = (ATTN_W) // IN_TN
SRC_U_TILE0 = (ATTN_W + 2 * KV_W + ATTN_W) // IN_TN


US_PITCH = CHUNK if (CHUNK // 8) % 2 else CHUNK + 8
IN_PER_ROUND = 2
IN_RING = 2 * IN_PER_ROUND


def _dst_tile(src_tile):
    if src_tile < SRC_KV_TILE:
        return src_tile
    if src_tile == SRC_KV_TILE:
        return N_IN_TILES - 1
    return src_tile - 1


def _in_proj_kernel(x_ref, nw_ref, w_hbm, o_hbm, uc_ref, h_ref, w_buf, o_buf, us_ref, w_sem, o_sem,
                    *, tm):
    i = pl.program_id(0)
    rows = tm // CHUNK
    n_sl = IN_TN // LANES

    def w_copy(t):
        return pltpu.make_async_copy(w_hbm.at[:, pl.ds(t * IN_TN, IN_TN)], w_buf.at[t % IN_RING],
                                     w_sem.at[t % IN_RING])

    def o_copy(t):
        dst = o_hbm.at[pl.ds(pl.multiple_of(i * tm, tm), tm), pl.ds(_dst_tile(t) * IN_TN, IN_TN)]
        return pltpu.make_async_copy(o_buf.at[t % IN_RING], dst, o_sem.at[t % IN_RING])

    def round_tiles(r):
        return [t for t in range(r * IN_PER_ROUND, (r + 1) * IN_PER_ROUND) if 0 <= t < N_IN_TILES]

    for t in round_tiles(0):
        w_copy(t).start()
    x = x_ref[...]
    ms = jnp.mean(x * x, axis=-1, keepdims=True)
    h_ref[...] = (x * lax.rsqrt(ms + NORM_EPS) * nw_ref[...]).astype(BF16)

    def finish(t, acc):
        o_buf[t % IN_RING] = acc.astype(BF16)
        if SRC_U_TILE0 <= t < SRC_U_TILE0 + SSM_W // IN_TN:
            slab0 = (t - SRC_U_TILE0) * n_sl
            for s in range(n_sl):
                for c in range(rows):
                    r0 = (s * rows + c) * US_PITCH
                    us_ref[r0:r0 + CHUNK, :] = acc[c * CHUNK:(c + 1) * CHUNK, s * LANES:(s + 1) * LANES]
            for s in range(n_sl):
                for tok in range(CHUNK):
                    piece = us_ref[pl.ds(s * rows * US_PITCH + tok, rows, stride=US_PITCH), :]
                    uc_ref[slab0 + s, :, tok * LANES:(tok + 1) * LANES] = piece.astype(BF16)

    n_rounds = pl.cdiv(N_IN_TILES, IN_PER_ROUND)
    for r in range(n_rounds):
        for t in round_tiles(r + 1):
            w_copy(t).start()
        for t in round_tiles(r):
            w_copy(t).wait()
        for t in round_tiles(r - 2):
            o_copy(t).wait()
        for t in round_tiles(r - 1):
            o_copy(t).start()
        for t in round_tiles(r):
            acc = None
            for k in range(D_MODEL // MXU):
                ks = slice(k * MXU, (k + 1) * MXU)
                part = _dot(h_ref[:, ks], w_buf[t % IN_RING, ks, :].astype(BF16))
                acc = part if acc is None else acc + part
            finish(t, acc)

    for t in round_tiles(n_rounds - 1):
        o_copy(t).start()
    for t in round_tiles(n_rounds - 2) + round_tiles(n_rounds - 1):
        o_copy(t).wait()


def _in_proj(x2, norm_w, w_in, tm=1024):
    m = x2.shape[0]
    kernel = functools.partial(_in_proj_kernel, tm=tm)
    return pl.pallas_call(
        kernel,
        grid=(m // tm,),
        in_specs=[
            pl.BlockSpec((tm, D_MODEL), lambda i: (i, 0)),
            pl.BlockSpec((1, D_MODEL), lambda i: (0, 0)),
            pl.BlockSpec(memory_space=pl.ANY),
        ],
        out_specs=[
            pl.BlockSpec(memory_space=pl.ANY),
            pl.BlockSpec((N_SLABS, tm // CHUNK, CHUNK_W), lambda i: (0, i, 0)),
        ],
        out_shape=[
            jax.ShapeDtypeStruct((m, IN_W), BF16),
            jax.ShapeDtypeStruct((N_SLABS, m // CHUNK, CHUNK_W), BF16),
        ],
        scratch_shapes=[
            pltpu.VMEM((tm, D_MODEL), BF16),
            pltpu.VMEM((IN_RING, D_MODEL, IN_TN), F32),
            pltpu.VMEM((IN_RING, tm, IN_TN), BF16),
            pltpu.VMEM((IN_TN // LANES * (tm // CHUNK) * US_PITCH, LANES), F32),
            pltpu.SemaphoreType.DMA((IN_RING,)),
            pltpu.SemaphoreType.DMA((IN_RING,)),
        ],
        compiler_params=pltpu.CompilerParams(
            dimension_semantics=("arbitrary",), vmem_limit_bytes=BIG_VMEM_LIMIT),
        name="in_proj",
    )(x2, norm_w.reshape(1, D_MODEL), w_in)


def _head_norm(t, w):
    ms = jnp.mean(t * t, axis=-1, keepdims=True)
    return t * lax.rsqrt(ms + NORM_EPS) * w


_NT = (((1,), (1,)), ((), ()))
_TN = (((0,), (0,)), ((), ()))


SWA_TQ = 1024
SWA_SUB = SWA_TQ // WINDOW


def _swa_tile(sink_ref, q_ref, g_ref, kc_ref, vc_ref, kp_ref, vp_ref, qw_ref, kw_ref, o_ref,
              first_tile):
    log2e = math.log2(math.e)
    kqw = kw_ref[...] * qw_ref[...] * (log2e / math.sqrt(HEAD_DIM))
    n_col = Q_PER_KV * WINDOW
    key = lax.broadcasted_iota(jnp.int32, (WINDOW, n_col), 0)
    qry = lax.broadcasted_iota(jnp.int32, (WINDOW, n_col), 1) % WINDOW
    from_prev = key > qry
    no_prev = jnp.where(first_tile, -1e30, 0.0)
    head_of_col = lax.broadcasted_iota(jnp.int32, (1, n_col), 1) // WINDOW
    gw = Q_PER_KV * HEAD_DIM
    sel_r = lax.broadcasted_iota(jnp.int32, (8, 2 * gw), 0)
    sel_l = (lax.broadcasted_iota(jnp.int32, (8, 2 * gw), 1) % gw) // HEAD_DIM
    head_sel = jnp.where(sel_r == sel_l, 1.0, 0.0).astype(BF16)

    parts = []
    carry = {}

    def block(g, n):
        kcol = slice(g * HEAD_DIM, (g + 1) * HEAD_DIM)
        gcols = slice(g * Q_PER_KV * HEAD_DIM, (g + 1) * Q_PER_KV * HEAD_DIM)
        if n == 0:
            sink = jnp.zeros((1, n_col), F32)
            for r in range(Q_PER_KV):
                sink = jnp.where(head_of_col == r, sink_ref[g * Q_PER_KV + r] * log2e, sink)
            carry['sink'] = sink
            carry['k'] = _head_norm(kp_ref[:, kcol].astype(F32), kqw).astype(BF16)
            carry['v'] = vp_ref[:, kcol]
        sink, k_prev, v_prev = carry['sink'], carry['k'], carry['v']
        rows = slice(n * WINDOW, (n + 1) * WINDOW)
        k_cur = _head_norm(kc_ref[rows, kcol].astype(F32), kqw).astype(BF16)
        v_cur = vc_ref[rows, kcol]
        k_ctx = jnp.concatenate([k_prev, k_cur], axis=0)
        v_ctx = jnp.concatenate([v_prev, v_cur], axis=0)

        qg = q_ref[rows, gcols]
        qf = qg.astype(F32)
        q_t = qf.T.astype(BF16)
        q_t = jnp.concatenate([q_t[r * HEAD_DIM:(r + 1) * HEAD_DIM] for r in range(Q_PER_KV)],
                              axis=1)
        q2 = qf * qf
        q2_hi = q2.astype(BF16)
        q2_lo = (q2 - q2_hi.astype(F32)).astype(BF16)
        ssq = lax.dot_general(head_sel, jnp.concatenate([q2_hi, q2_lo], axis=1), _NT,
                              preferred_element_type=F32)
        rms = lax.rsqrt(ssq * (1.0 / HEAD_DIM) + NORM_EPS)
        rms_q = jnp.concatenate([rms[r:r + 1] for r in range(Q_PER_KV)], axis=1)

        s = _dot(k_ctx, q_t)
        s_prev = s[:WINDOW] + no_prev if n == 0 else s[:WINDOW]
        s = jnp.where(from_prev, s_prev, s[WINDOW:]) * rms_q
        mx = jnp.maximum(jnp.max(s, axis=0, keepdims=True), sink)
        p = jnp.exp2(s - mx)
        den = jnp.sum(p, axis=0, keepdims=True) + jnp.exp2(sink - mx)
        p_ctx = jnp.concatenate([jnp.where(from_prev, p, 0.0), jnp.where(from_prev, 0.0, p)],
                                axis=0).astype(BF16)
        o_t = lax.dot_general(v_ctx, p_ctx, _TN, preferred_element_type=F32) * (1.0 / den)
        halves = [jnp.concatenate([o_t[:, (2 * h) * WINDOW:(2 * h + 1) * WINDOW],
                                   o_t[:, (2 * h + 1) * WINDOW:(2 * h + 2) * WINDOW]], axis=0).T
                  for h in range(Q_PER_KV // 2)]
        og = jnp.concatenate(halves, axis=1)
        gate = g_ref[rows, gcols].astype(F32)
        o_ref[rows, gcols] = (og * (gate * jax.nn.sigmoid(gate))).astype(BF16)
        carry['k'], carry['v'] = k_cur, v_cur

    for g in range(N_KV_HEADS):
        for n in range(SWA_SUB):
            parts.append(functools.partial(block, g, n))
    return parts


def _swa_kernel(*refs, tiles_per_seq):
    for block in _swa_tile(*refs, pl.program_id(0) % tiles_per_seq == 0):
        block()


def _swa(proj, sinks, q_norm_w, k_norm_w, batch, seq):
    m = batch * seq
    cur = lambda col: (lambda s: (s, col))
    prev = lambda col: (lambda s: (jnp.maximum(s * SWA_SUB - 1, 0), col))
    kernel = functools.partial(_swa_kernel, tiles_per_seq=seq // SWA_TQ)
    return pl.pallas_call(
        kernel,
        grid=(m // SWA_TQ,),
        in_specs=[
            pl.BlockSpec(memory_space=pltpu.SMEM),
            pl.BlockSpec((SWA_TQ, ATTN_W), cur(COL_Q // ATTN_W)),
            pl.BlockSpec((SWA_TQ, ATTN_W), cur(COL_GATE // ATTN_W)),
            pl.BlockSpec((SWA_TQ, KV_W), cur(COL_K // KV_W)),
            pl.BlockSpec((SWA_TQ, KV_W), cur(COL_V // KV_W)),
            pl.BlockSpec((WINDOW, KV_W), prev(COL_K // KV_W)),
            pl.BlockSpec((WINDOW, KV_W), prev(COL_V // KV_W)),
            pl.BlockSpec((1, HEAD_DIM), lambda s: (0, 0)),
            pl.BlockSpec((1, HEAD_DIM), lambda s: (0, 0)),
        ],
        out_specs=pl.BlockSpec((SWA_TQ, ATTN_W), cur(0)),
        out_shape=jax.ShapeDtypeStruct((m, ATTN_W), BF16),
        compiler_params=pltpu.CompilerParams(
            dimension_semantics=("arbitrary",), vmem_limit_bytes=VMEM_LIMIT),
        name="swa",
    )(sinks, proj, proj, proj, proj, proj, proj,
      q_norm_w.reshape(1, HEAD_DIM), k_norm_w.reshape(1, HEAD_DIM))


def _slab_powers(ar, ai, dt, btr, bti):
    dta_re, dta_im = dt * ar, dt * ai
    mag = jnp.exp(dta_re)
    ab_re, ab_im = mag * jnp.cos(dta_im), mag * jnp.sin(dta_im)
    pw = [(jnp.ones_like(ar), jnp.zeros_like(ar))]
    for _ in range(CHUNK):
        pr, pi = pw[-1]
        pw.append((pr * ab_re - pi * ab_im, pr * ab_im + pi * ab_re))
    den = ar * ar + ai * ai
    num_re, num_im = ab_re - 1.0, ab_im
    cf_re = (num_re * ar + num_im * ai) / den
    cf_im = (num_im * ar - num_re * ai) / den
    bb_re = cf_re * btr - cf_im * bti
    bb_im = cf_re * bti + cf_im * btr
    abb = [(pr * bb_re - pi * bb_im, pr * bb_im + pi * bb_re) for pr, pi in pw[:CHUNK]]
    return pw, abb


def _build_state_in(abb, p_s):
    row_g = lax.broadcasted_iota(jnp.int32, (LANES, STATE), 0) // GROUP
    even = row_g % 2 == 0
    pair_of_row = lax.broadcasted_iota(jnp.int32, (LANES, 2 * LANES), 0) // (2 * GROUP)
    for lag in range(CHUNK):
        t = CHUNK - 1 - lag
        re, im = abb[lag]
        blk = jnp.concatenate([jnp.where(even, re, 0.0), jnp.where(even, 0.0, re),
                               jnp.where(even, im, 0.0), jnp.where(even, 0.0, im)], axis=1)
        for k in range(PAIRS):
            p_s[k, t * LANES:(t + 1) * LANES, :] = jnp.where(pair_of_row == k, blk, 0.0).astype(BF16)


def _build_state_out(pw, cr, ci, q_s):
    lane_g = lax.broadcasted_iota(jnp.int32, (STATE, LANES), 1) // GROUP
    for t in range(CHUNK):
        pr, pi = pw[t + 1]
        cat = jnp.concatenate([cr * pr - ci * pi, -(cr * pi + ci * pr)], axis=1)
        cat_t = cat.T
        for k in range(PAIRS):
            for ri in range(2):
                for half in range(2):
                    piece = jnp.where(lane_g == 2 * k + half, cat_t[ri * STATE:(ri + 1) * STATE], 0.0)
                    r0 = ri * LANES + half * STATE
                    q_s[k, r0:r0 + STATE, t * LANES:(t + 1) * LANES] = piece.astype(BF16)


def _build_toeplitz(abb, cr, ci, d_row, w_s):
    def split(v):
        hi = v.astype(BF16)
        return hi, (v - hi.astype(F32)).astype(BF16)

    rhs_hi, rhs_lo = split(jnp.concatenate([cr, -ci], axis=1))
    rhs = jnp.concatenate([rhs_hi, rhs_lo, rhs_hi], axis=1)
    row = lax.broadcasted_iota(jnp.int32, (LANES, LANES), 0)
    col = lax.broadcasted_iota(jnp.int32, (LANES, LANES), 1)
    same_group = (row // GROUP) == (col // GROUP)
    taps = []
    for lag in range(CHUNK):
        lhs_hi, lhs_lo = split(jnp.concatenate(abb[lag], axis=1))
        k = lax.dot_general(jnp.concatenate([lhs_hi, lhs_hi, lhs_lo], axis=1), rhs,
                            (((1,), (1,)), ((), ())), preferred_element_type=F32)
        k = jnp.where(same_group, k, 0.0)
        if lag == 0:
            k = k + jnp.where(row == col, d_row, 0.0)
        taps.append(k.astype(BF16))
    zero = jnp.zeros((LANES, LANES), BF16)
    for d in range(N_TILES):
        w_s[d, :LANES, :LANES] = taps[2 * d]
        w_s[d, :LANES, LANES:] = taps[2 * d + 1]
        w_s[d, LANES:, :LANES] = taps[2 * d - 1] if d > 0 else zero
        w_s[d, LANES:, LANES:] = taps[2 * d]


def _ssm_kernel(u_ref, ar_ref, ai_ref, dt_ref, btr_ref, bti_ref, cr_ref, ci_ref, d_ref,
                dtare_ref, dtaim_ref, w0_ref, w1_ref, w2_ref, w3_ref, y_ref, c0_ref, c1_ref, c2_ref,
                c3_ref, xre_s, xim_s, p_s, q_s, w_s, *, batch, n_chunks):
    step = pl.program_id(0)
    for src, dst in ((w0_ref, c0_ref), (w1_ref, c1_ref), (w2_ref, c2_ref), (w3_ref, c3_ref)):
        dst[...] = src[...].astype(BF16)
    rows8 = batch * PAIRS
    slab_rows = n_chunks * rows8

    def state_rows(slab, b, k):
        return pl.ds(slab * slab_rows + b * PAIRS + k, n_chunks, stride=rows8)

    def powers():
        return _slab_powers(ar_ref[...], ai_ref[...], dt_ref[...], btr_ref[...], bti_ref[...])

    @pl.when(step < N_SLABS)
    def _():
        _, abb = powers()
        _build_state_in(abb, p_s)
        u = u_ref[...]
        for k in range(PAIRS):
            x = _dot(u, p_s[k])
            for b in range(batch):
                xb = x[b * n_chunks:(b + 1) * n_chunks]
                xre_s[state_rows(step, b, k), :] = xb[:, :LANES]
                xim_s[state_rows(step, b, k), :] = xb[:, LANES:]

    @pl.when(step == N_SLABS)
    def _():
        dta_re = jnp.concatenate([dtare_ref[...]] * batch, axis=1).reshape(N_SLABS * rows8, LANES)
        dta_im = jnp.concatenate([dtaim_ref[...]] * batch, axis=1).reshape(N_SLABS * rows8, LANES)
        mag = jnp.exp(CHUNK * dta_re)
        a_re, a_im = mag * jnp.cos(CHUNK * dta_im), mag * jnp.sin(CHUNK * dta_im)

        def body(c, carry):
            s_re, s_im = carry
            x_re, x_im = [], []
            for sl in range(N_SLABS):
                off = pl.multiple_of(sl * slab_rows + c * rows8, rows8)
                x_re.append(xre_s[pl.ds(off, rows8), :])
                x_im.append(xim_s[pl.ds(off, rows8), :])
                xre_s[pl.ds(off, rows8), :] = s_re[sl * rows8:(sl + 1) * rows8]
                xim_s[pl.ds(off, rows8), :] = s_im[sl * rows8:(sl + 1) * rows8]
            x_re = jnp.concatenate(x_re, axis=0)
            x_im = jnp.concatenate(x_im, axis=0)
            return (a_re * s_re - a_im * s_im + x_re, a_re * s_im + a_im * s_re + x_im)

        zero = jnp.zeros((N_SLABS * rows8, LANES), F32)
        lax.fori_loop(0, n_chunks, body, (zero, zero))

    @pl.when(step > N_SLABS)
    def _():
        slab = step - (N_SLABS + 1)
        pw, abb = powers()
        cr, ci = cr_ref[...], ci_ref[...]
        _build_state_out(pw, cr, ci, q_s)
        _build_toeplitz(abb, cr, ci, d_ref[...], w_s)
        u = u_ref[...]
        y_state = None
        for k in range(PAIRS):
            s_in = jnp.concatenate(
                [jnp.concatenate([xre_s[state_rows(slab, b, k), :], xim_s[state_rows(slab, b, k), :]],
                                 axis=1) for b in range(batch)], axis=0).astype(BF16)
            part = _dot(s_in, q_s[k])
            y_state = part if y_state is None else y_state + part
        for t2 in range(N_TILES):
            acc = y_state[:, t2 * MXU:(t2 + 1) * MXU]
            for t1 in range(t2 + 1):
                acc = acc + _dot(u[:, t1 * MXU:(t1 + 1) * MXU], w_s[t2 - t1])
            y_ref[:, t2 * MXU:(t2 + 1) * MXU] = jax.nn.gelu(acc).astype(BF16)


N_CAST = 16


def _ssm(u_chunks, A_re, A_im, log_dt, B_re, B_im, C_re, C_im, D_skip, batch, tail_weights):
    _, rows, _ = u_chunks.shape
    n_chunks = rows // batch
    n_steps = 2 * N_SLABS + 1
    kernel = functools.partial(_ssm_kernel, batch=batch, n_chunks=n_chunks)

    dt = jnp.exp(log_dt)[:, None]
    per_row = lambda v: jnp.repeat(v, GROUP, axis=0)
    ar, ai = per_row(A_re), per_row(A_im)
    dtr = per_row(jnp.broadcast_to(dt, A_re.shape))
    btr = B_re.transpose(0, 2, 1).reshape(SSM_W, STATE)
    bti = B_im.transpose(0, 2, 1).reshape(SSM_W, STATE)
    cr, ci = C_re.reshape(SSM_W, STATE), C_im.reshape(SSM_W, STATE)
    d_rows = D_skip.reshape(N_SLABS, 1, LANES)
    dta_re = (dt * A_re).reshape(N_SLABS, PAIRS, LANES)
    dta_im = (dt * A_im).reshape(N_SLABS, PAIRS, LANES)

    def slab_in(s):
        return jnp.where(s <= N_SLABS, jnp.minimum(s, N_SLABS - 1), s - (N_SLABS + 1))

    def slab_out(s):
        return jnp.maximum(s - (N_SLABS + 1), 0)

    def cast_spec(w):
        return pl.BlockSpec((w.shape[0] // N_CAST, w.shape[1]),
                            lambda s: (jnp.minimum(s, N_CAST - 1), 0))

    coeff = pl.BlockSpec((LANES, STATE), lambda s: (slab_in(s), 0))
    whole = pl.BlockSpec((N_SLABS, PAIRS, LANES), lambda s: (0, 0, 0))
    return pl.pallas_call(
        kernel,
        grid=(n_steps,),
        in_specs=[
            pl.BlockSpec((None, rows, CHUNK_W), lambda s: (slab_in(s), 0, 0)),
            coeff, coeff, coeff, coeff, coeff, coeff, coeff,
            pl.BlockSpec((None, 1, LANES), lambda s: (slab_in(s), 0, 0)),
            whole, whole,
        ] + [cast_spec(w) for w in tail_weights],
        out_specs=[pl.BlockSpec((None, rows, CHUNK_W), lambda s: (slab_out(s), 0, 0))]
        + [cast_spec(w) for w in tail_weights],
        out_shape=[jax.ShapeDtypeStruct(u_chunks.shape, BF16)]
        + [jax.ShapeDtypeStruct(w.shape, BF16) for w in tail_weights],
        scratch_shapes=[
            pltpu.VMEM((N_SLABS * n_chunks * batch * PAIRS, LANES), F32),
            pltpu.VMEM((N_SLABS * n_chunks * batch * PAIRS, LANES), F32),
            pltpu.VMEM((PAIRS, CHUNK_W, 2 * LANES), BF16),
            pltpu.VMEM((PAIRS, 2 * LANES, CHUNK_W), BF16),
            pltpu.VMEM((N_TILES, MXU, MXU), BF16),
        ],
        compiler_params=pltpu.CompilerParams(
            dimension_semantics=("arbitrary",), vmem_limit_bytes=VMEM_LIMIT),
        name="ssm",
    )(u_chunks, ar, ai, dtr, btr, bti, cr, ci, d_rows, dta_re, dta_im, *tail_weights)


def _tail_kernel(ag_ref, yc_ref, z_ref, ga_ref, gs_ref, x_ref, wa_ref, wg_ref, bg_ref, ws_ref,
                 wo_ref, o_ref, ys_ref, *, tm):
    rows = tm // CHUNK
    for s in range(N_SLABS):
        for t in range(CHUNK):
            ys_ref[pl.ds(s * rows * US_PITCH + t, rows, stride=US_PITCH), :] = (
                yc_ref[s, :, t * LANES:(t + 1) * LANES].astype(F32))
    y_g = jnp.concatenate(
        [jnp.concatenate([ys_ref[(s * rows + c) * US_PITCH:(s * rows + c) * US_PITCH + CHUNK, :]
                          for c in range(rows)], axis=0) for s in range(N_SLABS)],
        axis=1).astype(BF16)

    y_a = _dot(ag_ref[...], wa_ref[...])
    glu = _dot(y_g, wg_ref[...]) + bg_ref[...]
    z = z_ref[...].astype(F32)
    t = glu[:, :SSM_W] * jax.nn.sigmoid(glu[:, SSM_W:]) * (z * jax.nn.sigmoid(z))
    y_s = _dot(t.astype(BF16), ws_ref[...])
    merged = (jax.nn.sigmoid(ga_ref[...].astype(F32)) * y_a
              + jax.nn.sigmoid(gs_ref[...].astype(F32)) * y_s)
    o_ref[...] = x_ref[...] + _dot(merged.astype(BF16), wo_ref[...])


def _tail(attn_g, y_chunks, proj, x2, w_attn, w_glu, b_glu, w_ssm, w_out, tm=256):
    m = x2.shape[0]
    row = lambda blk: (lambda i: (i, blk))
    const = lambda i: (0, 0)
    resident = functools.partial(pl.BlockSpec, index_map=const, pipeline_mode=pl.Buffered(1))
    kernel = functools.partial(_tail_kernel, tm=tm)
    return pl.pallas_call(
        kernel,
        grid=(m // tm,),
        in_specs=[
            pl.BlockSpec((tm, ATTN_W), row(0)),
            pl.BlockSpec((N_SLABS, tm // CHUNK, CHUNK_W), lambda i: (0, i, 0)),
            pl.BlockSpec((tm, SSM_W), row(COL_Z // SSM_W)),
            pl.BlockSpec((tm, D_MODEL), row(COL_GA // D_MODEL)),
            pl.BlockSpec((tm, D_MODEL), row(COL_GS // D_MODEL)),
            pl.BlockSpec((tm, D_MODEL), row(0)),
            resident((ATTN_W, D_MODEL)),
            resident((SSM_W, 2 * SSM_W)),
            resident((1, 2 * SSM_W)),
            resident((SSM_W, D_MODEL)),
            resident((D_MODEL, D_MODEL)),
        ],
        out_specs=pl.BlockSpec((tm, D_MODEL), row(0)),
        out_shape=jax.ShapeDtypeStruct((m, D_MODEL), F32),
        scratch_shapes=[pltpu.VMEM((N_SLABS * (tm // CHUNK) * US_PITCH, LANES), F32)],
        compiler_params=pltpu.CompilerParams(
            dimension_semantics=("arbitrary",), vmem_limit_bytes=VMEM_LIMIT),
        name="tail",
    )(attn_g, y_chunks, proj, proj, proj, x2, w_attn, w_glu, b_glu.reshape(1, -1), w_ssm, w_out)


def kernel(x, norm_w, w_in, q_norm_w, k_norm_w, sinks, w_attn_proj, A_re, A_im, log_dt, B_re, B_im,
           C_re, C_im, D_skip, w_glu, b_glu, w_ssm_proj, w_out):
    batch, seq, _ = x.shape
    m = batch * seq
    x2 = x.reshape(m, D_MODEL)

    proj, u_chunks = _in_proj(x2, norm_w, w_in)

    attn_g = _swa(proj, sinks, q_norm_w, k_norm_w, batch, seq)

    y_chunks, w_attn_bf, w_glu_bf, w_ssm_bf, w_out_bf = _ssm(
        u_chunks, A_re, A_im, log_dt, B_re, B_im, C_re, C_im, D_skip, batch,
        (w_attn_proj, w_glu, w_ssm_proj, w_out))

    out = _tail(attn_g, y_chunks, proj, x2, w_attn_bf, w_glu_bf, b_glu, w_ssm_bf, w_out_bf)
    return out.reshape(batch, seq, D_MODEL)
```

```python
import functools
import math

import jax
import jax.numpy as jnp
from jax import lax
from jax.experimental import pallas as pl
from jax.experimental.pallas import tpu as pltpu

F32 = jnp.float32
BF16 = jnp.bfloat16

D_MODEL = 2048
HEAD_DIM = 64
N_Q_HEADS = 16
N_KV_HEADS = 4
Q_PER_KV = 4
ATTN_W = N_Q_HEADS * HEAD_DIM
KV_W = N_KV_HEADS * HEAD_DIM
WINDOW = 128
SSM_W = D_MODEL // 2
GROUP = 16
N_GROUPS = SSM_W // GROUP
STATE = 64
NORM_EPS = 1e-6

COL_Q = 0
COL_GATE = COL_Q + ATTN_W
COL_U = COL_GATE + ATTN_W
COL_Z = COL_U + SSM_W
COL_GA = COL_Z + SSM_W
COL_GS = COL_GA + D_MODEL
COL_K = COL_GS + D_MODEL
COL_V = COL_K + KV_W
IN_W = COL_V + KV_W

LANES = 128
MXU = 256
CHUNK = 16
SLAB_G = LANES // GROUP
N_SLABS = N_GROUPS // SLAB_G
PAIRS = SLAB_G // 2
CHUNK_W = CHUNK * LANES
N_TILES = CHUNK_W // MXU
VMEM_LIMIT = 48 * 1024 * 1024
BIG_VMEM_LIMIT = 56 * 1024 * 1024


def _dot(a, b):
    return jnp.dot(a, b, preferred_element_type=F32)


IN_TN = 512
N_IN_TILES = IN_W // IN_TN
SRC_KV_TILE = (ATTN_W) // IN_TN
SRC_U_TILE0 = (ATTN_W + 2 * KV_W + ATTN_W) // IN_TN


US_PITCH = CHUNK if (CHUNK // 8) % 2 else CHUNK + 8
IN_PER_ROUND = 2
IN_RING = 2 * IN_PER_ROUND


def _dst_tile(src_tile):
    if src_tile < SRC_KV_TILE:
        return src_tile
    if src_tile == SRC_KV_TILE:
        return N_IN_TILES - 1
    return src_tile - 1


def _in_proj_kernel(x_ref, nw_ref, w_hbm, o_hbm, uc_ref, h_ref, w_buf, o_buf, us_ref, w_sem, o_sem,
                    *, tm):
    i = pl.program_id(0)
    rows = tm // CHUNK
    n_sl = IN_TN // LANES

    def w_copy(t):
        return pltpu.make_async_copy(w_hbm.at[:, pl.ds(t * IN_TN, IN_TN)], w_buf.at[t % IN_RING],
                                     w_sem.at[t % IN_RING])

    def o_copy(t):
        dst = o_hbm.at[pl.ds(pl.multiple_of(i * tm, tm), tm), pl.ds(_dst_tile(t) * IN_TN, IN_TN)]
        return pltpu.make_async_copy(o_buf.at[t % IN_RING], dst, o_sem.at[t % IN_RING])

    def round_tiles(r):
        return [t for t in range(r * IN_PER_ROUND, (r + 1) * IN_PER_ROUND) if 0 <= t < N_IN_TILES]

    for t in round_tiles(0):
        w_copy(t).start()
    x = x_ref[...]
    ms = jnp.mean(x * x, axis=-1, keepdims=True)
    h_ref[...] = (x * lax.rsqrt(ms + NORM_EPS) * nw_ref[...]).astype(BF16)

    def finish(t, acc):
        o_buf[t % IN_RING] = acc.astype(BF16)
        if SRC_U_TILE0 <= t < SRC_U_TILE0 + SSM_W // IN_TN:
            slab0 = (t - SRC_U_TILE0) * n_sl
            for s in range(n_sl):
                for c in range(rows):
                    r0 = (s * rows + c) * US_PITCH
                    us_ref[r0:r0 + CHUNK, :] = acc[c * CHUNK:(c + 1) * CHUNK, s * LANES:(s + 1) * LANES]
            for s in range(n_sl):
                for tok in range(CHUNK):
                    piece = us_ref[pl.ds(s * rows * US_PITCH + tok, rows, stride=US_PITCH), :]
                    uc_ref[slab0 + s, :, tok * LANES:(tok + 1) * LANES] = piece.astype(BF16)

    n_rounds = pl.cdiv(N_IN_TILES, IN_PER_ROUND)
    for r in range(n_rounds):
        for t in round_tiles(r + 1):
            w_copy(t).start()
        for t in round_tiles(r):
            w_copy(t).wait()
        for t in round_tiles(r - 2):
            o_copy(t).wait()
        for t in round_tiles(r - 1):
            o_copy(t).start()
        for t in round_tiles(r):
            acc = None
            for k in range(D_MODEL // MXU):
                ks = slice(k * MXU, (k + 1) * MXU)
                part = _dot(h_ref[:, ks], w_buf[t % IN_RING, ks, :].astype(BF16))
                acc = part if acc is None else acc + part
            finish(t, acc)

    for t in round_tiles(n_rounds - 1):
        o_copy(t).start()
    for t in round_tiles(n_rounds - 2) + round_tiles(n_rounds - 1):
        o_copy(t).wait()


def _in_proj(x2, norm_w, w_in, tm=1024):
    m = x2.shape[0]
    kernel = functools.partial(_in_proj_kernel, tm=tm)
    return pl.pallas_call(
        kernel,
        grid=(m // tm,),
        in_specs=[
            pl.BlockSpec((tm, D_MODEL), lambda i: (i, 0)),
            pl.BlockSpec((1, D_MODEL), lambda i: (0, 0)),
            pl.BlockSpec(memory_space=pl.ANY),
        ],
        out_specs=[
            pl.BlockSpec(memory_space=pl.ANY),
            pl.BlockSpec((N_SLABS, tm // CHUNK, CHUNK_W), lambda i: (0, i, 0)),
        ],
        out_shape=[
            jax.ShapeDtypeStruct((m, IN_W), BF16),
            jax.ShapeDtypeStruct((N_SLABS, m // CHUNK, CHUNK_W), BF16),
        ],
        scratch_shapes=[
            pltpu.VMEM((tm, D_MODEL), BF16),
            pltpu.VMEM((IN_RING, D_MODEL, IN_TN), F32),
            pltpu.VMEM((IN_RING, tm, IN_TN), BF16),
            pltpu.VMEM((IN_TN // LANES * (tm // CHUNK) * US_PITCH, LANES), F32),
            pltpu.SemaphoreType.DMA((IN_RING,)),
            pltpu.SemaphoreType.DMA((IN_RING,)),
        ],
        compiler_params=pltpu.CompilerParams(
            dimension_semantics=("arbitrary",), vmem_limit_bytes=BIG_VMEM_LIMIT),
        name="in_proj",
    )(x2, norm_w.reshape(1, D_MODEL), w_in)


def _head_norm(t, w):
    ms = jnp.mean(t * t, axis=-1, keepdims=True)
    return t * lax.rsqrt(ms + NORM_EPS) * w


_NT = (((1,), (1,)), ((), ()))
_TN = (((0,), (0,)), ((), ()))


SWA_TQ = 1024
SWA_SUB = SWA_TQ // WINDOW


def _swa_tile(sink_ref, q_ref, g_ref, kc_ref, vc_ref, kp_ref, vp_ref, qw_ref, kw_ref, o_ref,
              first_tile):
    log2e = math.log2(math.e)
    kqw = kw_ref[...] * qw_ref[...] * (log2e / math.sqrt(HEAD_DIM))
    n_col = Q_PER_KV * WINDOW
    key = lax.broadcasted_iota(jnp.int32, (WINDOW, n_col), 0)
    qry = lax.broadcasted_iota(jnp.int32, (WINDOW, n_col), 1) % WINDOW
    from_prev = key > qry
    no_prev = jnp.where(first_tile, -1e30, 0.0)
    head_of_col = lax.broadcasted_iota(jnp.int32, (1, n_col), 1) // WINDOW
    gw = Q_PER_KV * HEAD_DIM
    sel_r = lax.broadcasted_iota(jnp.int32, (8, 2 * gw), 0)
    sel_l = (lax.broadcasted_iota(jnp.int32, (8, 2 * gw), 1) % gw) // HEAD_DIM
    head_sel = jnp.where(sel_r == sel_l, 1.0, 0.0).astype(BF16)

    parts = []
    carry = {}

    def block(g, n):
        kcol = slice(g * HEAD_DIM, (g + 1) * HEAD_DIM)
        gcols = slice(g * Q_PER_KV * HEAD_DIM, (g + 1) * Q_PER_KV * HEAD_DIM)
        if n == 0:
            sink = jnp.zeros((1, n_col), F32)
            for r in range(Q_PER_KV):
                sink = jnp.where(head_of_col == r, sink_ref[g * Q_PER_KV + r] * log2e, sink)
            carry['sink'] = sink
            carry['k'] = _head_norm(kp_ref[:, kcol].astype(F32), kqw).astype(BF16)
            carry['v'] = vp_ref[:, kcol]
        sink, k_prev, v_prev = carry['sink'], carry['k'], carry['v']
        rows = slice(n * WINDOW, (n + 1) * WINDOW)
        k_cur = _head_norm(kc_ref[rows, kcol].astype(F32), kqw).astype(BF16)
        v_cur = vc_ref[rows, kcol]
        k_ctx = jnp.concatenate([k_prev, k_cur], axis=0)
        v_ctx = jnp.concatenate([v_prev, v_cur], axis=0)

        qg = q_ref[rows, gcols]
        qf = qg.astype(F32)
        q_t = qf.T.astype(BF16)
        q_t = jnp.concatenate([q_t[r * HEAD_DIM:(r + 1) * HEAD_DIM] for r in range(Q_PER_KV)],
                              axis=1)
        q2 = qf * qf
        q2_hi = q2.astype(BF16)
        q2_lo = (q2 - q2_hi.astype(F32)).astype(BF16)
        ssq = lax.dot_general(head_sel, jnp.concatenate([q2_hi, q2_lo], axis=1), _NT,
                              preferred_element_type=F32)
        rms = lax.rsqrt(ssq * (1.0 / HEAD_DIM) + NORM_EPS)
        rms_q = jnp.concatenate([rms[r:r + 1] for r in range(Q_PER_KV)], axis=1)

        s = _dot(k_ctx, q_t)
        s_prev = s[:WINDOW] + no_prev if n == 0 else s[:WINDOW]
        s = jnp.where(from_prev, s_prev, s[WINDOW:]) * rms_q
        mx = jnp.maximum(jnp.max(s, axis=0, keepdims=True), sink)
        p = jnp.exp2(s - mx)
        den = jnp.sum(p, axis=0, keepdims=True) + jnp.exp2(sink - mx)
        p_ctx = jnp.concatenate([jnp.where(from_prev, p, 0.0), jnp.where(from_prev, 0.0, p)],
                                axis=0).astype(BF16)
        o_t = lax.dot_general(v_ctx, p_ctx, _TN, preferred_element_type=F32) * (1.0 / den)
        halves = [jnp.concatenate([o_t[:, (2 * h) * WINDOW:(2 * h + 1) * WINDOW],
                                   o_t[:, (2 * h + 1) * WINDOW:(2 * h + 2) * WINDOW]], axis=0).T
                  for h in range(Q_PER_KV // 2)]
        og = jnp.concatenate(halves, axis=1)
        gate = g_ref[rows, gcols].astype(F32)
        o_ref[rows, gcols] = (og * (gate * jax.nn.sigmoid(gate))).astype(BF16)
        carry['k'], carry['v'] = k_cur, v_cur

    for g in range(N_KV_HEADS):
        for n in range(SWA_SUB):
            parts.append(functools.partial(block, g, n))
    return parts


def _swa_kernel(*refs, tiles_per_seq):
    for block in _swa_tile(*refs, pl.program_id(0) % tiles_per_seq == 0):
        block()


def _swa(proj, sinks, q_norm_w, k_norm_w, batch, seq):
    m = batch * seq
    cur = lambda col: (lambda s: (s, col))
    prev = lambda col: (lambda s: (jnp.maximum(s * SWA_SUB - 1, 0), col))
    kernel = functools.partial(_swa_kernel, tiles_per_seq=seq // SWA_TQ)
    return pl.pallas_call(
        kernel,
        grid=(m // SWA_TQ,),
        in_specs=[
            pl.BlockSpec(memory_space=pltpu.SMEM),
            pl.BlockSpec((SWA_TQ, ATTN_W), cur(COL_Q // ATTN_W)),
            pl.BlockSpec((SWA_TQ, ATTN_W), cur(COL_GATE // ATTN_W)),
            pl.BlockSpec((SWA_TQ, KV_W), cur(COL_K // KV_W)),
            pl.BlockSpec((SWA_TQ, KV_W), cur(COL_V // KV_W)),
            pl.BlockSpec((WINDOW, KV_W), prev(COL_K // KV_W)),
            pl.BlockSpec((WINDOW, KV_W), prev(COL_V // KV_W)),
            pl.BlockSpec((1, HEAD_DIM), lambda s: (0, 0)),
            pl.BlockSpec((1, HEAD_DIM), lambda s: (0, 0)),
        ],
        out_specs=pl.BlockSpec((SWA_TQ, ATTN_W), cur(0)),
        out_shape=jax.ShapeDtypeStruct((m, ATTN_W), BF16),
        compiler_params=pltpu.CompilerParams(
            dimension_semantics=("arbitrary",), vmem_limit_bytes=VMEM_LIMIT),
        name="swa",
    )(sinks, proj, proj, proj, proj, proj, proj,
      q_norm_w.reshape(1, HEAD_DIM), k_norm_w.reshape(1, HEAD_DIM))


def _slab_powers(ar, ai, dt, btr, bti):
    dta_re, dta_im = dt * ar, dt * ai
    mag = jnp.exp(dta_re)
    ab_re, ab_im = mag * jnp.cos(dta_im), mag * jnp.sin(dta_im)
    pw = [(jnp.ones_like(ar), jnp.zeros_like(ar))]
    for _ in range(CHUNK):
        pr, pi = pw[-1]
        pw.append((pr * ab_re - pi * ab_im, pr * ab_im + pi * ab_re))
    den = ar * ar + ai * ai
    num_re, num_im = ab_re - 1.0, ab_im
    cf_re = (num_re * ar + num_im * ai) / den
    cf_im = (num_im * ar - num_re * ai) / den
    bb_re = cf_re * btr - cf_im * bti
    bb_im = cf_re * bti + cf_im * btr
    abb = [(pr * bb_re - pi * bb_im, pr * bb_im + pi * bb_re) for pr, pi in pw[:CHUNK]]
    return pw, abb


def _build_state_in(abb, p_s):
    row_g = lax.broadcasted_iota(jnp.int32, (LANES, STATE), 0) // GROUP
    even = row_g % 2 == 0
    pair_of_row = lax.broadcasted_iota(jnp.int32, (LANES, 2 * LANES), 0) // (2 * GROUP)
    for lag in range(CHUNK):
        t = CHUNK - 1 - lag
        re, im = abb[lag]
        blk = jnp.concatenate([jnp.where(even, re, 0.0), jnp.where(even, 0.0, re),
                               jnp.where(even, im, 0.0), jnp.where(even, 0.0, im)], axis=1)
        for k in range(PAIRS):
            p_s[k, t * LANES:(t + 1) * LANES, :] = jnp.where(pair_of_row == k, blk, 0.0).astype(BF16)


def _build_state_out(pw, cr, ci, q_s):
    lane_g = lax.broadcasted_iota(jnp.int32, (STATE, LANES), 1) // GROUP
    for t in range(CHUNK):
        pr, pi = pw[t + 1]
        cat = jnp.concatenate([cr * pr - ci * pi, -(cr * pi + ci * pr)], axis=1)
        cat_t = cat.T
        for k in range(PAIRS):
            for ri in range(2):
                for half in range(2):
                    piece = jnp.where(lane_g == 2 * k + half, cat_t[ri * STATE:(ri + 1) * STATE], 0.0)
                    r0 = ri * LANES + half * STATE
                    q_s[k, r0:r0 + STATE, t * LANES:(t + 1) * LANES] = piece.astype(BF16)


def _build_toeplitz(abb, cr, ci, d_row, w_s):
    def split(v):
        hi = v.astype(BF16)
        return hi, (v - hi.astype(F32)).astype(BF16)

    rhs_hi, rhs_lo = split(jnp.concatenate([cr, -ci], axis=1))
    rhs = jnp.concatenate([rhs_hi, rhs_lo, rhs_hi], axis=1)
    row = lax.broadcasted_iota(jnp.int32, (LANES, LANES), 0)
    col = lax.broadcasted_iota(jnp.int32, (LANES, LANES), 1)
    same_group = (row // GROUP) == (col // GROUP)
    taps = []
    for lag in range(CHUNK):
        lhs_hi, lhs_lo = split(jnp.concatenate(abb[lag], axis=1))
        k = lax.dot_general(jnp.concatenate([lhs_hi, lhs_hi, lhs_lo], axis=1), rhs,
                            (((1,), (1,)), ((), ())), preferred_element_type=F32)
        k = jnp.where(same_group, k, 0.0)
        if lag == 0:
            k = k + jnp.where(row == col, d_row, 0.0)
        taps.append(k.astype(BF16))
    zero = jnp.zeros((LANES, LANES), BF16)
    for d in range(N_TILES):
        w_s[d, :LANES, :LANES] = taps[2 * d]
        w_s[d, :LANES, LANES:] = taps[2 * d + 1]
        w_s[d, LANES:, :LANES] = taps[2 * d - 1] if d > 0 else zero
        w_s[d, LANES:, LANES:] = taps[2 * d]


def _ssm_kernel(u_ref, ar_ref, ai_ref, dt_ref, btr_ref, bti_ref, cr_ref, ci_ref, d_ref,
                dtare_ref, dtaim_ref, w0_ref, w1_ref, w2_ref, w3_ref, y_ref, c0_ref, c1_ref, c2_ref,
                c3_ref, xre_s, xim_s, p_s, q_s, w_s, *, batch, n_chunks):
    step = pl.program_id(0)
    for src, dst in ((w0_ref, c0_ref), (w1_ref, c1_ref), (w2_ref, c2_ref), (w3_ref, c3_ref)):
        dst[...] = src[...].astype(BF16)
    rows8 = batch * PAIRS
    slab_rows = n_chunks * rows8

    def state_rows(slab, b, k):
        return pl.ds(slab * slab_rows + b * PAIRS + k, n_chunks, stride=rows8)

    def per_row(v):
        v = jnp.broadcast_to(v, (SLAB_G, STATE))
        return jnp.broadcast_to(v[:, None, :], (SLAB_G, GROUP, STATE)).reshape(LANES, STATE)

    def powers():
        return _slab_powers(per_row(ar_ref[...]), per_row(ai_ref[...]),
                            per_row(jnp.exp(dt_ref[...])), btr_ref[...], bti_ref[...])

    @pl.when(step < N_SLABS)
    def _():
        _, abb = powers()
        _build_state_in(abb, p_s)
        u = u_ref[...]
        for k in range(PAIRS):
            x = _dot(u, p_s[k])
            for b in range(batch):
                xb = x[b * n_chunks:(b + 1) * n_chunks]
                xre_s[state_rows(step, b, k), :] = xb[:, :LANES]
                xim_s[state_rows(step, b, k), :] = xb[:, LANES:]

    @pl.when(step == N_SLABS)
    def _():
        dta_re = jnp.concatenate([dtare_ref[...]] * batch, axis=1).reshape(N_SLABS * rows8, LANES)
        dta_im = jnp.concatenate([dtaim_ref[...]] * batch, axis=1).reshape(N_SLABS * rows8, LANES)
        mag = jnp.exp(CHUNK * dta_re)
        a_re, a_im = mag * jnp.cos(CHUNK * dta_im), mag * jnp.sin(CHUNK * dta_im)

        def body(c, carry):
            s_re, s_im = carry
            x_re, x_im = [], []
            for sl in range(N_SLABS):
                off = pl.multiple_of(sl * slab_rows + c * rows8, rows8)
                x_re.append(xre_s[pl.ds(off, rows8), :])
                x_im.append(xim_s[pl.ds(off, rows8), :])
                xre_s[pl.ds(off, rows8), :] = s_re[sl * rows8:(sl + 1) * rows8]
                xim_s[pl.ds(off, rows8), :] = s_im[sl * rows8:(sl + 1) * rows8]
            x_re = jnp.concatenate(x_re, axis=0)
            x_im = jnp.concatenate(x_im, axis=0)
            return (a_re * s_re - a_im * s_im + x_re, a_re * s_im + a_im * s_re + x_im)

        zero = jnp.zeros((N_SLABS * rows8, LANES), F32)
        lax.fori_loop(0, n_chunks, body, (zero, zero))

    @pl.when(step > N_SLABS)
    def _():
        slab = step - (N_SLABS + 1)
        pw, abb = powers()
        cr, ci = cr_ref[...], ci_ref[...]
        _build_state_out(pw, cr, ci, q_s)
        _build_toeplitz(abb, cr, ci, d_ref[...], w_s)
        u = u_ref[...]
        y_state = None
        for k in range(PAIRS):
            s_in = jnp.concatenate(
                [jnp.concatenate([xre_s[state_rows(slab, b, k), :], xim_s[state_rows(slab, b, k), :]],
                                 axis=1) for b in range(batch)], axis=0).astype(BF16)
            part = _dot(s_in, q_s[k])
            y_state = part if y_state is None else y_state + part
        for t2 in range(N_TILES):
            acc = y_state[:, t2 * MXU:(t2 + 1) * MXU]
            for t1 in range(t2 + 1):
                acc = acc + _dot(u[:, t1 * MXU:(t1 + 1) * MXU], w_s[t2 - t1])
            y_ref[:, t2 * MXU:(t2 + 1) * MXU] = jax.nn.gelu(acc).astype(BF16)


N_CAST = 16


def _ssm(u_chunks, A_re, A_im, log_dt, B_re, B_im, C_re, C_im, D_skip, batch, tail_weights):
    _, rows, _ = u_chunks.shape
    n_chunks = rows // batch
    n_steps = 2 * N_SLABS + 1
    kernel = functools.partial(_ssm_kernel, batch=batch, n_chunks=n_chunks)

    dt = jnp.exp(log_dt)[:, None]
    btr = B_re.transpose(0, 2, 1).reshape(SSM_W, STATE)
    bti = B_im.transpose(0, 2, 1).reshape(SSM_W, STATE)
    cr, ci = C_re.reshape(SSM_W, STATE), C_im.reshape(SSM_W, STATE)
    d_rows = D_skip.reshape(N_SLABS, 1, LANES)
    dta_re = (dt * A_re).reshape(N_SLABS, PAIRS, LANES)
    dta_im = (dt * A_im).reshape(N_SLABS, PAIRS, LANES)

    def slab_in(s):
        return jnp.where(s <= N_SLABS, jnp.minimum(s, N_SLABS - 1), s - (N_SLABS + 1))

    def slab_out(s):
        return jnp.maximum(s - (N_SLABS + 1), 0)

    def cast_spec(w):
        return pl.BlockSpec((w.shape[0] // N_CAST, w.shape[1]),
                            lambda s: (jnp.minimum(s, N_CAST - 1), 0))

    coeff = pl.BlockSpec((LANES, STATE), lambda s: (slab_in(s), 0))
    whole = pl.BlockSpec((N_SLABS, PAIRS, LANES), lambda s: (0, 0, 0))
    return pl.pallas_call(
        kernel,
        grid=(n_steps,),
        in_specs=[
            pl.BlockSpec((None, rows, CHUNK_W), lambda s: (slab_in(s), 0, 0)),
            pl.BlockSpec((SLAB_G, STATE), lambda s: (slab_in(s), 0)),
            pl.BlockSpec((SLAB_G, STATE), lambda s: (slab_in(s), 0)),
            pl.BlockSpec((SLAB_G, 1), lambda s: (slab_in(s), 0)),
            coeff, coeff, coeff, coeff,
            pl.BlockSpec((None, 1, LANES), lambda s: (slab_in(s), 0, 0)),
            whole, whole,
        ] + [cast_spec(w) for w in tail_weights],
        out_specs=[pl.BlockSpec((None, rows, CHUNK_W), lambda s: (slab_out(s), 0, 0))]
        + [cast_spec(w) for w in tail_weights],
        out_shape=[jax.ShapeDtypeStruct(u_chunks.shape, BF16)]
        + [jax.ShapeDtypeStruct(w.shape, BF16) for w in tail_weights],
        scratch_shapes=[
            pltpu.VMEM((N_SLABS * n_chunks * batch * PAIRS, LANES), F32),
            pltpu.VMEM((N_SLABS * n_chunks * batch * PAIRS, LANES), F32),
            pltpu.VMEM((PAIRS, CHUNK_W, 2 * LANES), BF16),
            pltpu.VMEM((PAIRS, 2 * LANES, CHUNK_W), BF16),
            pltpu.VMEM((N_TILES, MXU, MXU), BF16),
        ],
        compiler_params=pltpu.CompilerParams(
            dimension_semantics=("arbitrary",), vmem_limit_bytes=VMEM_LIMIT),
        name="ssm",
    )(u_chunks, A_re, A_im, log_dt[:, None], btr, bti, cr, ci, d_rows, dta_re, dta_im,
      *tail_weights)


def _tail_kernel(ag_ref, yc_ref, z_ref, ga_ref, gs_ref, x_ref, wa_ref, wg_ref, bg_ref, ws_ref,
                 wo_ref, o_ref, ys_ref, *, tm):
    rows = tm // CHUNK
    for s in range(N_SLABS):
        for t in range(CHUNK):
            ys_ref[pl.ds(s * rows * US_PITCH + t, rows, stride=US_PITCH), :] = (
                yc_ref[s, :, t * LANES:(t + 1) * LANES].astype(F32))
    y_g = jnp.concatenate(
        [jnp.concatenate([ys_ref[(s * rows + c) * US_PITCH:(s * rows + c) * US_PITCH + CHUNK, :]
                          for c in range(rows)], axis=0) for s in range(N_SLABS)],
        axis=1).astype(BF16)

    y_a = _dot(ag_ref[...], wa_ref[...])
    glu = _dot(y_g, wg_ref[...]) + bg_ref[...]
    z = z_ref[...].astype(F32)
    t = glu[:, :SSM_W] * jax.nn.sigmoid(glu[:, SSM_W:]) * (z * jax.nn.sigmoid(z))
    y_s = _dot(t.astype(BF16), ws_ref[...])
    merged = (jax.nn.sigmoid(ga_ref[...].astype(F32)) * y_a
              + jax.nn.sigmoid(gs_ref[...].astype(F32)) * y_s)
    o_ref[...] = x_ref[...] + _dot(merged.astype(BF16), wo_ref[...])


def _tail(attn_g, y_chunks, proj, x2, w_attn, w_glu, b_glu, w_ssm, w_out, tm=256):
    m = x2.shape[0]
    row = lambda blk: (lambda i: (i, blk))
    const = lambda i: (0, 0)
    resident = functools.partial(pl.BlockSpec, index_map=const, pipeline_mode=pl.Buffered(1))
    kernel = functools.partial(_tail_kernel, tm=tm)
    return pl.pallas_call(
        kernel,
        grid=(m // tm,),
        in_specs=[
            pl.BlockSpec((tm, ATTN_W), row(0)),
            pl.BlockSpec((N_SLABS, tm // CHUNK, CHUNK_W), lambda i: (0, i, 0)),
            pl.BlockSpec((tm, SSM_W), row(COL_Z // SSM_W)),
            pl.BlockSpec((tm, D_MODEL), row(COL_GA // D_MODEL)),
            pl.BlockSpec((tm, D_MODEL), row(COL_GS // D_MODEL)),
            pl.BlockSpec((tm, D_MODEL), row(0)),
            resident((ATTN_W, D_MODEL)),
            resident((SSM_W, 2 * SSM_W)),
            resident((1, 2 * SSM_W)),
            resident((SSM_W, D_MODEL)),
            resident((D_MODEL, D_MODEL)),
        ],
        out_specs=pl.BlockSpec((tm, D_MODEL), row(0)),
        out_shape=jax.ShapeDtypeStruct((m, D_MODEL), F32),
        scratch_shapes=[pltpu.VMEM((N_SLABS * (tm // CHUNK) * US_PITCH, LANES), F32)],
        compiler_params=pltpu.CompilerParams(
            dimension_semantics=("arbitrary",), vmem_limit_bytes=VMEM_LIMIT),
        name="tail",
    )(attn_g, y_chunks, proj, proj, proj, x2, w_attn, w_glu, b_glu.reshape(1, -1), w_ssm, w_out)


def kernel(x, norm_w, w_in, q_norm_w, k_norm_w, sinks, w_attn_proj, A_re, A_im, log_dt, B_re, B_im,
           C_re, C_im, D_skip, w_glu, b_glu, w_ssm_proj, w_out):
    batch, seq, _ = x.shape
    m = batch * seq
    x2 = x.reshape(m, D_MODEL)

    proj, u_chunks = _in_proj(x2, norm_w, w_in)

    attn_g = _swa(proj, sinks, q_norm_w, k_norm_w, batch, seq)

    y_chunks, w_attn_bf, w_glu_bf, w_ssm_bf, w_out_bf = _ssm(
        u_chunks, A_re, A_im, log_dt, B_re, B_im, C_re, C_im, D_skip, batch,
        (w_attn_proj, w_glu, w_ssm_proj, w_out))

    out = _tail(attn_g, y_chunks, proj, x2, w_attn_bf, w_glu_bf, b_glu, w_ssm_bf, w_out_bf)
    return out.reshape(batch, seq, D_MODEL)
```

```python
import functools
import math

import jax
import jax.numpy as jnp
from jax import lax
from jax.experimental import pallas as pl
from jax.experimental.pallas import tpu as pltpu

F32 = jnp.float32
BF16 = jnp.bfloat16

D_MODEL = 2048
HEAD_DIM = 64
N_Q_HEADS = 16
N_KV_HEADS = 4
Q_PER_KV = 4
ATTN_W = N_Q_HEADS * HEAD_DIM
KV_W = N_KV_HEADS * HEAD_DIM
WINDOW = 128
SSM_W = D_MODEL // 2
GROUP = 16
N_GROUPS = SSM_W // GROUP
STATE = 64
NORM_EPS = 1e-6

COL_Q = 0
COL_GATE = COL_Q + ATTN_W
COL_U = COL_GATE + ATTN_W
COL_Z = COL_U + SSM_W
COL_GA = COL_Z + SSM_W
COL_GS = COL_GA + D_MODEL
COL_K = COL_GS + D_MODEL
COL_V = COL_K + KV_W
IN_W = COL_V + KV_W

LANES = 128
MXU = 256
CHUNK = 16
SLAB_G = LANES // GROUP
N_SLABS = N_GROUPS // SLAB_G
PAIRS = SLAB_G // 2
CHUNK_W = CHUNK * LANES
N_TILES = CHUNK_W // MXU
VMEM_LIMIT = 48 * 1024 * 1024
BIG_VMEM_LIMIT = 56 * 1024 * 1024


def _dot(a, b):
    return jnp.dot(a, b, preferred_element_type=F32)


IN_TN = 512
N_IN_TILES = IN_W // IN_TN
SRC_KV_TILE = (ATTN_W) // IN_TN
SRC_U_TILE0 = (ATTN_W + 2 * KV_W + ATTN_W) // IN_TN


US_PITCH = CHUNK if (CHUNK // 8) % 2 else CHUNK + 8
IN_PER_ROUND = 2
IN_RING = 2 * IN_PER_ROUND


def _dst_tile(src_tile):
    if src_tile < SRC_KV_TILE:
        return src_tile
    if src_tile == SRC_KV_TILE:
        return N_IN_TILES - 1
    return src_tile - 1


def _in_proj_kernel(x_ref, nw_ref, w_hbm, o_hbm, uc_ref, h_ref, w_buf, o_buf, us_ref, w_sem, o_sem,
                    *, tm):
    i = pl.program_id(0)
    rows = tm // CHUNK
    n_sl = IN_TN // LANES

    def w_copy(t):
        return pltpu.make_async_copy(w_hbm.at[:, pl.ds(t * IN_TN, IN_TN)], w_buf.at[t % IN_RING],
                                     w_sem.at[t % IN_RING])

    def o_copy(t):
        dst = o_hbm.at[pl.ds(pl.multiple_of(i * tm, tm), tm), pl.ds(_dst_tile(t) * IN_TN, IN_TN)]
        return pltpu.make_async_copy(o_buf.at[t % IN_RING], dst, o_sem.at[t % IN_RING])

    def round_tiles(r):
        return [t for t in range(r * IN_PER_ROUND, (r + 1) * IN_PER_ROUND) if 0 <= t < N_IN_TILES]

    for t in round_tiles(0):
        w_copy(t).start()
    x = x_ref[...]
    ms = jnp.mean(x * x, axis=-1, keepdims=True)
    h_ref[...] = (x * lax.rsqrt(ms + NORM_EPS) * nw_ref[...]).astype(BF16)

    def finish(t, acc):
        o_buf[t % IN_RING] = acc.astype(BF16)
        if SRC_U_TILE0 <= t < SRC_U_TILE0 + SSM_W // IN_TN:
            slab0 = (t - SRC_U_TILE0) * n_sl
            for s in range(n_sl):
                for c in range(rows):
                    r0 = (s * rows + c) * US_PITCH
                    us_ref[r0:r0 + CHUNK, :] = acc[c * CHUNK:(c + 1) * CHUNK, s * LANES:(s + 1) * LANES]
            for s in range(n_sl):
                for tok in range(CHUNK):
                    piece = us_ref[pl.ds(s * rows * US_PITCH + tok, rows, stride=US_PITCH), :]
                    uc_ref[slab0 + s, :, tok * LANES:(tok + 1) * LANES] = piece.astype(BF16)

    n_rounds = pl.cdiv(N_IN_TILES, IN_PER_ROUND)
    for r in range(n_rounds):
        for t in round_tiles(r + 1):
            w_copy(t).start()
        for t in round_tiles(r):
            w_copy(t).wait()
        for t in round_tiles(r - 2):
            o_copy(t).wait()
        for t in round_tiles(r - 1):
            o_copy(t).start()
        for t in round_tiles(r):
            acc = None
            for k in range(D_MODEL // MXU):
                ks = slice(k * MXU, (k + 1) * MXU)
                part = _dot(h_ref[:, ks], w_buf[t % IN_RING, ks, :].astype(BF16))
                acc = part if acc is None else acc + part
            finish(t, acc)

    for t in round_tiles(n_rounds - 1):
        o_copy(t).start()
    for t in round_tiles(n_rounds - 2) + round_tiles(n_rounds - 1):
        o_copy(t).wait()


def _in_proj(x2, norm_w, w_in, tm=1024):
    m = x2.shape[0]
    kernel = functools.partial(_in_proj_kernel, tm=tm)
    return pl.pallas_call(
        kernel,
        grid=(m // tm,),
        in_specs=[
            pl.BlockSpec((tm, D_MODEL), lambda i: (i, 0)),
            pl.BlockSpec((1, D_MODEL), lambda i: (0, 0)),
            pl.BlockSpec(memory_space=pl.ANY),
        ],
        out_specs=[
            pl.BlockSpec(memory_space=pl.ANY),
            pl.BlockSpec((N_SLABS, tm // CHUNK, CHUNK_W), lambda i: (0, i, 0)),
        ],
        out_shape=[
            jax.ShapeDtypeStruct((m, IN_W), BF16),
            jax.ShapeDtypeStruct((N_SLABS, m // CHUNK, CHUNK_W), BF16),
        ],
        scratch_shapes=[
            pltpu.VMEM((tm, D_MODEL), BF16),
            pltpu.VMEM((IN_RING, D_MODEL, IN_TN), F32),
            pltpu.VMEM((IN_RING, tm, IN_TN), BF16),
            pltpu.VMEM((IN_TN // LANES * (tm // CHUNK) * US_PITCH, LANES), F32),
            pltpu.SemaphoreType.DMA((IN_RING,)),
            pltpu.SemaphoreType.DMA((IN_RING,)),
        ],
        compiler_params=pltpu.CompilerParams(
            dimension_semantics=("arbitrary",), vmem_limit_bytes=BIG_VMEM_LIMIT),
        name="in_proj",
    )(x2, norm_w.reshape(1, D_MODEL), w_in)


def _head_norm(t, w):
    ms = jnp.mean(t * t, axis=-1, keepdims=True)
    return t * lax.rsqrt(ms + NORM_EPS) * w


_NT = (((1,), (1,)), ((), ()))
_TN = (((0,), (0,)), ((), ()))


SWA_TQ = 1024
SWA_SUB = SWA_TQ // WINDOW


def _swa_tile(sink_ref, q_ref, g_ref, kc_ref, vc_ref, kp_ref, vp_ref, qw_ref, kw_ref, o_ref,
              first_tile):
    log2e = math.log2(math.e)
    kqw = kw_ref[...] * qw_ref[...] * (log2e / math.sqrt(HEAD_DIM))
    n_col = Q_PER_KV * WINDOW
    key = lax.broadcasted_iota(jnp.int32, (WINDOW, n_col), 0)
    qry = lax.broadcasted_iota(jnp.int32, (WINDOW, n_col), 1) % WINDOW
    from_prev = key > qry
    no_prev = jnp.where(first_tile, -1e30, 0.0)
    head_of_col = lax.broadcasted_iota(jnp.int32, (1, n_col), 1) // WINDOW
    gw = Q_PER_KV * HEAD_DIM
    sel_r = lax.broadcasted_iota(jnp.int32, (8, 2 * gw), 0)
    sel_l = (lax.broadcasted_iota(jnp.int32, (8, 2 * gw), 1) % gw) // HEAD_DIM
    head_sel = jnp.where(sel_r == sel_l, 1.0, 0.0).astype(BF16)

    parts = []
    carry = {}

    def block(g, n):
        kcol = slice(g * HEAD_DIM, (g + 1) * HEAD_DIM)
        gcols = slice(g * Q_PER_KV * HEAD_DIM, (g + 1) * Q_PER_KV * HEAD_DIM)
        if n == 0:
            sink = jnp.zeros((1, n_col), F32)
            for r in range(Q_PER_KV):
                sink = jnp.where(head_of_col == r, sink_ref[g * Q_PER_KV + r] * log2e, sink)
            carry['sink'] = sink
            carry['k'] = _head_norm(kp_ref[:, kcol].astype(F32), kqw).astype(BF16)
            carry['v'] = vp_ref[:, kcol]
        sink, k_prev, v_prev = carry['sink'], carry['k'], carry['v']
        rows = slice(n * WINDOW, (n + 1) * WINDOW)
        k_cur = _head_norm(kc_ref[rows, kcol].astype(F32), kqw).astype(BF16)
        v_cur = vc_ref[rows, kcol]
        k_ctx = jnp.concatenate([k_prev, k_cur], axis=0)
        v_ctx = jnp.concatenate([v_prev, v_cur], axis=0)

        qg = q_ref[rows, gcols]
        qf = qg.astype(F32)
        q_t = qf.T.astype(BF16)
        q_t = jnp.concatenate([q_t[r * HEAD_DIM:(r + 1) * HEAD_DIM] for r in range(Q_PER_KV)],
                              axis=1)
        q2 = qf * qf
        q2_hi = q2.astype(BF16)
        q2_lo = (q2 - q2_hi.astype(F32)).astype(BF16)
        ssq = lax.dot_general(head_sel, jnp.concatenate([q2_hi, q2_lo], axis=1), _NT,
                              preferred_element_type=F32)
        rms = lax.rsqrt(ssq * (1.0 / HEAD_DIM) + NORM_EPS)
        rms_q = jnp.concatenate([rms[r:r + 1] for r in range(Q_PER_KV)], axis=1)

        s = _dot(k_ctx, q_t)
        s_prev = s[:WINDOW] + no_prev if n == 0 else s[:WINDOW]
        s = jnp.where(from_prev, s_prev, s[WINDOW:]) * rms_q
        mx = jnp.maximum(jnp.max(s, axis=0, keepdims=True), sink)
        p = jnp.exp2(s - mx)
        den = jnp.sum(p, axis=0, keepdims=True) + jnp.exp2(sink - mx)
        p_ctx = jnp.concatenate([jnp.where(from_prev, p, 0.0), jnp.where(from_prev, 0.0, p)],
                                axis=0).astype(BF16)
        o_t = lax.dot_general(v_ctx, p_ctx, _TN, preferred_element_type=F32) * (1.0 / den)
        halves = [jnp.concatenate([o_t[:, (2 * h) * WINDOW:(2 * h + 1) * WINDOW],
                                   o_t[:, (2 * h + 1) * WINDOW:(2 * h + 2) * WINDOW]], axis=0).T
                  for h in range(Q_PER_KV // 2)]
        og = jnp.concatenate(halves, axis=1)
        gate = g_ref[rows, gcols].astype(F32)
        o_ref[rows, gcols] = (og * (gate * jax.nn.sigmoid(gate))).astype(BF16)
        carry['k'], carry['v'] = k_cur, v_cur

    for g in range(N_KV_HEADS):
        for n in range(SWA_SUB):
            parts.append(functools.partial(block, g, n))
    return parts


def _swa_kernel(*refs, tiles_per_seq):
    for block in _swa_tile(*refs, pl.program_id(0) % tiles_per_seq == 0):
        block()


def _swa(proj, sinks, q_norm_w, k_norm_w, batch, seq):
    m = batch * seq
    cur = lambda col: (lambda s: (s, col))
    prev = lambda col: (lambda s: (jnp.maximum(s * SWA_SUB - 1, 0), col))
    kernel = functools.partial(_swa_kernel, tiles_per_seq=seq // SWA_TQ)
    return pl.pallas_call(
        kernel,
        grid=(m // SWA_TQ,),
        in_specs=[
            pl.BlockSpec(memory_space=pltpu.SMEM),
            pl.BlockSpec((SWA_TQ, ATTN_W), cur(COL_Q // ATTN_W)),
            pl.BlockSpec((SWA_TQ, ATTN_W), cur(COL_GATE // ATTN_W)),
            pl.BlockSpec((SWA_TQ, KV_W), cur(COL_K // KV_W)),
            pl.BlockSpec((SWA_TQ, KV_W), cur(COL_V // KV_W)),
            pl.BlockSpec((WINDOW, KV_W), prev(COL_K // KV_W)),
            pl.BlockSpec((WINDOW, KV_W), prev(COL_V // KV_W)),
            pl.BlockSpec((1, HEAD_DIM), lambda s: (0, 0)),
            pl.BlockSpec((1, HEAD_DIM), lambda s: (0, 0)),
        ],
        out_specs=pl.BlockSpec((SWA_TQ, ATTN_W), cur(0)),
        out_shape=jax.ShapeDtypeStruct((m, ATTN_W), BF16),
        compiler_params=pltpu.CompilerParams(
            dimension_semantics=("arbitrary",), vmem_limit_bytes=VMEM_LIMIT),
        name="swa",
    )(sinks, proj, proj, proj, proj, proj, proj,
      q_norm_w.reshape(1, HEAD_DIM), k_norm_w.reshape(1, HEAD_DIM))


def _slab_powers(ar, ai, dt, btr, bti):
    dta_re, dta_im = dt * ar, dt * ai
    mag = jnp.exp(dta_re)
    ab_re, ab_im = mag * jnp.cos(dta_im), mag * jnp.sin(dta_im)
    pw = [(jnp.ones_like(ar), jnp.zeros_like(ar))]
    for _ in range(CHUNK):
        pr, pi = pw[-1]
        pw.append((pr * ab_re - pi * ab_im, pr * ab_im + pi * ab_re))
    den = ar * ar + ai * ai
    num_re, num_im = ab_re - 1.0, ab_im
    cf_re = (num_re * ar + num_im * ai) / den
    cf_im = (num_im * ar - num_re * ai) / den
    bb_re = cf_re * btr - cf_im * bti
    bb_im = cf_re * bti + cf_im * btr
    abb = [(pr * bb_re - pi * bb_im, pr * bb_im + pi * bb_re) for pr, pi in pw[:CHUNK]]
    return pw, abb


def _build_state_in(abb, p_s):
    row_g = lax.broadcasted_iota(jnp.int32, (LANES, STATE), 0) // GROUP
    even = row_g % 2 == 0
    pair_of_row = lax.broadcasted_iota(jnp.int32, (LANES, 2 * LANES), 0) // (2 * GROUP)
    for lag in range(CHUNK):
        t = CHUNK - 1 - lag
        re, im = abb[lag]
        blk = jnp.concatenate([jnp.where(even, re, 0.0), jnp.where(even, 0.0, re),
                               jnp.where(even, im, 0.0), jnp.where(even, 0.0, im)], axis=1)
        for k in range(PAIRS):
            p_s[k, t * LANES:(t + 1) * LANES, :] = jnp.where(pair_of_row == k, blk, 0.0).astype(BF16)


def _build_state_out(pw, cr, ci, q_s):
    lane_g = lax.broadcasted_iota(jnp.int32, (STATE, LANES), 1) // GROUP
    for t in range(CHUNK):
        pr, pi = pw[t + 1]
        cat = jnp.concatenate([cr * pr - ci * pi, -(cr * pi + ci * pr)], axis=1)
        cat_t = cat.T
        for k in range(PAIRS):
            for ri in range(2):
                for half in range(2):
                    piece = jnp.where(lane_g == 2 * k + half, cat_t[ri * STATE:(ri + 1) * STATE], 0.0)
                    r0 = ri * LANES + half * STATE
                    q_s[k, r0:r0 + STATE, t * LANES:(t + 1) * LANES] = piece.astype(BF16)


def _build_toeplitz(abb, cr, ci, d_row, w_s):
    def split(v):
        hi = v.astype(BF16)
        return hi, (v - hi.astype(F32)).astype(BF16)

    rhs_hi, rhs_lo = split(jnp.concatenate([cr, -ci], axis=1))
    rhs = jnp.concatenate([rhs_hi, rhs_lo, rhs_hi], axis=1)
    row = lax.broadcasted_iota(jnp.int32, (LANES, LANES), 0)
    col = lax.broadcasted_iota(jnp.int32, (LANES, LANES), 1)
    same_group = (row // GROUP) == (col // GROUP)
    taps = []
    for lag in range(CHUNK):
        lhs_hi, lhs_lo = split(jnp.concatenate(abb[lag], axis=1))
        k = lax.dot_general(jnp.concatenate([lhs_hi, lhs_hi, lhs_lo], axis=1), rhs,
                            (((1,), (1,)), ((), ())), preferred_element_type=F32)
        k = jnp.where(same_group, k, 0.0)
        if lag == 0:
            k = k + jnp.where(row == col, d_row, 0.0)
        taps.append(k.astype(BF16))
    zero = jnp.zeros((LANES, LANES), BF16)
    for d in range(N_TILES):
        w_s[d, :LANES, :LANES] = taps[2 * d]
        w_s[d, :LANES, LANES:] = taps[2 * d + 1]
        w_s[d, LANES:, :LANES] = taps[2 * d - 1] if d > 0 else zero
        w_s[d, LANES:, LANES:] = taps[2 * d]


def _ssm_kernel(u_ref, ar_ref, ai_ref, dt_ref, btr_ref, bti_ref, cr_ref, ci_ref, d_ref,
                dtare_ref, dtaim_ref, w0_ref, w1_ref, w2_ref, w3_ref, y_ref, c0_ref, c1_ref, c2_ref,
                c3_ref, xre_s, xim_s, p_s, q_s, w_s, *, batch, n_chunks):
    step = pl.program_id(0)
    for src, dst in ((w0_ref, c0_ref), (w1_ref, c1_ref), (w2_ref, c2_ref), (w3_ref, c3_ref)):
        dst[...] = src[...].astype(BF16)
    rows8 = batch * PAIRS
    slab_rows = n_chunks * rows8

    def state_rows(slab, b, k):
        return pl.ds(slab * slab_rows + b * PAIRS + k, n_chunks, stride=rows8)

    def per_row(v):
        v = jnp.broadcast_to(v, (SLAB_G, STATE))
        return jnp.broadcast_to(v[:, None, :], (SLAB_G, GROUP, STATE)).reshape(LANES, STATE)

    def powers():
        return _slab_powers(per_row(ar_ref[...]), per_row(ai_ref[...]),
                            per_row(jnp.exp(dt_ref[...])), btr_ref[...], bti_ref[...])

    @pl.when(step < N_SLABS)
    def _():
        _, abb = powers()
        _build_state_in(abb, p_s)
        u = u_ref[...]
        for k in range(PAIRS):
            x = _dot(u, p_s[k])
            for b in range(batch):
                xb = x[b * n_chunks:(b + 1) * n_chunks]
                xre_s[state_rows(step, b, k), :] = xb[:, :LANES]
                xim_s[state_rows(step, b, k), :] = xb[:, LANES:]

    @pl.when(step == N_SLABS)
    def _():
        dta_re = jnp.concatenate([dtare_ref[...]] * batch, axis=1).reshape(N_SLABS * rows8, LANES)
        dta_im = jnp.concatenate([dtaim_ref[...]] * batch, axis=1).reshape(N_SLABS * rows8, LANES)
        mag = jnp.exp(CHUNK * dta_re)
        a_re, a_im = mag * jnp.cos(CHUNK * dta_im), mag * jnp.sin(CHUNK * dta_im)

        def body(c, carry):
            s_re, s_im = carry
            x_re, x_im = [], []
            for sl in range(N_SLABS):
                off = pl.multiple_of(sl * slab_rows + c * rows8, rows8)
                x_re.append(xre_s[pl.ds(off, rows8), :])
                x_im.append(xim_s[pl.ds(off, rows8), :])
                xre_s[pl.ds(off, rows8), :] = s_re[sl * rows8:(sl + 1) * rows8]
                xim_s[pl.ds(off, rows8), :] = s_im[sl * rows8:(sl + 1) * rows8]
            x_re = jnp.concatenate(x_re, axis=0)
            x_im = jnp.concatenate(x_im, axis=0)
            return (a_re * s_re - a_im * s_im + x_re, a_re * s_im + a_im * s_re + x_im)

        zero = jnp.zeros((N_SLABS * rows8, LANES), F32)
        lax.fori_loop(0, n_chunks, body, (zero, zero))

    @pl.when(step > N_SLABS)
    def _():
        slab = step - (N_SLABS + 1)
        pw, abb = powers()
        cr, ci = cr_ref[...], ci_ref[...]
        _build_state_out(pw, cr, ci, q_s)
        _build_toeplitz(abb, cr, ci, d_ref[...], w_s)
        u = u_ref[...]
        y_state = None
        for k in range(PAIRS):
            s_in = jnp.concatenate(
                [jnp.concatenate([xre_s[state_rows(slab, b, k), :], xim_s[state_rows(slab, b, k), :]],
                                 axis=1) for b in range(batch)], axis=0).astype(BF16)
            part = _dot(s_in, q_s[k])
            y_state = part if y_state is None else y_state + part
        for t2 in range(N_TILES):
            acc = y_state[:, t2 * MXU:(t2 + 1) * MXU]
            for t1 in range(t2 + 1):
                acc = acc + _dot(u[:, t1 * MXU:(t1 + 1) * MXU], w_s[t2 - t1])
            y_ref[:, t2 * MXU:(t2 + 1) * MXU] = jax.nn.gelu(acc).astype(BF16)


N_CAST = 16


def _ssm(u_chunks, A_re, A_im, log_dt, B_re, B_im, C_re, C_im, D_skip, batch, tail_weights):
    _, rows, _ = u_chunks.shape
    n_chunks = rows // batch
    n_steps = 2 * N_SLABS + 1
    kernel = functools.partial(_ssm_kernel, batch=batch, n_chunks=n_chunks)

    dt = jnp.exp(log_dt)[:, None]
    btr = B_re.transpose(0, 2, 1).reshape(SSM_W, STATE)
    bti = B_im.transpose(0, 2, 1).reshape(SSM_W, STATE)
    cr, ci = C_re.reshape(SSM_W, STATE), C_im.reshape(SSM_W, STATE)
    d_rows = D_skip.reshape(N_SLABS, 1, LANES)
    dta_re = (dt * A_re).reshape(N_SLABS, PAIRS, LANES)
    dta_im = (dt * A_im).reshape(N_SLABS, PAIRS, LANES)

    def slab_in(s):
        return jnp.where(s <= N_SLABS, jnp.minimum(s, N_SLABS - 1), s - (N_SLABS + 1))

    def slab_out(s):
        return jnp.maximum(s - (N_SLABS + 1), 0)

    def cast_spec(w):
        return pl.BlockSpec((w.shape[0] // N_CAST, w.shape[1]),
                            lambda s: (jnp.minimum(s, N_CAST - 1), 0))

    coeff = pl.BlockSpec((LANES, STATE), lambda s: (slab_in(s), 0))
    whole = pl.BlockSpec((N_SLABS, PAIRS, LANES), lambda s: (0, 0, 0))
    return pl.pallas_call(
        kernel,
        grid=(n_steps,),
        in_specs=[
            pl.BlockSpec((None, rows, CHUNK_W), lambda s: (slab_in(s), 0, 0)),
            pl.BlockSpec((SLAB_G, STATE), lambda s: (slab_in(s), 0)),
            pl.BlockSpec((SLAB_G, STATE), lambda s: (slab_in(s), 0)),
            pl.BlockSpec((SLAB_G, 1), lambda s: (slab_in(s), 0)),
            coeff, coeff, coeff, coeff,
            pl.BlockSpec((None, 1, LANES), lambda s: (slab_in(s), 0, 0)),
            whole, whole,
        ] + [cast_spec(w) for w in tail_weights],
        out_specs=[pl.BlockSpec((None, rows, CHUNK_W), lambda s: (slab_out(s), 0, 0))]
        + [cast_spec(w) for w in tail_weights],
        out_shape=[jax.ShapeDtypeStruct(u_chunks.shape, BF16)]
        + [jax.ShapeDtypeStruct(w.shape, BF16) for w in tail_weights],
        scratch_shapes=[
            pltpu.VMEM((N_SLABS * n_chunks * batch * PAIRS, LANES), F32),
            pltpu.VMEM((N_SLABS * n_chunks * batch * PAIRS, LANES), F32),
            pltpu.VMEM((PAIRS, CHUNK_W, 2 * LANES), BF16),
            pltpu.VMEM((PAIRS, 2 * LANES, CHUNK_W), BF16),
            pltpu.VMEM((N_TILES, MXU, MXU), BF16),
        ],
        compiler_params=pltpu.CompilerParams(
            dimension_semantics=("arbitrary",), vmem_limit_bytes=VMEM_LIMIT),
        name="ssm",
    )(u_chunks, A_re, A_im, log_dt[:, None], btr, bti, cr, ci, d_rows, dta_re, dta_im,
      *tail_weights)


def _tail_kernel(ag_ref, yc_ref, z_ref, ga_ref, gs_ref, x_ref, wa_ref, wg_ref, bg_ref, ws_ref,
                 wo_ref, o_ref, ys_ref, *, tm):
    sub_chunks = TAIL_SUB // CHUNK
    for part in range(tm // TAIL_SUB):
        rs = slice(part * TAIL_SUB, (part + 1) * TAIL_SUB)
        cs = slice(part * sub_chunks, (part + 1) * sub_chunks)
        base = part * N_SLABS * sub_chunks * US_PITCH
        for s in range(N_SLABS):
            for t in range(CHUNK):
                ys_ref[pl.ds(base + s * sub_chunks * US_PITCH + t, sub_chunks, stride=US_PITCH), :] = (
                    yc_ref[s, cs, t * LANES:(t + 1) * LANES].astype(F32))
        y_g = jnp.concatenate(
            [jnp.concatenate([ys_ref[base + (s * sub_chunks + c) * US_PITCH:
                                     base + (s * sub_chunks + c) * US_PITCH + CHUNK, :]
                              for c in range(sub_chunks)], axis=0) for s in range(N_SLABS)],
            axis=1).astype(BF16)

        y_a = _dot(ag_ref[rs, :], wa_ref[...])
        glu = _dot(y_g, wg_ref[...]) + bg_ref[...]
        z = z_ref[rs, :].astype(F32)
        t = glu[:, :SSM_W] * jax.nn.sigmoid(glu[:, SSM_W:]) * (z * jax.nn.sigmoid(z))
        y_s = _dot(t.astype(BF16), ws_ref[...])
        merged = (jax.nn.sigmoid(ga_ref[rs, :].astype(F32)) * y_a
                  + jax.nn.sigmoid(gs_ref[rs, :].astype(F32)) * y_s)
        o_ref[rs, :] = x_ref[rs, :] + _dot(merged.astype(BF16), wo_ref[...])


TAIL_SUB = 256


def _tail(attn_g, y_chunks, proj, x2, w_attn, w_glu, b_glu, w_ssm, w_out, tm=512):
    m = x2.shape[0]
    row = lambda blk: (lambda i: (i, blk))
    const = lambda i: (0, 0)
    resident = functools.partial(pl.BlockSpec, index_map=const, pipeline_mode=pl.Buffered(1))
    kernel = functools.partial(_tail_kernel, tm=tm)
    return pl.pallas_call(
        kernel,
        grid=(m // tm,),
        in_specs=[
            pl.BlockSpec((tm, ATTN_W), row(0)),
            pl.BlockSpec((N_SLABS, tm // CHUNK, CHUNK_W), lambda i: (0, i, 0)),
            pl.BlockSpec((tm, SSM_W), row(COL_Z // SSM_W)),
            pl.BlockSpec((tm, D_MODEL), row(COL_GA // D_MODEL)),
            pl.BlockSpec((tm, D_MODEL), row(COL_GS // D_MODEL)),
            pl.BlockSpec((tm, D_MODEL), row(0)),
            resident((ATTN_W, D_MODEL)),
            resident((SSM_W, 2 * SSM_W)),
            resident((1, 2 * SSM_W)),
            resident((SSM_W, D_MODEL)),
            resident((D_MODEL, D_MODEL)),
        ],
        out_specs=pl.BlockSpec((tm, D_MODEL), row(0)),
        out_shape=jax.ShapeDtypeStruct((m, D_MODEL), F32),
        scratch_shapes=[pltpu.VMEM((N_SLABS * (tm // CHUNK) * US_PITCH, LANES), F32)],
        compiler_params=pltpu.CompilerParams(
            dimension_semantics=("arbitrary",), vmem_limit_bytes=62 * 1024 * 1024),
        name="tail",
    )(attn_g, y_chunks, proj, proj, proj, x2, w_attn, w_glu, b_glu.reshape(1, -1), w_ssm, w_out)


def kernel(x, norm_w, w_in, q_norm_w, k_norm_w, sinks, w_attn_proj, A_re, A_im, log_dt, B_re, B_im,
           C_re, C_im, D_skip, w_glu, b_glu, w_ssm_proj, w_out):
    batch, seq, _ = x.shape
    m = batch * seq
    x2 = x.reshape(m, D_MODEL)

    proj, u_chunks = _in_proj(x2, norm_w, w_in)

    attn_g = _swa(proj, sinks, q_norm_w, k_norm_w, batch, seq)

    y_chunks, w_attn_bf, w_glu_bf, w_ssm_bf, w_out_bf = _ssm(
        u_chunks, A_re, A_im, log_dt, B_re, B_im, C_re, C_im, D_skip, batch,
        (w_attn_proj, w_glu, w_ssm_proj, w_out))

    out = _tail(attn_g, y_chunks, proj, x2, w_attn_bf, w_glu_bf, b_glu, w_ssm_bf, w_out_bf)
    return out.reshape(batch, seq, D_MODEL)
```

```python
import functools
import math

import jax
import jax.numpy as jnp
from jax import lax
from jax.experimental import pallas as pl
from jax.experimental.pallas import tpu as pltpu

F32 = jnp.float32
BF16 = jnp.bfloat16

D_MODEL = 2048
HEAD_DIM = 64
N_Q_HEADS = 16
N_KV_HEADS = 4
Q_PER_KV = 4
ATTN_W = N_Q_HEADS * HEAD_DIM
KV_W = N_KV_HEADS * HEAD_DIM
WINDOW = 128
SSM_W = D_MODEL // 2
GROUP = 16
N_GROUPS = SSM_W // GROUP
STATE = 64
NORM_EPS = 1e-6

COL_Q = 0
COL_GATE = COL_Q + ATTN_W
COL_U = COL_GATE + ATTN_W
COL_Z = COL_U + SSM_W
COL_GA = COL_Z + SSM_W
COL_GS = COL_GA + D_MODEL
COL_K = COL_GS + D_MODEL
COL_V = COL_K + KV_W
IN_W = COL_V + KV_W

LANES = 128
MXU = 256
CHUNK = 16
SLAB_G = LANES // GROUP
N_SLABS = N_GROUPS // SLAB_G
PAIRS = SLAB_G // 2
CHUNK_W = CHUNK * LANES
N_TILES = CHUNK_W // MXU
VMEM_LIMIT = 48 * 1024 * 1024
BIG_VMEM_LIMIT = 56 * 1024 * 1024


def _dot(a, b):
    return jnp.dot(a, b, preferred_element_type=F32)


IN_TN = 512
N_IN_TILES = IN_W // IN_TN
SRC_KV_TILE = (ATTN_W) // IN_TN
SRC_U_TILE0 = (ATTN_W + 2 * KV_W + ATTN_W) // IN_TN


US_PITCH = CHUNK if (CHUNK // 8) % 2 else CHUNK + 8
IN_PER_ROUND = 2


def _dst_tile(src_tile):
    if src_tile < SRC_KV_TILE:
        return src_tile
    if src_tile == SRC_KV_TILE:
        return N_IN_TILES - 1
    return src_tile - 1


def _in_proj_kernel(x_ref, nw_ref, w_hbm, o_hbm, uc_ref, h_ref, w_buf, o_buf, us_ref, w_sem, o_sem,
                    *, tm):
    i = pl.program_id(0)
    rows = tm // CHUNK
    n_sl = IN_TN // LANES

    def round_tiles(r):
        return [t for t in range(r * IN_PER_ROUND, (r + 1) * IN_PER_ROUND) if 0 <= t < N_IN_TILES]

    def w_copy(r):
        tiles = round_tiles(r)
        width = len(tiles) * IN_TN
        return pltpu.make_async_copy(w_hbm.at[:, pl.ds(tiles[0] * IN_TN, width)],
                                     w_buf.at[r % 2, :, pl.ds(0, width)], w_sem.at[r % 2])

    def o_copies(r):
        tiles = round_tiles(r)
        runs, start = [], 0
        for j in range(1, len(tiles) + 1):
            if j == len(tiles) or _dst_tile(tiles[j]) != _dst_tile(tiles[j - 1]) + 1:
                runs.append((start, j - start))
                start = j
        row0 = pl.multiple_of(i * tm, tm)
        return [pltpu.make_async_copy(
            o_buf.at[r % 2, :, pl.ds(j0 * IN_TN, n * IN_TN)],
            o_hbm.at[pl.ds(row0, tm), pl.ds(_dst_tile(tiles[j0]) * IN_TN, n * IN_TN)],
            o_sem.at[r % 2, k]) for k, (j0, n) in enumerate(runs)]

    w_copy(0).start()
    x = x_ref[...]
    ms = jnp.mean(x * x, axis=-1, keepdims=True)
    h_ref[...] = (x * lax.rsqrt(ms + NORM_EPS) * nw_ref[...]).astype(BF16)

    def finish(t, acc):
        r, j = divmod(t, IN_PER_ROUND)
        o_buf[r % 2, :, j * IN_TN:(j + 1) * IN_TN] = acc.astype(BF16)
        if SRC_U_TILE0 <= t < SRC_U_TILE0 + SSM_W // IN_TN:
            slab0 = (t - SRC_U_TILE0) * n_sl
            for s in range(n_sl):
                for c in range(rows):
                    r0 = (s * rows + c) * US_PITCH
                    us_ref[r0:r0 + CHUNK, :] = acc[c * CHUNK:(c + 1) * CHUNK, s * LANES:(s + 1) * LANES]
            for s in range(n_sl):
                for tok in range(CHUNK):
                    piece = us_ref[pl.ds(s * rows * US_PITCH + tok, rows, stride=US_PITCH), :]
                    uc_ref[slab0 + s, :, tok * LANES:(tok + 1) * LANES] = piece.astype(BF16)

    n_rounds = pl.cdiv(N_IN_TILES, IN_PER_ROUND)
    for r in range(n_rounds):
        if r + 1 < n_rounds:
            w_copy(r + 1).start()
        w_copy(r).wait()
        if r >= 2:
            for c in o_copies(r - 2):
                c.wait()
        if r >= 1:
            for c in o_copies(r - 1):
                c.start()
        for j, t in enumerate(round_tiles(r)):
            acc = None
            for k in range(D_MODEL // MXU):
                ks = slice(k * MXU, (k + 1) * MXU)
                part = _dot(h_ref[:, ks], w_buf[r % 2, ks, j * IN_TN:(j + 1) * IN_TN].astype(BF16))
                acc = part if acc is None else acc + part
            finish(t, acc)

    for c in o_copies(n_rounds - 1):
        c.start()
    for c in o_copies(n_rounds - 2) + o_copies(n_rounds - 1):
        c.wait()


def _in_proj(x2, norm_w, w_in, tm=1024):
    m = x2.shape[0]
    kernel = functools.partial(_in_proj_kernel, tm=tm)
    return pl.pallas_call(
        kernel,
        grid=(m // tm,),
        in_specs=[
            pl.BlockSpec((tm, D_MODEL), lambda i: (i, 0)),
            pl.BlockSpec((1, D_MODEL), lambda i: (0, 0)),
            pl.BlockSpec(memory_space=pl.ANY),
        ],
        out_specs=[
            pl.BlockSpec(memory_space=pl.ANY),
            pl.BlockSpec((N_SLABS, tm // CHUNK, CHUNK_W), lambda i: (0, i, 0)),
        ],
        out_shape=[
            jax.ShapeDtypeStruct((m, IN_W), BF16),
            jax.ShapeDtypeStruct((N_SLABS, m // CHUNK, CHUNK_W), BF16),
        ],
        scratch_shapes=[
            pltpu.VMEM((tm, D_MODEL), BF16),
            pltpu.VMEM((2, D_MODEL, IN_PER_ROUND * IN_TN), F32),
            pltpu.VMEM((2, tm, IN_PER_ROUND * IN_TN), BF16),
            pltpu.VMEM((IN_TN // LANES * (tm // CHUNK) * US_PITCH, LANES), F32),
            pltpu.SemaphoreType.DMA((2,)),
            pltpu.SemaphoreType.DMA((2, IN_PER_ROUND)),
        ],
        compiler_params=pltpu.CompilerParams(
            dimension_semantics=("arbitrary",), vmem_limit_bytes=BIG_VMEM_LIMIT),
        name="in_proj",
    )(x2, norm_w.reshape(1, D_MODEL), w_in)


def _head_norm(t, w):
    ms = jnp.mean(t * t, axis=-1, keepdims=True)
    return t * lax.rsqrt(ms + NORM_EPS) * w


_NT = (((1,), (1,)), ((), ()))
_TN = (((0,), (0,)), ((), ()))


SWA_TQ = 1024
SWA_SUB = SWA_TQ // WINDOW


def _swa_tile(sink_ref, q_ref, g_ref, kc_ref, vc_ref, kp_ref, vp_ref, qw_ref, kw_ref, o_ref,
              first_tile):
    log2e = math.log2(math.e)
    kqw = kw_ref[...] * qw_ref[...] * (log2e / math.sqrt(HEAD_DIM))
    n_col = Q_PER_KV * WINDOW
    key = lax.broadcasted_iota(jnp.int32, (WINDOW, n_col), 0)
    qry = lax.broadcasted_iota(jnp.int32, (WINDOW, n_col), 1) % WINDOW
    from_prev = key > qry
    no_prev = jnp.where(first_tile, -1e30, 0.0)
    head_of_col = lax.broadcasted_iota(jnp.int32, (1, n_col), 1) // WINDOW
    gw = Q_PER_KV * HEAD_DIM
    sel_r = lax.broadcasted_iota(jnp.int32, (8, 2 * gw), 0)
    sel_l = (lax.broadcasted_iota(jnp.int32, (8, 2 * gw), 1) % gw) // HEAD_DIM
    head_sel = jnp.where(sel_r == sel_l, 1.0, 0.0).astype(BF16)

    parts = []
    carry = {}

    def block(g, n):
        kcol = slice(g * HEAD_DIM, (g + 1) * HEAD_DIM)
        gcols = slice(g * Q_PER_KV * HEAD_DIM, (g + 1) * Q_PER_KV * HEAD_DIM)
        if n == 0:
            sink = jnp.zeros((1, n_col), F32)
            for r in range(Q_PER_KV):
                sink = jnp.where(head_of_col == r, sink_ref[g * Q_PER_KV + r] * log2e, sink)
            carry['sink'] = sink
            carry['k'] = _head_norm(kp_ref[:, kcol].astype(F32), kqw).astype(BF16)
            carry['v'] = vp_ref[:, kcol]
        sink, k_prev, v_prev = carry['sink'], carry['k'], carry['v']
        rows = slice(n * WINDOW, (n + 1) * WINDOW)
        k_cur = _head_norm(kc_ref[rows, kcol].astype(F32), kqw).astype(BF16)
        v_cur = vc_ref[rows, kcol]
        k_ctx = jnp.concatenate([k_prev, k_cur], axis=0)
        v_ctx = jnp.concatenate([v_prev, v_cur], axis=0)

        qg = q_ref[rows, gcols]
        qf = qg.astype(F32)
        q_t = qf.T.astype(BF16)
        q_t = jnp.concatenate([q_t[r * HEAD_DIM:(r + 1) * HEAD_DIM] for r in range(Q_PER_KV)],
                              axis=1)
        q2 = qf * qf
        q2_hi = q2.astype(BF16)
        q2_lo = (q2 - q2_hi.astype(F32)).astype(BF16)
        ssq = lax.dot_general(head_sel, jnp.concatenate([q2_hi, q2_lo], axis=1), _NT,
                              preferred_element_type=F32)
        rms = lax.rsqrt(ssq * (1.0 / HEAD_DIM) + NORM_EPS)
        rms_q = jnp.concatenate([rms[r:r + 1] for r in range(Q_PER_KV)], axis=1)

        s = _dot(k_ctx, q_t)
        s_prev = s[:WINDOW] + no_prev if n == 0 else s[:WINDOW]
        s = jnp.where(from_prev, s_prev, s[WINDOW:]) * rms_q
        mx = jnp.maximum(jnp.max(s, axis=0, keepdims=True), sink)
        p = jnp.exp2(s - mx)
        den = jnp.sum(p, axis=0, keepdims=True) + jnp.exp2(sink - mx)
        p_ctx = jnp.concatenate([jnp.where(from_prev, p, 0.0), jnp.where(from_prev, 0.0, p)],
                                axis=0).astype(BF16)
        o_t = lax.dot_general(v_ctx, p_ctx, _TN, preferred_element_type=F32) * (1.0 / den)
        halves = [jnp.concatenate([o_t[:, (2 * h) * WINDOW:(2 * h + 1) * WINDOW],
                                   o_t[:, (2 * h + 1) * WINDOW:(2 * h + 2) * WINDOW]], axis=0).T
                  for h in range(Q_PER_KV // 2)]
        og = jnp.concatenate(halves, axis=1)
        gate = g_ref[rows, gcols].astype(F32)
        o_ref[rows, gcols] = (og * (gate * jax.nn.sigmoid(gate))).astype(BF16)
        carry['k'], carry['v'] = k_cur, v_cur

    for g in range(N_KV_HEADS):
        for n in range(SWA_SUB):
            parts.append(functools.partial(block, g, n))
    return parts


def _swa_kernel(*refs, tiles_per_seq):
    for block in _swa_tile(*refs, pl.program_id(0) % tiles_per_seq == 0):
        block()


def _swa(proj, sinks, q_norm_w, k_norm_w, batch, seq):
    m = batch * seq
    cur = lambda col: (lambda s: (s, col))
    prev = lambda col: (lambda s: (jnp.maximum(s * SWA_SUB - 1, 0), col))
    kernel = functools.partial(_swa_kernel, tiles_per_seq=seq // SWA_TQ)
    return pl.pallas_call(
        kernel,
        grid=(m // SWA_TQ,),
        in_specs=[
            pl.BlockSpec(memory_space=pltpu.SMEM),
            pl.BlockSpec((SWA_TQ, ATTN_W), cur(COL_Q // ATTN_W)),
            pl.BlockSpec((SWA_TQ, ATTN_W), cur(COL_GATE // ATTN_W)),
            pl.BlockSpec((SWA_TQ, KV_W), cur(COL_K // KV_W)),
            pl.BlockSpec((SWA_TQ, KV_W), cur(COL_V // KV_W)),
            pl.BlockSpec((WINDOW, KV_W), prev(COL_K // KV_W)),
            pl.BlockSpec((WINDOW, KV_W), prev(COL_V // KV_W)),
            pl.BlockSpec((1, HEAD_DIM), lambda s: (0, 0)),
            pl.BlockSpec((1, HEAD_DIM), lambda s: (0, 0)),
        ],
        out_specs=pl.BlockSpec((SWA_TQ, ATTN_W), cur(0)),
        out_shape=jax.ShapeDtypeStruct((m, ATTN_W), BF16),
        compiler_params=pltpu.CompilerParams(
            dimension_semantics=("arbitrary",), vmem_limit_bytes=VMEM_LIMIT),
        name="swa",
    )(sinks, proj, proj, proj, proj, proj, proj,
      q_norm_w.reshape(1, HEAD_DIM), k_norm_w.reshape(1, HEAD_DIM))


def _slab_powers(ar, ai, dt, btr, bti):
    dta_re, dta_im = dt * ar, dt * ai
    mag = jnp.exp(dta_re)
    ab_re, ab_im = mag * jnp.cos(dta_im), mag * jnp.sin(dta_im)
    pw = [(jnp.ones_like(ar), jnp.zeros_like(ar))]
    for _ in range(CHUNK):
        pr, pi = pw[-1]
        pw.append((pr * ab_re - pi * ab_im, pr * ab_im + pi * ab_re))
    den = ar * ar + ai * ai
    num_re, num_im = ab_re - 1.0, ab_im
    cf_re = (num_re * ar + num_im * ai) / den
    cf_im = (num_im * ar - num_re * ai) / den
    bb_re = cf_re * btr - cf_im * bti
    bb_im = cf_re * bti + cf_im * btr
    abb = [(pr * bb_re - pi * bb_im, pr * bb_im + pi * bb_re) for pr, pi in pw[:CHUNK]]
    return pw, abb


def _build_state_in(abb, p_s):
    row_g = lax.broadcasted_iota(jnp.int32, (LANES, STATE), 0) // GROUP
    even = row_g % 2 == 0
    pair_of_row = lax.broadcasted_iota(jnp.int32, (LANES, 2 * LANES), 0) // (2 * GROUP)
    for lag in range(CHUNK):
        t = CHUNK - 1 - lag
        re, im = abb[lag]
        blk = jnp.concatenate([jnp.where(even, re, 0.0), jnp.where(even, 0.0, re),
                               jnp.where(even, im, 0.0), jnp.where(even, 0.0, im)], axis=1)
        for k in range(PAIRS):
            p_s[k, t * LANES:(t + 1) * LANES, :] = jnp.where(pair_of_row == k, blk, 0.0).astype(BF16)


def _build_state_out(pw, cr, ci, q_s):
    lane_g = lax.broadcasted_iota(jnp.int32, (STATE, LANES), 1) // GROUP
    for t in range(CHUNK):
        pr, pi = pw[t + 1]
        cat = jnp.concatenate([cr * pr - ci * pi, -(cr * pi + ci * pr)], axis=1)
        cat_t = cat.T
        for k in range(PAIRS):
            for ri in range(2):
                for half in range(2):
                    piece = jnp.where(lane_g == 2 * k + half, cat_t[ri * STATE:(ri + 1) * STATE], 0.0)
                    r0 = ri * LANES + half * STATE
                    q_s[k, r0:r0 + STATE, t * LANES:(t + 1) * LANES] = piece.astype(BF16)


def _build_toeplitz(abb, cr, ci, d_row, w_s):
    def split(v):
        hi = v.astype(BF16)
        return hi, (v - hi.astype(F32)).astype(BF16)

    rhs_hi, rhs_lo = split(jnp.concatenate([cr, -ci], axis=1))
    rhs = jnp.concatenate([rhs_hi, rhs_lo, rhs_hi], axis=1)
    row = lax.broadcasted_iota(jnp.int32, (LANES, LANES), 0)
    col = lax.broadcasted_iota(jnp.int32, (LANES, LANES), 1)
    same_group = (row // GROUP) == (col // GROUP)
    taps = []
    for lag in range(CHUNK):
        lhs_hi, lhs_lo = split(jnp.concatenate(abb[lag], axis=1))
        k = lax.dot_general(jnp.concatenate([lhs_hi, lhs_hi, lhs_lo], axis=1), rhs,
                            (((1,), (1,)), ((), ())), preferred_element_type=F32)
        k = jnp.where(same_group, k, 0.0)
        if lag == 0:
            k = k + jnp.where(row == col, d_row, 0.0)
        taps.append(k.astype(BF16))
    zero = jnp.zeros((LANES, LANES), BF16)
    for d in range(N_TILES):
        w_s[d, :LANES, :LANES] = taps[2 * d]
        w_s[d, :LANES, LANES:] = taps[2 * d + 1]
        w_s[d, LANES:, :LANES] = taps[2 * d - 1] if d > 0 else zero
        w_s[d, LANES:, LANES:] = taps[2 * d]


def _ssm_kernel(u_ref, ar_ref, ai_ref, dt_ref, btr_ref, bti_ref, cr_ref, ci_ref, d_ref,
                dtare_ref, dtaim_ref, w0_ref, w1_ref, w2_ref, w3_ref, y_ref, c0_ref, c1_ref, c2_ref,
                c3_ref, xre_s, xim_s, p_s, q_s, w_s, *, batch, n_chunks):
    step = pl.program_id(0)
    for src, dst in ((w0_ref, c0_ref), (w1_ref, c1_ref), (w2_ref, c2_ref), (w3_ref, c3_ref)):
        dst[...] = src[...].astype(BF16)
    rows8 = batch * PAIRS
    slab_rows = n_chunks * rows8

    def state_rows(slab, b, k):
        return pl.ds(slab * slab_rows + b * PAIRS + k, n_chunks, stride=rows8)

    def per_row(v):
        v = jnp.broadcast_to(v, (SLAB_G, STATE))
        return jnp.broadcast_to(v[:, None, :], (SLAB_G, GROUP, STATE)).reshape(LANES, STATE)

    def powers():
        return _slab_powers(per_row(ar_ref[...]), per_row(ai_ref[...]),
                            per_row(jnp.exp(dt_ref[...])), btr_ref[...], bti_ref[...])

    @pl.when(step < N_SLABS)
    def _():
        _, abb = powers()
        _build_state_in(abb, p_s)
        u = u_ref[...]
        for k in range(PAIRS):
            x = _dot(u, p_s[k])
            for b in range(batch):
                xb = x[b * n_chunks:(b + 1) * n_chunks]
                xre_s[state_rows(step, b, k), :] = xb[:, :LANES]
                xim_s[state_rows(step, b, k), :] = xb[:, LANES:]

    @pl.when(step == N_SLABS)
    def _():
        dta_re = jnp.concatenate([dtare_ref[...]] * batch, axis=1).reshape(N_SLABS * rows8, LANES)
        dta_im = jnp.concatenate([dtaim_ref[...]] * batch, axis=1).reshape(N_SLABS * rows8, LANES)
        mag = jnp.exp(CHUNK * dta_re)
        a_re, a_im = mag * jnp.cos(CHUNK * dta_im), mag * jnp.sin(CHUNK * dta_im)

        def body(c, carry):
            s_re, s_im = carry
            x_re, x_im = [], []
            for sl in range(N_SLABS):
                off = pl.multiple_of(sl * slab_rows + c * rows8, rows8)
                x_re.append(xre_s[pl.ds(off, rows8), :])
                x_im.append(xim_s[pl.ds(off, rows8), :])
                xre_s[pl.ds(off, rows8), :] = s_re[sl * rows8:(sl + 1) * rows8]
                xim_s[pl.ds(off, rows8), :] = s_im[sl * rows8:(sl + 1) * rows8]
            x_re = jnp.concatenate(x_re, axis=0)
            x_im = jnp.concatenate(x_im, axis=0)
            return (a_re * s_re - a_im * s_im + x_re, a_re * s_im + a_im * s_re + x_im)

        zero = jnp.zeros((N_SLABS * rows8, LANES), F32)
        lax.fori_loop(0, n_chunks, body, (zero, zero))

    @pl.when(step > N_SLABS)
    def _():
        slab = step - (N_SLABS + 1)
        pw, abb = powers()
        cr, ci = cr_ref[...], ci_ref[...]
        _build_state_out(pw, cr, ci, q_s)
        _build_toeplitz(abb, cr, ci, d_ref[...], w_s)
        u = u_ref[...]
        y_state = None
        for k in range(PAIRS):
            s_in = jnp.concatenate(
                [jnp.concatenate([xre_s[state_rows(slab, b, k), :], xim_s[state_rows(slab, b, k), :]],
                                 axis=1) for b in range(batch)], axis=0).astype(BF16)
            part = _dot(s_in, q_s[k])
            y_state = part if y_state is None else y_state + part
        for t2 in range(N_TILES):
            acc = y_state[:, t2 * MXU:(t2 + 1) * MXU]
            for t1 in range(t2 + 1):
                acc = acc + _dot(u[:, t1 * MXU:(t1 + 1) * MXU], w_s[t2 - t1])
            y_ref[:, t2 * MXU:(t2 + 1) * MXU] = jax.nn.gelu(acc).astype(BF16)


N_CAST = 16


def _ssm(u_chunks, A_re, A_im, log_dt, B_re, B_im, C_re, C_im, D_skip, batch, tail_weights):
    _, rows, _ = u_chunks.shape
    n_chunks = rows // batch
    n_steps = 2 * N_SLABS + 1
    kernel = functools.partial(_ssm_kernel, batch=batch, n_chunks=n_chunks)

    dt = jnp.exp(log_dt)[:, None]
    btr = B_re.transpose(0, 2, 1).reshape(SSM_W, STATE)
    bti = B_im.transpose(0, 2, 1).reshape(SSM_W, STATE)
    cr, ci = C_re.reshape(SSM_W, STATE), C_im.reshape(SSM_W, STATE)
    d_rows = D_skip.reshape(N_SLABS, 1, LANES)
    dta_re = (dt * A_re).reshape(N_SLABS, PAIRS, LANES)
    dta_im = (dt * A_im).reshape(N_SLABS, PAIRS, LANES)

    def slab_in(s):
        return jnp.where(s <= N_SLABS, jnp.minimum(s, N_SLABS - 1), s - (N_SLABS + 1))

    def slab_out(s):
        return jnp.maximum(s - (N_SLABS + 1), 0)

    def cast_spec(w):
        return pl.BlockSpec((w.shape[0] // N_CAST, w.shape[1]),
                            lambda s: (jnp.minimum(s, N_CAST - 1), 0))

    coeff = pl.BlockSpec((LANES, STATE), lambda s: (slab_in(s), 0))
    whole = pl.BlockSpec((N_SLABS, PAIRS, LANES), lambda s: (0, 0, 0))
    return pl.pallas_call(
        kernel,
        grid=(n_steps,),
        in_specs=[
            pl.BlockSpec((None, rows, CHUNK_W), lambda s: (slab_in(s), 0, 0)),
            pl.BlockSpec((SLAB_G, STATE), lambda s: (slab_in(s), 0)),
            pl.BlockSpec((SLAB_G, STATE), lambda s: (slab_in(s), 0)),
            pl.BlockSpec((SLAB_G, 1), lambda s: (slab_in(s), 0)),
            coeff, coeff, coeff, coeff,
            pl.BlockSpec((None, 1, LANES), lambda s: (slab_in(s), 0, 0)),
            whole, whole,
        ] + [cast_spec(w) for w in tail_weights],
        out_specs=[pl.BlockSpec((None, rows, CHUNK_W), lambda s: (slab_out(s), 0, 0))]
        + [cast_spec(w) for w in tail_weights],
        out_shape=[jax.ShapeDtypeStruct(u_chunks.shape, BF16)]
        + [jax.ShapeDtypeStruct(w.shape, BF16) for w in tail_weights],
        scratch_shapes=[
            pltpu.VMEM((N_SLABS * n_chunks * batch * PAIRS, LANES), F32),
            pltpu.VMEM((N_SLABS * n_chunks * batch * PAIRS, LANES), F32),
            pltpu.VMEM((PAIRS, CHUNK_W, 2 * LANES), BF16),
            pltpu.VMEM((PAIRS, 2 * LANES, CHUNK_W), BF16),
            pltpu.VMEM((N_TILES, MXU, MXU), BF16),
        ],
        compiler_params=pltpu.CompilerParams(
            dimension_semantics=("arbitrary",), vmem_limit_bytes=VMEM_LIMIT),
        name="ssm",
    )(u_chunks, A_re, A_im, log_dt[:, None], btr, bti, cr, ci, d_rows, dta_re, dta_im,
      *tail_weights)


def _tail_kernel(ag_ref, yc_ref, z_ref, ga_ref, gs_ref, x_ref, wa_ref, wg_ref, bg_ref, ws_ref,
                 wo_ref, o_ref, ys_ref, *, tm):
    rows = tm // CHUNK
    for s in range(N_SLABS):
        for t in range(CHUNK):
            ys_ref[pl.ds(s * rows * US_PITCH + t, rows, stride=US_PITCH), :] = (
                yc_ref[s, :, t * LANES:(t + 1) * LANES].astype(F32))
    y_g = jnp.concatenate(
        [jnp.concatenate([ys_ref[(s * rows + c) * US_PITCH:(s * rows + c) * US_PITCH + CHUNK, :]
                          for c in range(rows)], axis=0) for s in range(N_SLABS)],
        axis=1).astype(BF16)

    y_a = _dot(ag_ref[...], wa_ref[...])
    glu = _dot(y_g, wg_ref[...]) + bg_ref[...]
    z = z_ref[...].astype(F32)
    t = glu[:, :SSM_W] * jax.nn.sigmoid(glu[:, SSM_W:]) * (z * jax.nn.sigmoid(z))
    y_s = _dot(t.astype(BF16), ws_ref[...])
    merged = (jax.nn.sigmoid(ga_ref[...].astype(F32)) * y_a
              + jax.nn.sigmoid(gs_ref[...].astype(F32)) * y_s)
    o_ref[...] = x_ref[...] + _dot(merged.astype(BF16), wo_ref[...])


def _tail(attn_g, y_chunks, proj, x2, w_attn, w_glu, b_glu, w_ssm, w_out, tm=256):
    m = x2.shape[0]
    row = lambda blk: (lambda i: (i, blk))
    const = lambda i: (0, 0)
    resident = functools.partial(pl.BlockSpec, index_map=const, pipeline_mode=pl.Buffered(1))
    kernel = functools.partial(_tail_kernel, tm=tm)
    return pl.pallas_call(
        kernel,
        grid=(m // tm,),
        in_specs=[
            pl.BlockSpec((tm, ATTN_W), row(0)),
            pl.BlockSpec((N_SLABS, tm // CHUNK, CHUNK_W), lambda i: (0, i, 0)),
            pl.BlockSpec((tm, SSM_W), row(COL_Z // SSM_W)),
            pl.BlockSpec((tm, D_MODEL), row(COL_GA // D_MODEL)),
            pl.BlockSpec((tm, D_MODEL), row(COL_GS // D_MODEL)),
            pl.BlockSpec((tm, D_MODEL), row(0)),
            resident((ATTN_W, D_MODEL)),
            resident((SSM_W, 2 * SSM_W)),
            resident((1, 2 * SSM_W)),
            resident((SSM_W, D_MODEL)),
            resident((D_MODEL, D_MODEL)),
        ],
        out_specs=pl.BlockSpec((tm, D_MODEL), row(0)),
        out_shape=jax.ShapeDtypeStruct((m, D_MODEL), F32),
        scratch_shapes=[pltpu.VMEM((N_SLABS * (tm // CHUNK) * US_PITCH, LANES), F32)],
        compiler_params=pltpu.CompilerParams(
            dimension_semantics=("arbitrary",), vmem_limit_bytes=VMEM_LIMIT),
        name="tail",
    )(attn_g, y_chunks, proj, proj, proj, x2, w_attn, w_glu, b_glu.reshape(1, -1), w_ssm, w_out)


def kernel(x, norm_w, w_in, q_norm_w, k_norm_w, sinks, w_attn_proj, A_re, A_im, log_dt, B_re, B_im,
           C_re, C_im, D_skip, w_glu, b_glu, w_ssm_proj, w_out):
    batch, seq, _ = x.shape
    m = batch * seq
    x2 = x.reshape(m, D_MODEL)

    proj, u_chunks = _in_proj(x2, norm_w, w_in)

    attn_g = _swa(proj, sinks, q_norm_w, k_norm_w, batch, seq)

    y_chunks, w_attn_bf, w_glu_bf, w_ssm_bf, w_out_bf = _ssm(
        u_chunks, A_re, A_im, log_dt, B_re, B_im, C_re, C_im, D_skip, batch,
        (w_attn_proj, w_glu, w_ssm_proj, w_out))

    out = _tail(attn_g, y_chunks, proj, x2, w_attn_bf, w_glu_bf, b_glu, w_ssm_bf, w_out_bf)
    return out.reshape(batch, seq, D_MODEL)
```

```python
import functools
import math

import jax
import jax.numpy as jnp
from jax import lax
from jax.experimental import pallas as pl
from jax.experimental.pallas import tpu as pltpu

F32 = jnp.float32
BF16 = jnp.bfloat16

D_MODEL = 2048
HEAD_DIM = 64
N_Q_HEADS = 16
N_KV_HEADS = 4
Q_PER_KV = 4
ATTN_W = N_Q_HEADS * HEAD_DIM
KV_W = N_KV_HEADS * HEAD_DIM
WINDOW = 128
SSM_W = D_MODEL // 2
GROUP = 16
N_GROUPS = SSM_W // GROUP
STATE = 64
NORM_EPS = 1e-6

COL_Q = 0
COL_GATE = COL_Q + ATTN_W
COL_U = COL_GATE + ATTN_W
COL_Z = COL_U + SSM_W
COL_GA = COL_Z + SSM_W
COL_GS = COL_GA + D_MODEL
COL_K = COL_GS + D_MODEL
COL_V = COL_K + KV_W
IN_W = COL_V + KV_W

LANES = 128
MXU = 256
CHUNK = 16
SLAB_G = LANES // GROUP
N_SLABS = N_GROUPS // SLAB_G
PAIRS = SLAB_G // 2
CHUNK_W = CHUNK * LANES
N_TILES = CHUNK_W // MXU
VMEM_LIMIT = 48 * 1024 * 1024
BIG_VMEM_LIMIT = 56 * 1024 * 1024


def _dot(a, b):
    return jnp.dot(a, b, preferred_element_type=F32)


IN_TN = 512
N_IN_TILES = IN_W // IN_TN
SRC_KV_TILE = (ATTN_W) // IN_TN
SRC_U_TILE0 = (ATTN_W + 2 * KV_W + ATTN_W) // IN_TN


US_PITCH = CHUNK if (CHUNK // 8) % 2 else CHUNK + 8
IN_PER_ROUND = 2


def _dst_tile(src_tile):
    if src_tile < SRC_KV_TILE:
        return src_tile
    if src_tile == SRC_KV_TILE:
        return N_IN_TILES - 1
    return src_tile - 1


def _in_proj_kernel(x_ref, nw_ref, w_hbm, o_hbm, uc_ref, h_ref, w_buf, o_buf, us_ref, w_sem, o_sem,
                    *, tm, n_steps):
    i = pl.program_id(0)
    rows = tm // CHUNK
    n_sl = IN_TN // LANES
    n_rounds = pl.cdiv(N_IN_TILES, IN_PER_ROUND)
    assert n_rounds % 2 == 1

    def round_tiles(r):
        return [t for t in range(r * IN_PER_ROUND, (r + 1) * IN_PER_ROUND) if 0 <= t < N_IN_TILES]

    def w_half(step, r):
        return (step + r) % 2

    def w_copy(step, r):
        tiles = round_tiles(r)
        width = len(tiles) * IN_TN
        half = w_half(step, r)
        return pltpu.make_async_copy(w_hbm.at[:, pl.ds(tiles[0] * IN_TN, width)],
                                     w_buf.at[half, :, pl.ds(0, width)], w_sem.at[half])

    def o_copies(r):
        tiles = round_tiles(r)
        runs, start = [], 0
        for j in range(1, len(tiles) + 1):
            if j == len(tiles) or _dst_tile(tiles[j]) != _dst_tile(tiles[j - 1]) + 1:
                runs.append((start, j - start))
                start = j
        row0 = pl.multiple_of(i * tm, tm)
        return [pltpu.make_async_copy(
            o_buf.at[r % 2, :, pl.ds(j0 * IN_TN, n * IN_TN)],
            o_hbm.at[pl.ds(row0, tm), pl.ds(_dst_tile(tiles[j0]) * IN_TN, n * IN_TN)],
            o_sem.at[r % 2, k]) for k, (j0, n) in enumerate(runs)]

    @pl.when(i == 0)
    def _():
        w_copy(0, 0).start()

    x = x_ref[...]
    ms = jnp.mean(x * x, axis=-1, keepdims=True)
    h_ref[...] = (x * lax.rsqrt(ms + NORM_EPS) * nw_ref[...]).astype(BF16)

    def finish(t, acc):
        r, j = divmod(t, IN_PER_ROUND)
        o_buf[r % 2, :, j * IN_TN:(j + 1) * IN_TN] = acc.astype(BF16)
        if SRC_U_TILE0 <= t < SRC_U_TILE0 + SSM_W // IN_TN:
            slab0 = (t - SRC_U_TILE0) * n_sl
            for s in range(n_sl):
                for c in range(rows):
                    r0 = (s * rows + c) * US_PITCH
                    us_ref[r0:r0 + CHUNK, :] = acc[c * CHUNK:(c + 1) * CHUNK, s * LANES:(s + 1) * LANES]
            for s in range(n_sl):
                for tok in range(CHUNK):
                    piece = us_ref[pl.ds(s * rows * US_PITCH + tok, rows, stride=US_PITCH), :]
                    uc_ref[slab0 + s, :, tok * LANES:(tok + 1) * LANES] = piece.astype(BF16)

    for r in range(n_rounds):
        if r + 1 < n_rounds:
            w_copy(i, r + 1).start()
        else:
            @pl.when(i + 1 < n_steps)
            def _():
                w_copy(i + 1, 0).start()
        w_copy(i, r).wait()
        if r >= 2:
            for c in o_copies(r - 2):
                c.wait()
        if r >= 1:
            for c in o_copies(r - 1):
                c.start()
        for j, t in enumerate(round_tiles(r)):
            acc = None
            for k in range(D_MODEL // MXU):
                ks = slice(k * MXU, (k + 1) * MXU)
                w_tile = w_buf[w_half(i, r), ks, j * IN_TN:(j + 1) * IN_TN]
                part = _dot(h_ref[:, ks], w_tile.astype(BF16))
                acc = part if acc is None else acc + part
            finish(t, acc)

    for c in o_copies(n_rounds - 1):
        c.start()
    for c in o_copies(n_rounds - 2) + o_copies(n_rounds - 1):
        c.wait()


def _in_proj(x2, norm_w, w_in, tm=1024):
    m = x2.shape[0]
    kernel = functools.partial(_in_proj_kernel, tm=tm, n_steps=m // tm)
    return pl.pallas_call(
        kernel,
        grid=(m // tm,),
        in_specs=[
            pl.BlockSpec((tm, D_MODEL), lambda i: (i, 0)),
            pl.BlockSpec((1, D_MODEL), lambda i: (0, 0)),
            pl.BlockSpec(memory_space=pl.ANY),
        ],
        out_specs=[
            pl.BlockSpec(memory_space=pl.ANY),
            pl.BlockSpec((N_SLABS, tm // CHUNK, CHUNK_W), lambda i: (0, i, 0)),
        ],
        out_shape=[
            jax.ShapeDtypeStruct((m, IN_W), BF16),
            jax.ShapeDtypeStruct((N_SLABS, m // CHUNK, CHUNK_W), BF16),
        ],
        scratch_shapes=[
            pltpu.VMEM((tm, D_MODEL), BF16),
            pltpu.VMEM((2, D_MODEL, IN_PER_ROUND * IN_TN), F32),
            pltpu.VMEM((2, tm, IN_PER_ROUND * IN_TN), BF16),
            pltpu.VMEM((IN_TN // LANES * (tm // CHUNK) * US_PITCH, LANES), F32),
            pltpu.SemaphoreType.DMA((2,)),
            pltpu.SemaphoreType.DMA((2, IN_PER_ROUND)),
        ],
        compiler_params=pltpu.CompilerParams(
            dimension_semantics=("arbitrary",), vmem_limit_bytes=BIG_VMEM_LIMIT),
        name="in_proj",
    )(x2, norm_w.reshape(1, D_MODEL), w_in)


def _head_norm(t, w):
    ms = jnp.mean(t * t, axis=-1, keepdims=True)
    return t * lax.rsqrt(ms + NORM_EPS) * w


_NT = (((1,), (1,)), ((), ()))
_TN = (((0,), (0,)), ((), ()))


SWA_TQ = 1024
SWA_SUB = SWA_TQ // WINDOW


def _swa_tile(sink_ref, q_ref, g_ref, kc_ref, vc_ref, kp_ref, vp_ref, qw_ref, kw_ref, o_ref,
              first_tile):
    log2e = math.log2(math.e)
    kqw = kw_ref[...] * qw_ref[...] * (log2e / math.sqrt(HEAD_DIM))
    n_col = Q_PER_KV * WINDOW
    key = lax.broadcasted_iota(jnp.int32, (WINDOW, n_col), 0)
    qry = lax.broadcasted_iota(jnp.int32, (WINDOW, n_col), 1) % WINDOW
    from_prev = key > qry
    no_prev = jnp.where(first_tile, -1e30, 0.0)
    head_of_col = lax.broadcasted_iota(jnp.int32, (1, n_col), 1) // WINDOW
    gw = Q_PER_KV * HEAD_DIM
    sel_r = lax.broadcasted_iota(jnp.int32, (8, 2 * gw), 0)
    sel_l = (lax.broadcasted_iota(jnp.int32, (8, 2 * gw), 1) % gw) // HEAD_DIM
    head_sel = jnp.where(sel_r == sel_l, 1.0, 0.0).astype(BF16)

    parts = []
    carry = {}

    def block(g, n):
        kcol = slice(g * HEAD_DIM, (g + 1) * HEAD_DIM)
        gcols = slice(g * Q_PER_KV * HEAD_DIM, (g + 1) * Q_PER_KV * HEAD_DIM)
        if n == 0:
            sink = jnp.zeros((1, n_col), F32)
            for r in range(Q_PER_KV):
                sink = jnp.where(head_of_col == r, sink_ref[g * Q_PER_KV + r] * log2e, sink)
            carry['sink'] = sink
            carry['k'] = _head_norm(kp_ref[:, kcol].astype(F32), kqw).astype(BF16)
            carry['v'] = vp_ref[:, kcol]
        sink, k_prev, v_prev = carry['sink'], carry['k'], carry['v']
        rows = slice(n * WINDOW, (n + 1) * WINDOW)
        k_cur = _head_norm(kc_ref[rows, kcol].astype(F32), kqw).astype(BF16)
        v_cur = vc_ref[rows, kcol]
        k_ctx = jnp.concatenate([k_prev, k_cur], axis=0)
        v_ctx = jnp.concatenate([v_prev, v_cur], axis=0)

        qg = q_ref[rows, gcols]
        qf = qg.astype(F32)
        q_t = qf.T.astype(BF16)
        q_t = jnp.concatenate([q_t[r * HEAD_DIM:(r + 1) * HEAD_DIM] for r in range(Q_PER_KV)],
                              axis=1)
        q2 = qf * qf
        q2_hi = q2.astype(BF16)
        q2_lo = (q2 - q2_hi.astype(F32)).astype(BF16)
        ssq = lax.dot_general(head_sel, jnp.concatenate([q2_hi, q2_lo], axis=1), _NT,
                              preferred_element_type=F32)
        rms = lax.rsqrt(ssq * (1.0 / HEAD_DIM) + NORM_EPS)
        rms_q = jnp.concatenate([rms[r:r + 1] for r in range(Q_PER_KV)], axis=1)

        s = _dot(k_ctx, q_t)
        s_prev = s[:WINDOW] + no_prev if n == 0 else s[:WINDOW]
        s = jnp.where(from_prev, s_prev, s[WINDOW:]) * rms_q
        mx = jnp.maximum(jnp.max(s, axis=0, keepdims=True), sink)
        p = jnp.exp2(s - mx)
        den = jnp.sum(p, axis=0, keepdims=True) + jnp.exp2(sink - mx)
        p_ctx = jnp.concatenate([jnp.where(from_prev, p, 0.0), jnp.where(from_prev, 0.0, p)],
                                axis=0).astype(BF16)
        o_t = lax.dot_general(v_ctx, p_ctx, _TN, preferred_element_type=F32) * (1.0 / den)
        halves = [jnp.concatenate([o_t[:, (2 * h) * WINDOW:(2 * h + 1) * WINDOW],
                                   o_t[:, (2 * h + 1) * WINDOW:(2 * h + 2) * WINDOW]], axis=0).T
                  for h in range(Q_PER_KV // 2)]
        og = jnp.concatenate(halves, axis=1)
        gate = g_ref[rows, gcols].astype(F32)
        o_ref[rows, gcols] = (og * (gate * jax.nn.sigmoid(gate))).astype(BF16)
        carry['k'], carry['v'] = k_cur, v_cur

    for g in range(N_KV_HEADS):
        for n in range(SWA_SUB):
            parts.append(functools.partial(block, g, n))
    return parts


def _swa_kernel(*refs, tiles_per_seq):
    for block in _swa_tile(*refs, pl.program_id(0) % tiles_per_seq == 0):
        block()


def _swa(proj, sinks, q_norm_w, k_norm_w, batch, seq):
    m = batch * seq
    cur = lambda col: (lambda s: (s, col))
    prev = lambda col: (lambda s: (jnp.maximum(s * SWA_SUB - 1, 0), col))
    kernel = functools.partial(_swa_kernel, tiles_per_seq=seq // SWA_TQ)
    return pl.pallas_call(
        kernel,
        grid=(m // SWA_TQ,),
        in_specs=[
            pl.BlockSpec(memory_space=pltpu.SMEM),
            pl.BlockSpec((SWA_TQ, ATTN_W), cur(COL_Q // ATTN_W)),
            pl.BlockSpec((SWA_TQ, ATTN_W), cur(COL_GATE // ATTN_W)),
            pl.BlockSpec((SWA_TQ, KV_W), cur(COL_K // KV_W)),
            pl.BlockSpec((SWA_TQ, KV_W), cur(COL_V // KV_W)),
            pl.BlockSpec((WINDOW, KV_W), prev(COL_K // KV_W)),
            pl.BlockSpec((WINDOW, KV_W), prev(COL_V // KV_W)),
            pl.BlockSpec((1, HEAD_DIM), lambda s: (0, 0)),
            pl.BlockSpec((1, HEAD_DIM), lambda s: (0, 0)),
        ],
        out_specs=pl.BlockSpec((SWA_TQ, ATTN_W), cur(0)),
        out_shape=jax.ShapeDtypeStruct((m, ATTN_W), BF16),
        compiler_params=pltpu.CompilerParams(
            dimension_semantics=("arbitrary",), vmem_limit_bytes=VMEM_LIMIT),
        name="swa",
    )(sinks, proj, proj, proj, proj, proj, proj,
      q_norm_w.reshape(1, HEAD_DIM), k_norm_w.reshape(1, HEAD_DIM))


def _slab_powers(ar, ai, dt, btr, bti):
    dta_re, dta_im = dt * ar, dt * ai
    mag = jnp.exp(dta_re)
    ab_re, ab_im = mag * jnp.cos(dta_im), mag * jnp.sin(dta_im)
    pw = [(jnp.ones_like(ar), jnp.zeros_like(ar))]
    for _ in range(CHUNK):
        pr, pi = pw[-1]
        pw.append((pr * ab_re - pi * ab_im, pr * ab_im + pi * ab_re))
    den = ar * ar + ai * ai
    num_re, num_im = ab_re - 1.0, ab_im
    cf_re = (num_re * ar + num_im * ai) / den
    cf_im = (num_im * ar - num_re * ai) / den
    bb_re = cf_re * btr - cf_im * bti
    bb_im = cf_re * bti + cf_im * btr
    abb = [(pr * bb_re - pi * bb_im, pr * bb_im + pi * bb_re) for pr, pi in pw[:CHUNK]]
    return pw, abb


def _build_state_in(abb, p_s):
    row_g = lax.broadcasted_iota(jnp.int32, (LANES, STATE), 0) // GROUP
    even = row_g % 2 == 0
    pair_of_row = lax.broadcasted_iota(jnp.int32, (LANES, 2 * LANES), 0) // (2 * GROUP)
    for lag in range(CHUNK):
        t = CHUNK - 1 - lag
        re, im = abb[lag]
        blk = jnp.concatenate([jnp.where(even, re, 0.0), jnp.where(even, 0.0, re),
                               jnp.where(even, im, 0.0), jnp.where(even, 0.0, im)], axis=1)
        for k in range(PAIRS):
            p_s[k, t * LANES:(t + 1) * LANES, :] = jnp.where(pair_of_row == k, blk, 0.0).astype(BF16)


def _build_state_out(pw, cr, ci, q_s):
    lane_g = lax.broadcasted_iota(jnp.int32, (STATE, LANES), 1) // GROUP
    for t in range(CHUNK):
        pr, pi = pw[t + 1]
        cat = jnp.concatenate([cr * pr - ci * pi, -(cr * pi + ci * pr)], axis=1)
        cat_t = cat.T
        for k in range(PAIRS):
            for ri in range(2):
                for half in range(2):
                    piece = jnp.where(lane_g == 2 * k + half, cat_t[ri * STATE:(ri + 1) * STATE], 0.0)
                    r0 = ri * LANES + half * STATE
                    q_s[k, r0:r0 + STATE, t * LANES:(t + 1) * LANES] = piece.astype(BF16)


def _build_toeplitz(abb, cr, ci, d_row, w_s):
    def split(v):
        hi = v.astype(BF16)
        return hi, (v - hi.astype(F32)).astype(BF16)

    rhs_hi, rhs_lo = split(jnp.concatenate([cr, -ci], axis=1))
    rhs = jnp.concatenate([rhs_hi, rhs_lo, rhs_hi], axis=1)
    row = lax.broadcasted_iota(jnp.int32, (LANES, LANES), 0)
    col = lax.broadcasted_iota(jnp.int32, (LANES, LANES), 1)
    same_group = (row // GROUP) == (col // GROUP)
    taps = []
    for lag in range(CHUNK):
        lhs_hi, lhs_lo = split(jnp.concatenate(abb[lag], axis=1))
        k = lax.dot_general(jnp.concatenate([lhs_hi, lhs_hi, lhs_lo], axis=1), rhs,
                            (((1,), (1,)), ((), ())), preferred_element_type=F32)
        k = jnp.where(same_group, k, 0.0)
        if lag == 0:
            k = k + jnp.where(row == col, d_row, 0.0)
        taps.append(k.astype(BF16))
    zero = jnp.zeros((LANES, LANES), BF16)
    for d in range(N_TILES):
        w_s[d, :LANES, :LANES] = taps[2 * d]
        w_s[d, :LANES, LANES:] = taps[2 * d + 1]
        w_s[d, LANES:, :LANES] = taps[2 * d - 1] if d > 0 else zero
        w_s[d, LANES:, LANES:] = taps[2 * d]


def _ssm_kernel(u_ref, ar_ref, ai_ref, dt_ref, btr_ref, bti_ref, cr_ref, ci_ref, d_ref,
                dtare_ref, dtaim_ref, w0_ref, w1_ref, w2_ref, w3_ref, y_ref, c0_ref, c1_ref, c2_ref,
                c3_ref, xre_s, xim_s, p_s, q_s, w_s, *, batch, n_chunks):
    step = pl.program_id(0)
    for src, dst in ((w0_ref, c0_ref), (w1_ref, c1_ref), (w2_ref, c2_ref), (w3_ref, c3_ref)):
        dst[...] = src[...].astype(BF16)
    rows8 = batch * PAIRS
    slab_rows = n_chunks * rows8

    def state_rows(slab, b, k):
        return pl.ds(slab * slab_rows + b * PAIRS + k, n_chunks, stride=rows8)

    def per_row(v):
        v = jnp.broadcast_to(v, (SLAB_G, STATE))
        return jnp.broadcast_to(v[:, None, :], (SLAB_G, GROUP, STATE)).reshape(LANES, STATE)

    def powers():
        return _slab_powers(per_row(ar_ref[...]), per_row(ai_ref[...]),
                            per_row(jnp.exp(dt_ref[...])), btr_ref[...], bti_ref[...])

    @pl.when(step < N_SLABS)
    def _():
        _, abb = powers()
        _build_state_in(abb, p_s)
        u = u_ref[...]
        for k in range(PAIRS):
            x = _dot(u, p_s[k])
            for b in range(batch):
                xb = x[b * n_chunks:(b + 1) * n_chunks]
                xre_s[state_rows(step, b, k), :] = xb[:, :LANES]
                xim_s[state_rows(step, b, k), :] = xb[:, LANES:]

    @pl.when(step == N_SLABS)
    def _():
        dta_re = jnp.concatenate([dtare_ref[...]] * batch, axis=1).reshape(N_SLABS * rows8, LANES)
        dta_im = jnp.concatenate([dtaim_ref[...]] * batch, axis=1).reshape(N_SLABS * rows8, LANES)
        mag = jnp.exp(CHUNK * dta_re)
        a_re, a_im = mag * jnp.cos(CHUNK * dta_im), mag * jnp.sin(CHUNK * dta_im)

        def body(c, carry):
            s_re, s_im = carry
            x_re, x_im = [], []
            for sl in range(N_SLABS):
                off = pl.multiple_of(sl * slab_rows + c * rows8, rows8)
                x_re.append(xre_s[pl.ds(off, rows8), :])
                x_im.append(xim_s[pl.ds(off, rows8), :])
                xre_s[pl.ds(off, rows8), :] = s_re[sl * rows8:(sl + 1) * rows8]
                xim_s[pl.ds(off, rows8), :] = s_im[sl * rows8:(sl + 1) * rows8]
            x_re = jnp.concatenate(x_re, axis=0)
            x_im = jnp.concatenate(x_im, axis=0)
            return (a_re * s_re - a_im * s_im + x_re, a_re * s_im + a_im * s_re + x_im)

        zero = jnp.zeros((N_SLABS * rows8, LANES), F32)
        lax.fori_loop(0, n_chunks, body, (zero, zero))

    @pl.when(step > N_SLABS)
    def _():
        slab = step - (N_SLABS + 1)
        pw, abb = powers()
        cr, ci = cr_ref[...], ci_ref[...]
        _build_state_out(pw, cr, ci, q_s)
        _build_toeplitz(abb, cr, ci, d_ref[...], w_s)
        u = u_ref[...]
        y_state = None
        for k in range(PAIRS):
            s_in = jnp.concatenate(
                [jnp.concatenate([xre_s[state_rows(slab, b, k), :], xim_s[state_rows(slab, b, k), :]],
                                 axis=1) for b in range(batch)], axis=0).astype(BF16)
            part = _dot(s_in, q_s[k])
            y_state = part if y_state is None else y_state + part
        for t2 in range(N_TILES):
            acc = y_state[:, t2 * MXU:(t2 + 1) * MXU]
            for t1 in range(t2 + 1):
                acc = acc + _dot(u[:, t1 * MXU:(t1 + 1) * MXU], w_s[t2 - t1])
            y_ref[:, t2 * MXU:(t2 + 1) * MXU] = jax.nn.gelu(acc).astype(BF16)


N_CAST = 16


def _ssm(u_chunks, A_re, A_im, log_dt, B_re, B_im, C_re, C_im, D_skip, batch, tail_weights):
    _, rows, _ = u_chunks.shape
    n_chunks = rows // batch
    n_steps = 2 * N_SLABS + 1
    kernel = functools.partial(_ssm_kernel, batch=batch, n_chunks=n_chunks)

    dt = jnp.exp(log_dt)[:, None]
    btr = B_re.transpose(0, 2, 1).reshape(SSM_W, STATE)
    bti = B_im.transpose(0, 2, 1).reshape(SSM_W, STATE)
    cr, ci = C_re.reshape(SSM_W, STATE), C_im.reshape(SSM_W, STATE)
    d_rows = D_skip.reshape(N_SLABS, 1, LANES)
    dta_re = (dt * A_re).reshape(N_SLABS, PAIRS, LANES)
    dta_im = (dt * A_im).reshape(N_SLABS, PAIRS, LANES)

    def slab_in(s):
        return jnp.where(s <= N_SLABS, jnp.minimum(s, N_SLABS - 1), s - (N_SLABS + 1))

    def slab_out(s):
        return jnp.maximum(s - (N_SLABS + 1), 0)

    def cast_spec(w):
        return pl.BlockSpec((w.shape[0] // N_CAST, w.shape[1]),
                            lambda s: (jnp.minimum(s, N_CAST - 1), 0))

    coeff = pl.BlockSpec((LANES, STATE), lambda s: (slab_in(s), 0))
    whole = pl.BlockSpec((N_SLABS, PAIRS, LANES), lambda s: (0, 0, 0))
    return pl.pallas_call(
        kernel,
        grid=(n_steps,),
        in_specs=[
            pl.BlockSpec((None, rows, CHUNK_W), lambda s: (slab_in(s), 0, 0)),
            pl.BlockSpec((SLAB_G, STATE), lambda s: (slab_in(s), 0)),
            pl.BlockSpec((SLAB_G, STATE), lambda s: (slab_in(s), 0)),
            pl.BlockSpec((SLAB_G, 1), lambda s: (slab_in(s), 0)),
            coeff, coeff, coeff, coeff,
            pl.BlockSpec((None, 1, LANES), lambda s: (slab_in(s), 0, 0)),
            whole, whole,
        ] + [cast_spec(w) for w in tail_weights],
        out_specs=[pl.BlockSpec((None, rows, CHUNK_W), lambda s: (slab_out(s), 0, 0))]
        + [cast_spec(w) for w in tail_weights],
        out_shape=[jax.ShapeDtypeStruct(u_chunks.shape, BF16)]
        + [jax.ShapeDtypeStruct(w.shape, BF16) for w in tail_weights],
        scratch_shapes=[
            pltpu.VMEM((N_SLABS * n_chunks * batch * PAIRS, LANES), F32),
            pltpu.VMEM((N_SLABS * n_chunks * batch * PAIRS, LANES), F32),
            pltpu.VMEM((PAIRS, CHUNK_W, 2 * LANES), BF16),
            pltpu.VMEM((PAIRS, 2 * LANES, CHUNK_W), BF16),
            pltpu.VMEM((N_TILES, MXU, MXU), BF16),
        ],
        compiler_params=pltpu.CompilerParams(
            dimension_semantics=("arbitrary",), vmem_limit_bytes=VMEM_LIMIT),
        name="ssm",
    )(u_chunks, A_re, A_im, log_dt[:, None], btr, bti, cr, ci, d_rows, dta_re, dta_im,
      *tail_weights)


def _tail_kernel(ag_ref, yc_ref, z_ref, ga_ref, gs_ref, x_ref, wa_ref, wg_ref, bg_ref, ws_ref,
                 wo_ref, o_ref, ys_ref, *, tm):
    rows = tm // CHUNK
    for s in range(N_SLABS):
        for t in range(CHUNK):
            ys_ref[pl.ds(s * rows * US_PITCH + t, rows, stride=US_PITCH), :] = (
                yc_ref[s, :, t * LANES:(t + 1) * LANES].astype(F32))
    y_g = jnp.concatenate(
        [jnp.concatenate([ys_ref[(s * rows + c) * US_PITCH:(s * rows + c) * US_PITCH + CHUNK, :]
                          for c in range(rows)], axis=0) for s in range(N_SLABS)],
        axis=1).astype(BF16)

    y_a = _dot(ag_ref[...], wa_ref[...])
    glu = _dot(y_g, wg_ref[...]) + bg_ref[...]
    z = z_ref[...].astype(F32)
    t = glu[:, :SSM_W] * jax.nn.sigmoid(glu[:, SSM_W:]) * (z * jax.nn.sigmoid(z))
    y_s = _dot(t.astype(BF16), ws_ref[...])
    merged = (jax.nn.sigmoid(ga_ref[...].astype(F32)) * y_a
              + jax.nn.sigmoid(gs_ref[...].astype(F32)) * y_s)
    o_ref[...] = x_ref[...] + _dot(merged.astype(BF16), wo_ref[...])


def _tail(attn_g, y_chunks, proj, x2, w_attn, w_glu, b_glu, w_ssm, w_out, tm=256):
    m = x2.shape[0]
    row = lambda blk: (lambda i: (i, blk))
    const = lambda i: (0, 0)
    resident = functools.partial(pl.BlockSpec, index_map=const, pipeline_mode=pl.Buffered(1))
    kernel = functools.partial(_tail_kernel, tm=tm)
    return pl.pallas_call(
        kernel,
        grid=(m // tm,),
        in_specs=[
            pl.BlockSpec((tm, ATTN_W), row(0)),
            pl.BlockSpec((N_SLABS, tm // CHUNK, CHUNK_W), lambda i: (0, i, 0)),
            pl.BlockSpec((tm, SSM_W), row(COL_Z // SSM_W)),
            pl.BlockSpec((tm, D_MODEL), row(COL_GA // D_MODEL)),
            pl.BlockSpec((tm, D_MODEL), row(COL_GS // D_MODEL)),
            pl.BlockSpec((tm, D_MODEL), row(0)),
            resident((ATTN_W, D_MODEL)),
            resident((SSM_W, 2 * SSM_W)),
            resident((1, 2 * SSM_W)),
            resident((SSM_W, D_MODEL)),
            resident((D_MODEL, D_MODEL)),
        ],
        out_specs=pl.BlockSpec((tm, D_MODEL), row(0)),
        out_shape=jax.ShapeDtypeStruct((m, D_MODEL), F32),
        scratch_shapes=[pltpu.VMEM((N_SLABS * (tm // CHUNK) * US_PITCH, LANES), F32)],
        compiler_params=pltpu.CompilerParams(
            dimension_semantics=("arbitrary",), vmem_limit_bytes=VMEM_LIMIT),
        name="tail",
    )(attn_g, y_chunks, proj, proj, proj, x2, w_attn, w_glu, b_glu.reshape(1, -1), w_ssm, w_out)


def kernel(x, norm_w, w_in, q_norm_w, k_norm_w, sinks, w_attn_proj, A_re, A_im, log_dt, B_re, B_im,
           C_re, C_im, D_skip, w_glu, b_glu, w_ssm_proj, w_out):
    batch, seq, _ = x.shape
    m = batch * seq
    x2 = x.reshape(m, D_MODEL)

    proj, u_chunks = _in_proj(x2, norm_w, w_in)

    attn_g = _swa(proj, sinks, q_norm_w, k_norm_w, batch, seq)

    y_chunks, w_attn_bf, w_glu_bf, w_ssm_bf, w_out_bf = _ssm(
        u_chunks, A_re, A_im, log_dt, B_re, B_im, C_re, C_im, D_skip, batch,
        (w_attn_proj, w_glu, w_ssm_proj, w_out))

    out = _tail(attn_g, y_chunks, proj, x2, w_attn_bf, w_glu_bf, b_glu, w_ssm_bf, w_out_bf)
    return out.reshape(batch, seq, D_MODEL)
```

```python
import functools
import math

import jax
import jax.numpy as jnp
from jax import lax
from jax.experimental import pallas as pl
from jax.experimental.pallas import tpu as pltpu

F32 = jnp.float32
BF16 = jnp.bfloat16

D_MODEL = 2048
HEAD_DIM = 64
N_Q_HEADS = 16
N_KV_HEADS = 4
Q_PER_KV = 4
ATTN_W = N_Q_HEADS * HEAD_DIM
KV_W = N_KV_HEADS * HEAD_DIM
WINDOW = 128
SSM_W = D_MODEL // 2
GROUP = 16
N_GROUPS = SSM_W // GROUP
STATE = 64
NORM_EPS = 1e-6

COL_Q = 0
COL_GATE = COL_Q + ATTN_W
COL_U = COL_GATE + ATTN_W
COL_Z = COL_U + SSM_W
COL_GA = COL_Z + SSM_W
COL_GS = COL_GA + D_MODEL
COL_K = COL_GS + D_MODEL
COL_V = COL_K + KV_W
IN_W = COL_V + KV_W

LANES = 128
MXU = 256
CHUNK = 16
SLAB_G = LANES // GROUP
N_SLABS = N_GROUPS // SLAB_G
PAIRS = SLAB_G // 2
CHUNK_W = CHUNK * LANES
N_TILES = CHUNK_W // MXU
VMEM_LIMIT = 48 * 1024 * 1024


def _dot(a, b):
    return jnp.dot(a, b, preferred_element_type=F32)


IN_TN = 512
N_IN_TILES = IN_W // IN_TN
SRC_KV_TILE = (ATTN_W) // IN_TN
SRC_U_TILE0 = (ATTN_W + 2 * KV_W + ATTN_W) // IN_TN


US_PITCH = CHUNK if (CHUNK // 8) % 2 else CHUNK + 8
X_ROWS = 128
IN_PER_ROUND = 2


def _dst_tile(src_tile):
    if src_tile < SRC_KV_TILE:
        return src_tile
    if src_tile == SRC_KV_TILE:
        return N_IN_TILES - 1
    return src_tile - 1


def _in_proj_kernel(nw_ref, x_hbm, w_hbm, o_hbm, uc_ref, h_ref, x_buf, w_buf, o_buf, us_ref, x_sem,
                    w_sem, o_sem, *, tm, n_steps):
    i = pl.program_id(0)
    rows = tm // CHUNK
    n_sl = IN_TN // LANES
    n_rounds = pl.cdiv(N_IN_TILES, IN_PER_ROUND)
    assert n_rounds % 2 == 1
    n_xc = tm // X_ROWS
    assert n_xc < n_rounds
    cur, nxt = i % 2, (i + 1) % 2
    nxt_tile = jnp.minimum(i + 1, n_steps - 1)

    def x_copy(tile, c):
        src = x_hbm.at[pl.ds(pl.multiple_of(tile * tm + c * X_ROWS, X_ROWS), X_ROWS), :]
        return pltpu.make_async_copy(src, x_buf.at[c % 2], x_sem.at[c % 2])

    def norm_chunk(c, dst):
        x = x_buf[c % 2]
        ms = jnp.mean(x * x, axis=-1, keepdims=True)
        h_ref[dst, c * X_ROWS:(c + 1) * X_ROWS, :] = (
            x * lax.rsqrt(ms + NORM_EPS) * nw_ref[...]).astype(BF16)

    def round_tiles(r):
        return [t for t in range(r * IN_PER_ROUND, (r + 1) * IN_PER_ROUND) if 0 <= t < N_IN_TILES]

    def w_half(step, r):
        return (step + r) % 2

    def w_copy(step, r):
        tiles = round_tiles(r)
        width = len(tiles) * IN_TN
        half = w_half(step, r)
        return pltpu.make_async_copy(w_hbm.at[:, pl.ds(tiles[0] * IN_TN, width)],
                                     w_buf.at[half, :, pl.ds(0, width)], w_sem.at[half])

    def o_copies(r):
        tiles = round_tiles(r)
        runs, start = [], 0
        for j in range(1, len(tiles) + 1):
            if j == len(tiles) or _dst_tile(tiles[j]) != _dst_tile(tiles[j - 1]) + 1:
                runs.append((start, j - start))
                start = j
        row0 = pl.multiple_of(i * tm, tm)
        return [pltpu.make_async_copy(
            o_buf.at[r % 2, :, pl.ds(j0 * IN_TN, n * IN_TN)],
            o_hbm.at[pl.ds(row0, tm), pl.ds(_dst_tile(tiles[j0]) * IN_TN, n * IN_TN)],
            o_sem.at[r % 2, k]) for k, (j0, n) in enumerate(runs)]

    @pl.when(i == 0)
    def _():
        w_copy(0, 0).start()
        x_copy(0, 0).start()
        for c in range(n_xc):
            if c + 1 < n_xc:
                x_copy(0, c + 1).start()
            x_copy(0, c).wait()
            norm_chunk(c, 0)

    def finish(t, acc):
        r, j = divmod(t, IN_PER_ROUND)
        o_buf[r % 2, :, j * IN_TN:(j + 1) * IN_TN] = acc.astype(BF16)
        if SRC_U_TILE0 <= t < SRC_U_TILE0 + SSM_W // IN_TN:
            slab0 = (t - SRC_U_TILE0) * n_sl
            for s in range(n_sl):
                for c in range(rows):
                    r0 = (s * rows + c) * US_PITCH
                    us_ref[r0:r0 + CHUNK, :] = acc[c * CHUNK:(c + 1) * CHUNK, s * LANES:(s + 1) * LANES]
            for s in range(n_sl):
                for tok in range(CHUNK):
                    piece = us_ref[pl.ds(s * rows * US_PITCH + tok, rows, stride=US_PITCH), :]
                    uc_ref[slab0 + s, :, tok * LANES:(tok + 1) * LANES] = piece.astype(BF16)

    for r in range(n_rounds):
        if r + 1 < n_rounds:
            w_copy(i, r + 1).start()
        else:
            @pl.when(i + 1 < n_steps)
            def _():
                w_copy(i + 1, 0).start()
        w_copy(i, r).wait()
        if r >= 2:
            for c in o_copies(r - 2):
                c.wait()
        if r >= 1:
            for c in o_copies(r - 1):
                c.start()
        if r < n_xc:
            x_copy(nxt_tile, r).start()
        if 1 <= r <= n_xc:
            x_copy(nxt_tile, r - 1).wait()
        for j, t in enumerate(round_tiles(r)):
            acc = None
            for k in range(D_MODEL // MXU):
                ks = slice(k * MXU, (k + 1) * MXU)
                w_tile = w_buf[w_half(i, r), ks, j * IN_TN:(j + 1) * IN_TN]
                part = _dot(h_ref[cur, :, ks], w_tile.astype(BF16))
                acc = part if acc is None else acc + part
            finish(t, acc)
        if 1 <= r <= n_xc:
            norm_chunk(r - 1, nxt)

    for c in o_copies(n_rounds - 1):
        c.start()
    for c in o_copies(n_rounds - 2) + o_copies(n_rounds - 1):
        c.wait()


def _in_proj(x2, norm_w, w_in, tm=1024):
    m = x2.shape[0]
    kernel = functools.partial(_in_proj_kernel, tm=tm, n_steps=m // tm)
    return pl.pallas_call(
        kernel,
        grid=(m // tm,),
        in_specs=[
            pl.BlockSpec((1, D_MODEL), lambda i: (0, 0)),
            pl.BlockSpec(memory_space=pl.ANY),
            pl.BlockSpec(memory_space=pl.ANY),
        ],
        out_specs=[
            pl.BlockSpec(memory_space=pl.ANY),
            pl.BlockSpec((N_SLABS, tm // CHUNK, CHUNK_W), lambda i: (0, i, 0)),
        ],
        out_shape=[
            jax.ShapeDtypeStruct((m, IN_W), BF16),
            jax.ShapeDtypeStruct((N_SLABS, m // CHUNK, CHUNK_W), BF16),
        ],
        scratch_shapes=[
            pltpu.VMEM((2, tm, D_MODEL), BF16),
            pltpu.VMEM((2, X_ROWS, D_MODEL), F32),
            pltpu.VMEM((2, D_MODEL, IN_PER_ROUND * IN_TN), F32),
            pltpu.VMEM((2, tm, IN_PER_ROUND * IN_TN), BF16),
            pltpu.VMEM((IN_TN // LANES * (tm // CHUNK) * US_PITCH, LANES), F32),
            pltpu.SemaphoreType.DMA((2,)),
            pltpu.SemaphoreType.DMA((2,)),
            pltpu.SemaphoreType.DMA((2, IN_PER_ROUND)),
        ],
        compiler_params=pltpu.CompilerParams(
            dimension_semantics=("arbitrary",), vmem_limit_bytes=VMEM_LIMIT),
        name="in_proj",
    )(norm_w.reshape(1, D_MODEL), x2, w_in)


def _head_norm(t, w):
    ms = jnp.mean(t * t, axis=-1, keepdims=True)
    return t * lax.rsqrt(ms + NORM_EPS) * w


_NT = (((1,), (1,)), ((), ()))
_TN = (((0,), (0,)), ((), ()))


SWA_TQ = 1024
SWA_SUB = SWA_TQ // WINDOW


def _swa_tile(sink_ref, q_ref, g_ref, kc_ref, vc_ref, kp_ref, vp_ref, qw_ref, kw_ref, o_ref,
              first_tile):
    log2e = math.log2(math.e)
    kqw = kw_ref[...] * qw_ref[...] * (log2e / math.sqrt(HEAD_DIM))
    n_col = Q_PER_KV * WINDOW
    key = lax.broadcasted_iota(jnp.int32, (WINDOW, n_col), 0)
    qry = lax.broadcasted_iota(jnp.int32, (WINDOW, n_col), 1) % WINDOW
    from_prev = key > qry
    no_prev = jnp.where(first_tile, -1e30, 0.0)
    head_of_col = lax.broadcasted_iota(jnp.int32, (1, n_col), 1) // WINDOW
    gw = Q_PER_KV * HEAD_DIM
    sel_r = lax.broadcasted_iota(jnp.int32, (8, 2 * gw), 0)
    sel_l = (lax.broadcasted_iota(jnp.int32, (8, 2 * gw), 1) % gw) // HEAD_DIM
    head_sel = jnp.where(sel_r == sel_l, 1.0, 0.0).astype(BF16)

    parts = []
    carry = {}

    def block(g, n):
        kcol = slice(g * HEAD_DIM, (g + 1) * HEAD_DIM)
        gcols = slice(g * Q_PER_KV * HEAD_DIM, (g + 1) * Q_PER_KV * HEAD_DIM)
        if n == 0:
            sink = jnp.zeros((1, n_col), F32)
            for r in range(Q_PER_KV):
                sink = jnp.where(head_of_col == r, sink_ref[g * Q_PER_KV + r] * log2e, sink)
            carry['sink'] = sink
            carry['k'] = _head_norm(kp_ref[:, kcol].astype(F32), kqw).astype(BF16)
            carry['v'] = vp_ref[:, kcol]
        sink, k_prev, v_prev = carry['sink'], carry['k'], carry['v']
        rows = slice(n * WINDOW, (n + 1) * WINDOW)
        k_cur = _head_norm(kc_ref[rows, kcol].astype(F32), kqw).astype(BF16)
        v_cur = vc_ref[rows, kcol]
        k_ctx = jnp.concatenate([k_prev, k_cur], axis=0)
        v_ctx = jnp.concatenate([v_prev, v_cur], axis=0)

        qg = q_ref[rows, gcols]
        qf = qg.astype(F32)
        q_t = qf.T.astype(BF16)
        q_t = jnp.concatenate([q_t[r * HEAD_DIM:(r + 1) * HEAD_DIM] for r in range(Q_PER_KV)],
                              axis=1)
        q2 = qf * qf
        q2_hi = q2.astype(BF16)
        q2_lo = (q2 - q2_hi.astype(F32)).astype(BF16)
        ssq = lax.dot_general(head_sel, jnp.concatenate([q2_hi, q2_lo], axis=1), _NT,
                              preferred_element_type=F32)
        rms = lax.rsqrt(ssq * (1.0 / HEAD_DIM) + NORM_EPS)
        rms_q = jnp.concatenate([rms[r:r + 1] for r in range(Q_PER_KV)], axis=1)

        s = _dot(k_ctx, q_t)
        s_prev = s[:WINDOW] + no_prev if n == 0 else s[:WINDOW]
        s = jnp.where(from_prev, s_prev, s[WINDOW:]) * rms_q
        mx = jnp.maximum(jnp.max(s, axis=0, keepdims=True), sink)
        p = jnp.exp2(s - mx)
        den = jnp.sum(p, axis=0, keepdims=True) + jnp.exp2(sink - mx)
        p_ctx = jnp.concatenate([jnp.where(from_prev, p, 0.0), jnp.where(from_prev, 0.0, p)],
                                axis=0).astype(BF16)
        o_t = lax.dot_general(v_ctx, p_ctx, _TN, preferred_element_type=F32) * (1.0 / den)
        halves = [jnp.concatenate([o_t[:, (2 * h) * WINDOW:(2 * h + 1) * WINDOW],
                                   o_t[:, (2 * h + 1) * WINDOW:(2 * h + 2) * WINDOW]], axis=0).T
                  for h in range(Q_PER_KV // 2)]
        og = jnp.concatenate(halves, axis=1)
        gate = g_ref[rows, gcols].astype(F32)
        o_ref[rows, gcols] = (og * (gate * jax.nn.sigmoid(gate))).astype(BF16)
        carry['k'], carry['v'] = k_cur, v_cur

    for g in range(N_KV_HEADS):
        for n in range(SWA_SUB):
            parts.append(functools.partial(block, g, n))
    return parts


def _swa_kernel(*refs, tiles_per_seq):
    for block in _swa_tile(*refs, pl.program_id(0) % tiles_per_seq == 0):
        block()


def _swa(proj, sinks, q_norm_w, k_norm_w, batch, seq):
    m = batch * seq
    cur = lambda col: (lambda s: (s, col))
    prev = lambda col: (lambda s: (jnp.maximum(s * SWA_SUB - 1, 0), col))
    kernel = functools.partial(_swa_kernel, tiles_per_seq=seq // SWA_TQ)
    return pl.pallas_call(
        kernel,
        grid=(m // SWA_TQ,),
        in_specs=[
            pl.BlockSpec(memory_space=pltpu.SMEM),
            pl.BlockSpec((SWA_TQ, ATTN_W), cur(COL_Q // ATTN_W)),
            pl.BlockSpec((SWA_TQ, ATTN_W), cur(COL_GATE // ATTN_W)),
            pl.BlockSpec((SWA_TQ, KV_W), cur(COL_K // KV_W)),
            pl.BlockSpec((SWA_TQ, KV_W), cur(COL_V // KV_W)),
            pl.BlockSpec((WINDOW, KV_W), prev(COL_K // KV_W)),
            pl.BlockSpec((WINDOW, KV_W), prev(COL_V // KV_W)),
            pl.BlockSpec((1, HEAD_DIM), lambda s: (0, 0)),
            pl.BlockSpec((1, HEAD_DIM), lambda s: (0, 0)),
        ],
        out_specs=pl.BlockSpec((SWA_TQ, ATTN_W), cur(0)),
        out_shape=jax.ShapeDtypeStruct((m, ATTN_W), BF16),
        compiler_params=pltpu.CompilerParams(
            dimension_semantics=("arbitrary",), vmem_limit_bytes=VMEM_LIMIT),
        name="swa",
    )(sinks, proj, proj, proj, proj, proj, proj,
      q_norm_w.reshape(1, HEAD_DIM), k_norm_w.reshape(1, HEAD_DIM))


def _slab_powers(ar, ai, dt, btr, bti):
    dta_re, dta_im = dt * ar, dt * ai
    mag = jnp.exp(dta_re)
    ab_re, ab_im = mag * jnp.cos(dta_im), mag * jnp.sin(dta_im)
    pw = [(jnp.ones_like(ar), jnp.zeros_like(ar))]
    for _ in range(CHUNK):
        pr, pi = pw[-1]
        pw.append((pr * ab_re - pi * ab_im, pr * ab_im + pi * ab_re))
    den = ar * ar + ai * ai
    num_re, num_im = ab_re - 1.0, ab_im
    cf_re = (num_re * ar + num_im * ai) / den
    cf_im = (num_im * ar - num_re * ai) / den
    bb_re = cf_re * btr - cf_im * bti
    bb_im = cf_re * bti + cf_im * btr
    abb = [(pr * bb_re - pi * bb_im, pr * bb_im + pi * bb_re) for pr, pi in pw[:CHUNK]]
    return pw, abb


def _build_state_in(abb, p_s):
    row_g = lax.broadcasted_iota(jnp.int32, (LANES, STATE), 0) // GROUP
    even = row_g % 2 == 0
    pair_of_row = lax.broadcasted_iota(jnp.int32, (LANES, 2 * LANES), 0) // (2 * GROUP)
    for lag in range(CHUNK):
        t = CHUNK - 1 - lag
        re, im = abb[lag]
        blk = jnp.concatenate([jnp.where(even, re, 0.0), jnp.where(even, 0.0, re),
                               jnp.where(even, im, 0.0), jnp.where(even, 0.0, im)], axis=1)
        for k in range(PAIRS):
            p_s[k, t * LANES:(t + 1) * LANES, :] = jnp.where(pair_of_row == k, blk, 0.0).astype(BF16)


def _build_state_out(pw, cr, ci, q_s):
    lane_g = lax.broadcasted_iota(jnp.int32, (STATE, LANES), 1) // GROUP
    for t in range(CHUNK):
        pr, pi = pw[t + 1]
        cat = jnp.concatenate([cr * pr - ci * pi, -(cr * pi + ci * pr)], axis=1)
        cat_t = cat.T
        for k in range(PAIRS):
            for ri in range(2):
                for half in range(2):
                    piece = jnp.where(lane_g == 2 * k + half, cat_t[ri * STATE:(ri + 1) * STATE], 0.0)
                    r0 = ri * LANES + half * STATE
                    q_s[k, r0:r0 + STATE, t * LANES:(t + 1) * LANES] = piece.astype(BF16)


def _build_toeplitz(abb, cr, ci, d_row, w_s):
    def split(v):
        hi = v.astype(BF16)
        return hi, (v - hi.astype(F32)).astype(BF16)

    rhs_hi, rhs_lo = split(jnp.concatenate([cr, -ci], axis=1))
    rhs = jnp.concatenate([rhs_hi, rhs_lo, rhs_hi], axis=1)
    row = lax.broadcasted_iota(jnp.int32, (LANES, LANES), 0)
    col = lax.broadcasted_iota(jnp.int32, (LANES, LANES), 1)
    same_group = (row // GROUP) == (col // GROUP)
    taps = []
    for lag in range(CHUNK):
        lhs_hi, lhs_lo = split(jnp.concatenate(abb[lag], axis=1))
        k = lax.dot_general(jnp.concatenate([lhs_hi, lhs_hi, lhs_lo], axis=1), rhs,
                            (((1,), (1,)), ((), ())), preferred_element_type=F32)
        k = jnp.where(same_group, k, 0.0)
        if lag == 0:
            k = k + jnp.where(row == col, d_row, 0.0)
        taps.append(k.astype(BF16))
    zero = jnp.zeros((LANES, LANES), BF16)
    for d in range(N_TILES):
        w_s[d, :LANES, :LANES] = taps[2 * d]
        w_s[d, :LANES, LANES:] = taps[2 * d + 1]
        w_s[d, LANES:, :LANES] = taps[2 * d - 1] if d > 0 else zero
        w_s[d, LANES:, LANES:] = taps[2 * d]


def _ssm_kernel(u_ref, ar_ref, ai_ref, dt_ref, btr_ref, bti_ref, cr_ref, ci_ref, d_ref,
                dtare_ref, dtaim_ref, w0_ref, w1_ref, w2_ref, w3_ref, y_ref, c0_ref, c1_ref, c2_ref,
                c3_ref, xre_s, xim_s, p_s, q_s, w_s, *, batch, n_chunks):
    step = pl.program_id(0)
    for src, dst in ((w0_ref, c0_ref), (w1_ref, c1_ref), (w2_ref, c2_ref), (w3_ref, c3_ref)):
        dst[...] = src[...].astype(BF16)
    rows8 = batch * PAIRS
    slab_rows = n_chunks * rows8

    def state_rows(slab, b, k):
        return pl.ds(slab * slab_rows + b * PAIRS + k, n_chunks, stride=rows8)

    def per_row(v):
        v = jnp.broadcast_to(v, (SLAB_G, STATE))
        return jnp.broadcast_to(v[:, None, :], (SLAB_G, GROUP, STATE)).reshape(LANES, STATE)

    def powers():
        return _slab_powers(per_row(ar_ref[...]), per_row(ai_ref[...]),
                            per_row(jnp.exp(dt_ref[...])), btr_ref[...], bti_ref[...])

    @pl.when(step < N_SLABS)
    def _():
        _, abb = powers()
        _build_state_in(abb, p_s)
        u = u_ref[...]
        for k in range(PAIRS):
            x = _dot(u, p_s[k])
            for b in range(batch):
                xb = x[b * n_chunks:(b + 1) * n_chunks]
                xre_s[state_rows(step, b, k), :] = xb[:, :LANES]
                xim_s[state_rows(step, b, k), :] = xb[:, LANES:]

    @pl.when(step == N_SLABS)
    def _():
        dta_re = jnp.concatenate([dtare_ref[...]] * batch, axis=1).reshape(N_SLABS * rows8, LANES)
        dta_im = jnp.concatenate([dtaim_ref[...]] * batch, axis=1).reshape(N_SLABS * rows8, LANES)
        mag = jnp.exp(CHUNK * dta_re)
        a_re, a_im = mag * jnp.cos(CHUNK * dta_im), mag * jnp.sin(CHUNK * dta_im)

        def body(c, carry):
            s_re, s_im = carry
            x_re, x_im = [], []
            for sl in range(N_SLABS):
                off = pl.multiple_of(sl * slab_rows + c * rows8, rows8)
                x_re.append(xre_s[pl.ds(off, rows8), :])
                x_im.append(xim_s[pl.ds(off, rows8), :])
                xre_s[pl.ds(off, rows8), :] = s_re[sl * rows8:(sl + 1) * rows8]
                xim_s[pl.ds(off, rows8), :] = s_im[sl * rows8:(sl + 1) * rows8]
            x_re = jnp.concatenate(x_re, axis=0)
            x_im = jnp.concatenate(x_im, axis=0)
            return (a_re * s_re - a_im * s_im + x_re, a_re * s_im + a_im * s_re + x_im)

        zero = jnp.zeros((N_SLABS * rows8, LANES), F32)
        lax.fori_loop(0, n_chunks, body, (zero, zero))

    @pl.when(step > N_SLABS)
    def _():
        slab = step - (N_SLABS + 1)
        pw, abb = powers()
        cr, ci = cr_ref[...], ci_ref[...]
        _build_state_out(pw, cr, ci, q_s)
        _build_toeplitz(abb, cr, ci, d_ref[...], w_s)
        u = u_ref[...]
        y_state = None
        for k in range(PAIRS):
            s_in = jnp.concatenate(
                [jnp.concatenate([xre_s[state_rows(slab, b, k), :], xim_s[state_rows(slab, b, k), :]],
                                 axis=1) for b in range(batch)], axis=0).astype(BF16)
            part = _dot(s_in, q_s[k])
            y_state = part if y_state is None else y_state + part
        for t2 in range(N_TILES):
            acc = y_state[:, t2 * MXU:(t2 + 1) * MXU]
            for t1 in range(t2 + 1):
                acc = acc + _dot(u[:, t1 * MXU:(t1 + 1) * MXU], w_s[t2 - t1])
            y_ref[:, t2 * MXU:(t2 + 1) * MXU] = jax.nn.gelu(acc).astype(BF16)


N_CAST = 16


def _ssm(u_chunks, A_re, A_im, log_dt, B_re, B_im, C_re, C_im, D_skip, batch, tail_weights):
    _, rows, _ = u_chunks.shape
    n_chunks = rows // batch
    n_steps = 2 * N_SLABS + 1
    kernel = functools.partial(_ssm_kernel, batch=batch, n_chunks=n_chunks)

    dt = jnp.exp(log_dt)[:, None]
    btr = B_re.transpose(0, 2, 1).reshape(SSM_W, STATE)
    bti = B_im.transpose(0, 2, 1).reshape(SSM_W, STATE)
    cr, ci = C_re.reshape(SSM_W, STATE), C_im.reshape(SSM_W, STATE)
    d_rows = D_skip.reshape(N_SLABS, 1, LANES)
    dta_re = (dt * A_re).reshape(N_SLABS, PAIRS, LANES)
    dta_im = (dt * A_im).reshape(N_SLABS, PAIRS, LANES)

    def slab_in(s):
        return jnp.where(s <= N_SLABS, jnp.minimum(s, N_SLABS - 1), s - (N_SLABS + 1))

    def slab_out(s):
        return jnp.maximum(s - (N_SLABS + 1), 0)

    def cast_spec(w):
        return pl.BlockSpec((w.shape[0] // N_CAST, w.shape[1]),
                            lambda s: (jnp.minimum(s, N_CAST - 1), 0))

    coeff = pl.BlockSpec((LANES, STATE), lambda s: (slab_in(s), 0))
    whole = pl.BlockSpec((N_SLABS, PAIRS, LANES), lambda s: (0, 0, 0))
    return pl.pallas_call(
        kernel,
        grid=(n_steps,),
        in_specs=[
            pl.BlockSpec((None, rows, CHUNK_W), lambda s: (slab_in(s), 0, 0)),
            pl.BlockSpec((SLAB_G, STATE), lambda s: (slab_in(s), 0)),
            pl.BlockSpec((SLAB_G, STATE), lambda s: (slab_in(s), 0)),
            pl.BlockSpec((SLAB_G, 1), lambda s: (slab_in(s), 0)),
            coeff, coeff, coeff, coeff,
            pl.BlockSpec((None, 1, LANES), lambda s: (slab_in(s), 0, 0)),
            whole, whole,
        ] + [cast_spec(w) for w in tail_weights],
        out_specs=[pl.BlockSpec((None, rows, CHUNK_W), lambda s: (slab_out(s), 0, 0))]
        + [cast_spec(w) for w in tail_weights],
        out_shape=[jax.ShapeDtypeStruct(u_chunks.shape, BF16)]
        + [jax.ShapeDtypeStruct(w.shape, BF16) for w in tail_weights],
        scratch_shapes=[
            pltpu.VMEM((N_SLABS * n_chunks * batch * PAIRS, LANES), F32),
            pltpu.VMEM((N_SLABS * n_chunks * batch * PAIRS, LANES), F32),
            pltpu.VMEM((PAIRS, CHUNK_W, 2 * LANES), BF16),
            pltpu.VMEM((PAIRS, 2 * LANES, CHUNK_W), BF16),
            pltpu.VMEM((N_TILES, MXU, MXU), BF16),
        ],
        compiler_params=pltpu.CompilerParams(
            dimension_semantics=("arbitrary",), vmem_limit_bytes=VMEM_LIMIT),
        name="ssm",
    )(u_chunks, A_re, A_im, log_dt[:, None], btr, bti, cr, ci, d_rows, dta_re, dta_im,
      *tail_weights)


def _tail_kernel(ag_ref, yc_ref, z_ref, ga_ref, gs_ref, x_ref, wa_ref, wg_ref, bg_ref, ws_ref,
                 wo_ref, o_ref, ys_ref, *, tm):
    rows = tm // CHUNK
    for s in range(N_SLABS):
        for t in range(CHUNK):
            ys_ref[pl.ds(s * rows * US_PITCH + t, rows, stride=US_PITCH), :] = (
                yc_ref[s, :, t * LANES:(t + 1) * LANES].astype(F32))
    y_g = jnp.concatenate(
        [jnp.concatenate([ys_ref[(s * rows + c) * US_PITCH:(s * rows + c) * US_PITCH + CHUNK, :]
                          for c in range(rows)], axis=0) for s in range(N_SLABS)],
        axis=1).astype(BF16)

    y_a = _dot(ag_ref[...], wa_ref[...])
    glu = _dot(y_g, wg_ref[...]) + bg_ref[...]
    z = z_ref[...].astype(F32)
    t = glu[:, :SSM_W] * jax.nn.sigmoid(glu[:, SSM_W:]) * (z * jax.nn.sigmoid(z))
    y_s = _dot(t.astype(BF16), ws_ref[...])
    merged = (jax.nn.sigmoid(ga_ref[...].astype(F32)) * y_a
              + jax.nn.sigmoid(gs_ref[...].astype(F32)) * y_s)
    o_ref[...] = x_ref[...] + _dot(merged.astype(BF16), wo_ref[...])


def _tail(attn_g, y_chunks, proj, x2, w_attn, w_glu, b_glu, w_ssm, w_out, tm=256):
    m = x2.shape[0]
    row = lambda blk: (lambda i: (i, blk))
    const = lambda i: (0, 0)
    resident = functools.partial(pl.BlockSpec, index_map=const, pipeline_mode=pl.Buffered(1))
    kernel = functools.partial(_tail_kernel, tm=tm)
    return pl.pallas_call(
        kernel,
        grid=(m // tm,),
        in_specs=[
            pl.BlockSpec((tm, ATTN_W), row(0)),
            pl.BlockSpec((N_SLABS, tm // CHUNK, CHUNK_W), lambda i: (0, i, 0)),
            pl.BlockSpec((tm, SSM_W), row(COL_Z // SSM_W)),
            pl.BlockSpec((tm, D_MODEL), row(COL_GA // D_MODEL)),
            pl.BlockSpec((tm, D_MODEL), row(COL_GS // D_MODEL)),
            pl.BlockSpec((tm, D_MODEL), row(0)),
            resident((ATTN_W, D_MODEL)),
            resident((SSM_W, 2 * SSM_W)),
            resident((1, 2 * SSM_W)),
            resident((SSM_W, D_MODEL)),
            resident((D_MODEL, D_MODEL)),
        ],
        out_specs=pl.BlockSpec((tm, D_MODEL), row(0)),
        out_shape=jax.ShapeDtypeStruct((m, D_MODEL), F32),
        scratch_shapes=[pltpu.VMEM((N_SLABS * (tm // CHUNK) * US_PITCH, LANES), F32)],
        compiler_params=pltpu.CompilerParams(
            dimension_semantics=("arbitrary",), vmem_limit_bytes=VMEM_LIMIT),
        name="tail",
    )(attn_g, y_chunks, proj, proj, proj, x2, w_attn, w_glu, b_glu.reshape(1, -1), w_ssm, w_out)


def kernel(x, norm_w, w_in, q_norm_w, k_norm_w, sinks, w_attn_proj, A_re, A_im, log_dt, B_re, B_im,
           C_re, C_im, D_skip, w_glu, b_glu, w_ssm_proj, w_out):
    batch, seq, _ = x.shape
    m = batch * seq
    x2 = x.reshape(m, D_MODEL)

    proj, u_chunks = _in_proj(x2, norm_w, w_in)

    attn_g = _swa(proj, sinks, q_norm_w, k_norm_w, batch, seq)

    y_chunks, w_attn_bf, w_glu_bf, w_ssm_bf, w_out_bf = _ssm(
        u_chunks, A_re, A_im, log_dt, B_re, B_im, C_re, C_im, D_skip, batch,
        (w_attn_proj, w_glu, w_ssm_proj, w_out))

    out = _tail(attn_g, y_chunks, proj, x2, w_attn_bf, w_glu_bf, b_glu, w_ssm_bf, w_out_bf)
    return out.reshape(batch, seq, D_MODEL)
```

```python
import functools
import math

import jax
import jax.numpy as jnp
from jax import lax
from jax.experimental import pallas as pl
from jax.experimental.pallas import tpu as pltpu

F32 = jnp.float32
BF16 = jnp.bfloat16

D_MODEL = 2048
HEAD_DIM = 64
N_Q_HEADS = 16
N_KV_HEADS = 4
Q_PER_KV = 4
ATTN_W = N_Q_HEADS * HEAD_DIM
KV_W = N_KV_HEADS * HEAD_DIM
WINDOW = 128
SSM_W = D_MODEL // 2
GROUP = 16
N_GROUPS = SSM_W // GROUP
STATE = 64
NORM_EPS = 1e-6

COL_Q = 0
COL_GATE = COL_Q + ATTN_W
COL_U = COL_GATE + ATTN_W
COL_Z = COL_U + SSM_W
COL_GA = COL_Z + SSM_W
COL_GS = COL_GA + D_MODEL
COL_K = COL_GS + D_MODEL
COL_V = COL_K + KV_W
IN_W = COL_V + KV_W

LANES = 128
MXU = 256
CHUNK = 16
SLAB_G = LANES // GROUP
N_SLABS = N_GROUPS // SLAB_G
PAIRS = SLAB_G // 2
CHUNK_W = CHUNK * LANES
N_TILES = CHUNK_W // MXU
VMEM_LIMIT = 48 * 1024 * 1024


def _dot(a, b):
    return jnp.dot(a, b, preferred_element_type=F32)


IN_TN = 512
N_IN_TILES = IN_W // IN_TN
SRC_KV_TILE = (ATTN_W) // IN_TN
SRC_U_TILE0 = (ATTN_W + 2 * KV_W + ATTN_W) // IN_TN


US_PITCH = CHUNK if (CHUNK // 8) % 2 else CHUNK + 8
X_ROWS = 128
IN_PER_ROUND = 2


def _dst_tile(src_tile):
    if src_tile < SRC_KV_TILE:
        return src_tile
    if src_tile == SRC_KV_TILE:
        return N_IN_TILES - 1
    return src_tile - 1


def _in_proj_kernel(nw_ref, x_hbm, w_hbm, o_hbm, uc_ref, h_ref, x_buf, w_buf, o_buf, us_ref, x_sem,
                    w_sem, o_sem, *, tm, n_steps):
    i = pl.program_id(0)
    rows = tm // CHUNK
    n_sl = IN_TN // LANES
    n_rounds = pl.cdiv(N_IN_TILES, IN_PER_ROUND)
    assert n_rounds % 2 == 1
    n_xc = tm // X_ROWS
    assert n_xc < n_rounds
    cur, nxt = i % 2, (i + 1) % 2
    nxt_tile = jnp.minimum(i + 1, n_steps - 1)

    def x_copy(tile, c):
        src = x_hbm.at[pl.ds(pl.multiple_of(tile * tm + c * X_ROWS, X_ROWS), X_ROWS), :]
        return pltpu.make_async_copy(src, x_buf.at[c % 2], x_sem.at[c % 2])

    def norm_chunk(c, dst):
        x = x_buf[c % 2]
        ms = jnp.mean(x * x, axis=-1, keepdims=True)
        h_ref[dst, c * X_ROWS:(c + 1) * X_ROWS, :] = (
            x * lax.rsqrt(ms + NORM_EPS) * nw_ref[...]).astype(BF16)

    def round_tiles(r):
        return [t for t in range(r * IN_PER_ROUND, (r + 1) * IN_PER_ROUND) if 0 <= t < N_IN_TILES]

    def half_of(step, r):
        return (step + r) % 2

    def w_copy(step, r):
        tiles = round_tiles(r)
        width = len(tiles) * IN_TN
        half = half_of(step, r)
        return pltpu.make_async_copy(w_hbm.at[:, pl.ds(tiles[0] * IN_TN, width)],
                                     w_buf.at[half, :, pl.ds(0, width)], w_sem.at[half])

    def o_copies(step, r):
        tiles = round_tiles(r)
        runs, start = [], 0
        for j in range(1, len(tiles) + 1):
            if j == len(tiles) or _dst_tile(tiles[j]) != _dst_tile(tiles[j - 1]) + 1:
                runs.append((start, j - start))
                start = j
        row0 = pl.multiple_of(step * tm, tm)
        half = half_of(step, r)
        return [pltpu.make_async_copy(
            o_buf.at[half, :, pl.ds(j0 * IN_TN, n * IN_TN)],
            o_hbm.at[pl.ds(row0, tm), pl.ds(_dst_tile(tiles[j0]) * IN_TN, n * IN_TN)],
            o_sem.at[half, k]) for k, (j0, n) in enumerate(runs)]

    @pl.when(i == 0)
    def _():
        w_copy(0, 0).start()
        x_copy(0, 0).start()
        for c in range(n_xc):
            if c + 1 < n_xc:
                x_copy(0, c + 1).start()
            x_copy(0, c).wait()
            norm_chunk(c, 0)

    def finish(t, acc):
        r, j = divmod(t, IN_PER_ROUND)
        o_buf[half_of(i, r), :, j * IN_TN:(j + 1) * IN_TN] = acc.astype(BF16)
        if SRC_U_TILE0 <= t < SRC_U_TILE0 + SSM_W // IN_TN:
            slab0 = (t - SRC_U_TILE0) * n_sl
            for s in range(n_sl):
                for c in range(rows):
                    r0 = (s * rows + c) * US_PITCH
                    us_ref[r0:r0 + CHUNK, :] = acc[c * CHUNK:(c + 1) * CHUNK, s * LANES:(s + 1) * LANES]
            for s in range(n_sl):
                for tok in range(CHUNK):
                    piece = us_ref[pl.ds(s * rows * US_PITCH + tok, rows, stride=US_PITCH), :]
                    uc_ref[slab0 + s, :, tok * LANES:(tok + 1) * LANES] = piece.astype(BF16)

    for r in range(n_rounds):
        if r + 1 < n_rounds:
            w_copy(i, r + 1).start()
        else:
            @pl.when(i + 1 < n_steps)
            def _():
                w_copy(i + 1, 0).start()
        w_copy(i, r).wait()
        if r >= 2:
            for c in o_copies(i, r - 2):
                c.wait()
        else:
            @pl.when(i > 0)
            def _(r=r):
                for c in o_copies(i - 1, n_rounds - 2 + r):
                    c.wait()
        if r >= 1:
            for c in o_copies(i, r - 1):
                c.start()
        if r < n_xc:
            x_copy(nxt_tile, r).start()
        if 1 <= r <= n_xc:
            x_copy(nxt_tile, r - 1).wait()
        for j, t in enumerate(round_tiles(r)):
            acc = None
            for k in range(D_MODEL // MXU):
                ks = slice(k * MXU, (k + 1) * MXU)
                w_tile = w_buf[half_of(i, r), ks, j * IN_TN:(j + 1) * IN_TN]
                part = _dot(h_ref[cur, :, ks], w_tile.astype(BF16))
                acc = part if acc is None else acc + part
            finish(t, acc)
        if 1 <= r <= n_xc:
            norm_chunk(r - 1, nxt)

    for c in o_copies(i, n_rounds - 1):
        c.start()

    @pl.when(i == n_steps - 1)
    def _():
        for c in o_copies(i, n_rounds - 2) + o_copies(i, n_rounds - 1):
            c.wait()


def _in_proj(x2, norm_w, w_in, tm=1024):
    m = x2.shape[0]
    kernel = functools.partial(_in_proj_kernel, tm=tm, n_steps=m // tm)
    return pl.pallas_call(
        kernel,
        grid=(m // tm,),
        in_specs=[
            pl.BlockSpec((1, D_MODEL), lambda i: (0, 0)),
            pl.BlockSpec(memory_space=pl.ANY),
            pl.BlockSpec(memory_space=pl.ANY),
        ],
        out_specs=[
            pl.BlockSpec(memory_space=pl.ANY),
            pl.BlockSpec((N_SLABS, tm // CHUNK, CHUNK_W), lambda i: (0, i, 0)),
        ],
        out_shape=[
            jax.ShapeDtypeStruct((m, IN_W), BF16),
            jax.ShapeDtypeStruct((N_SLABS, m // CHUNK, CHUNK_W), BF16),
        ],
        scratch_shapes=[
            pltpu.VMEM((2, tm, D_MODEL), BF16),
            pltpu.VMEM((2, X_ROWS, D_MODEL), F32),
            pltpu.VMEM((2, D_MODEL, IN_PER_ROUND * IN_TN), F32),
            pltpu.VMEM((2, tm, IN_PER_ROUND * IN_TN), BF16),
            pltpu.VMEM((IN_TN // LANES * (tm // CHUNK) * US_PITCH, LANES), F32),
            pltpu.SemaphoreType.DMA((2,)),
            pltpu.SemaphoreType.DMA((2,)),
            pltpu.SemaphoreType.DMA((2, IN_PER_ROUND)),
        ],
        compiler_params=pltpu.CompilerParams(
            dimension_semantics=("arbitrary",), vmem_limit_bytes=VMEM_LIMIT),
        name="in_proj",
    )(norm_w.reshape(1, D_MODEL), x2, w_in)


def _head_norm(t, w):
    ms = jnp.mean(t * t, axis=-1, keepdims=True)
    return t * lax.rsqrt(ms + NORM_EPS) * w


_NT = (((1,), (1,)), ((), ()))
_TN = (((0,), (0,)), ((), ()))


SWA_TQ = 1024
SWA_SUB = SWA_TQ // WINDOW


def _swa_tile(sink_ref, q_ref, g_ref, kc_ref, vc_ref, kp_ref, vp_ref, qw_ref, kw_ref, o_ref,
              first_tile):
    log2e = math.log2(math.e)
    kqw = kw_ref[...] * qw_ref[...] * (log2e / math.sqrt(HEAD_DIM))
    n_col = Q_PER_KV * WINDOW
    key = lax.broadcasted_iota(jnp.int32, (WINDOW, n_col), 0)
    qry = lax.broadcasted_iota(jnp.int32, (WINDOW, n_col), 1) % WINDOW
    from_prev = key > qry
    no_prev = jnp.where(first_tile, -1e30, 0.0)
    head_of_col = lax.broadcasted_iota(jnp.int32, (1, n_col), 1) // WINDOW
    gw = Q_PER_KV * HEAD_DIM
    sel_r = lax.broadcasted_iota(jnp.int32, (8, 2 * gw), 0)
    sel_l = (lax.broadcasted_iota(jnp.int32, (8, 2 * gw), 1) % gw) // HEAD_DIM
    head_sel = jnp.where(sel_r == sel_l, 1.0, 0.0).astype(BF16)

    parts = []
    carry = {}

    def block(g, n):
        kcol = slice(g * HEAD_DIM, (g + 1) * HEAD_DIM)
        gcols = slice(g * Q_PER_KV * HEAD_DIM, (g + 1) * Q_PER_KV * HEAD_DIM)
        if n == 0:
            sink = jnp.zeros((1, n_col), F32)
            for r in range(Q_PER_KV):
                sink = jnp.where(head_of_col == r, sink_ref[g * Q_PER_KV + r] * log2e, sink)
            carry['sink'] = sink
            carry['k'] = _head_norm(kp_ref[:, kcol].astype(F32), kqw).astype(BF16)
            carry['v'] = vp_ref[:, kcol]
        sink, k_prev, v_prev = carry['sink'], carry['k'], carry['v']
        rows = slice(n * WINDOW, (n + 1) * WINDOW)
        k_cur = _head_norm(kc_ref[rows, kcol].astype(F32), kqw).astype(BF16)
        v_cur = vc_ref[rows, kcol]
        k_ctx = jnp.concatenate([k_prev, k_cur], axis=0)
        v_ctx = jnp.concatenate([v_prev, v_cur], axis=0)

        qg = q_ref[rows, gcols]
        qf = qg.astype(F32)
        q_t = qf.T.astype(BF16)
        q_t = jnp.concatenate([q_t[r * HEAD_DIM:(r + 1) * HEAD_DIM] for r in range(Q_PER_KV)],
                              axis=1)
        q2 = qf * qf
        q2_hi = q2.astype(BF16)
        q2_lo = (q2 - q2_hi.astype(F32)).astype(BF16)
        ssq = lax.dot_general(head_sel, jnp.concatenate([q2_hi, q2_lo], axis=1), _NT,
                              preferred_element_type=F32)
        rms = lax.rsqrt(ssq * (1.0 / HEAD_DIM) + NORM_EPS)
        rms_q = jnp.concatenate([rms[r:r + 1] for r in range(Q_PER_KV)], axis=1)

        s = _dot(k_ctx, q_t)
        s_prev = s[:WINDOW] + no_prev if n == 0 else s[:WINDOW]
        s = jnp.where(from_prev, s_prev, s[WINDOW:]) * rms_q
        mx = jnp.maximum(jnp.max(s, axis=0, keepdims=True), sink)
        p = jnp.exp2(s - mx)
        den = jnp.sum(p, axis=0, keepdims=True) + jnp.exp2(sink - mx)
        p_ctx = jnp.concatenate([jnp.where(from_prev, p, 0.0), jnp.where(from_prev, 0.0, p)],
                                axis=0).astype(BF16)
        o_t = lax.dot_general(v_ctx, p_ctx, _TN, preferred_element_type=F32) * (1.0 / den)
        halves = [jnp.concatenate([o_t[:, (2 * h) * WINDOW:(2 * h + 1) * WINDOW],
                                   o_t[:, (2 * h + 1) * WINDOW:(2 * h + 2) * WINDOW]], axis=0).T
                  for h in range(Q_PER_KV // 2)]
        og = jnp.concatenate(halves, axis=1)
        gate = g_ref[rows, gcols].astype(F32)
        o_ref[rows, gcols] = (og * (gate * jax.nn.sigmoid(gate))).astype(BF16)
        carry['k'], carry['v'] = k_cur, v_cur

    for g in range(N_KV_HEADS):
        for n in range(SWA_SUB):
            parts.append(functools.partial(block, g, n))
    return parts


def _swa_kernel(*refs, tiles_per_seq):
    for block in _swa_tile(*refs, pl.program_id(0) % tiles_per_seq == 0):
        block()


def _swa(proj, sinks, q_norm_w, k_norm_w, batch, seq):
    m = batch * seq
    cur = lambda col: (lambda s: (s, col))
    prev = lambda col: (lambda s: (jnp.maximum(s * SWA_SUB - 1, 0), col))
    kernel = functools.partial(_swa_kernel, tiles_per_seq=seq // SWA_TQ)
    return pl.pallas_call(
        kernel,
        grid=(m // SWA_TQ,),
        in_specs=[
            pl.BlockSpec(memory_space=pltpu.SMEM),
            pl.BlockSpec((SWA_TQ, ATTN_W), cur(COL_Q // ATTN_W)),
            pl.BlockSpec((SWA_TQ, ATTN_W), cur(COL_GATE // ATTN_W)),
            pl.BlockSpec((SWA_TQ, KV_W), cur(COL_K // KV_W)),
            pl.BlockSpec((SWA_TQ, KV_W), cur(COL_V // KV_W)),
            pl.BlockSpec((WINDOW, KV_W), prev(COL_K // KV_W)),
            pl.BlockSpec((WINDOW, KV_W), prev(COL_V // KV_W)),
            pl.BlockSpec((1, HEAD_DIM), lambda s: (0, 0)),
            pl.BlockSpec((1, HEAD_DIM), lambda s: (0, 0)),
        ],
        out_specs=pl.BlockSpec((SWA_TQ, ATTN_W), cur(0)),
        out_shape=jax.ShapeDtypeStruct((m, ATTN_W), BF16),
        compiler_params=pltpu.CompilerParams(
            dimension_semantics=("arbitrary",), vmem_limit_bytes=VMEM_LIMIT),
        name="swa",
    )(sinks, proj, proj, proj, proj, proj, proj,
      q_norm_w.reshape(1, HEAD_DIM), k_norm_w.reshape(1, HEAD_DIM))


def _slab_powers(ar, ai, dt, btr, bti):
    dta_re, dta_im = dt * ar, dt * ai
    mag = jnp.exp(dta_re)
    ab_re, ab_im = mag * jnp.cos(dta_im), mag * jnp.sin(dta_im)
    pw = [(jnp.ones_like(ar), jnp.zeros_like(ar))]
    for _ in range(CHUNK):
        pr, pi = pw[-1]
        pw.append((pr * ab_re - pi * ab_im, pr * ab_im + pi * ab_re))
    den = ar * ar + ai * ai
    num_re, num_im = ab_re - 1.0, ab_im
    cf_re = (num_re * ar + num_im * ai) / den
    cf_im = (num_im * ar - num_re * ai) / den
    bb_re = cf_re * btr - cf_im * bti
    bb_im = cf_re * bti + cf_im * btr
    abb = [(pr * bb_re - pi * bb_im, pr * bb_im + pi * bb_re) for pr, pi in pw[:CHUNK]]
    return pw, abb


def _build_state_in(abb, p_s):
    row_g = lax.broadcasted_iota(jnp.int32, (LANES, STATE), 0) // GROUP
    even = row_g % 2 == 0
    pair_of_row = lax.broadcasted_iota(jnp.int32, (LANES, 2 * LANES), 0) // (2 * GROUP)
    for lag in range(CHUNK):
        t = CHUNK - 1 - lag
        re, im = abb[lag]
        blk = jnp.concatenate([jnp.where(even, re, 0.0), jnp.where(even, 0.0, re),
                               jnp.where(even, im, 0.0), jnp.where(even, 0.0, im)], axis=1)
        for k in range(PAIRS):
            p_s[k, t * LANES:(t + 1) * LANES, :] = jnp.where(pair_of_row == k, blk, 0.0).astype(BF16)


def _build_state_out(pw, cr, ci, q_s):
    lane_g = lax.broadcasted_iota(jnp.int32, (STATE, LANES), 1) // GROUP
    for t in range(CHUNK):
        pr, pi = pw[t + 1]
        cat = jnp.concatenate([cr * pr - ci * pi, -(cr * pi + ci * pr)], axis=1)
        cat_t = cat.T
        for k in range(PAIRS):
            for ri in range(2):
                for half in range(2):
                    piece = jnp.where(lane_g == 2 * k + half, cat_t[ri * STATE:(ri + 1) * STATE], 0.0)
                    r0 = ri * LANES + half * STATE
                    q_s[k, r0:r0 + STATE, t * LANES:(t + 1) * LANES] = piece.astype(BF16)


def _build_toeplitz(abb, cr, ci, d_row, w_s):
    def split(v):
        hi = v.astype(BF16)
        return hi, (v - hi.astype(F32)).astype(BF16)

    rhs_hi, rhs_lo = split(jnp.concatenate([cr, -ci], axis=1))
    rhs = jnp.concatenate([rhs_hi, rhs_lo, rhs_hi], axis=1)
    row = lax.broadcasted_iota(jnp.int32, (LANES, LANES), 0)
    col = lax.broadcasted_iota(jnp.int32, (LANES, LANES), 1)
    same_group = (row // GROUP) == (col // GROUP)
    taps = []
    for lag in range(CHUNK):
        lhs_hi, lhs_lo = split(jnp.concatenate(abb[lag], axis=1))
        k = lax.dot_general(jnp.concatenate([lhs_hi, lhs_hi, lhs_lo], axis=1), rhs,
                            (((1,), (1,)), ((), ())), preferred_element_type=F32)
        k = jnp.where(same_group, k, 0.0)
        if lag == 0:
            k = k + jnp.where(row == col, d_row, 0.0)
        taps.append(k.astype(BF16))
    zero = jnp.zeros((LANES, LANES), BF16)
    for d in range(N_TILES):
        w_s[d, :LANES, :LANES] = taps[2 * d]
        w_s[d, :LANES, LANES:] = taps[2 * d + 1]
        w_s[d, LANES:, :LANES] = taps[2 * d - 1] if d > 0 else zero
        w_s[d, LANES:, LANES:] = taps[2 * d]


def _ssm_kernel(u_ref, ar_ref, ai_ref, dt_ref, btr_ref, bti_ref, cr_ref, ci_ref, d_ref,
                dtare_ref, dtaim_ref, w0_ref, w1_ref, w2_ref, w3_ref, y_ref, c0_ref, c1_ref, c2_ref,
                c3_ref, xre_s, xim_s, p_s, q_s, w_s, *, batch, n_chunks):
    step = pl.program_id(0)
    for src, dst in ((w0_ref, c0_ref), (w1_ref, c1_ref), (w2_ref, c2_ref), (w3_ref, c3_ref)):
        dst[...] = src[...].astype(BF16)
    rows8 = batch * PAIRS
    slab_rows = n_chunks * rows8

    def state_rows(slab, b, k):
        return pl.ds(slab * slab_rows + b * PAIRS + k, n_chunks, stride=rows8)

    def per_row(v):
        v = jnp.broadcast_to(v, (SLAB_G, STATE))
        return jnp.broadcast_to(v[:, None, :], (SLAB_G, GROUP, STATE)).reshape(LANES, STATE)

    def powers():
        return _slab_powers(per_row(ar_ref[...]), per_row(ai_ref[...]),
                            per_row(jnp.exp(dt_ref[...])), btr_ref[...], bti_ref[...])

    @pl.when(step < N_SLABS)
    def _():
        _, abb = powers()
        _build_state_in(abb, p_s)
        u = u_ref[...]
        for k in range(PAIRS):
            x = _dot(u, p_s[k])
            for b in range(batch):
                xb = x[b * n_chunks:(b + 1) * n_chunks]
                xre_s[state_rows(step, b, k), :] = xb[:, :LANES]
                xim_s[state_rows(step, b, k), :] = xb[:, LANES:]

    @pl.when(step == N_SLABS)
    def _():
        dta_re = jnp.concatenate([dtare_ref[...]] * batch, axis=1).reshape(N_SLABS * rows8, LANES)
        dta_im = jnp.concatenate([dtaim_ref[...]] * batch, axis=1).reshape(N_SLABS * rows8, LANES)
        mag = jnp.exp(CHUNK * dta_re)
        a_re, a_im = mag * jnp.cos(CHUNK * dta_im), mag * jnp.sin(CHUNK * dta_im)

        def body(c, carry):
            s_re, s_im = carry
            x_re, x_im = [], []
            for sl in range(N_SLABS):
                off = pl.multiple_of(sl * slab_rows + c * rows8, rows8)
                x_re.append(xre_s[pl.ds(off, rows8), :])
                x_im.append(xim_s[pl.ds(off, rows8), :])
                xre_s[pl.ds(off, rows8), :] = s_re[sl * rows8:(sl + 1) * rows8]
                xim_s[pl.ds(off, rows8), :] = s_im[sl * rows8:(sl + 1) * rows8]
            x_re = jnp.concatenate(x_re, axis=0)
            x_im = jnp.concatenate(x_im, axis=0)
            return (a_re * s_re - a_im * s_im + x_re, a_re * s_im + a_im * s_re + x_im)

        zero = jnp.zeros((N_SLABS * rows8, LANES), F32)
        lax.fori_loop(0, n_chunks, body, (zero, zero))

    @pl.when(step > N_SLABS)
    def _():
        slab = step - (N_SLABS + 1)
        pw, abb = powers()
        cr, ci = cr_ref[...], ci_ref[...]
        _build_state_out(pw, cr, ci, q_s)
        _build_toeplitz(abb, cr, ci, d_ref[...], w_s)
        u = u_ref[...]
        y_state = None
        for k in range(PAIRS):
            s_in = jnp.concatenate(
                [jnp.concatenate([xre_s[state_rows(slab, b, k), :], xim_s[state_rows(slab, b, k), :]],
                                 axis=1) for b in range(batch)], axis=0).astype(BF16)
            part = _dot(s_in, q_s[k])
            y_state = part if y_state is None else y_state + part
        for t2 in range(N_TILES):
            acc = y_state[:, t2 * MXU:(t2 + 1) * MXU]
            for t1 in range(t2 + 1):
                acc = acc + _dot(u[:, t1 * MXU:(t1 + 1) * MXU], w_s[t2 - t1])
            y_ref[:, t2 * MXU:(t2 + 1) * MXU] = jax.nn.gelu(acc).astype(BF16)


N_CAST = 16


def _ssm(u_chunks, A_re, A_im, log_dt, B_re, B_im, C_re, C_im, D_skip, batch, tail_weights):
    _, rows, _ = u_chunks.shape
    n_chunks = rows // batch
    n_steps = 2 * N_SLABS + 1
    kernel = functools.partial(_ssm_kernel, batch=batch, n_chunks=n_chunks)

    dt = jnp.exp(log_dt)[:, None]
    btr = B_re.transpose(0, 2, 1).reshape(SSM_W, STATE)
    bti = B_im.transpose(0, 2, 1).reshape(SSM_W, STATE)
    cr, ci = C_re.reshape(SSM_W, STATE), C_im.reshape(SSM_W, STATE)
    d_rows = D_skip.reshape(N_SLABS, 1, LANES)
    dta_re = (dt * A_re).reshape(N_SLABS, PAIRS, LANES)
    dta_im = (dt * A_im).reshape(N_SLABS, PAIRS, LANES)

    def slab_in(s):
        return jnp.where(s <= N_SLABS, jnp.minimum(s, N_SLABS - 1), s - (N_SLABS + 1))

    def slab_out(s):
        return jnp.maximum(s - (N_SLABS + 1), 0)

    def cast_spec(w):
        return pl.BlockSpec((w.shape[0] // N_CAST, w.shape[1]),
                            lambda s: (jnp.minimum(s, N_CAST - 1), 0))

    coeff = pl.BlockSpec((LANES, STATE), lambda s: (slab_in(s), 0))
    whole = pl.BlockSpec((N_SLABS, PAIRS, LANES), lambda s: (0, 0, 0))
    return pl.pallas_call(
        kernel,
        grid=(n_steps,),
        in_specs=[
            pl.BlockSpec((None, rows, CHUNK_W), lambda s: (slab_in(s), 0, 0)),
            pl.BlockSpec((SLAB_G, STATE), lambda s: (slab_in(s), 0)),
            pl.BlockSpec((SLAB_G, STATE), lambda s: (slab_in(s), 0)),
            pl.BlockSpec((SLAB_G, 1), lambda s: (slab_in(s), 0)),
            coeff, coeff, coeff, coeff,
            pl.BlockSpec((None, 1, LANES), lambda s: (slab_in(s), 0, 0)),
            whole, whole,
        ] + [cast_spec(w) for w in tail_weights],
        out_specs=[pl.BlockSpec((None, rows, CHUNK_W), lambda s: (slab_out(s), 0, 0))]
        + [cast_spec(w) for w in tail_weights],
        out_shape=[jax.ShapeDtypeStruct(u_chunks.shape, BF16)]
        + [jax.ShapeDtypeStruct(w.shape, BF16) for w in tail_weights],
        scratch_shapes=[
            pltpu.VMEM((N_SLABS * n_chunks * batch * PAIRS, LANES), F32),
            pltpu.VMEM((N_SLABS * n_chunks * batch * PAIRS, LANES), F32),
            pltpu.VMEM((PAIRS, CHUNK_W, 2 * LANES), BF16),
            pltpu.VMEM((PAIRS, 2 * LANES, CHUNK_W), BF16),
            pltpu.VMEM((N_TILES, MXU, MXU), BF16),
        ],
        compiler_params=pltpu.CompilerParams(
            dimension_semantics=("arbitrary",), vmem_limit_bytes=VMEM_LIMIT),
        name="ssm",
    )(u_chunks, A_re, A_im, log_dt[:, None], btr, bti, cr, ci, d_rows, dta_re, dta_im,
      *tail_weights)


def _tail_kernel(ag_ref, yc_ref, z_ref, ga_ref, gs_ref, x_ref, wa_ref, wg_ref, bg_ref, ws_ref,
                 wo_ref, o_ref, ys_ref, *, tm):
    rows = tm // CHUNK
    for s in range(N_SLABS):
        for t in range(CHUNK):
            ys_ref[pl.ds(s * rows * US_PITCH + t, rows, stride=US_PITCH), :] = (
                yc_ref[s, :, t * LANES:(t + 1) * LANES].astype(F32))
    y_g = jnp.concatenate(
        [jnp.concatenate([ys_ref[(s * rows + c) * US_PITCH:(s * rows + c) * US_PITCH + CHUNK, :]
                          for c in range(rows)], axis=0) for s in range(N_SLABS)],
        axis=1).astype(BF16)

    y_a = _dot(ag_ref[...], wa_ref[...])
    glu = _dot(y_g, wg_ref[...]) + bg_ref[...]
    z = z_ref[...].astype(F32)
    t = glu[:, :SSM_W] * jax.nn.sigmoid(glu[:, SSM_W:]) * (z * jax.nn.sigmoid(z))
    y_s = _dot(t.astype(BF16), ws_ref[...])
    merged = (jax.nn.sigmoid(ga_ref[...].astype(F32)) * y_a
              + jax.nn.sigmoid(gs_ref[...].astype(F32)) * y_s)
    o_ref[...] = x_ref[...] + _dot(merged.astype(BF16), wo_ref[...])


def _tail(attn_g, y_chunks, proj, x2, w_attn, w_glu, b_glu, w_ssm, w_out, tm=256):
    m = x2.shape[0]
    row = lambda blk: (lambda i: (i, blk))
    const = lambda i: (0, 0)
    resident = functools.partial(pl.BlockSpec, index_map=const, pipeline_mode=pl.Buffered(1))
    kernel = functools.partial(_tail_kernel, tm=tm)
    return pl.pallas_call(
        kernel,
        grid=(m // tm,),
        in_specs=[
            pl.BlockSpec((tm, ATTN_W), row(0)),
            pl.BlockSpec((N_SLABS, tm // CHUNK, CHUNK_W), lambda i: (0, i, 0)),
            pl.BlockSpec((tm, SSM_W), row(COL_Z // SSM_W)),
            pl.BlockSpec((tm, D_MODEL), row(COL_GA // D_MODEL)),
            pl.BlockSpec((tm, D_MODEL), row(COL_GS // D_MODEL)),
            pl.BlockSpec((tm, D_MODEL), row(0)),
            resident((ATTN_W, D_MODEL)),
            resident((SSM_W, 2 * SSM_W)),
            resident((1, 2 * SSM_W)),
            resident((SSM_W, D_MODEL)),
            resident((D_MODEL, D_MODEL)),
        ],
        out_specs=pl.BlockSpec((tm, D_MODEL), row(0)),
        out_shape=jax.ShapeDtypeStruct((m, D_MODEL), F32),
        scratch_shapes=[pltpu.VMEM((N_SLABS * (tm // CHUNK) * US_PITCH, LANES), F32)],
        compiler_params=pltpu.CompilerParams(
            dimension_semantics=("arbitrary",), vmem_limit_bytes=VMEM_LIMIT),
        name="tail",
    )(attn_g, y_chunks, proj, proj, proj, x2, w_attn, w_glu, b_glu.reshape(1, -1), w_ssm, w_out)


def kernel(x, norm_w, w_in, q_norm_w, k_norm_w, sinks, w_attn_proj, A_re, A_im, log_dt, B_re, B_im,
           C_re, C_im, D_skip, w_glu, b_glu, w_ssm_proj, w_out):
    batch, seq, _ = x.shape
    m = batch * seq
    x2 = x.reshape(m, D_MODEL)

    proj, u_chunks = _in_proj(x2, norm_w, w_in)

    attn_g = _swa(proj, sinks, q_norm_w, k_norm_w, batch, seq)

    y_chunks, w_attn_bf, w_glu_bf, w_ssm_bf, w_out_bf = _ssm(
        u_chunks, A_re, A_im, log_dt, B_re, B_im, C_re, C_im, D_skip, batch,
        (w_attn_proj, w_glu, w_ssm_proj, w_out))

    out = _tail(attn_g, y_chunks, proj, x2, w_attn_bf, w_glu_bf, b_glu, w_ssm_bf, w_out_bf)
    return out.reshape(batch, seq, D_MODEL)
```

```python
import functools
import math

import jax
import jax.numpy as jnp
from jax import lax
from jax.experimental import pallas as pl
from jax.experimental.pallas import tpu as pltpu

F32 = jnp.float32
BF16 = jnp.bfloat16

D_MODEL = 2048
HEAD_DIM = 64
N_Q_HEADS = 16
N_KV_HEADS = 4
Q_PER_KV = 4
ATTN_W = N_Q_HEADS * HEAD_DIM
KV_W = N_KV_HEADS * HEAD_DIM
WINDOW = 128
SSM_W = D_MODEL // 2
GROUP = 16
N_GROUPS = SSM_W // GROUP
STATE = 64
NORM_EPS = 1e-6

COL_Q = 0
COL_GATE = COL_Q + ATTN_W
COL_U = COL_GATE + ATTN_W
COL_Z = COL_U + SSM_W
COL_GA = COL_Z + SSM_W
COL_GS = COL_GA + D_MODEL
COL_K = COL_GS + D_MODEL
COL_V = COL_K + KV_W
IN_W = COL_V + KV_W

LANES = 128
MXU = 256
CHUNK = 16
SLAB_G = LANES // GROUP
N_SLABS = N_GROUPS // SLAB_G
PAIRS = SLAB_G // 2
CHUNK_W = CHUNK * LANES
N_TILES = CHUNK_W // MXU
VMEM_LIMIT = 48 * 1024 * 1024


def _dot(a, b):
    return jnp.dot(a, b, preferred_element_type=F32)


IN_TN = 512
N_IN_TILES = IN_W // IN_TN
SRC_KV_TILE = (ATTN_W) // IN_TN
SRC_U_TILE0 = (ATTN_W + 2 * KV_W + ATTN_W) // IN_TN


US_PITCH = CHUNK if (CHUNK // 8) % 2 else CHUNK + 8
X_ROWS = 128
IN_PER_ROUND = 2


def _dst_tile(src_tile):
    if src_tile < SRC_KV_TILE:
        return src_tile
    if src_tile == SRC_KV_TILE:
        return N_IN_TILES - 1
    return src_tile - 1


def _in_proj_kernel(nw_ref, x_hbm, w_hbm, o_hbm, uc_ref, h_ref, x_buf, w_buf, o_buf, us_ref, x_sem,
                    w_sem, o_sem, *, tm, n_steps):
    i = pl.program_id(0)
    rows = tm // CHUNK
    n_sl = IN_TN // LANES
    n_rounds = pl.cdiv(N_IN_TILES, IN_PER_ROUND)
    assert n_rounds % 2 == 1
    n_xc = tm // X_ROWS
    assert n_xc < n_rounds
    cur, nxt = i % 2, (i + 1) % 2
    nxt_tile = jnp.minimum(i + 1, n_steps - 1)

    def x_copy(tile, c):
        src = x_hbm.at[pl.ds(pl.multiple_of(tile * tm + c * X_ROWS, X_ROWS), X_ROWS), :]
        return pltpu.make_async_copy(src, x_buf.at[c % 2], x_sem.at[c % 2])

    def norm_chunk(c, dst):
        x = x_buf[c % 2]
        ms = jnp.mean(x * x, axis=-1, keepdims=True)
        h_ref[dst, c * X_ROWS:(c + 1) * X_ROWS, :] = (
            x * lax.rsqrt(ms + NORM_EPS) * nw_ref[...]).astype(BF16)

    def round_tiles(r):
        return [t for t in range(r * IN_PER_ROUND, (r + 1) * IN_PER_ROUND) if 0 <= t < N_IN_TILES]

    def w_half(step, r):
        return (step + r) % 2

    def w_copy(step, r):
        tiles = round_tiles(r)
        width = len(tiles) * IN_TN
        half = w_half(step, r)
        return pltpu.make_async_copy(w_hbm.at[:, pl.ds(tiles[0] * IN_TN, width)],
                                     w_buf.at[half, :, pl.ds(0, width)], w_sem.at[half])

    def o_copies(r):
        tiles = round_tiles(r)
        runs, start = [], 0
        for j in range(1, len(tiles) + 1):
            if j == len(tiles) or _dst_tile(tiles[j]) != _dst_tile(tiles[j - 1]) + 1:
                runs.append((start, j - start))
                start = j
        row0 = pl.multiple_of(i * tm, tm)
        return [pltpu.make_async_copy(
            o_buf.at[r % 2, :, pl.ds(j0 * IN_TN, n * IN_TN)],
            o_hbm.at[pl.ds(row0, tm), pl.ds(_dst_tile(tiles[j0]) * IN_TN, n * IN_TN)],
            o_sem.at[r % 2, k]) for k, (j0, n) in enumerate(runs)]

    @pl.when(i == 0)
    def _():
        w_copy(0, 0).start()
        x_copy(0, 0).start()
        for c in range(n_xc):
            if c + 1 < n_xc:
                x_copy(0, c + 1).start()
            x_copy(0, c).wait()
            norm_chunk(c, 0)

    def finish(t, acc):
        r, j = divmod(t, IN_PER_ROUND)
        o_buf[r % 2, :, j * IN_TN:(j + 1) * IN_TN] = acc.astype(BF16)
        if SRC_U_TILE0 <= t < SRC_U_TILE0 + SSM_W // IN_TN:
            slab0 = (t - SRC_U_TILE0) * n_sl
            for s in range(n_sl):
                for c in range(rows):
                    r0 = (s * rows + c) * US_PITCH
                    us_ref[r0:r0 + CHUNK, :] = acc[c * CHUNK:(c + 1) * CHUNK, s * LANES:(s + 1) * LANES]
            for s in range(n_sl):
                for tok in range(CHUNK):
                    piece = us_ref[pl.ds(s * rows * US_PITCH + tok, rows, stride=US_PITCH), :]
                    uc_ref[slab0 + s, :, tok * LANES:(tok + 1) * LANES] = piece.astype(BF16)

    for r in range(n_rounds):
        if r + 1 < n_rounds:
            w_copy(i, r + 1).start()
        else:
            @pl.when(i + 1 < n_steps)
            def _():
                w_copy(i + 1, 0).start()
        w_copy(i, r).wait()
        if r >= 2:
            for c in o_copies(r - 2):
                c.wait()
        if r >= 1:
            for c in o_copies(r - 1):
                c.start()
        if r < n_xc:
            x_copy(nxt_tile, r).start()
        if 1 <= r <= n_xc:
            x_copy(nxt_tile, r - 1).wait()
        for j, t in enumerate(round_tiles(r)):
            acc = None
            for k in range(D_MODEL // MXU):
                ks = slice(k * MXU, (k + 1) * MXU)
                w_tile = w_buf[w_half(i, r), ks, j * IN_TN:(j + 1) * IN_TN]
                part = _dot(h_ref[cur, :, ks], w_tile.astype(BF16))
                acc = part if acc is None else acc + part
            finish(t, acc)
        if 1 <= r <= n_xc:
            norm_chunk(r - 1, nxt)

    for c in o_copies(n_rounds - 1):
        c.start()
    for c in o_copies(n_rounds - 2) + o_copies(n_rounds - 1):
        c.wait()


def _in_proj(x2, norm_w, w_in, tm=1024):
    m = x2.shape[0]
    kernel = functools.partial(_in_proj_kernel, tm=tm, n_steps=m // tm)
    return pl.pallas_call(
        kernel,
        grid=(m // tm,),
        in_specs=[
            pl.BlockSpec((1, D_MODEL), lambda i: (0, 0)),
            pl.BlockSpec(memory_space=pl.ANY),
            pl.BlockSpec(memory_space=pl.ANY),
        ],
        out_specs=[
            pl.BlockSpec(memory_space=pl.ANY),
            pl.BlockSpec((N_SLABS, tm // CHUNK, CHUNK_W), lambda i: (0, i, 0)),
        ],
        out_shape=[
            jax.ShapeDtypeStruct((m, IN_W), BF16),
            jax.ShapeDtypeStruct((N_SLABS, m // CHUNK, CHUNK_W), BF16),
        ],
        scratch_shapes=[
            pltpu.VMEM((2, tm, D_MODEL), BF16),
            pltpu.VMEM((2, X_ROWS, D_MODEL), F32),
            pltpu.VMEM((2, D_MODEL, IN_PER_ROUND * IN_TN), F32),
            pltpu.VMEM((2, tm, IN_PER_ROUND * IN_TN), BF16),
            pltpu.VMEM((IN_TN // LANES * (tm // CHUNK) * US_PITCH, LANES), F32),
            pltpu.SemaphoreType.DMA((2,)),
            pltpu.SemaphoreType.DMA((2,)),
            pltpu.SemaphoreType.DMA((2, IN_PER_ROUND)),
        ],
        compiler_params=pltpu.CompilerParams(
            dimension_semantics=("arbitrary",), vmem_limit_bytes=VMEM_LIMIT),
        name="in_proj",
    )(norm_w.reshape(1, D_MODEL), x2, w_in)


def _head_norm(t, w):
    ms = jnp.mean(t * t, axis=-1, keepdims=True)
    return t * lax.rsqrt(ms + NORM_EPS) * w


_NT = (((1,), (1,)), ((), ()))
_TN = (((0,), (0,)), ((), ()))


SWA_TQ = 1024
SWA_SUB = SWA_TQ // WINDOW


def _swa_tile(sink_ref, q_ref, g_ref, kc_ref, vc_ref, kp_ref, vp_ref, qw_ref, kw_ref, o_ref,
              first_tile):
    log2e = math.log2(math.e)
    kqw = kw_ref[...] * qw_ref[...] * (log2e / math.sqrt(HEAD_DIM))
    n_col = Q_PER_KV * WINDOW
    key = lax.broadcasted_iota(jnp.int32, (WINDOW, n_col), 0)
    qry = lax.broadcasted_iota(jnp.int32, (WINDOW, n_col), 1) % WINDOW
    from_prev = key > qry
    no_prev = jnp.where(first_tile, -1e30, 0.0)
    head_of_col = lax.broadcasted_iota(jnp.int32, (1, n_col), 1) // WINDOW
    gw = Q_PER_KV * HEAD_DIM
    sel_r = lax.broadcasted_iota(jnp.int32, (8, 2 * gw), 0)
    sel_l = (lax.broadcasted_iota(jnp.int32, (8, 2 * gw), 1) % gw) // HEAD_DIM
    head_sel = jnp.where(sel_r == sel_l, 1.0, 0.0).astype(BF16)

    parts = []
    carry = {}

    def block(g, n):
        kcol = slice(g * HEAD_DIM, (g + 1) * HEAD_DIM)
        gcols = slice(g * Q_PER_KV * HEAD_DIM, (g + 1) * Q_PER_KV * HEAD_DIM)
        if n == 0:
            sink = jnp.zeros((1, n_col), F32)
            for r in range(Q_PER_KV):
                sink = jnp.where(head_of_col == r, sink_ref[g * Q_PER_KV + r] * log2e, sink)
            carry['sink'] = sink
            carry['k'] = _head_norm(kp_ref[:, kcol].astype(F32), kqw).astype(BF16)
            carry['v'] = vp_ref[:, kcol]
        sink, k_prev, v_prev = carry['sink'], carry['k'], carry['v']
        rows = slice(n * WINDOW, (n + 1) * WINDOW)
        k_cur = _head_norm(kc_ref[rows, kcol].astype(F32), kqw).astype(BF16)
        v_cur = vc_ref[rows, kcol]
        k_ctx = jnp.concatenate([k_prev, k_cur], axis=0)
        v_ctx = jnp.concatenate([v_prev, v_cur], axis=0)

        qg = q_ref[rows, gcols]
        qf = qg.astype(F32)
        q_t = qf.T.astype(BF16)
        q_t = jnp.concatenate([q_t[r * HEAD_DIM:(r + 1) * HEAD_DIM] for r in range(Q_PER_KV)],
                              axis=1)
        q2 = qf * qf
        q2_hi = q2.astype(BF16)
        q2_lo = (q2 - q2_hi.astype(F32)).astype(BF16)
        ssq = lax.dot_general(head_sel, jnp.concatenate([q2_hi, q2_lo], axis=1), _NT,
                              preferred_element_type=F32)
        rms = lax.rsqrt(ssq * (1.0 / HEAD_DIM) + NORM_EPS)
        rms_q = jnp.concatenate([rms[r:r + 1] for r in range(Q_PER_KV)], axis=1)

        s = _dot(k_ctx, q_t)
        s_prev = s[:WINDOW] + no_prev if n == 0 else s[:WINDOW]
        s = jnp.where(from_prev, s_prev, s[WINDOW:]) * rms_q
        mx = jnp.maximum(jnp.max(s, axis=0, keepdims=True), sink)
        p = jnp.exp2(s - mx)
        den = jnp.sum(p, axis=0, keepdims=True) + jnp.exp2(sink - mx)
        p_ctx = jnp.concatenate([jnp.where(from_prev, p, 0.0), jnp.where(from_prev, 0.0, p)],
                                axis=0).astype(BF16)
        o_t = lax.dot_general(v_ctx, p_ctx, _TN, preferred_element_type=F32) * (1.0 / den)
        halves = [jnp.concatenate([o_t[:, (2 * h) * WINDOW:(2 * h + 1) * WINDOW],
                                   o_t[:, (2 * h + 1) * WINDOW:(2 * h + 2) * WINDOW]], axis=0).T
                  for h in range(Q_PER_KV // 2)]
        og = jnp.concatenate(halves, axis=1)
        gate = g_ref[rows, gcols].astype(F32)
        o_ref[rows, gcols] = (og * (gate * jax.nn.sigmoid(gate))).astype(BF16)
        carry['k'], carry['v'] = k_cur, v_cur

    for g in range(N_KV_HEADS):
        for n in range(SWA_SUB):
            parts.append(functools.partial(block, g, n))
    return parts


def _swa_kernel(*refs, tiles_per_seq):
    for block in _swa_tile(*refs, pl.program_id(0) % tiles_per_seq == 0):
        block()


def _swa(proj, sinks, q_norm_w, k_norm_w, batch, seq):
    m = batch * seq
    cur = lambda col: (lambda s: (s, col))
    prev = lambda col: (lambda s: (jnp.maximum(s * SWA_SUB - 1, 0), col))
    kernel = functools.partial(_swa_kernel, tiles_per_seq=seq // SWA_TQ)
    return pl.pallas_call(
        kernel,
        grid=(m // SWA_TQ,),
        in_specs=[
            pl.BlockSpec(memory_space=pltpu.SMEM),
            pl.BlockSpec((SWA_TQ, ATTN_W), cur(COL_Q // ATTN_W)),
            pl.BlockSpec((SWA_TQ, ATTN_W), cur(COL_GATE // ATTN_W)),
            pl.BlockSpec((SWA_TQ, KV_W), cur(COL_K // KV_W)),
            pl.BlockSpec((SWA_TQ, KV_W), cur(COL_V // KV_W)),
            pl.BlockSpec((WINDOW, KV_W), prev(COL_K // KV_W)),
            pl.BlockSpec((WINDOW, KV_W), prev(COL_V // KV_W)),
            pl.BlockSpec((1, HEAD_DIM), lambda s: (0, 0)),
            pl.BlockSpec((1, HEAD_DIM), lambda s: (0, 0)),
        ],
        out_specs=pl.BlockSpec((SWA_TQ, ATTN_W), cur(0)),
        out_shape=jax.ShapeDtypeStruct((m, ATTN_W), BF16),
        compiler_params=pltpu.CompilerParams(
            dimension_semantics=("arbitrary",), vmem_limit_bytes=VMEM_LIMIT),
        name="swa",
    )(sinks, proj, proj, proj, proj, proj, proj,
      q_norm_w.reshape(1, HEAD_DIM), k_norm_w.reshape(1, HEAD_DIM))


def _slab_powers(ar, ai, dt, btr, bti):
    dta_re, dta_im = dt * ar, dt * ai
    mag = jnp.exp(dta_re)
    ab_re, ab_im = mag * jnp.cos(dta_im), mag * jnp.sin(dta_im)
    pw = [(jnp.ones_like(ar), jnp.zeros_like(ar))]
    for _ in range(CHUNK):
        pr, pi = pw[-1]
        pw.append((pr * ab_re - pi * ab_im, pr * ab_im + pi * ab_re))
    den = ar * ar + ai * ai
    num_re, num_im = ab_re - 1.0, ab_im
    cf_re = (num_re * ar + num_im * ai) / den
    cf_im = (num_im * ar - num_re * ai) / den
    bb_re = cf_re * btr - cf_im * bti
    bb_im = cf_re * bti + cf_im * btr
    abb = [(pr * bb_re - pi * bb_im, pr * bb_im + pi * bb_re) for pr, pi in pw[:CHUNK]]
    return pw, abb


def _build_state_in(abb, p_s):
    row_g = lax.broadcasted_iota(jnp.int32, (LANES, STATE), 0) // GROUP
    even = row_g % 2 == 0
    pair_of_row = lax.broadcasted_iota(jnp.int32, (LANES, 2 * LANES), 0) // (2 * GROUP)
    for lag in range(CHUNK):
        t = CHUNK - 1 - lag
        re, im = abb[lag]
        blk = jnp.concatenate([jnp.where(even, re, 0.0), jnp.where(even, 0.0, re),
                               jnp.where(even, im, 0.0), jnp.where(even, 0.0, im)], axis=1)
        for k in range(PAIRS):
            p_s[k, t * LANES:(t + 1) * LANES, :] = jnp.where(pair_of_row == k, blk, 0.0).astype(BF16)


def _build_state_out(pw, cr, ci, q_s):
    lane_g = lax.broadcasted_iota(jnp.int32, (STATE, LANES), 1) // GROUP
    for t in range(CHUNK):
        pr, pi = pw[t + 1]
        cat = jnp.concatenate([cr * pr - ci * pi, -(cr * pi + ci * pr)], axis=1)
        cat_t = cat.T
        for k in range(PAIRS):
            for ri in range(2):
                for half in range(2):
                    piece = jnp.where(lane_g == 2 * k + half, cat_t[ri * STATE:(ri + 1) * STATE], 0.0)
                    r0 = ri * LANES + half * STATE
                    q_s[k, r0:r0 + STATE, t * LANES:(t + 1) * LANES] = piece.astype(BF16)


def _build_toeplitz(abb, cr, ci, d_row, w_s):
    def split(v):
        hi = v.astype(BF16)
        return hi, (v - hi.astype(F32)).astype(BF16)

    rhs_hi, rhs_lo = split(jnp.concatenate([cr, -ci], axis=1))
    rhs = jnp.concatenate([rhs_hi, rhs_lo, rhs_hi], axis=1)
    row = lax.broadcasted_iota(jnp.int32, (LANES, LANES), 0)
    col = lax.broadcasted_iota(jnp.int32, (LANES, LANES), 1)
    same_group = (row // GROUP) == (col // GROUP)
    taps = []
    for lag in range(CHUNK):
        lhs_hi, lhs_lo = split(jnp.concatenate(abb[lag], axis=1))
        k = lax.dot_general(jnp.concatenate([lhs_hi, lhs_hi, lhs_lo], axis=1), rhs,
                            (((1,), (1,)), ((), ())), preferred_element_type=F32)
        k = jnp.where(same_group, k, 0.0)
        if lag == 0:
            k = k + jnp.where(row == col, d_row, 0.0)
        taps.append(k.astype(BF16))
    zero = jnp.zeros((LANES, LANES), BF16)
    for d in range(N_TILES):
        w_s[d, :LANES, :LANES] = taps[2 * d]
        w_s[d, :LANES, LANES:] = taps[2 * d + 1]
        w_s[d, LANES:, :LANES] = taps[2 * d - 1] if d > 0 else zero
        w_s[d, LANES:, LANES:] = taps[2 * d]


def _ssm_kernel(u_ref, ar_ref, ai_ref, dt_ref, bt_ref, cr_ref, ci_ref, d_ref, dta_ref,
                w0_ref, w1_ref, w2_ref, w3_ref, y_ref, c0_ref, c1_ref, c2_ref, c3_ref,
                xre_s, xim_s, p_s, q_s, w_s, *, batch, n_chunks):
    step = pl.program_id(0)
    for src, dst in ((w0_ref, c0_ref), (w1_ref, c1_ref), (w2_ref, c2_ref), (w3_ref, c3_ref)):
        dst[...] = src[...].astype(BF16)
    rows8 = batch * PAIRS
    slab_rows = n_chunks * rows8

    def state_rows(slab, b, k):
        return pl.ds(slab * slab_rows + b * PAIRS + k, n_chunks, stride=rows8)

    def per_row(v):
        v = jnp.broadcast_to(v, (SLAB_G, STATE))
        return jnp.broadcast_to(v[:, None, :], (SLAB_G, GROUP, STATE)).reshape(LANES, STATE)

    def powers():
        return _slab_powers(per_row(ar_ref[...]), per_row(ai_ref[...]),
                            per_row(jnp.exp(dt_ref[...])), bt_ref[0], bt_ref[1])

    @pl.when(step < N_SLABS)
    def _():
        _, abb = powers()
        _build_state_in(abb, p_s)
        u = u_ref[...]
        for k in range(PAIRS):
            x = _dot(u, p_s[k])
            for b in range(batch):
                xb = x[b * n_chunks:(b + 1) * n_chunks]
                xre_s[state_rows(step, b, k), :] = xb[:, :LANES]
                xim_s[state_rows(step, b, k), :] = xb[:, LANES:]

    @pl.when(step == N_SLABS)
    def _():
        dta_re = jnp.concatenate([dta_ref[0]] * batch, axis=1).reshape(N_SLABS * rows8, LANES)
        dta_im = jnp.concatenate([dta_ref[1]] * batch, axis=1).reshape(N_SLABS * rows8, LANES)
        mag = jnp.exp(CHUNK * dta_re)
        a_re, a_im = mag * jnp.cos(CHUNK * dta_im), mag * jnp.sin(CHUNK * dta_im)

        def body(c, carry):
            s_re, s_im = carry
            x_re, x_im = [], []
            for sl in range(N_SLABS):
                off = pl.multiple_of(sl * slab_rows + c * rows8, rows8)
                x_re.append(xre_s[pl.ds(off, rows8), :])
                x_im.append(xim_s[pl.ds(off, rows8), :])
                xre_s[pl.ds(off, rows8), :] = s_re[sl * rows8:(sl + 1) * rows8]
                xim_s[pl.ds(off, rows8), :] = s_im[sl * rows8:(sl + 1) * rows8]
            x_re = jnp.concatenate(x_re, axis=0)
            x_im = jnp.concatenate(x_im, axis=0)
            return (a_re * s_re - a_im * s_im + x_re, a_re * s_im + a_im * s_re + x_im)

        zero = jnp.zeros((N_SLABS * rows8, LANES), F32)
        lax.fori_loop(0, n_chunks, body, (zero, zero))

    @pl.when(step > N_SLABS)
    def _():
        slab = step - (N_SLABS + 1)
        pw, abb = powers()
        cr, ci = cr_ref[...], ci_ref[...]
        _build_state_out(pw, cr, ci, q_s)
        _build_toeplitz(abb, cr, ci, d_ref[...], w_s)
        u = u_ref[...]
        y_state = None
        for k in range(PAIRS):
            s_in = jnp.concatenate(
                [jnp.concatenate([xre_s[state_rows(slab, b, k), :], xim_s[state_rows(slab, b, k), :]],
                                 axis=1) for b in range(batch)], axis=0).astype(BF16)
            part = _dot(s_in, q_s[k])
            y_state = part if y_state is None else y_state + part
        for t2 in range(N_TILES):
            acc = y_state[:, t2 * MXU:(t2 + 1) * MXU]
            for t1 in range(t2 + 1):
                acc = acc + _dot(u[:, t1 * MXU:(t1 + 1) * MXU], w_s[t2 - t1])
            y_ref[:, t2 * MXU:(t2 + 1) * MXU] = jax.nn.gelu(acc).astype(BF16)


N_CAST = 16


def _ssm(u_chunks, A_re, A_im, log_dt, B_re, B_im, C_re, C_im, D_skip, batch, tail_weights):
    _, rows, _ = u_chunks.shape
    n_chunks = rows // batch
    n_steps = 2 * N_SLABS + 1
    kernel = functools.partial(_ssm_kernel, batch=batch, n_chunks=n_chunks)

    dt = jnp.exp(log_dt)[:, None]
    bt = jnp.stack([B_re, B_im]).transpose(0, 1, 3, 2).reshape(2, SSM_W, STATE)
    cr, ci = C_re.reshape(SSM_W, STATE), C_im.reshape(SSM_W, STATE)
    d_rows = D_skip.reshape(N_SLABS, 1, LANES)
    dta = (dt * jnp.stack([A_re, A_im])).reshape(2, N_SLABS, PAIRS, LANES)

    def slab_in(s):
        return jnp.where(s <= N_SLABS, jnp.minimum(s, N_SLABS - 1), s - (N_SLABS + 1))

    def slab_out(s):
        return jnp.maximum(s - (N_SLABS + 1), 0)

    def cast_spec(w):
        return pl.BlockSpec((w.shape[0] // N_CAST, w.shape[1]),
                            lambda s: (jnp.minimum(s, N_CAST - 1), 0))

    coeff = pl.BlockSpec((LANES, STATE), lambda s: (slab_in(s), 0))
    return pl.pallas_call(
        kernel,
        grid=(n_steps,),
        in_specs=[
            pl.BlockSpec((None, rows, CHUNK_W), lambda s: (slab_in(s), 0, 0)),
            pl.BlockSpec((SLAB_G, STATE), lambda s: (slab_in(s), 0)),
            pl.BlockSpec((SLAB_G, STATE), lambda s: (slab_in(s), 0)),
            pl.BlockSpec((SLAB_G, 1), lambda s: (slab_in(s), 0)),
            pl.BlockSpec((2, LANES, STATE), lambda s: (0, slab_in(s), 0)),
            coeff, coeff,
            pl.BlockSpec((None, 1, LANES), lambda s: (slab_in(s), 0, 0)),
            pl.BlockSpec((2, N_SLABS, PAIRS, LANES), lambda s: (0, 0, 0, 0)),
        ] + [cast_spec(w) for w in tail_weights],
        out_specs=[pl.BlockSpec((None, rows, CHUNK_W), lambda s: (slab_out(s), 0, 0))]
        + [cast_spec(w) for w in tail_weights],
        out_shape=[jax.ShapeDtypeStruct(u_chunks.shape, BF16)]
        + [jax.ShapeDtypeStruct(w.shape, BF16) for w in tail_weights],
        scratch_shapes=[
            pltpu.VMEM((N_SLABS * n_chunks * batch * PAIRS, LANES), F32),
            pltpu.VMEM((N_SLABS * n_chunks * batch * PAIRS, LANES), F32),
            pltpu.VMEM((PAIRS, CHUNK_W, 2 * LANES), BF16),
            pltpu.VMEM((PAIRS, 2 * LANES, CHUNK_W), BF16),
            pltpu.VMEM((N_TILES, MXU, MXU), BF16),
        ],
        compiler_params=pltpu.CompilerParams(
            dimension_semantics=("arbitrary",), vmem_limit_bytes=VMEM_LIMIT),
        name="ssm",
    )(u_chunks, A_re, A_im, log_dt[:, None], bt, cr, ci, d_rows, dta, *tail_weights)


def _tail_kernel(ag_ref, yc_ref, z_ref, ga_ref, gs_ref, x_ref, wa_ref, wg_ref, bg_ref, ws_ref,
                 wo_ref, o_ref, ys_ref, *, tm):
    rows = tm // CHUNK
    for s in range(N_SLABS):
        for t in range(CHUNK):
            ys_ref[pl.ds(s * rows * US_PITCH + t, rows, stride=US_PITCH), :] = (
                yc_ref[s, :, t * LANES:(t + 1) * LANES].astype(F32))
    y_g = jnp.concatenate(
        [jnp.concatenate([ys_ref[(s * rows + c) * US_PITCH:(s * rows + c) * US_PITCH + CHUNK, :]
                          for c in range(rows)], axis=0) for s in range(N_SLABS)],
        axis=1).astype(BF16)

    y_a = _dot(ag_ref[...], wa_ref[...])
    glu = _dot(y_g, wg_ref[...]) + bg_ref[...]
    z = z_ref[...].astype(F32)
    t = glu[:, :SSM_W] * jax.nn.sigmoid(glu[:, SSM_W:]) * (z * jax.nn.sigmoid(z))
    y_s = _dot(t.astype(BF16), ws_ref[...])
    merged = (jax.nn.sigmoid(ga_ref[...].astype(F32)) * y_a
              + jax.nn.sigmoid(gs_ref[...].astype(F32)) * y_s)
    o_ref[...] = x_ref[...] + _dot(merged.astype(BF16), wo_ref[...])


def _tail(attn_g, y_chunks, proj, x2, w_attn, w_glu, b_glu, w_ssm, w_out, tm=256):
    m = x2.shape[0]
    row = lambda blk: (lambda i: (i, blk))
    const = lambda i: (0, 0)
    resident = functools.partial(pl.BlockSpec, index_map=const, pipeline_mode=pl.Buffered(1))
    kernel = functools.partial(_tail_kernel, tm=tm)
    return pl.pallas_call(
        kernel,
        grid=(m // tm,),
        in_specs=[
            pl.BlockSpec((tm, ATTN_W), row(0)),
            pl.BlockSpec((N_SLABS, tm // CHUNK, CHUNK_W), lambda i: (0, i, 0)),
            pl.BlockSpec((tm, SSM_W), row(COL_Z // SSM_W)),
            pl.BlockSpec((tm, D_MODEL), row(COL_GA // D_MODEL)),
            pl.BlockSpec((tm, D_MODEL), row(COL_GS // D_MODEL)),
            pl.BlockSpec((tm, D_MODEL), row(0)),
            resident((ATTN_W, D_MODEL)),
            resident((SSM_W, 2 * SSM_W)),
            resident((1, 2 * SSM_W)),
            resident((SSM_W, D_MODEL)),
            resident((D_MODEL, D_MODEL)),
        ],
        out_specs=pl.BlockSpec((tm, D_MODEL), row(0)),
        out_shape=jax.ShapeDtypeStruct((m, D_MODEL), F32),
        scratch_shapes=[pltpu.VMEM((N_SLABS * (tm // CHUNK) * US_PITCH, LANES), F32)],
        compiler_params=pltpu.CompilerParams(
            dimension_semantics=("arbitrary",), vmem_limit_bytes=VMEM_LIMIT),
        name="tail",
    )(attn_g, y_chunks, proj, proj, proj, x2, w_attn, w_glu, b_glu.reshape(1, -1), w_ssm, w_out)


def kernel(x, norm_w, w_in, q_norm_w, k_norm_w, sinks, w_attn_proj, A_re, A_im, log_dt, B_re, B_im,
           C_re, C_im, D_skip, w_glu, b_glu, w_ssm_proj, w_out):
    batch, seq, _ = x.shape
    m = batch * seq
    x2 = x.reshape(m, D_MODEL)

    proj, u_chunks = _in_proj(x2, norm_w, w_in)

    attn_g = _swa(proj, sinks, q_norm_w, k_norm_w, batch, seq)

    y_chunks, w_attn_bf, w_glu_bf, w_ssm_bf, w_out_bf = _ssm(
        u_chunks, A_re, A_im, log_dt, B_re, B_im, C_re, C_im, D_skip, batch,
        (w_attn_proj, w_glu, w_ssm_proj, w_out))

    out = _tail(attn_g, y_chunks, proj, x2, w_attn_bf, w_glu_bf, b_glu, w_ssm_bf, w_out_bf)
    return out.reshape(batch, seq, D_MODEL)
```

```python
import functools
import math

import jax
import jax.numpy as jnp
from jax import lax
from jax.experimental import pallas as pl
from jax.experimental.pallas import tpu as pltpu

F32 = jnp.float32
BF16 = jnp.bfloat16

D_MODEL = 2048
HEAD_DIM = 64
N_Q_HEADS = 16
N_KV_HEADS = 4
Q_PER_KV = 4
ATTN_W = N_Q_HEADS * HEAD_DIM
KV_W = N_KV_HEADS * HEAD_DIM
WINDOW = 128
SSM_W = D_MODEL // 2
GROUP = 16
N_GROUPS = SSM_W // GROUP
STATE = 64
NORM_EPS = 1e-6

COL_Q = 0
COL_GATE = COL_Q + ATTN_W
COL_U = COL_GATE + ATTN_W
COL_Z = COL_U + SSM_W
COL_GA = COL_Z + SSM_W
COL_GS = COL_GA + D_MODEL
COL_K = COL_GS + D_MODEL
COL_V = COL_K + KV_W
IN_W = COL_V + KV_W

LANES = 128
MXU = 256
CHUNK = 16
SLAB_G = LANES // GROUP
N_SLABS = N_GROUPS // SLAB_G
PAIRS = SLAB_G // 2
CHUNK_W = CHUNK * LANES
N_TILES = CHUNK_W // MXU
VMEM_LIMIT = 48 * 1024 * 1024


def _dot(a, b):
    return jnp.dot(a, b, preferred_element_type=F32)


IN_TN = 512
N_IN_TILES = IN_W // IN_TN
SRC_KV_TILE = (ATTN_W) // IN_TN
SRC_U_TILE0 = (ATTN_W + 2 * KV_W + ATTN_W) // IN_TN


US_PITCH = CHUNK if (CHUNK // 8) % 2 else CHUNK + 8
X_ROWS = 128
IN_PER_ROUND = 2


def _dst_tile(src_tile):
    if src_tile < SRC_KV_TILE:
        return src_tile
    if src_tile == SRC_KV_TILE:
        return N_IN_TILES - 1
    return src_tile - 1


def _in_proj_kernel(nw_ref, x_hbm, w_hbm, o_hbm, uc_ref, h_ref, x_buf, w_buf, o_buf, us_ref, x_sem,
                    w_sem, o_sem, *, tm, n_steps):
    i = pl.program_id(0)
    rows = tm // CHUNK
    n_sl = IN_TN // LANES
    n_rounds = pl.cdiv(N_IN_TILES, IN_PER_ROUND)
    assert n_rounds % 2 == 1
    n_xc = tm // X_ROWS
    assert n_xc < n_rounds
    cur, nxt = i % 2, (i + 1) % 2
    nxt_tile = jnp.minimum(i + 1, n_steps - 1)

    def x_copy(tile, c):
        src = x_hbm.at[pl.ds(pl.multiple_of(tile * tm + c * X_ROWS, X_ROWS), X_ROWS), :]
        return pltpu.make_async_copy(src, x_buf.at[c % 2], x_sem.at[c % 2])

    def norm_chunk(c, dst):
        x = x_buf[c % 2]
        ms = jnp.mean(x * x, axis=-1, keepdims=True)
        h_ref[dst, c * X_ROWS:(c + 1) * X_ROWS, :] = (
            x * lax.rsqrt(ms + NORM_EPS) * nw_ref[...]).astype(BF16)

    def round_tiles(r):
        return [t for t in range(r * IN_PER_ROUND, (r + 1) * IN_PER_ROUND) if 0 <= t < N_IN_TILES]

    def w_half(step, r):
        return (step + r) % 2

    def w_copy(step, r):
        tiles = round_tiles(r)
        width = len(tiles) * IN_TN
        half = w_half(step, r)
        return pltpu.make_async_copy(w_hbm.at[:, pl.ds(tiles[0] * IN_TN, width)],
                                     w_buf.at[half, :, pl.ds(0, width)], w_sem.at[half])

    def o_copies(r):
        tiles = round_tiles(r)
        runs, start = [], 0
        for j in range(1, len(tiles) + 1):
            if j == len(tiles) or _dst_tile(tiles[j]) != _dst_tile(tiles[j - 1]) + 1:
                runs.append((start, j - start))
                start = j
        row0 = pl.multiple_of(i * tm, tm)
        return [pltpu.make_async_copy(
            o_buf.at[r % 2, :, pl.ds(j0 * IN_TN, n * IN_TN)],
            o_hbm.at[pl.ds(row0, tm), pl.ds(_dst_tile(tiles[j0]) * IN_TN, n * IN_TN)],
            o_sem.at[r % 2, k]) for k, (j0, n) in enumerate(runs)]

    @pl.when(i == 0)
    def _():
        w_copy(0, 0).start()
        x_copy(0, 0).start()
        for c in range(n_xc):
            if c + 1 < n_xc:
                x_copy(0, c + 1).start()
            x_copy(0, c).wait()
            norm_chunk(c, 0)

    def finish(t, acc):
        r, j = divmod(t, IN_PER_ROUND)
        o_buf[r % 2, :, j * IN_TN:(j + 1) * IN_TN] = acc.astype(BF16)
        if SRC_U_TILE0 <= t < SRC_U_TILE0 + SSM_W // IN_TN:
            slab0 = (t - SRC_U_TILE0) * n_sl
            for s in range(n_sl):
                for c in range(rows):
                    r0 = (s * rows + c) * US_PITCH
                    us_ref[r0:r0 + CHUNK, :] = acc[c * CHUNK:(c + 1) * CHUNK, s * LANES:(s + 1) * LANES]
            for s in range(n_sl):
                for tok in range(CHUNK):
                    piece = us_ref[pl.ds(s * rows * US_PITCH + tok, rows, stride=US_PITCH), :]
                    uc_ref[slab0 + s, :, tok * LANES:(tok + 1) * LANES] = piece.astype(BF16)

    for r in range(n_rounds):
        if r + 1 < n_rounds:
            w_copy(i, r + 1).start()
        else:
            @pl.when(i + 1 < n_steps)
            def _():
                w_copy(i + 1, 0).start()
        w_copy(i, r).wait()
        if r >= 2:
            for c in o_copies(r - 2):
                c.wait()
        if r >= 1:
            for c in o_copies(r - 1):
                c.start()
        if r < n_xc:
            x_copy(nxt_tile, r).start()
        if 1 <= r <= n_xc:
            x_copy(nxt_tile, r - 1).wait()
        for j, t in enumerate(round_tiles(r)):
            acc = None
            for k in range(D_MODEL // MXU):
                ks = slice(k * MXU, (k + 1) * MXU)
                w_tile = w_buf[w_half(i, r), ks, j * IN_TN:(j + 1) * IN_TN]
                part = _dot(h_ref[cur, :, ks], w_tile.astype(BF16))
                acc = part if acc is None else acc + part
            finish(t, acc)
        if 1 <= r <= n_xc:
            norm_chunk(r - 1, nxt)

    for c in o_copies(n_rounds - 1):
        c.start()
    for c in o_copies(n_rounds - 2) + o_copies(n_rounds - 1):
        c.wait()


def _in_proj(x2, norm_w, w_in, tm=1024):
    m = x2.shape[0]
    kernel = functools.partial(_in_proj_kernel, tm=tm, n_steps=m // tm)
    return pl.pallas_call(
        kernel,
        grid=(m // tm,),
        in_specs=[
            pl.BlockSpec((1, D_MODEL), lambda i: (0, 0)),
            pl.BlockSpec(memory_space=pl.ANY),
            pl.BlockSpec(memory_space=pl.ANY),
        ],
        out_specs=[
            pl.BlockSpec(memory_space=pl.ANY),
            pl.BlockSpec((N_SLABS, tm // CHUNK, CHUNK_W), lambda i: (0, i, 0)),
        ],
        out_shape=[
            jax.ShapeDtypeStruct((m, IN_W), BF16),
            jax.ShapeDtypeStruct((N_SLABS, m // CHUNK, CHUNK_W), BF16),
        ],
        scratch_shapes=[
            pltpu.VMEM((2, tm, D_MODEL), BF16),
            pltpu.VMEM((2, X_ROWS, D_MODEL), F32),
            pltpu.VMEM((2, D_MODEL, IN_PER_ROUND * IN_TN), F32),
            pltpu.VMEM((2, tm, IN_PER_ROUND * IN_TN), BF16),
            pltpu.VMEM((IN_TN // LANES * (tm // CHUNK) * US_PITCH, LANES), F32),
            pltpu.SemaphoreType.DMA((2,)),
            pltpu.SemaphoreType.DMA((2,)),
            pltpu.SemaphoreType.DMA((2, IN_PER_ROUND)),
        ],
        compiler_params=pltpu.CompilerParams(
            dimension_semantics=("arbitrary",), vmem_limit_bytes=VMEM_LIMIT),
        name="in_proj",
    )(norm_w.reshape(1, D_MODEL), x2, w_in)


def _head_norm(t, w):
    ms = jnp.mean(t * t, axis=-1, keepdims=True)
    return t * lax.rsqrt(ms + NORM_EPS) * w


_NT = (((1,), (1,)), ((), ()))
_TN = (((0,), (0,)), ((), ()))


def _swa_tile(sink_ref, q_ref, g_ref, kc_ref, vc_ref, kp_ref, vp_ref, qw_ref, kw_ref, o_ref,
              first_tile, n_sub):
    log2e = math.log2(math.e)
    kqw = kw_ref[...] * qw_ref[...] * (log2e / math.sqrt(HEAD_DIM))
    n_col = Q_PER_KV * WINDOW
    key = lax.broadcasted_iota(jnp.int32, (WINDOW, n_col), 0)
    qry = lax.broadcasted_iota(jnp.int32, (WINDOW, n_col), 1) % WINDOW
    from_prev = key > qry
    no_prev = jnp.where(first_tile, -1e30, 0.0)
    head_of_col = lax.broadcasted_iota(jnp.int32, (1, n_col), 1) // WINDOW
    gw = Q_PER_KV * HEAD_DIM
    sel_r = lax.broadcasted_iota(jnp.int32, (8, 2 * gw), 0)
    sel_l = (lax.broadcasted_iota(jnp.int32, (8, 2 * gw), 1) % gw) // HEAD_DIM
    head_sel = jnp.where(sel_r == sel_l, 1.0, 0.0).astype(BF16)

    parts = []
    carry = {}

    def block(g, n):
        kcol = slice(g * HEAD_DIM, (g + 1) * HEAD_DIM)
        gcols = slice(g * Q_PER_KV * HEAD_DIM, (g + 1) * Q_PER_KV * HEAD_DIM)
        if n == 0:
            sink = jnp.zeros((1, n_col), F32)
            for r in range(Q_PER_KV):
                sink = jnp.where(head_of_col == r, sink_ref[g * Q_PER_KV + r] * log2e, sink)
            carry['sink'] = sink
            carry['k'] = _head_norm(kp_ref[:, kcol].astype(F32), kqw).astype(BF16)
            carry['v'] = vp_ref[:, kcol]
        sink, k_prev, v_prev = carry['sink'], carry['k'], carry['v']
        rows = slice(n * WINDOW, (n + 1) * WINDOW)
        k_cur = _head_norm(kc_ref[rows, kcol].astype(F32), kqw).astype(BF16)
        v_cur = vc_ref[rows, kcol]
        k_ctx = jnp.concatenate([k_prev, k_cur], axis=0)
        v_ctx = jnp.concatenate([v_prev, v_cur], axis=0)

        qg = q_ref[rows, gcols]
        qf = qg.astype(F32)
        q_t = qf.T.astype(BF16)
        q_t = jnp.concatenate([q_t[r * HEAD_DIM:(r + 1) * HEAD_DIM] for r in range(Q_PER_KV)],
                              axis=1)
        q2 = qf * qf
        q2_hi = q2.astype(BF16)
        q2_lo = (q2 - q2_hi.astype(F32)).astype(BF16)
        ssq = lax.dot_general(head_sel, jnp.concatenate([q2_hi, q2_lo], axis=1), _NT,
                              preferred_element_type=F32)
        rms = lax.rsqrt(ssq * (1.0 / HEAD_DIM) + NORM_EPS)
        rms_q = jnp.concatenate([rms[r:r + 1] for r in range(Q_PER_KV)], axis=1)

        s = _dot(k_ctx, q_t)
        s_prev = s[:WINDOW] + no_prev if n == 0 else s[:WINDOW]
        s = jnp.where(from_prev, s_prev, s[WINDOW:]) * rms_q
        mx = jnp.maximum(jnp.max(s, axis=0, keepdims=True), sink)
        p = jnp.exp2(s - mx)
        den = jnp.sum(p, axis=0, keepdims=True) + jnp.exp2(sink - mx)
        p_ctx = jnp.concatenate([jnp.where(from_prev, p, 0.0), jnp.where(from_prev, 0.0, p)],
                                axis=0).astype(BF16)
        o_t = lax.dot_general(v_ctx, p_ctx, _TN, preferred_element_type=F32) * (1.0 / den)
        halves = [jnp.concatenate([o_t[:, (2 * h) * WINDOW:(2 * h + 1) * WINDOW],
                                   o_t[:, (2 * h + 1) * WINDOW:(2 * h + 2) * WINDOW]], axis=0).T
                  for h in range(Q_PER_KV // 2)]
        og = jnp.concatenate(halves, axis=1)
        gate = g_ref[rows, gcols].astype(F32)
        o_ref[rows, gcols] = (og * (gate * jax.nn.sigmoid(gate))).astype(BF16)
        carry['k'], carry['v'] = k_cur, v_cur

    for g in range(N_KV_HEADS):
        for n in range(n_sub):
            parts.append(functools.partial(block, g, n))
    return parts


def _slab_powers(ar, ai, dt, btr, bti):
    dta_re, dta_im = dt * ar, dt * ai
    mag = jnp.exp(dta_re)
    ab_re, ab_im = mag * jnp.cos(dta_im), mag * jnp.sin(dta_im)
    pw = [(jnp.ones_like(ar), jnp.zeros_like(ar))]
    for _ in range(CHUNK):
        pr, pi = pw[-1]
        pw.append((pr * ab_re - pi * ab_im, pr * ab_im + pi * ab_re))
    den = ar * ar + ai * ai
    num_re, num_im = ab_re - 1.0, ab_im
    cf_re = (num_re * ar + num_im * ai) / den
    cf_im = (num_im * ar - num_re * ai) / den
    bb_re = cf_re * btr - cf_im * bti
    bb_im = cf_re * bti + cf_im * btr
    abb = [(pr * bb_re - pi * bb_im, pr * bb_im + pi * bb_re) for pr, pi in pw[:CHUNK]]
    return pw, abb


def _build_state_in(abb, p_s):
    row_g = lax.broadcasted_iota(jnp.int32, (LANES, STATE), 0) // GROUP
    even = row_g % 2 == 0
    pair_of_row = lax.broadcasted_iota(jnp.int32, (LANES, 2 * LANES), 0) // (2 * GROUP)
    for lag in range(CHUNK):
        t = CHUNK - 1 - lag
        re, im = abb[lag]
        blk = jnp.concatenate([jnp.where(even, re, 0.0), jnp.where(even, 0.0, re),
                               jnp.where(even, im, 0.0), jnp.where(even, 0.0, im)], axis=1)
        for k in range(PAIRS):
            p_s[k, t * LANES:(t + 1) * LANES, :] = jnp.where(pair_of_row == k, blk, 0.0).astype(BF16)


def _build_state_out(pw, cr, ci, q_s):
    lane_g = lax.broadcasted_iota(jnp.int32, (STATE, LANES), 1) // GROUP
    for t in range(CHUNK):
        pr, pi = pw[t + 1]
        cat = jnp.concatenate([cr * pr - ci * pi, -(cr * pi + ci * pr)], axis=1)
        cat_t = cat.T
        for k in range(PAIRS):
            for ri in range(2):
                for half in range(2):
                    piece = jnp.where(lane_g == 2 * k + half, cat_t[ri * STATE:(ri + 1) * STATE], 0.0)
                    r0 = ri * LANES + half * STATE
                    q_s[k, r0:r0 + STATE, t * LANES:(t + 1) * LANES] = piece.astype(BF16)


def _build_toeplitz(abb, cr, ci, d_row, w_s):
    def split(v):
        hi = v.astype(BF16)
        return hi, (v - hi.astype(F32)).astype(BF16)

    rhs_hi, rhs_lo = split(jnp.concatenate([cr, -ci], axis=1))
    rhs = jnp.concatenate([rhs_hi, rhs_lo, rhs_hi], axis=1)
    row = lax.broadcasted_iota(jnp.int32, (LANES, LANES), 0)
    col = lax.broadcasted_iota(jnp.int32, (LANES, LANES), 1)
    same_group = (row // GROUP) == (col // GROUP)
    taps = []
    for lag in range(CHUNK):
        lhs_hi, lhs_lo = split(jnp.concatenate(abb[lag], axis=1))
        k = lax.dot_general(jnp.concatenate([lhs_hi, lhs_hi, lhs_lo], axis=1), rhs,
                            (((1,), (1,)), ((), ())), preferred_element_type=F32)
        k = jnp.where(same_group, k, 0.0)
        if lag == 0:
            k = k + jnp.where(row == col, d_row, 0.0)
        taps.append(k.astype(BF16))
    zero = jnp.zeros((LANES, LANES), BF16)
    for d in range(N_TILES):
        w_s[d, :LANES, :LANES] = taps[2 * d]
        w_s[d, :LANES, LANES:] = taps[2 * d + 1]
        w_s[d, LANES:, :LANES] = taps[2 * d - 1] if d > 0 else zero
        w_s[d, LANES:, LANES:] = taps[2 * d]


def _ssm_kernel(u_ref, ar_ref, ai_ref, dt_ref, btr_ref, bti_ref, cr_ref, ci_ref, d_ref,
                dtare_ref, dtaim_ref, w0_ref, w1_ref, w2_ref, w3_ref, y_ref, c0_ref, c1_ref, c2_ref,
                c3_ref, xre_s, xim_s, p_s, q_s, w_s, *, batch, n_chunks):
    step = pl.program_id(0)
    for src, dst in ((w0_ref, c0_ref), (w1_ref, c1_ref), (w2_ref, c2_ref), (w3_ref, c3_ref)):
        dst[...] = src[...].astype(BF16)
    rows8 = batch * PAIRS
    slab_rows = n_chunks * rows8

    def state_rows(slab, b, k):
        return pl.ds(slab * slab_rows + b * PAIRS + k, n_chunks, stride=rows8)

    def per_row(v):
        v = jnp.broadcast_to(v, (SLAB_G, STATE))
        return jnp.broadcast_to(v[:, None, :], (SLAB_G, GROUP, STATE)).reshape(LANES, STATE)

    def powers():
        return _slab_powers(per_row(ar_ref[...]), per_row(ai_ref[...]),
                            per_row(jnp.exp(dt_ref[...])), btr_ref[...], bti_ref[...])

    @pl.when(step < N_SLABS)
    def _():
        _, abb = powers()
        _build_state_in(abb, p_s)
        u = u_ref[...]
        for k in range(PAIRS):
            x = _dot(u, p_s[k])
            for b in range(batch):
                xb = x[b * n_chunks:(b + 1) * n_chunks]
                xre_s[state_rows(step, b, k), :] = xb[:, :LANES]
                xim_s[state_rows(step, b, k), :] = xb[:, LANES:]

    @pl.when(step == N_SLABS)
    def _():
        dta_re = jnp.concatenate([dtare_ref[...]] * batch, axis=1).reshape(N_SLABS * rows8, LANES)
        dta_im = jnp.concatenate([dtaim_ref[...]] * batch, axis=1).reshape(N_SLABS * rows8, LANES)
        mag = jnp.exp(CHUNK * dta_re)
        a_re, a_im = mag * jnp.cos(CHUNK * dta_im), mag * jnp.sin(CHUNK * dta_im)

        def body(c, carry):
            s_re, s_im = carry
            x_re, x_im = [], []
            for sl in range(N_SLABS):
                off = pl.multiple_of(sl * slab_rows + c * rows8, rows8)
                x_re.append(xre_s[pl.ds(off, rows8), :])
                x_im.append(xim_s[pl.ds(off, rows8), :])
                xre_s[pl.ds(off, rows8), :] = s_re[sl * rows8:(sl + 1) * rows8]
                xim_s[pl.ds(off, rows8), :] = s_im[sl * rows8:(sl + 1) * rows8]
            x_re = jnp.concatenate(x_re, axis=0)
            x_im = jnp.concatenate(x_im, axis=0)
            return (a_re * s_re - a_im * s_im + x_re, a_re * s_im + a_im * s_re + x_im)

        zero = jnp.zeros((N_SLABS * rows8, LANES), F32)
        lax.fori_loop(0, n_chunks, body, (zero, zero))

    @pl.when(step > N_SLABS)
    def _():
        slab = step - (N_SLABS + 1)
        pw, abb = powers()
        cr, ci = cr_ref[...], ci_ref[...]
        _build_state_out(pw, cr, ci, q_s)
        _build_toeplitz(abb, cr, ci, d_ref[...], w_s)
        u = u_ref[...]
        y_state = None
        for k in range(PAIRS):
            s_in = jnp.concatenate(
                [jnp.concatenate([xre_s[state_rows(slab, b, k), :], xim_s[state_rows(slab, b, k), :]],
                                 axis=1) for b in range(batch)], axis=0).astype(BF16)
            part = _dot(s_in, q_s[k])
            y_state = part if y_state is None else y_state + part
        for t2 in range(N_TILES):
            acc = y_state[:, t2 * MXU:(t2 + 1) * MXU]
            for t1 in range(t2 + 1):
                acc = acc + _dot(u[:, t1 * MXU:(t1 + 1) * MXU], w_s[t2 - t1])
            y_ref[:, t2 * MXU:(t2 + 1) * MXU] = jax.nn.gelu(acc).astype(BF16)


N_CAST = 16


def _ssm(u_chunks, A_re, A_im, log_dt, B_re, B_im, C_re, C_im, D_skip, batch, tail_weights):
    _, rows, _ = u_chunks.shape
    n_chunks = rows // batch
    n_steps = 2 * N_SLABS + 1
    kernel = functools.partial(_ssm_kernel, batch=batch, n_chunks=n_chunks)

    dt = jnp.exp(log_dt)[:, None]
    btr = B_re.transpose(0, 2, 1).reshape(SSM_W, STATE)
    bti = B_im.transpose(0, 2, 1).reshape(SSM_W, STATE)
    cr, ci = C_re.reshape(SSM_W, STATE), C_im.reshape(SSM_W, STATE)
    d_rows = D_skip.reshape(N_SLABS, 1, LANES)
    dta_re = (dt * A_re).reshape(N_SLABS, PAIRS, LANES)
    dta_im = (dt * A_im).reshape(N_SLABS, PAIRS, LANES)

    def slab_in(s):
        return jnp.where(s <= N_SLABS, jnp.minimum(s, N_SLABS - 1), s - (N_SLABS + 1))

    def slab_out(s):
        return jnp.maximum(s - (N_SLABS + 1), 0)

    def cast_spec(w):
        return pl.BlockSpec((w.shape[0] // N_CAST, w.shape[1]),
                            lambda s: (jnp.minimum(s, N_CAST - 1), 0))

    coeff = pl.BlockSpec((LANES, STATE), lambda s: (slab_in(s), 0))
    whole = pl.BlockSpec((N_SLABS, PAIRS, LANES), lambda s: (0, 0, 0))
    return pl.pallas_call(
        kernel,
        grid=(n_steps,),
        in_specs=[
            pl.BlockSpec((None, rows, CHUNK_W), lambda s: (slab_in(s), 0, 0)),
            pl.BlockSpec((SLAB_G, STATE), lambda s: (slab_in(s), 0)),
            pl.BlockSpec((SLAB_G, STATE), lambda s: (slab_in(s), 0)),
            pl.BlockSpec((SLAB_G, 1), lambda s: (slab_in(s), 0)),
            coeff, coeff, coeff, coeff,
            pl.BlockSpec((None, 1, LANES), lambda s: (slab_in(s), 0, 0)),
            whole, whole,
        ] + [cast_spec(w) for w in tail_weights],
        out_specs=[pl.BlockSpec((None, rows, CHUNK_W), lambda s: (slab_out(s), 0, 0))]
        + [cast_spec(w) for w in tail_weights],
        out_shape=[jax.ShapeDtypeStruct(u_chunks.shape, BF16)]
        + [jax.ShapeDtypeStruct(w.shape, BF16) for w in tail_weights],
        scratch_shapes=[
            pltpu.VMEM((N_SLABS * n_chunks * batch * PAIRS, LANES), F32),
            pltpu.VMEM((N_SLABS * n_chunks * batch * PAIRS, LANES), F32),
            pltpu.VMEM((PAIRS, CHUNK_W, 2 * LANES), BF16),
            pltpu.VMEM((PAIRS, 2 * LANES, CHUNK_W), BF16),
            pltpu.VMEM((N_TILES, MXU, MXU), BF16),
        ],
        compiler_params=pltpu.CompilerParams(
            dimension_semantics=("arbitrary",), vmem_limit_bytes=VMEM_LIMIT),
        name="ssm",
    )(u_chunks, A_re, A_im, log_dt[:, None], btr, bti, cr, ci, d_rows, dta_re, dta_im,
      *tail_weights)


def _tail_kernel(sink_ref, q_ref, g_ref, kc_ref, vc_ref, kp_ref, vp_ref, qw_ref, kw_ref,
                 yc_ref, z_ref, ga_ref, gs_ref, x_ref, wa_ref, wg_ref, bg_ref, ws_ref,
                 wo_ref, o_ref, ys_ref, ag_ref, *, tm, tiles_per_seq):
    for block in _swa_tile(sink_ref, q_ref, g_ref, kc_ref, vc_ref, kp_ref, vp_ref, qw_ref, kw_ref,
                           ag_ref, pl.program_id(0) % tiles_per_seq == 0, n_sub=tm // WINDOW):
        block()

    rows = tm // CHUNK
    for s in range(N_SLABS):
        for t in range(CHUNK):
            ys_ref[pl.ds(s * rows * US_PITCH + t, rows, stride=US_PITCH), :] = (
                yc_ref[s, :, t * LANES:(t + 1) * LANES].astype(F32))
    y_g = jnp.concatenate(
        [jnp.concatenate([ys_ref[(s * rows + c) * US_PITCH:(s * rows + c) * US_PITCH + CHUNK, :]
                          for c in range(rows)], axis=0) for s in range(N_SLABS)],
        axis=1).astype(BF16)

    y_a = _dot(ag_ref[...], wa_ref[...])
    glu = _dot(y_g, wg_ref[...]) + bg_ref[...]
    z = z_ref[...].astype(F32)
    t = glu[:, :SSM_W] * jax.nn.sigmoid(glu[:, SSM_W:]) * (z * jax.nn.sigmoid(z))
    y_s = _dot(t.astype(BF16), ws_ref[...])
    merged = (jax.nn.sigmoid(ga_ref[...].astype(F32)) * y_a
              + jax.nn.sigmoid(gs_ref[...].astype(F32)) * y_s)
    o_ref[...] = x_ref[...] + _dot(merged.astype(BF16), wo_ref[...])


def _tail(proj, sinks, q_norm_w, k_norm_w, y_chunks, x2, w_attn, w_glu, b_glu, w_ssm, w_out, seq,
          tm=256):
    m = x2.shape[0]
    row = lambda blk: (lambda i: (i, blk))
    prev = lambda blk: (lambda i: (jnp.maximum(i * (tm // WINDOW) - 1, 0), blk))
    const = lambda i: (0, 0)
    resident = functools.partial(pl.BlockSpec, index_map=const, pipeline_mode=pl.Buffered(1))
    kernel = functools.partial(_tail_kernel, tm=tm, tiles_per_seq=seq // tm)
    return pl.pallas_call(
        kernel,
        grid=(m // tm,),
        in_specs=[
            pl.BlockSpec(memory_space=pltpu.SMEM),
            pl.BlockSpec((tm, ATTN_W), row(COL_Q // ATTN_W)),
            pl.BlockSpec((tm, ATTN_W), row(COL_GATE // ATTN_W)),
            pl.BlockSpec((tm, KV_W), row(COL_K // KV_W)),
            pl.BlockSpec((tm, KV_W), row(COL_V // KV_W)),
            pl.BlockSpec((WINDOW, KV_W), prev(COL_K // KV_W)),
            pl.BlockSpec((WINDOW, KV_W), prev(COL_V // KV_W)),
            pl.BlockSpec((1, HEAD_DIM), const),
            pl.BlockSpec((1, HEAD_DIM), const),
            pl.BlockSpec((N_SLABS, tm // CHUNK, CHUNK_W), lambda i: (0, i, 0)),
            pl.BlockSpec((tm, SSM_W), row(COL_Z // SSM_W)),
            pl.BlockSpec((tm, D_MODEL), row(COL_GA // D_MODEL)),
            pl.BlockSpec((tm, D_MODEL), row(COL_GS // D_MODEL)),
            pl.BlockSpec((tm, D_MODEL), row(0)),
            resident((ATTN_W, D_MODEL)),
            resident((SSM_W, 2 * SSM_W)),
            resident((1, 2 * SSM_W)),
            resident((SSM_W, D_MODEL)),
            resident((D_MODEL, D_MODEL)),
        ],
        out_specs=pl.BlockSpec((tm, D_MODEL), row(0)),
        out_shape=jax.ShapeDtypeStruct((m, D_MODEL), F32),
        scratch_shapes=[pltpu.VMEM((N_SLABS * (tm // CHUNK) * US_PITCH, LANES), F32),
                        pltpu.VMEM((tm, ATTN_W), BF16)],
        compiler_params=pltpu.CompilerParams(
            dimension_semantics=("arbitrary",), vmem_limit_bytes=VMEM_LIMIT),
        name="tail",
    )(sinks, proj, proj, proj, proj, proj, proj,
      q_norm_w.reshape(1, HEAD_DIM), k_norm_w.reshape(1, HEAD_DIM),
      y_chunks, proj, proj, proj, x2, w_attn, w_glu, b_glu.reshape(1, -1), w_ssm, w_out)


def kernel(x, norm_w, w_in, q_norm_w, k_norm_w, sinks, w_attn_proj, A_re, A_im, log_dt, B_re, B_im,
           C_re, C_im, D_skip, w_glu, b_glu, w_ssm_proj, w_out):
    batch, seq, _ = x.shape
    m = batch * seq
    x2 = x.reshape(m, D_MODEL)

    proj, u_chunks = _in_proj(x2, norm_w, w_in)

    y_chunks, w_attn_bf, w_glu_bf, w_ssm_bf, w_out_bf = _ssm(
        u_chunks, A_re, A_im, log_dt, B_re, B_im, C_re, C_im, D_skip, batch,
        (w_attn_proj, w_glu, w_ssm_proj, w_out))

    out = _tail(proj, sinks, q_norm_w, k_norm_w, y_chunks, x2, w_attn_bf, w_glu_bf, b_glu,
                w_ssm_bf, w_out_bf, seq)
    return out.reshape(batch, seq, D_MODEL)
```

```python
import functools
import math

import jax
import jax.numpy as jnp
from jax import lax
from jax.experimental import pallas as pl
from jax.experimental.pallas import tpu as pltpu

F32 = jnp.float32
BF16 = jnp.bfloat16

D_MODEL = 2048
HEAD_DIM = 64
N_Q_HEADS = 16
N_KV_HEADS = 4
Q_PER_KV = 4
ATTN_W = N_Q_HEADS * HEAD_DIM
KV_W = N_KV_HEADS * HEAD_DIM
WINDOW = 128
SSM_W = D_MODEL // 2
GROUP = 16
N_GROUPS = SSM_W // GROUP
STATE = 64
NORM_EPS = 1e-6

COL_Q = 0
COL_GATE = COL_Q + ATTN_W
COL_U = COL_GATE + ATTN_W
COL_Z = COL_U + SSM_W
COL_GA = COL_Z + SSM_W
COL_GS = COL_GA + D_MODEL
COL_K = COL_GS + D_MODEL
COL_V = COL_K + KV_W
IN_W = COL_V + KV_W

LANES = 128
MXU = 256
CHUNK = 16
SLAB_G = LANES // GROUP
N_SLABS = N_GROUPS // SLAB_G
PAIRS = SLAB_G // 2
CHUNK_W = CHUNK * LANES
N_TILES = CHUNK_W // MXU
VMEM_LIMIT = 48 * 1024 * 1024


def _dot(a, b):
    return jnp.dot(a, b, preferred_element_type=F32)


IN_TN = 512
N_IN_TILES = IN_W // IN_TN
SRC_KV_TILE = (ATTN_W) // IN_TN
SRC_U_TILE0 = (ATTN_W + 2 * KV_W + ATTN_W) // IN_TN


US_PITCH = CHUNK if (CHUNK // 8) % 2 else CHUNK + 8
X_ROWS = 128
IN_PER_ROUND = 2


def _dst_tile(src_tile):
    if src_tile < SRC_KV_TILE:
        return src_tile
    if src_tile == SRC_KV_TILE:
        return N_IN_TILES - 1
    return src_tile - 1


def _in_proj_kernel(nw_ref, x_hbm, w_hbm, o_hbm, uc_ref, h_ref, x_buf, w_buf, o_buf, us_ref, x_sem,
                    w_sem, o_sem, *, tm, n_steps):
    i = pl.program_id(0)
    rows = tm // CHUNK
    n_sl = IN_TN // LANES
    n_rounds = pl.cdiv(N_IN_TILES, IN_PER_ROUND)
    assert n_rounds % 2 == 1
    n_xc = tm // X_ROWS
    assert n_xc < n_rounds
    cur, nxt = i % 2, (i + 1) % 2
    nxt_tile = jnp.minimum(i + 1, n_steps - 1)

    def x_copy(tile, c):
        src = x_hbm.at[pl.ds(pl.multiple_of(tile * tm + c * X_ROWS, X_ROWS), X_ROWS), :]
        return pltpu.make_async_copy(src, x_buf.at[c % 2], x_sem.at[c % 2])

    def norm_chunk(c, dst):
        x = x_buf[c % 2]
        ms = jnp.mean(x * x, axis=-1, keepdims=True)
        h_ref[dst, c * X_ROWS:(c + 1) * X_ROWS, :] = (
            x * lax.rsqrt(ms + NORM_EPS) * nw_ref[...]).astype(BF16)

    def round_tiles(r):
        return [t for t in range(r * IN_PER_ROUND, (r + 1) * IN_PER_ROUND) if 0 <= t < N_IN_TILES]

    def w_half(step, r):
        return (step + r) % 2

    def w_copy(step, r):
        tiles = round_tiles(r)
        width = len(tiles) * IN_TN
        half = w_half(step, r)
        return pltpu.make_async_copy(w_hbm.at[:, pl.ds(tiles[0] * IN_TN, width)],
                                     w_buf.at[half, :, pl.ds(0, width)], w_sem.at[half])

    def o_copies(r):
        tiles = round_tiles(r)
        runs, start = [], 0
        for j in range(1, len(tiles) + 1):
            if j == len(tiles) or _dst_tile(tiles[j]) != _dst_tile(tiles[j - 1]) + 1:
                runs.append((start, j - start))
                start = j
        row0 = pl.multiple_of(i * tm, tm)
        return [pltpu.make_async_copy(
            o_buf.at[r % 2, :, pl.ds(j0 * IN_TN, n * IN_TN)],
            o_hbm.at[pl.ds(row0, tm), pl.ds(_dst_tile(tiles[j0]) * IN_TN, n * IN_TN)],
            o_sem.at[r % 2, k]) for k, (j0, n) in enumerate(runs)]

    @pl.when(i == 0)
    def _():
        w_copy(0, 0).start()
        x_copy(0, 0).start()
        for c in range(n_xc):
            if c + 1 < n_xc:
                x_copy(0, c + 1).start()
            x_copy(0, c).wait()
            norm_chunk(c, 0)

    def finish(t, acc):
        r, j = divmod(t, IN_PER_ROUND)
        o_buf[r % 2, :, j * IN_TN:(j + 1) * IN_TN] = acc.astype(BF16)
        if SRC_U_TILE0 <= t < SRC_U_TILE0 + SSM_W // IN_TN:
            slab0 = (t - SRC_U_TILE0) * n_sl
            for s in range(n_sl):
                for c in range(rows):
                    r0 = (s * rows + c) * US_PITCH
                    us_ref[r0:r0 + CHUNK, :] = acc[c * CHUNK:(c + 1) * CHUNK, s * LANES:(s + 1) * LANES]
            for s in range(n_sl):
                for tok in range(CHUNK):
                    piece = us_ref[pl.ds(s * rows * US_PITCH + tok, rows, stride=US_PITCH), :]
                    uc_ref[slab0 + s, :, tok * LANES:(tok + 1) * LANES] = piece.astype(BF16)

    for r in range(n_rounds):
        if r + 1 < n_rounds:
            w_copy(i, r + 1).start()
        else:
            @pl.when(i + 1 < n_steps)
            def _():
                w_copy(i + 1, 0).start()
        w_copy(i, r).wait()
        if r >= 2:
            for c in o_copies(r - 2):
                c.wait()
        if r >= 1:
            for c in o_copies(r - 1):
                c.start()
        if r < n_xc:
            x_copy(nxt_tile, r).start()
        if 1 <= r <= n_xc:
            x_copy(nxt_tile, r - 1).wait()
        for j, t in enumerate(round_tiles(r)):
            acc = None
            for k in range(D_MODEL // MXU):
                ks = slice(k * MXU, (k + 1) * MXU)
                w_tile = w_buf[w_half(i, r), ks, j * IN_TN:(j + 1) * IN_TN]
                part = _dot(h_ref[cur, :, ks], w_tile.astype(BF16))
                acc = part if acc is None else acc + part
            finish(t, acc)
        if 1 <= r <= n_xc:
            norm_chunk(r - 1, nxt)

    for c in o_copies(n_rounds - 1):
        c.start()
    for c in o_copies(n_rounds - 2) + o_copies(n_rounds - 1):
        c.wait()


def _in_proj(x2, norm_w, w_in, tm=1024):
    m = x2.shape[0]
    kernel = functools.partial(_in_proj_kernel, tm=tm, n_steps=m // tm)
    return pl.pallas_call(
        kernel,
        grid=(m // tm,),
        in_specs=[
            pl.BlockSpec((1, D_MODEL), lambda i: (0, 0)),
            pl.BlockSpec(memory_space=pl.ANY),
            pl.BlockSpec(memory_space=pl.ANY),
        ],
        out_specs=[
            pl.BlockSpec(memory_space=pl.ANY),
            pl.BlockSpec((N_SLABS, tm // CHUNK, CHUNK_W), lambda i: (0, i, 0)),
        ],
        out_shape=[
            jax.ShapeDtypeStruct((m, IN_W), BF16),
            jax.ShapeDtypeStruct((N_SLABS, m // CHUNK, CHUNK_W), BF16),
        ],
        scratch_shapes=[
            pltpu.VMEM((2, tm, D_MODEL), BF16),
            pltpu.VMEM((2, X_ROWS, D_MODEL), F32),
            pltpu.VMEM((2, D_MODEL, IN_PER_ROUND * IN_TN), F32),
            pltpu.VMEM((2, tm, IN_PER_ROUND * IN_TN), BF16),
            pltpu.VMEM((IN_TN // LANES * (tm // CHUNK) * US_PITCH, LANES), F32),
            pltpu.SemaphoreType.DMA((2,)),
            pltpu.SemaphoreType.DMA((2,)),
            pltpu.SemaphoreType.DMA((2, IN_PER_ROUND)),
        ],
        compiler_params=pltpu.CompilerParams(
            dimension_semantics=("arbitrary",), vmem_limit_bytes=VMEM_LIMIT),
        name="in_proj",
    )(norm_w.reshape(1, D_MODEL), x2, w_in)


def _head_norm(t, w):
    ms = jnp.mean(t * t, axis=-1, keepdims=True)
    return t * lax.rsqrt(ms + NORM_EPS) * w


_NT = (((1,), (1,)), ((), ()))
_TN = (((0,), (0,)), ((), ()))


def _swa_tile(sink_ref, q_ref, g_ref, kc_ref, vc_ref, kp_ref, vp_ref, qw_ref, kw_ref, o_ref,
              first_tile, n_sub):
    log2e = math.log2(math.e)
    kqw = kw_ref[...] * qw_ref[...] * (log2e / math.sqrt(HEAD_DIM))
    n_col = Q_PER_KV * WINDOW
    key = lax.broadcasted_iota(jnp.int32, (WINDOW, WINDOW), 0)
    qry = lax.broadcasted_iota(jnp.int32, (WINDOW, WINDOW), 1)
    band = key > qry
    no_prev = jnp.where(first_tile, -1e30, 0.0)
    head_of_col = lax.broadcasted_iota(jnp.int32, (1, n_col), 1) // WINDOW
    gw = Q_PER_KV * HEAD_DIM
    sel_r = lax.broadcasted_iota(jnp.int32, (8, 2 * gw), 0)
    sel_l = (lax.broadcasted_iota(jnp.int32, (8, 2 * gw), 1) % gw) // HEAD_DIM
    head_sel = jnp.where(sel_r == sel_l, 1.0, 0.0).astype(BF16)

    parts = []
    carry = {}

    def block(g, n):
        kcol = slice(g * HEAD_DIM, (g + 1) * HEAD_DIM)
        gcols = slice(g * Q_PER_KV * HEAD_DIM, (g + 1) * Q_PER_KV * HEAD_DIM)
        if n == 0:
            sink = jnp.zeros((1, n_col), F32)
            for r in range(Q_PER_KV):
                sink = jnp.where(head_of_col == r, sink_ref[g * Q_PER_KV + r] * log2e, sink)
            carry['sink'] = sink
            carry['k'] = _head_norm(kp_ref[:, kcol].astype(F32), kqw).astype(BF16)
            carry['v'] = vp_ref[:, kcol]
        sink, k_prev, v_prev = carry['sink'], carry['k'], carry['v']
        rows = slice(n * WINDOW, (n + 1) * WINDOW)
        k_cur = _head_norm(kc_ref[rows, kcol].astype(F32), kqw).astype(BF16)
        v_cur = vc_ref[rows, kcol]
        k_ctx = jnp.concatenate([k_prev, k_cur], axis=0)
        v_ctx = jnp.concatenate([v_prev, v_cur], axis=0)

        qg = q_ref[rows, gcols]
        qf = qg.astype(F32)
        q_t = qf.T.astype(BF16)
        q_t = jnp.concatenate([q_t[r * HEAD_DIM:(r + 1) * HEAD_DIM] for r in range(Q_PER_KV)],
                              axis=1)
        q2 = qf * qf
        q2_hi = q2.astype(BF16)
        q2_lo = (q2 - q2_hi.astype(F32)).astype(BF16)
        ssq = lax.dot_general(head_sel, jnp.concatenate([q2_hi, q2_lo], axis=1), _NT,
                              preferred_element_type=F32)
        rms = lax.rsqrt(ssq * (1.0 / HEAD_DIM) + NORM_EPS)
        rms_q = jnp.concatenate([rms[r:r + 1] for r in range(Q_PER_KV)], axis=1)

        s_all = _dot(k_ctx, q_t)
        o_heads = []
        for r in range(Q_PER_KV):
            lanes = slice(r * WINDOW, (r + 1) * WINDOW)
            s_prev = s_all[:WINDOW, lanes] + no_prev if n == 0 else s_all[:WINDOW, lanes]
            s = jnp.where(band, s_prev, s_all[WINDOW:, lanes]) * rms_q[:, lanes]
            mx = jnp.maximum(jnp.max(s, axis=0, keepdims=True), sink[:, lanes])
            p = jnp.exp2(s - mx)
            den = jnp.sum(p, axis=0, keepdims=True) + jnp.exp2(sink[:, lanes] - mx)
            p_ctx = jnp.concatenate([jnp.where(band, p, 0.0), jnp.where(band, 0.0, p)],
                                    axis=0).astype(BF16)
            o_heads.append(lax.dot_general(v_ctx, p_ctx, _TN, preferred_element_type=F32)
                           * (1.0 / den))
        o_t = jnp.concatenate(o_heads, axis=1)
        halves = [jnp.concatenate([o_t[:, (2 * h) * WINDOW:(2 * h + 1) * WINDOW],
                                   o_t[:, (2 * h + 1) * WINDOW:(2 * h + 2) * WINDOW]], axis=0).T
                  for h in range(Q_PER_KV // 2)]
        og = jnp.concatenate(halves, axis=1)
        gate = g_ref[rows, gcols].astype(F32)
        o_ref[rows, gcols] = (og * (gate * jax.nn.sigmoid(gate))).astype(BF16)
        carry['k'], carry['v'] = k_cur, v_cur

    for g in range(N_KV_HEADS):
        for n in range(n_sub):
            parts.append(functools.partial(block, g, n))
    return parts


def _slab_powers(ar, ai, dt, btr, bti):
    dta_re, dta_im = dt * ar, dt * ai
    mag = jnp.exp(dta_re)
    ab_re, ab_im = mag * jnp.cos(dta_im), mag * jnp.sin(dta_im)
    pw = [(jnp.ones_like(ar), jnp.zeros_like(ar))]
    for _ in range(CHUNK):
        pr, pi = pw[-1]
        pw.append((pr * ab_re - pi * ab_im, pr * ab_im + pi * ab_re))
    den = ar * ar + ai * ai
    num_re, num_im = ab_re - 1.0, ab_im
    cf_re = (num_re * ar + num_im * ai) / den
    cf_im = (num_im * ar - num_re * ai) / den
    bb_re = cf_re * btr - cf_im * bti
    bb_im = cf_re * bti + cf_im * btr
    abb = [(pr * bb_re - pi * bb_im, pr * bb_im + pi * bb_re) for pr, pi in pw[:CHUNK]]
    return pw, abb


def _build_state_in(abb, p_s):
    row_g = lax.broadcasted_iota(jnp.int32, (LANES, STATE), 0) // GROUP
    even = row_g % 2 == 0
    pair_of_row = lax.broadcasted_iota(jnp.int32, (LANES, 2 * LANES), 0) // (2 * GROUP)
    for lag in range(CHUNK):
        t = CHUNK - 1 - lag
        re, im = abb[lag]
        blk = jnp.concatenate([jnp.where(even, re, 0.0), jnp.where(even, 0.0, re),
                               jnp.where(even, im, 0.0), jnp.where(even, 0.0, im)], axis=1)
        for k in range(PAIRS):
            p_s[k, t * LANES:(t + 1) * LANES, :] = jnp.where(pair_of_row == k, blk, 0.0).astype(BF16)


def _build_state_out(pw, cr, ci, q_s):
    lane_g = lax.broadcasted_iota(jnp.int32, (STATE, LANES), 1) // GROUP
    for t in range(CHUNK):
        pr, pi = pw[t + 1]
        cat = jnp.concatenate([cr * pr - ci * pi, -(cr * pi + ci * pr)], axis=1)
        cat_t = cat.T
        for k in range(PAIRS):
            for ri in range(2):
                for half in range(2):
                    piece = jnp.where(lane_g == 2 * k + half, cat_t[ri * STATE:(ri + 1) * STATE], 0.0)
                    r0 = ri * LANES + half * STATE
                    q_s[k, r0:r0 + STATE, t * LANES:(t + 1) * LANES] = piece.astype(BF16)


def _build_toeplitz(abb, cr, ci, d_row, w_s):
    def split(v):
        hi = v.astype(BF16)
        return hi, (v - hi.astype(F32)).astype(BF16)

    rhs_hi, rhs_lo = split(jnp.concatenate([cr, -ci], axis=1))
    rhs = jnp.concatenate([rhs_hi, rhs_lo, rhs_hi], axis=1)
    row = lax.broadcasted_iota(jnp.int32, (LANES, LANES), 0)
    col = lax.broadcasted_iota(jnp.int32, (LANES, LANES), 1)
    same_group = (row // GROUP) == (col // GROUP)
    taps = []
    for lag in range(CHUNK):
        lhs_hi, lhs_lo = split(jnp.concatenate(abb[lag], axis=1))
        k = lax.dot_general(jnp.concatenate([lhs_hi, lhs_hi, lhs_lo], axis=1), rhs,
                            (((1,), (1,)), ((), ())), preferred_element_type=F32)
        k = jnp.where(same_group, k, 0.0)
        if lag == 0:
            k = k + jnp.where(row == col, d_row, 0.0)
        taps.append(k.astype(BF16))
    zero = jnp.zeros((LANES, LANES), BF16)
    for d in range(N_TILES):
        w_s[d, :LANES, :LANES] = taps[2 * d]
        w_s[d, :LANES, LANES:] = taps[2 * d + 1]
        w_s[d, LANES:, :LANES] = taps[2 * d - 1] if d > 0 else zero
        w_s[d, LANES:, LANES:] = taps[2 * d]


def _ssm_kernel(u_ref, ar_ref, ai_ref, dt_ref, btr_ref, bti_ref, cr_ref, ci_ref, d_ref,
                dtare_ref, dtaim_ref, w0_ref, w1_ref, w2_ref, w3_ref, y_ref, c0_ref, c1_ref, c2_ref,
                c3_ref, xre_s, xim_s, p_s, q_s, w_s, *, batch, n_chunks):
    step = pl.program_id(0)
    for src, dst in ((w0_ref, c0_ref), (w1_ref, c1_ref), (w2_ref, c2_ref), (w3_ref, c3_ref)):
        dst[...] = src[...].astype(BF16)
    rows8 = batch * PAIRS
    slab_rows = n_chunks * rows8

    def state_rows(slab, b, k):
        return pl.ds(slab * slab_rows + b * PAIRS + k, n_chunks, stride=rows8)

    def per_row(v):
        v = jnp.broadcast_to(v, (SLAB_G, STATE))
        return jnp.broadcast_to(v[:, None, :], (SLAB_G, GROUP, STATE)).reshape(LANES, STATE)

    def powers():
        return _slab_powers(per_row(ar_ref[...]), per_row(ai_ref[...]),
                            per_row(jnp.exp(dt_ref[...])), btr_ref[...], bti_ref[...])

    @pl.when(step < N_SLABS)
    def _():
        _, abb = powers()
        _build_state_in(abb, p_s)
        u = u_ref[...]
        for k in range(PAIRS):
            x = _dot(u, p_s[k])
            for b in range(batch):
                xb = x[b * n_chunks:(b + 1) * n_chunks]
                xre_s[state_rows(step, b, k), :] = xb[:, :LANES]
                xim_s[state_rows(step, b, k), :] = xb[:, LANES:]

    @pl.when(step == N_SLABS)
    def _():
        dta_re = jnp.concatenate([dtare_ref[...]] * batch, axis=1).reshape(N_SLABS * rows8, LANES)
        dta_im = jnp.concatenate([dtaim_ref[...]] * batch, axis=1).reshape(N_SLABS * rows8, LANES)
        mag = jnp.exp(CHUNK * dta_re)
        a_re, a_im = mag * jnp.cos(CHUNK * dta_im), mag * jnp.sin(CHUNK * dta_im)

        def body(c, carry):
            s_re, s_im = carry
            x_re, x_im = [], []
            for sl in range(N_SLABS):
                off = pl.multiple_of(sl * slab_rows + c * rows8, rows8)
                x_re.append(xre_s[pl.ds(off, rows8), :])
                x_im.append(xim_s[pl.ds(off, rows8), :])
                xre_s[pl.ds(off, rows8), :] = s_re[sl * rows8:(sl + 1) * rows8]
                xim_s[pl.ds(off, rows8), :] = s_im[sl * rows8:(sl + 1) * rows8]
            x_re = jnp.concatenate(x_re, axis=0)
            x_im = jnp.concatenate(x_im, axis=0)
            return (a_re * s_re - a_im * s_im + x_re, a_re * s_im + a_im * s_re + x_im)

        zero = jnp.zeros((N_SLABS * rows8, LANES), F32)
        lax.fori_loop(0, n_chunks, body, (zero, zero))

    @pl.when(step > N_SLABS)
    def _():
        slab = step - (N_SLABS + 1)
        pw, abb = powers()
        cr, ci = cr_ref[...], ci_ref[...]
        _build_state_out(pw, cr, ci, q_s)
        _build_toeplitz(abb, cr, ci, d_ref[...], w_s)
        u = u_ref[...]
        y_state = None
        for k in range(PAIRS):
            s_in = jnp.concatenate(
                [jnp.concatenate([xre_s[state_rows(slab, b, k), :], xim_s[state_rows(slab, b, k), :]],
                                 axis=1) for b in range(batch)], axis=0).astype(BF16)
            part = _dot(s_in, q_s[k])
            y_state = part if y_state is None else y_state + part
        for t2 in range(N_TILES):
            acc = y_state[:, t2 * MXU:(t2 + 1) * MXU]
            for t1 in range(t2 + 1):
                acc = acc + _dot(u[:, t1 * MXU:(t1 + 1) * MXU], w_s[t2 - t1])
            y_ref[:, t2 * MXU:(t2 + 1) * MXU] = jax.nn.gelu(acc).astype(BF16)


N_CAST = 16


def _ssm(u_chunks, A_re, A_im, log_dt, B_re, B_im, C_re, C_im, D_skip, batch, tail_weights):
    _, rows, _ = u_chunks.shape
    n_chunks = rows // batch
    n_steps = 2 * N_SLABS + 1
    kernel = functools.partial(_ssm_kernel, batch=batch, n_chunks=n_chunks)

    dt = jnp.exp(log_dt)[:, None]
    btr = B_re.transpose(0, 2, 1).reshape(SSM_W, STATE)
    bti = B_im.transpose(0, 2, 1).reshape(SSM_W, STATE)
    cr, ci = C_re.reshape(SSM_W, STATE), C_im.reshape(SSM_W, STATE)
    d_rows = D_skip.reshape(N_SLABS, 1, LANES)
    dta_re = (dt * A_re).reshape(N_SLABS, PAIRS, LANES)
    dta_im = (dt * A_im).reshape(N_SLABS, PAIRS, LANES)

    def slab_in(s):
        return jnp.where(s <= N_SLABS, jnp.minimum(s, N_SLABS - 1), s - (N_SLABS + 1))

    def slab_out(s):
        return jnp.maximum(s - (N_SLABS + 1), 0)

    def cast_spec(w):
        return pl.BlockSpec((w.shape[0] // N_CAST, w.shape[1]),
                            lambda s: (jnp.minimum(s, N_CAST - 1), 0))

    coeff = pl.BlockSpec((LANES, STATE), lambda s: (slab_in(s), 0))
    whole = pl.BlockSpec((N_SLABS, PAIRS, LANES), lambda s: (0, 0, 0))
    return pl.pallas_call(
        kernel,
        grid=(n_steps,),
        in_specs=[
            pl.BlockSpec((None, rows, CHUNK_W), lambda s: (slab_in(s), 0, 0)),
            pl.BlockSpec((SLAB_G, STATE), lambda s: (slab_in(s), 0)),
            pl.BlockSpec((SLAB_G, STATE), lambda s: (slab_in(s), 0)),
            pl.BlockSpec((SLAB_G, 1), lambda s: (slab_in(s), 0)),
            coeff, coeff, coeff, coeff,
            pl.BlockSpec((None, 1, LANES), lambda s: (slab_in(s), 0, 0)),
            whole, whole,
        ] + [cast_spec(w) for w in tail_weights],
        out_specs=[pl.BlockSpec((None, rows, CHUNK_W), lambda s: (slab_out(s), 0, 0))]
        + [cast_spec(w) for w in tail_weights],
        out_shape=[jax.ShapeDtypeStruct(u_chunks.shape, BF16)]
        + [jax.ShapeDtypeStruct(w.shape, BF16) for w in tail_weights],
        scratch_shapes=[
            pltpu.VMEM((N_SLABS * n_chunks * batch * PAIRS, LANES), F32),
            pltpu.VMEM((N_SLABS * n_chunks * batch * PAIRS, LANES), F32),
            pltpu.VMEM((PAIRS, CHUNK_W, 2 * LANES), BF16),
            pltpu.VMEM((PAIRS, 2 * LANES, CHUNK_W), BF16),
            pltpu.VMEM((N_TILES, MXU, MXU), BF16),
        ],
        compiler_params=pltpu.CompilerParams(
            dimension_semantics=("arbitrary",), vmem_limit_bytes=VMEM_LIMIT),
        name="ssm",
    )(u_chunks, A_re, A_im, log_dt[:, None], btr, bti, cr, ci, d_rows, dta_re, dta_im,
      *tail_weights)


def _tail_kernel(sink_ref, q_ref, g_ref, kc_ref, vc_ref, kp_ref, vp_ref, qw_ref, kw_ref,
                 yc_ref, z_ref, ga_ref, gs_ref, x_ref, wa_ref, wg_ref, bg_ref, ws_ref,
                 wo_ref, o_ref, ys_ref, ag_ref, *, tm, tiles_per_seq):
    for block in _swa_tile(sink_ref, q_ref, g_ref, kc_ref, vc_ref, kp_ref, vp_ref, qw_ref, kw_ref,
                           ag_ref, pl.program_id(0) % tiles_per_seq == 0, n_sub=tm // WINDOW):
        block()

    rows = tm // CHUNK
    for s in range(N_SLABS):
        for t in range(CHUNK):
            ys_ref[pl.ds(s * rows * US_PITCH + t, rows, stride=US_PITCH), :] = (
                yc_ref[s, :, t * LANES:(t + 1) * LANES].astype(F32))
    y_g = jnp.concatenate(
        [jnp.concatenate([ys_ref[(s * rows + c) * US_PITCH:(s * rows + c) * US_PITCH + CHUNK, :]
                          for c in range(rows)], axis=0) for s in range(N_SLABS)],
        axis=1).astype(BF16)

    y_a = _dot(ag_ref[...], wa_ref[...])
    glu = _dot(y_g, wg_ref[...]) + bg_ref[...]
    z = z_ref[...].astype(F32)
    t = glu[:, :SSM_W] * jax.nn.sigmoid(glu[:, SSM_W:]) * (z * jax.nn.sigmoid(z))
    y_s = _dot(t.astype(BF16), ws_ref[...])
    merged = (jax.nn.sigmoid(ga_ref[...].astype(F32)) * y_a
              + jax.nn.sigmoid(gs_ref[...].astype(F32)) * y_s)
    o_ref[...] = x_ref[...] + _dot(merged.astype(BF16), wo_ref[...])


def _tail(proj, sinks, q_norm_w, k_norm_w, y_chunks, x2, w_attn, w_glu, b_glu, w_ssm, w_out, seq,
          tm=256):
    m = x2.shape[0]
    row = lambda blk: (lambda i: (i, blk))
    prev = lambda blk: (lambda i: (jnp.maximum(i * (tm // WINDOW) - 1, 0), blk))
    const = lambda i: (0, 0)
    resident = functools.partial(pl.BlockSpec, index_map=const, pipeline_mode=pl.Buffered(1))
    kernel = functools.partial(_tail_kernel, tm=tm, tiles_per_seq=seq // tm)
    return pl.pallas_call(
        kernel,
        grid=(m // tm,),
        in_specs=[
            pl.BlockSpec(memory_space=pltpu.SMEM),
            pl.BlockSpec((tm, ATTN_W), row(COL_Q // ATTN_W)),
            pl.BlockSpec((tm, ATTN_W), row(COL_GATE // ATTN_W)),
            pl.BlockSpec((tm, KV_W), row(COL_K // KV_W)),
            pl.BlockSpec((tm, KV_W), row(COL_V // KV_W)),
            pl.BlockSpec((WINDOW, KV_W), prev(COL_K // KV_W)),
            pl.BlockSpec((WINDOW, KV_W), prev(COL_V // KV_W)),
            pl.BlockSpec((1, HEAD_DIM), const),
            pl.BlockSpec((1, HEAD_DIM), const),
            pl.BlockSpec((N_SLABS, tm // CHUNK, CHUNK_W), lambda i: (0, i, 0)),
            pl.BlockSpec((tm, SSM_W), row(COL_Z // SSM_W)),
            pl.BlockSpec((tm, D_MODEL), row(COL_GA // D_MODEL)),
            pl.BlockSpec((tm, D_MODEL), row(COL_GS // D_MODEL)),
            pl.BlockSpec((tm, D_MODEL), row(0)),
            resident((ATTN_W, D_MODEL)),
            resident((SSM_W, 2 * SSM_W)),
            resident((1, 2 * SSM_W)),
            resident((SSM_W, D_MODEL)),
            resident((D_MODEL, D_MODEL)),
        ],
        out_specs=pl.BlockSpec((tm, D_MODEL), row(0)),
        out_shape=jax.ShapeDtypeStruct((m, D_MODEL), F32),
        scratch_shapes=[pltpu.VMEM((N_SLABS * (tm // CHUNK) * US_PITCH, LANES), F32),
                        pltpu.VMEM((tm, ATTN_W), BF16)],
        compiler_params=pltpu.CompilerParams(
            dimension_semantics=("arbitrary",), vmem_limit_bytes=VMEM_LIMIT),
        name="tail",
    )(sinks, proj, proj, proj, proj, proj, proj,
      q_norm_w.reshape(1, HEAD_DIM), k_norm_w.reshape(1, HEAD_DIM),
      y_chunks, proj, proj, proj, x2, w_attn, w_glu, b_glu.reshape(1, -1), w_ssm, w_out)


def kernel(x, norm_w, w_in, q_norm_w, k_norm_w, sinks, w_attn_proj, A_re, A_im, log_dt, B_re, B_im,
           C_re, C_im, D_skip, w_glu, b_glu, w_ssm_proj, w_out):
    batch, seq, _ = x.shape
    m = batch * seq
    x2 = x.reshape(m, D_MODEL)

    proj, u_chunks = _in_proj(x2, norm_w, w_in)

    y_chunks, w_attn_bf, w_glu_bf, w_ssm_bf, w_out_bf = _ssm(
        u_chunks, A_re, A_im, log_dt, B_re, B_im, C_re, C_im, D_skip, batch,
        (w_attn_proj, w_glu, w_ssm_proj, w_out))

    out = _tail(proj, sinks, q_norm_w, k_norm_w, y_chunks, x2, w_attn_bf, w_glu_bf, b_glu,
                w_ssm_bf, w_out_bf, seq)
    return out.reshape(batch, seq, D_MODEL)
```
